```python
import jax, jax.numpy as jnp
from jax import lax
import numpy as np

D_MODEL = 1024
BATCH = 16
SEQ = 4096
DEPTH = 1

MEM_LEN = 256
ROPE_THETA = 10000.0
EPS = 1e-6
Q_BLOCK = 128

DSA_HEADS = 8
DSA_HEAD_DIM = 64
IDX_HEADS = 8
IDX_DIM = 64
TOPK_MAX = 256
MLA_HEADS = 8
MLA_NOPE = 64
MLA_ROPE = 32
MLA_QK = MLA_NOPE + MLA_ROPE
MLA_V = 64
MLA_Q_RANK = 256
MLA_KV_RANK = 128
MEM_HEADS = 4
MEM_HEAD_DIM = 64
N_GROUPS = 4
EXPERTS_PER_GROUP = 4
N_EXPERTS = N_GROUPS * EXPERTS_PER_GROUP
TOPK_IN_GROUP = 2
D_EXPERT = 256

DSA_WIDTH = DSA_HEADS * DSA_HEAD_DIM
MLA_WIDTH = MLA_HEADS * MLA_V
IN_SIZES = (DSA_WIDTH, DSA_HEAD_DIM, DSA_HEAD_DIM, IDX_HEADS * IDX_DIM, IDX_DIM, IDX_HEADS,
            MLA_Q_RANK, MLA_KV_RANK, MLA_ROPE, D_MODEL, D_MODEL)
IN_OFFSETS = tuple(int(v) for v in np.cumsum(IN_SIZES)[:-1])
N_IN = int(sum(IN_SIZES))

kernel_name = 'hybrid_dsa_mla_hmoe_block'


def rms_norm(x, g):
    xf = x.astype(jnp.float32)
    y = xf * lax.rsqrt(jnp.mean(xf * xf, axis=-1, keepdims=True) + EPS)
    return (y * g.astype(jnp.float32)).astype(x.dtype)


def rope(x, pos):
    d = x.shape[-1]
    half = d // 2
    inv_freq = ROPE_THETA ** (-jnp.arange(half, dtype=jnp.float32) / half)
    ang = pos.astype(jnp.float32)[..., None] * inv_freq
    ang = ang.reshape(ang.shape[:2] + (1,) * (x.ndim - 3) + (half,))
    cos, sin = jnp.cos(ang), jnp.sin(ang)
    xf = x.astype(jnp.float32)
    x1, x2 = xf[..., :half], xf[..., half:]
    return jnp.concatenate([x1 * cos - x2 * sin, x2 * cos + x1 * sin], axis=-1).astype(x.dtype)


def to_blocks(a):
    b, s = a.shape[:2]
    return jnp.moveaxis(a.reshape((b, s // Q_BLOCK, Q_BLOCK) + a.shape[2:]), 1, 0)


def from_blocks(a):
    nb, b, q = a.shape[:3]
    return jnp.moveaxis(a, 0, 1).reshape((b, nb * q) + a.shape[3:])


def gather_rows(table, idx):
    return jax.vmap(lambda tb, ib: tb[ib])(table, idx)


def dsa_attention(q, k, v, q_idx, k_idx, w_idx):
    s_len = q.shape[1]
    topk = min(TOPK_MAX, s_len // 4)
    key_pos = jnp.arange(s_len, dtype=jnp.int32)
    idx_scale = IDX_DIM ** -0.5 * IDX_HEADS ** -0.5
    att_scale = DSA_HEAD_DIM ** -0.5

    def block(args):
        qb, qib, wb, start = args
        qpos = start + jnp.arange(Q_BLOCK, dtype=jnp.int32)
        rel = jax.nn.relu(jnp.einsum('bqhd,bkd->bqhk', qib, k_idx).astype(jnp.float32))
        score = jnp.einsum('bqhk,bqh->bqk', rel, wb.astype(jnp.float32)) * idx_scale
        causal = key_pos[None, :] <= qpos[:, None]
        score = jnp.where(causal[None], score, -jnp.inf)
        _, sel = lax.top_k(score, topk)
        valid = sel <= qpos[None, :, None]
        ks = gather_rows(k, sel)
        vs = gather_rows(v, sel)
        logits = jnp.einsum('bqhd,bqkd->bqhk', qb, ks).astype(jnp.float32) * att_scale
        logits = jnp.where(valid[:, :, None, :], logits, -jnp.inf)
        p = jax.nn.softmax(logits, axis=-1).astype(vs.dtype)
        return jnp.einsum('bqhk,bqkd->bqhd', p, vs)

    starts = jnp.arange(s_len // Q_BLOCK, dtype=jnp.int32) * Q_BLOCK
    out = lax.map(block, (to_blocks(q), to_blocks(q_idx), to_blocks(w_idx), starts))
    return from_blocks(out)


def causal_block_attention(q, k, v, scale):
    s_len = q.shape[1]
    key_pos = jnp.arange(s_len, dtype=jnp.int32)

    def block(args):
        qb, start = args
        qpos = start + jnp.arange(Q_BLOCK, dtype=jnp.int32)
        logits = jnp.einsum('bqhd,bkhd->bhqk', qb, k).astype(jnp.float32) * scale
        mask = key_pos[None, :] <= qpos[:, None]
        logits = jnp.where(mask[None, None], logits, -jnp.inf)
        p = jax.nn.softmax(logits, axis=-1).astype(v.dtype)
        return jnp.einsum('bhqk,bkhd->bqhd', p, v)

    starts = jnp.arange(s_len // Q_BLOCK, dtype=jnp.int32) * Q_BLOCK
    return from_blocks(lax.map(block, (to_blocks(q), starts)))


def parallel_mixer(h, pos, w_in, dsa_q_g, dsa_k_g, mla_cq_g, mla_ckv_g, mla_w_uq, mla_w_ukv,
                   mla_q_g, mla_k_g, w_br_dsa, w_br_mla, w_out):
    b, s, _ = h.shape
    proj = h @ w_in
    dq, dk, dv, iq, ik, iw, cq, ckv, kpe, gd, gm = jnp.split(proj, IN_OFFSETS, axis=-1)

    q_a = rope(rms_norm(dq.reshape(b, s, DSA_HEADS, DSA_HEAD_DIM), dsa_q_g), pos)
    k_a = rope(rms_norm(dk, dsa_k_g), pos)
    q_i = rope(iq.reshape(b, s, IDX_HEADS, IDX_DIM), pos)
    k_i = rope(ik, pos)
    out_a = dsa_attention(q_a, k_a, dv, q_i, k_i, iw).reshape(b, s, DSA_WIDTH)

    q_b = (rms_norm(cq, mla_cq_g) @ mla_w_uq).reshape(b, s, MLA_HEADS, MLA_QK)
    kv_b = (rms_norm(ckv, mla_ckv_g) @ mla_w_ukv).reshape(b, s, MLA_HEADS, MLA_NOPE + MLA_V)
    k_nope, v_b = kv_b[..., :MLA_NOPE], kv_b[..., MLA_NOPE:]
    k_pe = jnp.broadcast_to(kpe[:, :, None, :], (b, s, MLA_HEADS, MLA_ROPE))
    k_b = jnp.concatenate([k_nope, k_pe], axis=-1)
    q_b = rms_norm(q_b, mla_q_g)
    k_b = rms_norm(k_b, mla_k_g)
    q_b = jnp.concatenate([q_b[..., :MLA_NOPE], rope(q_b[..., MLA_NOPE:], pos)], axis=-1)
    k_b = jnp.concatenate([k_b[..., :MLA_NOPE], rope(k_b[..., MLA_NOPE:], pos)], axis=-1)
    out_b = causal_block_attention(q_b, k_b, v_b, MLA_QK ** -0.5).reshape(b, s, MLA_WIDTH)

    merged = jax.nn.sigmoid(gd) * (out_a @ w_br_dsa) + jax.nn.sigmoid(gm) * (out_b @ w_br_mla)
    return merged @ w_out


def memory_cross_attention(h, mem, mem_g, w_q, w_kv, q_g, k_g, w_o):
    b, s, _ = h.shape
    m = mem.shape[1]
    q = rms_norm((h @ w_q).reshape(b, s, MEM_HEADS, MEM_HEAD_DIM), q_g)
    kv = (rms_norm(mem, mem_g) @ w_kv).reshape(b, m, 2, MEM_HEADS, MEM_HEAD_DIM)
    k = rms_norm(kv[:, :, 0], k_g)
    v = kv[:, :, 1]
    logits = jnp.einsum('bshd,bmhd->bhsm', q, k).astype(jnp.float32) * MEM_HEAD_DIM ** -0.5
    p = jax.nn.softmax(logits, axis=-1).astype(v.dtype)
    out = jnp.einsum('bhsm,bmhd->bshd', p, v).reshape(b, s, MEM_HEADS * MEM_HEAD_DIM)
    return out @ w_o


def hierarchical_moe(h, w_group, w_expert, expert_bias, w_gate, w_up, w_down):
    b, s, _ = h.shape
    p_group = jax.nn.softmax((h @ w_group).astype(jnp.float32), axis=-1)
    g_sel = jnp.argmax(p_group, axis=-1)
    p_g = jnp.take_along_axis(p_group, g_sel[..., None], axis=-1)
    aff = jax.nn.sigmoid((h @ w_expert).astype(jnp.float32)).reshape(b, s, N_GROUPS, EXPERTS_PER_GROUP)
    aff_grp = jnp.take_along_axis(aff, g_sel[..., None, None], axis=2)[..., 0, :]
    bias_grp = expert_bias.astype(jnp.float32).reshape(N_GROUPS, EXPERTS_PER_GROUP)[g_sel]
    _, local = lax.top_k(aff_grp + bias_grp, TOPK_IN_GROUP)
    a_sel = jnp.take_along_axis(aff_grp, local, axis=-1)
    w_sel = p_g * a_sel / jnp.sum(a_sel, axis=-1, keepdims=True)
    expert_id = g_sel[..., None] * EXPERTS_PER_GROUP + local
    combine = jnp.sum(jax.nn.one_hot(expert_id, N_EXPERTS, dtype=jnp.float32) * w_sel[..., None], axis=-2)
    y = jnp.zeros(h.shape, jnp.float32)
    for e in range(N_EXPERTS):
        act = jax.nn.silu(h @ w_gate[e]) * (h @ w_up[e])
        y = y + combine[..., e:e + 1] * (act @ w_down[e]).astype(jnp.float32)
    return y.astype(h.dtype)


def setup_inputs(seed: int = 0) -> dict:
    key = jax.random.key(seed)
    ks = iter(jax.random.split(key, 40))

    def w(shape, fan_in):
        return jax.random.normal(next(ks), shape, jnp.float32) * fan_in ** -0.5

    def gain(shape):
        return 1.0 + 0.02 * jax.random.normal(next(ks), shape, jnp.float32)

    L = DEPTH
    x = jax.random.normal(next(ks), (BATCH, SEQ, D_MODEL), jnp.float32)
    mem = jax.random.normal(next(ks), (BATCH, MEM_LEN, D_MODEL), jnp.float32)
    offsets = jax.random.randint(next(ks), (BATCH, 1), 0, 128, dtype=jnp.int32)
    positions = offsets + jnp.arange(SEQ, dtype=jnp.int32)[None, :]
    return {
        'x': x,
        'mem': mem,
        'positions': positions,
        'attn_norm_g': gain((L, D_MODEL)),
        'w_in': w((L, D_MODEL, N_IN), D_MODEL),
        'dsa_q_norm_g': gain((L, DSA_HEAD_DIM)),
        'dsa_k_norm_g': gain((L, DSA_HEAD_DIM)),
        'mla_cq_norm_g': gain((L, MLA_Q_RANK)),
        'mla_ckv_norm_g': gain((L, MLA_KV_RANK)),
        'mla_w_uq': w((L, MLA_Q_RANK, MLA_HEADS * MLA_QK), MLA_Q_RANK),
        'mla_w_ukv': w((L, MLA_KV_RANK, MLA_HEADS * (MLA_NOPE + MLA_V)), MLA_KV_RANK),
        'mla_q_norm_g': gain((L, MLA_QK)),
        'mla_k_norm_g': gain((L, MLA_QK)),
        'w_branch_dsa': w((L, DSA_WIDTH, D_MODEL), DSA_WIDTH),
        'w_branch_mla': w((L, MLA_WIDTH, D_MODEL), MLA_WIDTH),
        'w_out': w((L, D_MODEL, D_MODEL), D_MODEL),
        'mem_x_norm_g': gain((L, D_MODEL)),
        'mem_norm_g': gain((L, D_MODEL)),
        'mem_w_q': w((L, D_MODEL, MEM_HEADS * MEM_HEAD_DIM), D_MODEL),
        'mem_w_kv': w((L, D_MODEL, 2 * MEM_HEADS * MEM_HEAD_DIM), D_MODEL),
        'mem_q_norm_g': gain((L, MEM_HEAD_DIM)),
        'mem_k_norm_g': gain((L, MEM_HEAD_DIM)),
        'mem_w_o': w((L, MEM_HEADS * MEM_HEAD_DIM, D_MODEL), MEM_HEADS * MEM_HEAD_DIM),
        'moe_norm_g': gain((L, D_MODEL)),
        'moe_w_group': w((L, D_MODEL, N_GROUPS), D_MODEL),
        'moe_w_expert': w((L, D_MODEL, N_EXPERTS), D_MODEL),
        'moe_expert_bias': 0.01 * jax.random.normal(next(ks), (L, N_EXPERTS), jnp.float32),
        'moe_w_gate': w((L, N_EXPERTS, D_MODEL, D_EXPERT), D_MODEL),
        'moe_w_up': w((L, N_EXPERTS, D_MODEL, D_EXPERT), D_MODEL),
        'moe_w_down': w((L, N_EXPERTS, D_EXPERT, D_MODEL), D_EXPERT),
    }


def reference(x, mem, positions, attn_norm_g, w_in, dsa_q_norm_g, dsa_k_norm_g, mla_cq_norm_g,
              mla_ckv_norm_g, mla_w_uq, mla_w_ukv, mla_q_norm_g, mla_k_norm_g, w_branch_dsa,
              w_branch_mla, w_out, mem_x_norm_g, mem_norm_g, mem_w_q, mem_w_kv, mem_q_norm_g,
              mem_k_norm_g, mem_w_o, moe_norm_g, moe_w_group, moe_w_expert, moe_expert_bias,
              moe_w_gate, moe_w_up, moe_w_down):
    for l in range(DEPTH):
        h = rms_norm(x, attn_norm_g[l])
        x = x + parallel_mixer(h, positions, w_in[l], dsa_q_norm_g[l], dsa_k_norm_g[l],
                               mla_cq_norm_g[l], mla_ckv_norm_g[l], mla_w_uq[l], mla_w_ukv[l],
                               mla_q_norm_g[l], mla_k_norm_g[l], w_branch_dsa[l], w_branch_mla[l],
                               w_out[l])
        h = rms_norm(x, mem_x_norm_g[l])
        x = x + memory_cross_attention(h, mem, mem_norm_g[l], mem_w_q[l], mem_w_kv[l],
                                       mem_q_norm_g[l], mem_k_norm_g[l], mem_w_o[l])
        h = rms_norm(x, moe_norm_g[l])
        x = x + hierarchical_moe(h, moe_w_group[l], moe_w_expert[l], moe_expert_bias[l],
                                 moe_w_gate[l], moe_w_up[l], moe_w_down[l])
    return x
```

```python
import functools

import jax
import jax.numpy as jnp
from jax import lax
from jax.experimental import pallas as pl
from jax.experimental.pallas import tpu as pltpu

F32 = jnp.float32
BF16 = jnp.bfloat16

LANES = 128
VMEM_LIMIT = 56 * 1024 * 1024

ROPE_THETA = 10000.0
EPS = 1e-6
DSA_HEADS = 8
DSA_HEAD_DIM = 64
IDX_HEADS = 8
IDX_DIM = 64
TOPK_MAX = 256
MLA_HEADS = 8
MLA_NOPE = 64
MLA_ROPE = 32
MLA_QK = MLA_NOPE + MLA_ROPE
MLA_V = 64
MLA_Q_RANK = 256
MLA_KV_RANK = 128
MEM_HEADS = 4
MEM_HEAD_DIM = 64
N_GROUPS = 4
EXPERTS_PER_GROUP = 4
N_EXPERTS = N_GROUPS * EXPERTS_PER_GROUP
D_EXPERT = 256

NEG_BIG = -1e30
KEY_CHUNK = 256
Q_TILE_DSA = 128
Q_TILE_MLA = 256
BISECT_ITERS = 16


def _const_spec(shape):
    nd = len(shape)
    return pl.BlockSpec(shape, lambda *_: (0,) * nd)


def _params(sem):
    return pltpu.CompilerParams(dimension_semantics=sem, vmem_limit_bytes=VMEM_LIMIT)


def _rms(x, g, n):
    ms = jnp.sum(x * x, axis=-1, keepdims=True) * (1.0 / n)
    return (x * lax.rsqrt(ms + EPS)) * g


def _rope(x, cos, sin_signed, lo_mask, half):
    fwd = pltpu.roll(x, LANES - half, 1)
    bwd = pltpu.roll(x, half, 1)
    return x * cos + jnp.where(lo_mask, fwd, bwd) * sin_signed


def _in_proj_kernel(x_ref, pos_ref, g_ref, wqa_ref, wqi_ref, wsm_ref, wcq_ref, wg_ref, wuq_ref, wuk_ref,
                    wuv_ref, qag_ref, kag_ref, cqg_ref, ckvg_ref, mqg_ref, mkg_ref, fd_ref, fm_ref,
                    qa_o, qi_o, ka_o, va_o, ki_o, iw_o, mq_o, mk_o, mv_o, sgd_o, sgm_o):
    x = x_ref[...]
    d_model = x.shape[-1]
    h = _rms(x, g_ref[...], d_model).astype(BF16)

    pos = pos_ref[...].astype(F32)
    lane = lax.broadcasted_iota(jnp.int32, (x.shape[0], LANES), 1)
    ang_d = pos * fd_ref[...]
    lo_d = lane < DSA_HEAD_DIM // 2
    cos_d = jnp.cos(ang_d)
    sin_d = jnp.sin(ang_d)
    sin_d = jnp.where(lo_d, -sin_d, sin_d)
    ang_m = pos * fm_ref[...]
    lo_m = lane < MLA_NOPE + MLA_ROPE // 2
    cos_m = jnp.cos(ang_m)
    sin_m = jnp.sin(ang_m)
    sin_m = jnp.where(lo_m, -sin_m, sin_m)
    rope_d = functools.partial(_rope, cos=cos_d, sin_signed=sin_d, lo_mask=lo_d, half=DSA_HEAD_DIM // 2)
    rope_m = functools.partial(_rope, cos=cos_m, sin_signed=sin_m, lo_mask=lo_m, half=MLA_ROPE // 2)

    def dot(a, w_ref):
        return jnp.dot(a, w_ref[...], preferred_element_type=F32)

    att_scale = DSA_HEAD_DIM ** -0.5
    idx_scale = IDX_DIM ** -0.5 * IDX_HEADS ** -0.5
    mla_scale = MLA_QK ** -0.5

    qa = dot(h, wqa_ref)
    for hd in range(DSA_HEADS):
        sl = slice(hd * LANES, (hd + 1) * LANES)
        y = rope_d(_rms(qa[:, sl], qag_ref[...], DSA_HEAD_DIM))
        qa_o[:, sl] = (y * att_scale).astype(BF16)

    qi = dot(h, wqi_ref)
    for hd in range(IDX_HEADS):
        sl = slice(hd * LANES, (hd + 1) * LANES)
        qi_o[:, sl] = rope_d(qi[:, sl]).astype(BF16)

    sm = dot(h, wsm_ref)
    ka_o[...] = rope_d(_rms(sm[:, 0:LANES], kag_ref[...], DSA_HEAD_DIM)).astype(BF16)
    va_o[...] = sm[:, LANES:2 * LANES].astype(BF16)
    ki_o[...] = rope_d(sm[:, 2 * LANES:3 * LANES]).astype(BF16)
    iw_o[...] = sm[:, 3 * LANES:4 * LANES] * idx_scale
    ckv = _rms(sm[:, 4 * LANES:5 * LANES], ckvg_ref[...], MLA_KV_RANK).astype(BF16)
    kpe = sm[:, 5 * LANES:6 * LANES]

    kn = dot(ckv, wuk_ref)
    for hd in range(MLA_HEADS):
        sl = slice(hd * LANES, (hd + 1) * LANES)
        y = rope_m(_rms(kn[:, sl] + kpe, mkg_ref[...], MLA_QK))
        mk_o[:, sl] = y.astype(BF16)
    mv_o[...] = dot(ckv, wuv_ref).astype(BF16)

    cq = _rms(dot(h, wcq_ref), cqg_ref[...], MLA_Q_RANK).astype(BF16)
    qb = dot(cq, wuq_ref)
    for hd in range(MLA_HEADS):
        sl = slice(hd * LANES, (hd + 1) * LANES)
        y = rope_m(_rms(qb[:, sl], mqg_ref[...], MLA_QK))
        mq_o[:, sl] = (y * mla_scale).astype(BF16)

    gates = jax.nn.sigmoid(dot(h, wg_ref))
    sgd_o[...] = gates[:, :d_model].astype(BF16)
    sgm_o[...] = gates[:, d_model:].astype(BF16)


def _pad_heads(w, heads, dim):
    k = w.shape[0]
    w = w.reshape(k, heads, dim)
    w = jnp.pad(w, ((0, 0), (0, 0), (0, LANES - dim)))
    return w.reshape(k, heads * LANES)


def _pad_cols(w, width=LANES, offset=0):
    return jnp.pad(w, ((0, 0), (offset, width - offset - w.shape[1])))


def _in_proj(x2, pos2, attn_g, w_in, dsa_q_g, dsa_k_g, cq_g, ckv_g, w_uq, w_ukv, mq_g, mk_g, tm):
    n, d = x2.shape
    sizes = (DSA_HEADS * DSA_HEAD_DIM, DSA_HEAD_DIM, DSA_HEAD_DIM, IDX_HEADS * IDX_DIM, IDX_DIM, IDX_HEADS,
             MLA_Q_RANK, MLA_KV_RANK, MLA_ROPE, d, d)
    offs = [0]
    for s in sizes:
        offs.append(offs[-1] + s)
    seg = [w_in[:, offs[i]:offs[i + 1]] for i in range(len(sizes))]
    w_dq, w_dk, w_dv, w_iq, w_ik, w_iw, w_cq, w_ckv, w_kpe, w_gd, w_gm = seg

    wqa = _pad_heads(w_dq, DSA_HEADS, DSA_HEAD_DIM).astype(BF16)
    wqi = _pad_heads(w_iq, IDX_HEADS, IDX_DIM).astype(BF16)
    wsm = jnp.concatenate([_pad_cols(w_dk), _pad_cols(w_dv), _pad_cols(w_ik), _pad_cols(w_iw), w_ckv,
                           _pad_cols(w_kpe, offset=MLA_NOPE)], axis=1).astype(BF16)
    wcq = w_cq.astype(BF16)
    wg = jnp.concatenate([w_gd, w_gm], axis=1).astype(BF16)
    wuq = _pad_heads(w_uq, MLA_HEADS, MLA_QK).astype(BF16)
    ukv = w_ukv.reshape(MLA_KV_RANK, MLA_HEADS, MLA_NOPE + MLA_V)
    wuk = _pad_heads(ukv[:, :, :MLA_NOPE].reshape(MLA_KV_RANK, -1), MLA_HEADS, MLA_NOPE).astype(BF16)
    wuv = _pad_heads(ukv[:, :, MLA_NOPE:].reshape(MLA_KV_RANK, -1), MLA_HEADS, MLA_V).astype(BF16)

    row = lambda v: _pad_cols(v.reshape(1, -1).astype(F32), width=max(LANES, v.size))
    half_d = DSA_HEAD_DIM // 2
    inv_d = ROPE_THETA ** (-jnp.arange(half_d, dtype=F32) / half_d)
    fd = _pad_cols(jnp.concatenate([inv_d, inv_d]).reshape(1, -1))
    half_m = MLA_ROPE // 2
    inv_m = ROPE_THETA ** (-jnp.arange(half_m, dtype=F32) / half_m)
    fm = _pad_cols(jnp.concatenate([inv_m, inv_m]).reshape(1, -1), offset=MLA_NOPE)

    consts = [attn_g.reshape(1, d), wqa, wqi, wsm, wcq, wg, wuq, wuk, wuv, row(dsa_q_g), row(dsa_k_g),
              row(cq_g), row(ckv_g), row(mq_g), row(mk_g), fd, fm]
    wide = DSA_HEADS * LANES
    out_shapes = [
        jax.ShapeDtypeStruct((n, wide), BF16),
        jax.ShapeDtypeStruct((n, wide), BF16),
        jax.ShapeDtypeStruct((n, LANES), BF16),
        jax.ShapeDtypeStruct((n, LANES), BF16),
        jax.ShapeDtypeStruct((n, LANES), BF16),
        jax.ShapeDtypeStruct((n, LANES), F32),
        jax.ShapeDtypeStruct((n, wide), BF16),
        jax.ShapeDtypeStruct((n, wide), BF16),
        jax.ShapeDtypeStruct((n, wide), BF16),
        jax.ShapeDtypeStruct((n, d), BF16),
        jax.ShapeDtypeStruct((n, d), BF16),
    ]
    tile = lambda w: pl.BlockSpec((tm, w), lambda i: (i, 0))
    return pl.pallas_call(
        _in_proj_kernel,
        grid=(n // tm,),
        in_specs=[tile(d), tile(1)] + [_const_spec(c.shape) for c in consts],
        out_specs=[tile(s.shape[1]) for s in out_shapes],
        out_shape=out_shapes,
        compiler_params=_params(("parallel",)),
        name="in_proj",
    )(x2, pos2, *consts)


def _dsa_kernel(qa_ref, qi_ref, iw_ref, ka_ref, va_ref, ki_ref, o_ref, score_ref, *, topk):
    i = pl.program_id(1)
    tq = qa_ref.shape[1]
    heads = DSA_HEADS
    nch = (i * tq + tq + KEY_CHUNK - 1) // KEY_CHUNK
    kf = float(topk)

    qpos = i * tq + lax.broadcasted_iota(jnp.int32, (tq, KEY_CHUNK), 0)
    kidx = lax.broadcasted_iota(jnp.int32, (tq, KEY_CHUNK), 1)

    qi = jnp.concatenate([qi_ref[0, :, hd * LANES:(hd + 1) * LANES] for hd in range(IDX_HEADS)], axis=0)
    iw = iw_ref[0]
    wcols = [iw[:, hd:hd + 1] for hd in range(IDX_HEADS)]

    def score_body(c, carry):
        mx, mn = carry
        k0 = pl.multiple_of(c * KEY_CHUNK, KEY_CHUNK)
        kc = ki_ref[0, pl.ds(k0, KEY_CHUNK), :]
        rel = lax.dot_general(qi, kc, (((1,), (1,)), ((), ())), preferred_element_type=F32)
        rel = jnp.maximum(rel, 0.0)
        sc = rel[0:tq] * wcols[0]
        for hd in range(1, IDX_HEADS):
            sc = sc + rel[hd * tq:(hd + 1) * tq] * wcols[hd]
        causal = (kidx + k0) <= qpos
        score_ref[c] = jnp.where(causal, sc, -jnp.inf)
        mx = jnp.maximum(mx, jnp.where(causal, sc, -jnp.inf))
        mn = jnp.minimum(mn, jnp.where(causal, sc, jnp.inf))
        return mx, mn

    mx, mn = lax.fori_loop(0, nch, score_body,
                           (jnp.full((tq, KEY_CHUNK), -jnp.inf, F32), jnp.full((tq, KEY_CHUNK), jnp.inf, F32)))
    hi = jnp.max(mx, axis=-1, keepdims=True)
    lo = jnp.min(mn, axis=-1, keepdims=True)

    def count_gt(t):
        def body(c, acc):
            return acc + jnp.where(score_ref[c] > t, 1.0, 0.0)
        acc = lax.fori_loop(0, nch, body, jnp.zeros((tq, KEY_CHUNK), F32))
        return jnp.sum(acc, axis=-1, keepdims=True)

    def bisect(_, carry):
        lo, hi = carry
        mid = 0.5 * (lo + hi)
        below = count_gt(mid) < kf
        return jnp.where(below, lo, mid), jnp.where(below, mid, hi)

    lo, hi = lax.fori_loop(0, BISECT_ITERS, bisect, (lo, hi))

    def max_le(t):
        def body(c, acc):
            s = score_ref[c]
            return jnp.maximum(acc, jnp.where(s <= t, s, -jnp.inf))
        acc = lax.fori_loop(0, nch, body, jnp.full((tq, KEY_CHUNK), -jnp.inf, F32))
        return jnp.max(acc, axis=-1, keepdims=True)

    n_causal = (i * tq + lax.broadcasted_iota(jnp.int32, (tq, 1), 0) + 1).astype(F32)
    small = n_causal <= kf
    m0 = max_le(hi)

    def refine_cond(carry):
        _, _, done = carry
        return jnp.min(done) < 0.5

    def refine_body(carry):
        m, thr, done = carry

        def body(c, acc):
            cnt, nxt = acc
            s = score_ref[c]
            cnt = cnt + jnp.where(s >= m, 1.0, 0.0)
            nxt = jnp.maximum(nxt, jnp.where(s < m, s, -jnp.inf))
            return cnt, nxt

        cnt, nxt = lax.fori_loop(0, nch, body, (jnp.zeros((tq, KEY_CHUNK), F32),
                                                jnp.full((tq, KEY_CHUNK), -jnp.inf, F32)))
        cnt = jnp.sum(cnt, axis=-1, keepdims=True)
        nxt = jnp.max(nxt, axis=-1, keepdims=True)
        hit = jnp.logical_and(cnt >= kf, done < 0.5)
        thr = jnp.where(hit, m, thr)
        done = jnp.where(cnt >= kf, 1.0, done)
        return nxt, thr, done

    neg_inf = jnp.full((tq, 1), -jnp.inf, F32)
    _, thr, _ = lax.while_loop(refine_cond, refine_body, (m0, neg_inf, jnp.where(small, 1.0, 0.0)))
    quota = jnp.where(small, 0.0, kf - count_gt(thr))

    qa = jnp.concatenate([qa_ref[0, :, hd * LANES:(hd + 1) * LANES] for hd in range(heads)], axis=0)
    tri = (lax.broadcasted_iota(jnp.int32, (KEY_CHUNK, KEY_CHUNK), 0)
           < lax.broadcasted_iota(jnp.int32, (KEY_CHUNK, KEY_CHUNK), 1)).astype(BF16)

    def attn_body(c, carry):
        m_run, l_run, acc, seen = carry
        k0 = pl.multiple_of(c * KEY_CHUNK, KEY_CHUNK)
        s = score_ref[c]
        eq = s == thr
        before = jnp.dot(jnp.where(eq, 1.0, 0.0).astype(BF16), tri, preferred_element_type=F32) + seen
        sel = jnp.logical_or(s > thr, jnp.logical_and(eq, before < quota))
        seen = seen + jnp.sum(jnp.where(eq, 1.0, 0.0), axis=-1, keepdims=True)
        bias = jnp.where(sel, 0.0, NEG_BIG)
        kc = ka_ref[0, pl.ds(k0, KEY_CHUNK), :]
        vc = va_ref[0, pl.ds(k0, KEY_CHUNK), :]
        logit = lax.dot_general(qa, kc, (((1,), (1,)), ((), ())), preferred_element_type=F32)
        logit = logit.reshape(heads, tq, KEY_CHUNK) + bias[None]
        m_new = jnp.maximum(m_run, jnp.max(logit, axis=-1, keepdims=True))
        p = jnp.where(sel[None], jnp.exp(logit - m_new), 0.0)
        alpha = jnp.exp(m_run - m_new)
        l_run = alpha * l_run + jnp.sum(p, axis=-1, keepdims=True)
        pv = jnp.dot(p.reshape(heads * tq, KEY_CHUNK).astype(BF16), vc, preferred_element_type=F32)
        acc = alpha * acc + pv.reshape(heads, tq, LANES)
        return m_new, l_run, acc, seen

    init = (jnp.full((heads, tq, 1), NEG_BIG, F32), jnp.zeros((heads, tq, 1), F32),
            jnp.zeros((heads, tq, LANES), F32), jnp.zeros((tq, 1), F32))
    _, l_run, acc, _ = lax.fori_loop(0, nch, attn_body, init)
    o_ref[0] = (acc / l_run).astype(o_ref.dtype)


def _dsa_attention(qa, qi, iw, ka, va, ki, topk):
    b, s, wide = qa.shape
    tq = Q_TILE_DSA
    qspec = lambda w: pl.BlockSpec((1, tq, w), lambda bi, i: (bi, i, 0))
    kspec = pl.BlockSpec((1, s, LANES), lambda bi, i: (bi, 0, 0))
    return pl.pallas_call(
        functools.partial(_dsa_kernel, topk=topk),
        grid=(b, s // tq),
        in_specs=[qspec(wide), qspec(wide), qspec(LANES), kspec, kspec, kspec],
        out_specs=pl.BlockSpec((1, DSA_HEADS, tq, LANES), lambda bi, i: (bi, 0, i, 0)),
        out_shape=jax.ShapeDtypeStruct((b, DSA_HEADS, s, LANES), BF16),
        scratch_shapes=[pltpu.VMEM((s // KEY_CHUNK, tq, KEY_CHUNK), F32)],
        compiler_params=_params(("parallel", "arbitrary")),
        name="dsa_attn",
    )(qa, qi, iw, ka, va, ki)


def _mla_kernel(q_ref, k_ref, v_ref, o_ref):
    i = pl.program_id(2)
    tq = q_ref.shape[1]
    q = q_ref[0]

    def step(k0, carry, masked):
        m_run, l_run, acc = carry
        kc = k_ref[0, pl.ds(k0, tq), :]
        vc = v_ref[0, pl.ds(k0, tq), :]
        logit = lax.dot_general(q, kc, (((1,), (1,)), ((), ())), preferred_element_type=F32)
        if masked:
            keep = (lax.broadcasted_iota(jnp.int32, (tq, tq), 1) <= lax.broadcasted_iota(jnp.int32, (tq, tq), 0))
            logit = jnp.where(keep, logit, NEG_BIG)
        m_new = jnp.maximum(m_run, jnp.max(logit, axis=-1, keepdims=True))
        p = jnp.exp(logit - m_new)
        if masked:
            p = jnp.where(keep, p, 0.0)
        alpha = jnp.exp(m_run - m_new)
        l_run = alpha * l_run + jnp.sum(p, axis=-1, keepdims=True)
        acc = alpha * acc + jnp.dot(p.astype(BF16), vc, preferred_element_type=F32)
        return m_new, l_run, acc

    init = (jnp.full((tq, 1), NEG_BIG, F32), jnp.zeros((tq, 1), F32), jnp.zeros((tq, LANES), F32))
    carry = lax.fori_loop(0, i, lambda c, cr: step(pl.multiple_of(c * tq, tq), cr, False), init)
    _, l_run, acc = step(pl.multiple_of(i * tq, tq), carry, True)
    o_ref[0] = (acc / l_run).astype(o_ref.dtype)


def _mla_attention(mq, mk, mv):
    b, s, wide = mq.shape
    tq = Q_TILE_MLA
    return pl.pallas_call(
        _mla_kernel,
        grid=(b, MLA_HEADS, s // tq),
        in_specs=[pl.BlockSpec((1, tq, LANES), lambda bi, h, i: (bi, i, h)),
                  pl.BlockSpec((1, s, LANES), lambda bi, h, i: (bi, 0, h)),
                  pl.BlockSpec((1, s, LANES), lambda bi, h, i: (bi, 0, h))],
        out_specs=pl.BlockSpec((1, tq, LANES), lambda bi, h, i: (bi, i, h)),
        out_shape=jax.ShapeDtypeStruct((b, s, wide), BF16),
        compiler_params=_params(("parallel", "parallel", "arbitrary")),
        name="mla_attn",
    )(mq, mk, mv)


def _mem_kv_kernel(mem_ref, g_ref, wkv_ref, kg_ref, k_o, v_o):
    m = mem_ref[0]
    hm = _rms(m, g_ref[...], m.shape[-1]).astype(BF16)
    kv = jnp.dot(hm, wkv_ref[...], preferred_element_type=F32)
    for hd in range(MEM_HEADS):
        sl = slice(hd * LANES, (hd + 1) * LANES)
        k_o[0, :, sl] = _rms(kv[:, sl], kg_ref[...], MEM_HEAD_DIM).astype(BF16)
    v_o[0] = kv[:, MEM_HEADS * LANES:].astype(BF16)


def _mem_kv(mem, mem_g, w_kv, k_g):
    b, m, d = mem.shape
    hw = MEM_HEADS * MEM_HEAD_DIM
    wkv = jnp.concatenate([_pad_heads(w_kv[:, :hw], MEM_HEADS, MEM_HEAD_DIM),
                           _pad_heads(w_kv[:, hw:], MEM_HEADS, MEM_HEAD_DIM)], axis=1).astype(BF16)
    consts = [mem_g.reshape(1, d), wkv, _pad_cols(k_g.reshape(1, -1))]
    wide = MEM_HEADS * LANES
    spec = pl.BlockSpec((1, m, wide), lambda bi: (bi, 0, 0))
    return pl.pallas_call(
        _mem_kv_kernel,
        grid=(b,),
        in_specs=[pl.BlockSpec((1, m, d), lambda bi: (bi, 0, 0))] + [_const_spec(c.shape) for c in consts],
        out_specs=[spec, spec],
        out_shape=[jax.ShapeDtypeStruct((b, m, wide), BF16)] * 2,
        compiler_params=_params(("parallel",)),
        name="mem_kv",
    )(mem, *consts)


def _split_dot(a, w):
    a_hi = a.astype(BF16)
    a_lo = (a - a_hi.astype(F32)).astype(BF16)
    w_hi = w.astype(BF16)
    w_lo = (w - w_hi.astype(F32)).astype(BF16)
    d = functools.partial(jnp.dot, preferred_element_type=F32)
    return d(a_hi, w_hi) + (d(a_lo, w_hi) + d(a_hi, w_lo))


def _merge_kernel(oa_ref, ob_ref, sgd_ref, sgm_ref, x_ref, wa_ref, wb_ref, wo_ref, mxg_ref, wq_ref, qg_ref,
                  km_ref, vm_ref, wmo_ref, moeg_ref, wr_ref, bias_ref, x2_o, h3_o, comb_o):
    d = functools.partial(jnp.dot, preferred_element_type=F32)
    oa = jnp.concatenate([oa_ref[0, hd] for hd in range(DSA_HEADS)], axis=-1)
    merged = (sgd_ref[0].astype(F32) * d(oa, wa_ref[...]) + sgm_ref[0].astype(F32) * d(ob_ref[0], wb_ref[...]))
    x1 = x_ref[0] + d(merged.astype(BF16), wo_ref[...])

    d_model = x1.shape[-1]
    h2 = _rms(x1, mxg_ref[...], d_model).astype(BF16)
    qm = d(h2, wq_ref[...])
    outs = []
    for hd in range(MEM_HEADS):
        sl = slice(hd * LANES, (hd + 1) * LANES)
        q = (_rms(qm[:, sl], qg_ref[...], MEM_HEAD_DIM) * MEM_HEAD_DIM ** -0.5).astype(BF16)
        logit = lax.dot_general(q, km_ref[0, :, sl], (((1,), (1,)), ((), ())), preferred_element_type=F32)
        logit = logit - jnp.max(logit, axis=-1, keepdims=True)
        p = jnp.exp(logit)
        pv = d(p.astype(BF16), vm_ref[0, :, sl])
        outs.append((pv / jnp.sum(p, axis=-1, keepdims=True)).astype(BF16))
    x2 = x1 + d(jnp.concatenate(outs, axis=-1), wmo_ref[...])
    x2_o[0] = x2

    h3 = _rms(x2, moeg_ref[...], d_model)
    h3_o[0] = h3.astype(BF16)
    logits = _split_dot(h3, wr_ref[...])
    tm = logits.shape[0]
    lane = lax.broadcasted_iota(jnp.int32, (tm, LANES), 1)
    big = jnp.int32(LANES)
    is_grp = jnp.logical_and(lane >= N_EXPERTS, lane < N_EXPERTS + N_GROUPS)
    glog = jnp.where(is_grp, logits, -jnp.inf)
    gmax = jnp.max(glog, axis=-1, keepdims=True)
    g_sel = jnp.min(jnp.where(glog == gmax, lane, big), axis=-1, keepdims=True) - N_EXPERTS
    p_g = 1.0 / jnp.sum(jnp.where(is_grp, jnp.exp(logits - gmax), 0.0), axis=-1, keepdims=True)
    aff = jax.nn.sigmoid(logits)
    in_grp = jnp.logical_and(lane >= g_sel * EXPERTS_PER_GROUP, lane < (g_sel + 1) * EXPERTS_PER_GROUP)
    val = jnp.where(in_grp, aff + bias_ref[...], -jnp.inf)
    m1 = jnp.max(val, axis=-1, keepdims=True)
    i1 = jnp.min(jnp.where(val == m1, lane, big), axis=-1, keepdims=True)
    val2 = jnp.where(lane == i1, -jnp.inf, val)
    m2 = jnp.max(val2, axis=-1, keepdims=True)
    i2 = jnp.min(jnp.where(val2 == m2, lane, big), axis=-1, keepdims=True)
    chosen = jnp.logical_or(lane == i1, lane == i2)
    a_sel = jnp.where(chosen, aff, 0.0)
    comb_o[0] = p_g * a_sel / jnp.sum(a_sel, axis=-1, keepdims=True)


def _merge_mem_router(oa, ob, sgd, sgm, x, w_br_dsa, w_br_mla, w_out, mem_x_g, mem_w_q, mem_q_g, km, vm,
                      mem_w_o, moe_g, w_group, w_expert, expert_bias, tm):
    b, s, d = x.shape
    wa = jnp.pad(w_br_dsa.reshape(DSA_HEADS, DSA_HEAD_DIM, d), ((0, 0), (0, LANES - DSA_HEAD_DIM), (0, 0)))
    wa = wa.reshape(DSA_HEADS * LANES, d).astype(BF16)
    wb = jnp.pad(w_br_mla.reshape(MLA_HEADS, MLA_V, d), ((0, 0), (0, LANES - MLA_V), (0, 0)))
    wb = wb.reshape(MLA_HEADS * LANES, d).astype(BF16)
    wq = _pad_heads(mem_w_q, MEM_HEADS, MEM_HEAD_DIM).astype(BF16)
    wmo = jnp.pad(mem_w_o.reshape(MEM_HEADS, MEM_HEAD_DIM, d), ((0, 0), (0, LANES - MEM_HEAD_DIM), (0, 0)))
    wmo = wmo.reshape(MEM_HEADS * LANES, d).astype(BF16)
    wr = _pad_cols(jnp.concatenate([w_expert, w_group], axis=1).astype(F32))
    bias = _pad_cols(expert_bias.reshape(1, -1).astype(F32))
    consts_a = [wa, wb, w_out.astype(BF16), mem_x_g.reshape(1, d), wq, _pad_cols(mem_q_g.reshape(1, -1))]
    consts_b = [wmo, moe_g.reshape(1, d), wr, bias]
    m = km.shape[1]
    mw = MEM_HEADS * LANES
    tok = lambda w: pl.BlockSpec((1, tm, w), lambda bi, i: (bi, i, 0))
    memspec = pl.BlockSpec((1, m, mw), lambda bi, i: (bi, 0, 0))
    return pl.pallas_call(
        _merge_kernel,
        grid=(b, s // tm),
        in_specs=[pl.BlockSpec((1, DSA_HEADS, tm, LANES), lambda bi, i: (bi, 0, i, 0)),
                  tok(MLA_HEADS * LANES), tok(d), tok(d), tok(d)]
                 + [_const_spec(c.shape) for c in consts_a] + [memspec, memspec]
                 + [_const_spec(c.shape) for c in consts_b],
        out_specs=[tok(d), tok(d), tok(LANES)],
        out_shape=[jax.ShapeDtypeStruct((b, s, d), F32), jax.ShapeDtypeStruct((b, s, d), BF16),
                   jax.ShapeDtypeStruct((b, s, LANES), F32)],
        compiler_params=_params(("parallel", "parallel")),
        name="merge_mem",
    )(oa, ob, sgd, sgm, x, *consts_a, km, vm, *consts_b)


def _moe_kernel(h_ref, comb_ref, x_ref, wgu_ref, wd_ref, o_ref, acc_ref):
    e = pl.program_id(1)
    gu = jnp.dot(h_ref[...], wgu_ref[0], preferred_element_type=F32)
    act = jax.nn.silu(gu[:, :D_EXPERT]) * gu[:, D_EXPERT:]
    comb = comb_ref[...]
    lane = lax.broadcasted_iota(jnp.int32, comb.shape, 1)
    c_e = jnp.sum(jnp.where(lane == e, comb, 0.0), axis=-1, keepdims=True)
    contrib = jnp.dot((act * c_e).astype(BF16), wd_ref[0], preferred_element_type=F32)

    @pl.when(e == 0)
    def _():
        acc_ref[...] = contrib

    @pl.when(e > 0)
    def _():
        acc_ref[...] += contrib

    @pl.when(e == N_EXPERTS - 1)
    def _():
        o_ref[...] = x_ref[...] + acc_ref[...]


def _moe(h3, comb, x2, w_gate, w_up, w_down, tm):
    n, d = x2.shape
    wgu = jnp.concatenate([w_gate, w_up], axis=-1).astype(BF16)
    wd = w_down.astype(BF16)
    return pl.pallas_call(
        _moe_kernel,
        grid=(n // tm, N_EXPERTS),
        in_specs=[pl.BlockSpec((tm, d), lambda i, e: (i, 0)),
                  pl.BlockSpec((tm, LANES), lambda i, e: (i, 0)),
                  pl.BlockSpec((tm, d), lambda i, e: (i, 0)),
                  pl.BlockSpec((1, d, 2 * D_EXPERT), lambda i, e: (e, 0, 0)),
                  pl.BlockSpec((1, D_EXPERT, d), lambda i, e: (e, 0, 0))],
        out_specs=pl.BlockSpec((tm, d), lambda i, e: (i, 0)),
        out_shape=jax.ShapeDtypeStruct((n, d), F32),
        scratch_shapes=[pltpu.VMEM((tm, d), F32)],
        compiler_params=_params(("parallel", "arbitrary")),
        name="moe",
    )(h3, comb, x2, wgu, wd)


def _layer(x, mem, positions, p):
    b, s, d = x.shape
    n = b * s
    assert s % KEY_CHUNK == 0 and s % Q_TILE_MLA == 0
    topk = min(TOPK_MAX, s // 4)
    tm = min(256, s)
    qa, qi, ka, va, ki, iw, mq, mk, mv, sgd, sgm = _in_proj(
        x.reshape(n, d), positions.reshape(n, 1), p["attn_norm_g"], p["w_in"], p["dsa_q_norm_g"],
        p["dsa_k_norm_g"], p["mla_cq_norm_g"], p["mla_ckv_norm_g"], p["mla_w_uq"], p["mla_w_ukv"],
        p["mla_q_norm_g"], p["mla_k_norm_g"], tm)
    r3 = lambda a: a.reshape(b, s, a.shape[-1])
    oa = _dsa_attention(r3(qa), r3(qi), r3(iw), r3(ka), r3(va), r3(ki), topk)
    ob = _mla_attention(r3(mq), r3(mk), r3(mv))
    km, vm = _mem_kv(mem, p["mem_norm_g"], p["mem_w_kv"], p["mem_k_norm_g"])
    x2, h3, comb = _merge_mem_router(
        oa, ob, r3(sgd), r3(sgm), x, p["w_branch_dsa"], p["w_branch_mla"], p["w_out"], p["mem_x_norm_g"],
        p["mem_w_q"], p["mem_q_norm_g"], km, vm, p["mem_w_o"], p["moe_norm_g"], p["moe_w_group"],
        p["moe_w_expert"], p["moe_expert_bias"], min(512, s))
    out = _moe(h3.reshape(n, d), comb.reshape(n, LANES), x2.reshape(n, d), p["moe_w_gate"], p["moe_w_up"],
               p["moe_w_down"], min(1024, n))
    return out.reshape(b, s, d)


_PARAM_NAMES = ("attn_norm_g", "w_in", "dsa_q_norm_g", "dsa_k_norm_g", "mla_cq_norm_g", "mla_ckv_norm_g",
                "mla_w_uq", "mla_w_ukv", "mla_q_norm_g", "mla_k_norm_g", "w_branch_dsa", "w_branch_mla", "w_out",
                "mem_x_norm_g", "mem_norm_g", "mem_w_q", "mem_w_kv", "mem_q_norm_g", "mem_k_norm_g", "mem_w_o",
                "moe_norm_g", "moe_w_group", "moe_w_expert", "moe_expert_bias", "moe_w_gate", "moe_w_up",
                "moe_w_down")


def kernel(x, mem, positions, attn_norm_g, w_in, dsa_q_norm_g, dsa_k_norm_g, mla_cq_norm_g, mla_ckv_norm_g, mla_w_uq, mla_w_ukv, mla_q_norm_g, mla_k_norm_g, w_branch_dsa, w_branch_mla, w_out, mem_x_norm_g, mem_norm_g, mem_w_q, mem_w_kv, mem_q_norm_g, mem_k_norm_g, mem_w_o, moe_norm_g, moe_w_group, moe_w_expert, moe_expert_bias, moe_w_gate, moe_w_up, moe_w_down):
    stacked = (attn_norm_g, w_in, dsa_q_norm_g, dsa_k_norm_g, mla_cq_norm_g, mla_ckv_norm_g, mla_w_uq, mla_w_ukv,
               mla_q_norm_g, mla_k_norm_g, w_branch_dsa, w_branch_mla, w_out, mem_x_norm_g, mem_norm_g, mem_w_q,
               mem_w_kv, mem_q_norm_g, mem_k_norm_g, mem_w_o, moe_norm_g, moe_w_group, moe_w_expert,
               moe_expert_bias, moe_w_gate, moe_w_up, moe_w_down)
    for layer in range(attn_norm_g.shape[0]):
        p = {name: arr[layer] for name, arr in zip(_PARAM_NAMES, stacked)}
        x = _layer(x, mem, positions, p)
    return x
```

```python
import functools

import jax
import jax.numpy as jnp
from jax import lax
from jax.experimental import pallas as pl
from jax.experimental.pallas import tpu as pltpu

F32 = jnp.float32
BF16 = jnp.bfloat16

LANES = 128
VMEM_LIMIT = 56 * 1024 * 1024

ROPE_THETA = 10000.0
EPS = 1e-6
DSA_HEADS = 8
DSA_HEAD_DIM = 64
IDX_HEADS = 8
IDX_DIM = 64
TOPK_MAX = 256
MLA_HEADS = 8
MLA_NOPE = 64
MLA_ROPE = 32
MLA_QK = MLA_NOPE + MLA_ROPE
MLA_V = 64
MLA_Q_RANK = 256
MLA_KV_RANK = 128
MEM_HEADS = 4
MEM_HEAD_DIM = 64
N_GROUPS = 4
EXPERTS_PER_GROUP = 4
N_EXPERTS = N_GROUPS * EXPERTS_PER_GROUP
D_EXPERT = 256

NEG_BIG = -1e30
KEY_CHUNK = 256
Q_TILE_DSA = 128
Q_TILE_MLA = 512
MLA_HEAD_GROUP = 4
LOG2_E = 1.4426950408889634
BISECT_ITERS = 16


def _const_spec(shape):
    nd = len(shape)
    return pl.BlockSpec(shape, lambda *_: (0,) * nd)


def _params(sem):
    return pltpu.CompilerParams(dimension_semantics=sem, vmem_limit_bytes=VMEM_LIMIT)


def _rms(x, g, n):
    ms = jnp.sum(x * x, axis=-1, keepdims=True) * (1.0 / n)
    return (x * lax.rsqrt(ms + EPS)) * g


def _rope(x, cos, sin_signed, lo_mask, half):
    fwd = pltpu.roll(x, LANES - half, 1)
    bwd = pltpu.roll(x, half, 1)
    return x * cos + jnp.where(lo_mask, fwd, bwd) * sin_signed


def _in_proj_kernel(x_ref, pos_ref, g_ref, wqa_ref, wqi_ref, wsm_ref, wcq_ref, wg_ref, wuq_ref, wuk_ref,
                    wuv_ref, qag_ref, kag_ref, cqg_ref, ckvg_ref, mqg_ref, mkg_ref, fd_ref, fm_ref,
                    qa_o, qi_o, ka_o, va_o, ki_o, iw_o, mq_o, mk_o, mv_o, sgd_o, sgm_o):
    x = x_ref[...]
    d_model = x.shape[-1]
    h = _rms(x, g_ref[...], d_model).astype(BF16)

    pos = pos_ref[...].astype(F32)
    lane = lax.broadcasted_iota(jnp.int32, (x.shape[0], LANES), 1)
    ang_d = pos * fd_ref[...]
    lo_d = lane < DSA_HEAD_DIM // 2
    cos_d = jnp.cos(ang_d)
    sin_d = jnp.sin(ang_d)
    sin_d = jnp.where(lo_d, -sin_d, sin_d)
    ang_m = pos * fm_ref[...]
    lo_m = lane < MLA_NOPE + MLA_ROPE // 2
    cos_m = jnp.cos(ang_m)
    sin_m = jnp.sin(ang_m)
    sin_m = jnp.where(lo_m, -sin_m, sin_m)
    rope_d = functools.partial(_rope, cos=cos_d, sin_signed=sin_d, lo_mask=lo_d, half=DSA_HEAD_DIM // 2)
    rope_m = functools.partial(_rope, cos=cos_m, sin_signed=sin_m, lo_mask=lo_m, half=MLA_ROPE // 2)

    def dot(a, w_ref):
        return jnp.dot(a, w_ref[...], preferred_element_type=F32)

    att_scale = DSA_HEAD_DIM ** -0.5 * LOG2_E
    idx_scale = IDX_DIM ** -0.5 * IDX_HEADS ** -0.5
    mla_scale = MLA_QK ** -0.5 * LOG2_E

    qa = dot(h, wqa_ref)
    for hd in range(DSA_HEADS):
        sl = slice(hd * LANES, (hd + 1) * LANES)
        y = rope_d(_rms(qa[:, sl], qag_ref[...], DSA_HEAD_DIM))
        qa_o[:, sl] = (y * att_scale).astype(BF16)

    qi = dot(h, wqi_ref)
    for hd in range(IDX_HEADS):
        sl = slice(hd * LANES, (hd + 1) * LANES)
        qi_o[:, sl] = rope_d(qi[:, sl]).astype(BF16)

    sm = dot(h, wsm_ref)
    ka_o[...] = rope_d(_rms(sm[:, 0:LANES], kag_ref[...], DSA_HEAD_DIM)).astype(BF16)
    va_o[...] = (sm[:, LANES:2 * LANES] + jnp.where(lane[0:1] == DSA_HEAD_DIM, 1.0, 0.0)).astype(BF16)
    ki_o[...] = rope_d(sm[:, 2 * LANES:3 * LANES]).astype(BF16)
    iw_o[...] = sm[:, 3 * LANES:4 * LANES] * idx_scale
    ckv = _rms(sm[:, 4 * LANES:5 * LANES], ckvg_ref[...], MLA_KV_RANK).astype(BF16)
    kpe = sm[:, 5 * LANES:6 * LANES]

    kn = dot(ckv, wuk_ref)
    for hd in range(MLA_HEADS):
        sl = slice(hd * LANES, (hd + 1) * LANES)
        y = rope_m(_rms(kn[:, sl] + kpe, mkg_ref[...], MLA_QK))
        mk_o[:, sl] = y.astype(BF16)
    wide_lane = lax.broadcasted_iota(jnp.int32, (1, MLA_HEADS * LANES), 1)
    ones_col = jnp.where(wide_lane % LANES == MLA_V, 1.0, 0.0)
    mv_o[...] = (dot(ckv, wuv_ref) + ones_col).astype(BF16)

    cq = _rms(dot(h, wcq_ref), cqg_ref[...], MLA_Q_RANK).astype(BF16)
    qb = dot(cq, wuq_ref)
    for hd in range(MLA_HEADS):
        sl = slice(hd * LANES, (hd + 1) * LANES)
        y = rope_m(_rms(qb[:, sl], mqg_ref[...], MLA_QK))
        mq_o[:, sl] = (y * mla_scale).astype(BF16)

    gates = jax.nn.sigmoid(dot(h, wg_ref))
    sgd_o[...] = gates[:, :d_model].astype(BF16)
    sgm_o[...] = gates[:, d_model:].astype(BF16)


def _pad_heads(w, heads, dim):
    k = w.shape[0]
    w = w.reshape(k, heads, dim)
    w = jnp.pad(w, ((0, 0), (0, 0), (0, LANES - dim)))
    return w.reshape(k, heads * LANES)


def _pad_cols(w, width=LANES, offset=0):
    return jnp.pad(w, ((0, 0), (offset, width - offset - w.shape[1])))


def _in_proj(x2, pos2, attn_g, w_in, dsa_q_g, dsa_k_g, cq_g, ckv_g, w_uq, w_ukv, mq_g, mk_g, tm):
    n, d = x2.shape
    sizes = (DSA_HEADS * DSA_HEAD_DIM, DSA_HEAD_DIM, DSA_HEAD_DIM, IDX_HEADS * IDX_DIM, IDX_DIM, IDX_HEADS,
             MLA_Q_RANK, MLA_KV_RANK, MLA_ROPE, d, d)
    offs = [0]
    for s in sizes:
        offs.append(offs[-1] + s)
    seg = [w_in[:, offs[i]:offs[i + 1]] for i in range(len(sizes))]
    w_dq, w_dk, w_dv, w_iq, w_ik, w_iw, w_cq, w_ckv, w_kpe, w_gd, w_gm = seg

    wqa = _pad_heads(w_dq, DSA_HEADS, DSA_HEAD_DIM).astype(BF16)
    wqi = _pad_heads(w_iq, IDX_HEADS, IDX_DIM).astype(BF16)
    wsm = jnp.concatenate([_pad_cols(w_dk), _pad_cols(w_dv), _pad_cols(w_ik), _pad_cols(w_iw), w_ckv,
                           _pad_cols(w_kpe, offset=MLA_NOPE)], axis=1).astype(BF16)
    wcq = w_cq.astype(BF16)
    wg = jnp.concatenate([w_gd, w_gm], axis=1).astype(BF16)
    wuq = _pad_heads(w_uq, MLA_HEADS, MLA_QK).astype(BF16)
    ukv = w_ukv.reshape(MLA_KV_RANK, MLA_HEADS, MLA_NOPE + MLA_V)
    wuk = _pad_heads(ukv[:, :, :MLA_NOPE].reshape(MLA_KV_RANK, -1), MLA_HEADS, MLA_NOPE).astype(BF16)
    wuv = _pad_heads(ukv[:, :, MLA_NOPE:].reshape(MLA_KV_RANK, -1), MLA_HEADS, MLA_V).astype(BF16)

    row = lambda v: _pad_cols(v.reshape(1, -1).astype(F32), width=max(LANES, v.size))
    half_d = DSA_HEAD_DIM // 2
    inv_d = ROPE_THETA ** (-jnp.arange(half_d, dtype=F32) / half_d)
    fd = _pad_cols(jnp.concatenate([inv_d, inv_d]).reshape(1, -1))
    half_m = MLA_ROPE // 2
    inv_m = ROPE_THETA ** (-jnp.arange(half_m, dtype=F32) / half_m)
    fm = _pad_cols(jnp.concatenate([inv_m, inv_m]).reshape(1, -1), offset=MLA_NOPE)

    consts = [attn_g.reshape(1, d), wqa, wqi, wsm, wcq, wg, wuq, wuk, wuv, row(dsa_q_g), row(dsa_k_g),
              row(cq_g), row(ckv_g), row(mq_g), row(mk_g), fd, fm]
    wide = DSA_HEADS * LANES
    out_shapes = [
        jax.ShapeDtypeStruct((n, wide), BF16),
        jax.ShapeDtypeStruct((n, wide), BF16),
        jax.ShapeDtypeStruct((n, LANES), BF16),
        jax.ShapeDtypeStruct((n, LANES), BF16),
        jax.ShapeDtypeStruct((n, LANES), BF16),
        jax.ShapeDtypeStruct((n, LANES), F32),
        jax.ShapeDtypeStruct((n, wide), BF16),
        jax.ShapeDtypeStruct((n, wide), BF16),
        jax.ShapeDtypeStruct((n, wide), BF16),
        jax.ShapeDtypeStruct((n, d), BF16),
        jax.ShapeDtypeStruct((n, d), BF16),
    ]
    tile = lambda w: pl.BlockSpec((tm, w), lambda i: (i, 0))
    return pl.pallas_call(
        _in_proj_kernel,
        grid=(n // tm,),
        in_specs=[tile(d), tile(1)] + [_const_spec(c.shape) for c in consts],
        out_specs=[tile(s.shape[1]) for s in out_shapes],
        out_shape=out_shapes,
        compiler_params=_params(("parallel",)),
        name="in_proj",
    )(x2, pos2, *consts)


def _dsa_kernel(qa_ref, qi_ref, iw_ref, ka_ref, vt_ref, ki_ref, o_ref, score_ref, *, topk):
    i = pl.program_id(1)
    tq = qa_ref.shape[1]
    heads = DSA_HEADS
    npair = (i * tq + tq + 2 * KEY_CHUNK - 1) // (2 * KEY_CHUNK)
    kf = float(topk)
    shape = (KEY_CHUNK, tq)

    krow = lax.broadcasted_iota(jnp.int32, shape, 0)
    qcol = i * tq + lax.broadcasted_iota(jnp.int32, shape, 1)
    contract_last = (((1,), (1,)), ((), ()))

    def pair_loop(body, init):
        return lax.fori_loop(0, npair, lambda j, carry: body(2 * j, 2 * j + 1, carry), init)

    qi = jnp.concatenate([qi_ref[0, :, hd * LANES:(hd + 1) * LANES] for hd in range(IDX_HEADS)], axis=0)
    iw = iw_ref[0]

    def score_body(c0, c1, carry):
        mx, mn = carry
        for c in (c0, c1):
            kc = ki_ref[0, pl.ds(pl.multiple_of(c * KEY_CHUNK, KEY_CHUNK), KEY_CHUNK), :]
            sc = None
            for hp in range(IDX_HEADS // 2):
                rel = lax.dot_general(kc, qi[2 * hp * tq:(2 * hp + 2) * tq], contract_last,
                                      preferred_element_type=F32)
                rel = jnp.maximum(rel, 0.0)
                part = rel[:, :tq] * iw[2 * hp:2 * hp + 1, :] + rel[:, tq:] * iw[2 * hp + 1:2 * hp + 2, :]
                sc = part if sc is None else sc + part
            causal = (krow + c * KEY_CHUNK) <= qcol
            masked = jnp.where(causal, sc, -jnp.inf)
            score_ref[c] = masked
            mx = jnp.maximum(mx, masked)
            mn = jnp.minimum(mn, jnp.where(causal, sc, jnp.inf))
        return mx, mn

    mx, mn = pair_loop(score_body, (jnp.full(shape, -jnp.inf, F32), jnp.full(shape, jnp.inf, F32)))
    hi = jnp.max(mx, axis=0, keepdims=True)
    lo = jnp.min(mn, axis=0, keepdims=True)

    def count_gt(t):
        def body(c0, c1, acc):
            return acc + (jnp.where(score_ref[c0] > t, 1.0, 0.0) + jnp.where(score_ref[c1] > t, 1.0, 0.0))
        return jnp.sum(pair_loop(body, jnp.zeros(shape, F32)), axis=0, keepdims=True)

    def bisect(_, carry):
        lo, hi = carry
        mid = 0.5 * (lo + hi)
        below = count_gt(mid) < kf
        return jnp.where(below, lo, mid), jnp.where(below, mid, hi)

    lo, hi = lax.fori_loop(0, BISECT_ITERS, bisect, (lo, hi))

    def max_le(t):
        def body(c0, c1, acc):
            for c in (c0, c1):
                s = score_ref[c]
                acc = jnp.maximum(acc, jnp.where(s <= t, s, -jnp.inf))
            return acc
        return jnp.max(pair_loop(body, jnp.full(shape, -jnp.inf, F32)), axis=0, keepdims=True)

    n_causal = (i * tq + lax.broadcasted_iota(jnp.int32, (1, tq), 1) + 1).astype(F32)
    small = n_causal <= kf

    def refine_cond(carry):
        return jnp.min(carry[2]) < 0.5

    def refine_body(carry):
        m, thr, done = carry

        def body(c0, c1, acc):
            cnt, nxt = acc
            for c in (c0, c1):
                s = score_ref[c]
                cnt = cnt + jnp.where(s >= m, 1.0, 0.0)
                nxt = jnp.maximum(nxt, jnp.where(s < m, s, -jnp.inf))
            return cnt, nxt

        cnt, nxt = pair_loop(body, (jnp.zeros(shape, F32), jnp.full(shape, -jnp.inf, F32)))
        cnt = jnp.sum(cnt, axis=0, keepdims=True)
        nxt = jnp.max(nxt, axis=0, keepdims=True)
        reached = cnt >= kf
        thr = jnp.where(jnp.logical_and(reached, done < 0.5), m, thr)
        return nxt, thr, jnp.where(reached, 1.0, done)

    init = (max_le(hi), jnp.full((1, tq), -jnp.inf, F32), jnp.where(small, 1.0, 0.0))
    _, thr, _ = lax.while_loop(refine_cond, refine_body, init)
    quota = jnp.where(small, 0.0, kf - count_gt(thr))

    qa = jnp.concatenate([qa_ref[0, :, hd * LANES:(hd + 1) * LANES] for hd in range(heads)], axis=0)
    earlier = (lax.broadcasted_iota(jnp.int32, (KEY_CHUNK, KEY_CHUNK), 1)
               < lax.broadcasted_iota(jnp.int32, (KEY_CHUNK, KEY_CHUNK), 0)).astype(BF16)

    nch = (i * tq + tq + KEY_CHUNK - 1) // KEY_CHUNK

    def attn_body(c, carry):
        m_run, acc, seen = carry
        s = score_ref[c]
        eq = s == thr
        eqf = jnp.where(eq, 1.0, 0.0)
        before = jnp.dot(earlier, eqf.astype(BF16), preferred_element_type=F32) + seen
        sel = jnp.logical_or(s > thr, jnp.logical_and(eq, before < quota))
        seen = seen + jnp.sum(eqf, axis=0, keepdims=True)
        bias = jnp.where(sel, 0.0, NEG_BIG)
        kc = ka_ref[0, pl.ds(pl.multiple_of(c * KEY_CHUNK, KEY_CHUNK), KEY_CHUNK), :]
        logit = lax.dot_general(kc, qa, contract_last, preferred_element_type=F32)
        probs, alphas, maxes = [], [], []
        for hd in range(heads):
            sl = slice(hd * tq, (hd + 1) * tq)
            lh = logit[:, sl] + bias
            m_old = m_run[:, sl]
            m_new = jnp.maximum(m_old, jnp.max(lh, axis=0, keepdims=True))
            probs.append(jnp.where(sel, jnp.exp2(lh - m_new), 0.0).astype(BF16))
            alphas.append(jnp.exp2(m_old - m_new))
            maxes.append(m_new)
        pv = jnp.dot(vt_ref[0, c], jnp.concatenate(probs, axis=1), preferred_element_type=F32)
        acc = jnp.concatenate(alphas, axis=1) * acc + pv
        return jnp.concatenate(maxes, axis=1), acc, seen

    init = (jnp.full((1, heads * tq), NEG_BIG, F32), jnp.zeros((LANES, heads * tq), F32), jnp.zeros((1, tq), F32))
    _, acc, _ = lax.fori_loop(0, nch, attn_body, init)
    for hd in range(heads):
        blk = acc[:, hd * tq:(hd + 1) * tq]
        o_ref[0, hd] = (blk / blk[DSA_HEAD_DIM:DSA_HEAD_DIM + 1, :]).T.astype(o_ref.dtype)


def _dsa_attention(qa, qi, iw, ka, va, ki, topk):
    b, s, wide = qa.shape
    tq = Q_TILE_DSA
    nkc = s // KEY_CHUNK
    iw_t = jnp.swapaxes(iw[..., :IDX_HEADS], 1, 2)
    v_t = jnp.swapaxes(va.reshape(b, nkc, KEY_CHUNK, LANES), 2, 3)
    qspec = pl.BlockSpec((1, tq, wide), lambda bi, i: (bi, i, 0))
    kspec = pl.BlockSpec((1, s, LANES), lambda bi, i: (bi, 0, 0))
    return pl.pallas_call(
        functools.partial(_dsa_kernel, topk=topk),
        grid=(b, s // tq),
        in_specs=[qspec, qspec, pl.BlockSpec((1, IDX_HEADS, tq), lambda bi, i: (bi, 0, i)), kspec,
                  pl.BlockSpec((1, nkc, LANES, KEY_CHUNK), lambda bi, i: (bi, 0, 0, 0)), kspec],
        out_specs=pl.BlockSpec((1, DSA_HEADS, tq, LANES), lambda bi, i: (bi, 0, i, 0)),
        out_shape=jax.ShapeDtypeStruct((b, DSA_HEADS, s, LANES), BF16),
        scratch_shapes=[pltpu.VMEM((nkc, KEY_CHUNK, tq), F32)],
        compiler_params=_params(("parallel", "arbitrary")),
        name="dsa_attn",
    )(qa, qi, iw_t, ka, v_t, ki)


def _mla_kernel(q_ref, k_ref, v_ref, o_ref):
    i = pl.program_id(2)
    tq = q_ref.shape[1]
    group = q_ref.shape[2] // LANES
    qs = [q_ref[0, :, g * LANES:(g + 1) * LANES] for g in range(group)]
    keep = lax.broadcasted_iota(jnp.int32, (tq, tq), 1) <= lax.broadcasted_iota(jnp.int32, (tq, tq), 0)

    def step(k0, carry, masked):
        logits = [lax.dot_general(qs[g], k_ref[0, pl.ds(k0, tq), g * LANES:(g + 1) * LANES],
                                  (((1,), (1,)), ((), ())), preferred_element_type=F32) for g in range(group)]
        probs, alphas, out = [], [], []
        for g in range(group):
            m_run = carry[2 * g]
            logit = jnp.where(keep, logits[g], NEG_BIG) if masked else logits[g]
            m_new = jnp.maximum(m_run, jnp.max(logit, axis=-1, keepdims=True))
            p = jnp.exp2(logit - m_new)
            if masked:
                p = jnp.where(keep, p, 0.0)
            probs.append(p.astype(BF16))
            alphas.append(jnp.exp2(m_run - m_new))
            out.append(m_new)
        res = []
        for g in range(group):
            vc = v_ref[0, pl.ds(k0, tq), g * LANES:(g + 1) * LANES]
            acc = alphas[g] * carry[2 * g + 1] + jnp.dot(probs[g], vc, preferred_element_type=F32)
            res += [out[g], acc]
        return tuple(res)

    init = (jnp.full((tq, 1), NEG_BIG, F32), jnp.zeros((tq, LANES), F32)) * group
    carry = lax.fori_loop(0, i, lambda c, cr: step(pl.multiple_of(c * tq, tq), cr, False), init)
    carry = step(pl.multiple_of(i * tq, tq), carry, True)
    for g in range(group):
        acc = carry[2 * g + 1]
        o_ref[0, :, g * LANES:(g + 1) * LANES] = (acc / acc[:, MLA_V:MLA_V + 1]).astype(o_ref.dtype)


def _mla_attention(mq, mk, mv):
    b, s, wide = mq.shape
    tq = Q_TILE_MLA
    gw = MLA_HEAD_GROUP * LANES
    return pl.pallas_call(
        _mla_kernel,
        grid=(b, MLA_HEADS // MLA_HEAD_GROUP, s // tq),
        in_specs=[pl.BlockSpec((1, tq, gw), lambda bi, h, i: (bi, i, h)),
                  pl.BlockSpec((1, s, gw), lambda bi, h, i: (bi, 0, h)),
                  pl.BlockSpec((1, s, gw), lambda bi, h, i: (bi, 0, h))],
        out_specs=pl.BlockSpec((1, tq, gw), lambda bi, h, i: (bi, i, h)),
        out_shape=jax.ShapeDtypeStruct((b, s, wide), BF16),
        compiler_params=_params(("parallel", "parallel", "arbitrary")),
        name="mla_attn",
    )(mq, mk, mv)


def _mem_kv_kernel(mem_ref, g_ref, wkv_ref, kg_ref, k_o, v_o):
    m = mem_ref[0]
    hm = _rms(m, g_ref[...], m.shape[-1]).astype(BF16)
    kv = jnp.dot(hm, wkv_ref[...], preferred_element_type=F32)
    for hd in range(MEM_HEADS):
        sl = slice(hd * LANES, (hd + 1) * LANES)
        k_o[0, :, sl] = _rms(kv[:, sl], kg_ref[...], MEM_HEAD_DIM).astype(BF16)
    v_o[0] = kv[:, MEM_HEADS * LANES:].astype(BF16)


def _mem_kv(mem, mem_g, w_kv, k_g):
    b, m, d = mem.shape
    hw = MEM_HEADS * MEM_HEAD_DIM
    wkv = jnp.concatenate([_pad_heads(w_kv[:, :hw], MEM_HEADS, MEM_HEAD_DIM),
                           _pad_heads(w_kv[:, hw:], MEM_HEADS, MEM_HEAD_DIM)], axis=1).astype(BF16)
    consts = [mem_g.reshape(1, d), wkv, _pad_cols(k_g.reshape(1, -1))]
    wide = MEM_HEADS * LANES
    spec = pl.BlockSpec((1, m, wide), lambda bi: (bi, 0, 0))
    return pl.pallas_call(
        _mem_kv_kernel,
        grid=(b,),
        in_specs=[pl.BlockSpec((1, m, d), lambda bi: (bi, 0, 0))] + [_const_spec(c.shape) for c in consts],
        out_specs=[spec, spec],
        out_shape=[jax.ShapeDtypeStruct((b, m, wide), BF16)] * 2,
        compiler_params=_params(("parallel",)),
        name="mem_kv",
    )(mem, *consts)


def _split_dot(a, w):
    a_hi = a.astype(BF16)
    a_lo = (a - a_hi.astype(F32)).astype(BF16)
    w_hi = w.astype(BF16)
    w_lo = (w - w_hi.astype(F32)).astype(BF16)
    d = functools.partial(jnp.dot, preferred_element_type=F32)
    return d(a_hi, w_hi) + (d(a_lo, w_hi) + d(a_hi, w_lo))


def _merge_kernel(oa_ref, ob_ref, sgd_ref, sgm_ref, x_ref, wa_ref, wb_ref, wo_ref, mxg_ref, wq_ref, qg_ref,
                  km_ref, vm_ref, wmo_ref, moeg_ref, wr_ref, bias_ref, x2_o, h3_o, comb_o):
    d = functools.partial(jnp.dot, preferred_element_type=F32)
    oa = jnp.concatenate([oa_ref[0, hd] for hd in range(DSA_HEADS)], axis=-1)
    merged = (sgd_ref[0].astype(F32) * d(oa, wa_ref[...]) + sgm_ref[0].astype(F32) * d(ob_ref[0], wb_ref[...]))
    x1 = x_ref[0] + d(merged.astype(BF16), wo_ref[...])

    d_model = x1.shape[-1]
    h2 = _rms(x1, mxg_ref[...], d_model).astype(BF16)
    qm = d(h2, wq_ref[...])
    outs = []
    for hd in range(MEM_HEADS):
        sl = slice(hd * LANES, (hd + 1) * LANES)
        q = (_rms(qm[:, sl], qg_ref[...], MEM_HEAD_DIM) * MEM_HEAD_DIM ** -0.5).astype(BF16)
        logit = lax.dot_general(q, km_ref[0, :, sl], (((1,), (1,)), ((), ())), preferred_element_type=F32)
        logit = logit - jnp.max(logit, axis=-1, keepdims=True)
        p = jnp.exp(logit)
        pv = d(p.astype(BF16), vm_ref[0, :, sl])
        outs.append((pv / jnp.sum(p, axis=-1, keepdims=True)).astype(BF16))
    x2 = x1 + d(jnp.concatenate(outs, axis=-1), wmo_ref[...])
    x2_o[0] = x2

    h3 = _rms(x2, moeg_ref[...], d_model)
    h3_o[0] = h3.astype(BF16)
    logits = _split_dot(h3, wr_ref[...])
    tm = logits.shape[0]
    lane = lax.broadcasted_iota(jnp.int32, (tm, LANES), 1)
    big = jnp.int32(LANES)
    is_grp = jnp.logical_and(lane >= N_EXPERTS, lane < N_EXPERTS + N_GROUPS)
    glog = jnp.where(is_grp, logits, -jnp.inf)
    gmax = jnp.max(glog, axis=-1, keepdims=True)
    g_sel = jnp.min(jnp.where(glog == gmax, lane, big), axis=-1, keepdims=True) - N_EXPERTS
    p_g = 1.0 / jnp.sum(jnp.where(is_grp, jnp.exp(logits - gmax), 0.0), axis=-1, keepdims=True)
    aff = jax.nn.sigmoid(logits)
    in_grp = jnp.logical_and(lane >= g_sel * EXPERTS_PER_GROUP, lane < (g_sel + 1) * EXPERTS_PER_GROUP)
    val = jnp.where(in_grp, aff + bias_ref[...], -jnp.inf)
    m1 = jnp.max(val, axis=-1, keepdims=True)
    i1 = jnp.min(jnp.where(val == m1, lane, big), axis=-1, keepdims=True)
    val2 = jnp.where(lane == i1, -jnp.inf, val)
    m2 = jnp.max(val2, axis=-1, keepdims=True)
    i2 = jnp.min(jnp.where(val2 == m2, lane, big), axis=-1, keepdims=True)
    chosen = jnp.logical_or(lane == i1, lane == i2)
    a_sel = jnp.where(chosen, aff, 0.0)
    comb_o[0] = p_g * a_sel / jnp.sum(a_sel, axis=-1, keepdims=True)


def _merge_mem_router(oa, ob, sgd, sgm, x, w_br_dsa, w_br_mla, w_out, mem_x_g, mem_w_q, mem_q_g, km, vm,
                      mem_w_o, moe_g, w_group, w_expert, expert_bias, tm):
    b, s, d = x.shape
    wa = jnp.pad(w_br_dsa.reshape(DSA_HEADS, DSA_HEAD_DIM, d), ((0, 0), (0, LANES - DSA_HEAD_DIM), (0, 0)))
    wa = wa.reshape(DSA_HEADS * LANES, d).astype(BF16)
    wb = jnp.pad(w_br_mla.reshape(MLA_HEADS, MLA_V, d), ((0, 0), (0, LANES - MLA_V), (0, 0)))
    wb = wb.reshape(MLA_HEADS * LANES, d).astype(BF16)
    wq = _pad_heads(mem_w_q, MEM_HEADS, MEM_HEAD_DIM).astype(BF16)
    wmo = jnp.pad(mem_w_o.reshape(MEM_HEADS, MEM_HEAD_DIM, d), ((0, 0), (0, LANES - MEM_HEAD_DIM), (0, 0)))
    wmo = wmo.reshape(MEM_HEADS * LANES, d).astype(BF16)
    wr = _pad_cols(jnp.concatenate([w_expert, w_group], axis=1).astype(F32))
    bias = _pad_cols(expert_bias.reshape(1, -1).astype(F32))
    consts_a = [wa, wb, w_out.astype(BF16), mem_x_g.reshape(1, d), wq, _pad_cols(mem_q_g.reshape(1, -1))]
    consts_b = [wmo, moe_g.reshape(1, d), wr, bias]
    m = km.shape[1]
    mw = MEM_HEADS * LANES
    tok = lambda w: pl.BlockSpec((1, tm, w), lambda bi, i: (bi, i, 0))
    memspec = pl.BlockSpec((1, m, mw), lambda bi, i: (bi, 0, 0))
    return pl.pallas_call(
        _merge_kernel,
        grid=(b, s // tm),
        in_specs=[pl.BlockSpec((1, DSA_HEADS, tm, LANES), lambda bi, i: (bi, 0, i, 0)),
                  tok(MLA_HEADS * LANES), tok(d), tok(d), tok(d)]
                 + [_const_spec(c.shape) for c in consts_a] + [memspec, memspec]
                 + [_const_spec(c.shape) for c in consts_b],
        out_specs=[tok(d), tok(d), tok(LANES)],
        out_shape=[jax.ShapeDtypeStruct((b, s, d), F32), jax.ShapeDtypeStruct((b, s, d), BF16),
                   jax.ShapeDtypeStruct((b, s, LANES), F32)],
        compiler_params=_params(("parallel", "parallel")),
        name="merge_mem",
    )(oa, ob, sgd, sgm, x, *consts_a, km, vm, *consts_b)


def _moe_kernel(h_ref, comb_ref, x_ref, wgu_ref, wd_ref, o_ref, acc_ref):
    e = pl.program_id(1)
    gu = jnp.dot(h_ref[...], wgu_ref[0], preferred_element_type=F32)
    act = jax.nn.silu(gu[:, :D_EXPERT]) * gu[:, D_EXPERT:]
    comb = comb_ref[...]
    lane = lax.broadcasted_iota(jnp.int32, comb.shape, 1)
    c_e = jnp.sum(jnp.where(lane == e, comb, 0.0), axis=-1, keepdims=True)
    contrib = jnp.dot((act * c_e).astype(BF16), wd_ref[0], preferred_element_type=F32)

    @pl.when(e == 0)
    def _():
        acc_ref[...] = contrib

    @pl.when(e > 0)
    def _():
        acc_ref[...] += contrib

    @pl.when(e == N_EXPERTS - 1)
    def _():
        o_ref[...] = x_ref[...] + acc_ref[...]


def _moe(h3, comb, x2, w_gate, w_up, w_down, tm):
    n, d = x2.shape
    wgu = jnp.concatenate([w_gate, w_up], axis=-1).astype(BF16)
    wd = w_down.astype(BF16)
    return pl.pallas_call(
        _moe_kernel,
        grid=(n // tm, N_EXPERTS),
        in_specs=[pl.BlockSpec((tm, d), lambda i, e: (i, 0)),
                  pl.BlockSpec((tm, LANES), lambda i, e: (i, 0)),
                  pl.BlockSpec((tm, d), lambda i, e: (i, 0)),
                  pl.BlockSpec((1, d, 2 * D_EXPERT), lambda i, e: (e, 0, 0)),
                  pl.BlockSpec((1, D_EXPERT, d), lambda i, e: (e, 0, 0))],
        out_specs=pl.BlockSpec((tm, d), lambda i, e: (i, 0)),
        out_shape=jax.ShapeDtypeStruct((n, d), F32),
        scratch_shapes=[pltpu.VMEM((tm, d), F32)],
        compiler_params=_params(("parallel", "arbitrary")),
        name="moe",
    )(h3, comb, x2, wgu, wd)


def _layer(x, mem, positions, p):
    b, s, d = x.shape
    n = b * s
    assert s % (2 * KEY_CHUNK) == 0 and s % Q_TILE_MLA == 0
    topk = min(TOPK_MAX, s // 4)
    tm = min(256, s)
    qa, qi, ka, va, ki, iw, mq, mk, mv, sgd, sgm = _in_proj(
        x.reshape(n, d), positions.reshape(n, 1), p["attn_norm_g"], p["w_in"], p["dsa_q_norm_g"],
        p["dsa_k_norm_g"], p["mla_cq_norm_g"], p["mla_ckv_norm_g"], p["mla_w_uq"], p["mla_w_ukv"],
        p["mla_q_norm_g"], p["mla_k_norm_g"], tm)
    r3 = lambda a: a.reshape(b, s, a.shape[-1])
    oa = _dsa_attention(r3(qa), r3(qi), r3(iw), r3(ka), r3(va), r3(ki), topk)
    ob = _mla_attention(r3(mq), r3(mk), r3(mv))
    km, vm = _mem_kv(mem, p["mem_norm_g"], p["mem_w_kv"], p["mem_k_norm_g"])
    x2, h3, comb = _merge_mem_router(
        oa, ob, r3(sgd), r3(sgm), x, p["w_branch_dsa"], p["w_branch_mla"], p["w_out"], p["mem_x_norm_g"],
        p["mem_w_q"], p["mem_q_norm_g"], km, vm, p["mem_w_o"], p["moe_norm_g"], p["moe_w_group"],
        p["moe_w_expert"], p["moe_expert_bias"], min(512, s))
    out = _moe(h3.reshape(n, d), comb.reshape(n, LANES), x2.reshape(n, d), p["moe_w_gate"], p["moe_w_up"],
               p["moe_w_down"], min(1024, n))
    return out.reshape(b, s, d)


_PARAM_NAMES = ("attn_norm_g", "w_in", "dsa_q_norm_g", "dsa_k_norm_g", "mla_cq_norm_g", "mla_ckv_norm_g",
                "mla_w_uq", "mla_w_ukv", "mla_q_norm_g", "mla_k_norm_g", "w_branch_dsa", "w_branch_mla", "w_out",
                "mem_x_norm_g", "mem_norm_g", "mem_w_q", "mem_w_kv", "mem_q_norm_g", "mem_k_norm_g", "mem_w_o",
                "moe_norm_g", "moe_w_group", "moe_w_expert", "moe_expert_bias", "moe_w_gate", "moe_w_up",
                "moe_w_down")


def kernel(x, mem, positions, attn_norm_g, w_in, dsa_q_norm_g, dsa_k_norm_g, mla_cq_norm_g, mla_ckv_norm_g, mla_w_uq, mla_w_ukv, mla_q_norm_g, mla_k_norm_g, w_branch_dsa, w_branch_mla, w_out, mem_x_norm_g, mem_norm_g, mem_w_q, mem_w_kv, mem_q_norm_g, mem_k_norm_g, mem_w_o, moe_norm_g, moe_w_group, moe_w_expert, moe_expert_bias, moe_w_gate, moe_w_up, moe_w_down):
    stacked = (attn_norm_g, w_in, dsa_q_norm_g, dsa_k_norm_g, mla_cq_norm_g, mla_ckv_norm_g, mla_w_uq, mla_w_ukv,
               mla_q_norm_g, mla_k_norm_g, w_branch_dsa, w_branch_mla, w_out, mem_x_norm_g, mem_norm_g, mem_w_q,
               mem_w_kv, mem_q_norm_g, mem_k_norm_g, mem_w_o, moe_norm_g, moe_w_group, moe_w_expert,
               moe_expert_bias, moe_w_gate, moe_w_up, moe_w_down)
    for layer in range(attn_norm_g.shape[0]):
        p = {name: arr[layer] for name, arr in zip(_PARAM_NAMES, stacked)}
        x = _layer(x, mem, positions, p)
    return x
```

```python
import functools

import jax
import jax.numpy as jnp
from jax import lax
from jax.experimental import pallas as pl
from jax.experimental.pallas import tpu as pltpu

F32 = jnp.float32
BF16 = jnp.bfloat16

LANES = 128
VMEM_LIMIT = 56 * 1024 * 1024

ROPE_THETA = 10000.0
EPS = 1e-6
DSA_HEADS = 8
DSA_HEAD_DIM = 64
IDX_HEADS = 8
IDX_DIM = 64
TOPK_MAX = 256
MLA_HEADS = 8
MLA_NOPE = 64
MLA_ROPE = 32
MLA_QK = MLA_NOPE + MLA_ROPE
MLA_V = 64
MLA_Q_RANK = 256
MLA_KV_RANK = 128
MEM_HEADS = 4
MEM_HEAD_DIM = 64
N_GROUPS = 4
EXPERTS_PER_GROUP = 4
N_EXPERTS = N_GROUPS * EXPERTS_PER_GROUP
D_EXPERT = 256

NEG_BIG = -1e30
KEY_CHUNK = 256
Q_TILE_DSA = 128
Q_TILE_MLA = 512
MLA_HEAD_GROUP = 4
LOG2_E = 1.4426950408889634
BISECT_ITERS = 16


def _const_spec(shape):
    nd = len(shape)
    return pl.BlockSpec(shape, lambda *_: (0,) * nd)


def _params(sem):
    return pltpu.CompilerParams(dimension_semantics=sem, vmem_limit_bytes=VMEM_LIMIT)


def _rms(x, g, n):
    ms = jnp.sum(x * x, axis=-1, keepdims=True) * (1.0 / n)
    return (x * lax.rsqrt(ms + EPS)) * g


def _rope(x, cos, sin_signed, lo_mask, half):
    fwd = pltpu.roll(x, LANES - half, 1)
    bwd = pltpu.roll(x, half, 1)
    return x * cos + jnp.where(lo_mask, fwd, bwd) * sin_signed


def _in_proj_kernel(x_ref, pos_ref, g_ref, wqa_ref, wqi_ref, wsm_ref, wcq_ref, wg_ref, wuq_ref, wuk_ref,
                    wuv_ref, qag_ref, kag_ref, cqg_ref, ckvg_ref, mqg_ref, mkg_ref, fd_ref, fm_ref,
                    qa_o, qi_o, ka_o, va_o, ki_o, iw_o, mq_o, mk_o, mv_o, sgd_o, sgm_o):
    x = x_ref[...]
    d_model = x.shape[-1]
    h = _rms(x, g_ref[...], d_model).astype(BF16)

    pos = pos_ref[...].astype(F32)
    lane = lax.broadcasted_iota(jnp.int32, (x.shape[0], LANES), 1)
    ang_d = pos * fd_ref[...]
    lo_d = lane < DSA_HEAD_DIM // 2
    cos_d = jnp.cos(ang_d)
    sin_d = jnp.sin(ang_d)
    sin_d = jnp.where(lo_d, -sin_d, sin_d)
    ang_m = pos * fm_ref[...]
    lo_m = lane < MLA_NOPE + MLA_ROPE // 2
    cos_m = jnp.cos(ang_m)
    sin_m = jnp.sin(ang_m)
    sin_m = jnp.where(lo_m, -sin_m, sin_m)
    rope_d = functools.partial(_rope, cos=cos_d, sin_signed=sin_d, lo_mask=lo_d, half=DSA_HEAD_DIM // 2)
    rope_m = functools.partial(_rope, cos=cos_m, sin_signed=sin_m, lo_mask=lo_m, half=MLA_ROPE // 2)

    def dot(a, w_ref):
        return jnp.dot(a, w_ref[...], preferred_element_type=F32)

    att_scale = DSA_HEAD_DIM ** -0.5 * LOG2_E
    idx_scale = IDX_DIM ** -0.5 * IDX_HEADS ** -0.5
    mla_scale = MLA_QK ** -0.5 * LOG2_E

    qa = dot(h, wqa_ref)
    for hd in range(DSA_HEADS):
        sl = slice(hd * LANES, (hd + 1) * LANES)
        y = rope_d(_rms(qa[:, sl], qag_ref[...], DSA_HEAD_DIM))
        qa_o[:, sl] = (y * att_scale).astype(BF16)

    qi = dot(h, wqi_ref)
    for hd in range(IDX_HEADS):
        sl = slice(hd * LANES, (hd + 1) * LANES)
        qi_o[:, sl] = rope_d(qi[:, sl]).astype(BF16)

    sm = dot(h, wsm_ref)
    ka_o[...] = rope_d(_rms(sm[:, 0:LANES], kag_ref[...], DSA_HEAD_DIM)).astype(BF16)
    va_o[...] = (sm[:, LANES:2 * LANES] + jnp.where(lane[0:1] == DSA_HEAD_DIM, 1.0, 0.0)).astype(BF16)
    ki_o[...] = rope_d(sm[:, 2 * LANES:3 * LANES]).astype(BF16)
    iw_o[...] = sm[:, 3 * LANES:4 * LANES] * idx_scale
    ckv = _rms(sm[:, 4 * LANES:5 * LANES], ckvg_ref[...], MLA_KV_RANK).astype(BF16)
    kpe = sm[:, 5 * LANES:6 * LANES]

    kn = dot(ckv, wuk_ref)
    for hd in range(MLA_HEADS):
        sl = slice(hd * LANES, (hd + 1) * LANES)
        y = rope_m(_rms(kn[:, sl] + kpe, mkg_ref[...], MLA_QK))
        mk_o[:, sl] = y.astype(BF16)
    wide_lane = lax.broadcasted_iota(jnp.int32, (1, MLA_HEADS * LANES), 1)
    ones_col = jnp.where(wide_lane % LANES == MLA_V, 1.0, 0.0)
    mv_o[...] = (dot(ckv, wuv_ref) + ones_col).astype(BF16)

    cq = _rms(dot(h, wcq_ref), cqg_ref[...], MLA_Q_RANK).astype(BF16)
    qb = dot(cq, wuq_ref)
    for hd in range(MLA_HEADS):
        sl = slice(hd * LANES, (hd + 1) * LANES)
        y = rope_m(_rms(qb[:, sl], mqg_ref[...], MLA_QK))
        mq_o[:, sl] = (y * mla_scale).astype(BF16)

    gates = jax.nn.sigmoid(dot(h, wg_ref))
    sgd_o[...] = gates[:, :d_model].astype(BF16)
    sgm_o[...] = gates[:, d_model:].astype(BF16)


def _pad_heads(w, heads, dim):
    k = w.shape[0]
    w = w.reshape(k, heads, dim)
    w = jnp.pad(w, ((0, 0), (0, 0), (0, LANES - dim)))
    return w.reshape(k, heads * LANES)


def _pad_cols(w, width=LANES, offset=0):
    return jnp.pad(w, ((0, 0), (offset, width - offset - w.shape[1])))


def _in_proj(x2, pos2, attn_g, w_in, dsa_q_g, dsa_k_g, cq_g, ckv_g, w_uq, w_ukv, mq_g, mk_g, tm):
    n, d = x2.shape
    sizes = (DSA_HEADS * DSA_HEAD_DIM, DSA_HEAD_DIM, DSA_HEAD_DIM, IDX_HEADS * IDX_DIM, IDX_DIM, IDX_HEADS,
             MLA_Q_RANK, MLA_KV_RANK, MLA_ROPE, d, d)
    offs = [0]
    for s in sizes:
        offs.append(offs[-1] + s)
    seg = [w_in[:, offs[i]:offs[i + 1]] for i in range(len(sizes))]
    w_dq, w_dk, w_dv, w_iq, w_ik, w_iw, w_cq, w_ckv, w_kpe, w_gd, w_gm = seg

    wqa = _pad_heads(w_dq, DSA_HEADS, DSA_HEAD_DIM).astype(BF16)
    wqi = _pad_heads(w_iq, IDX_HEADS, IDX_DIM).astype(BF16)
    wsm = jnp.concatenate([_pad_cols(w_dk), _pad_cols(w_dv), _pad_cols(w_ik), _pad_cols(w_iw), w_ckv,
                           _pad_cols(w_kpe, offset=MLA_NOPE)], axis=1).astype(BF16)
    wcq = w_cq.astype(BF16)
    wg = jnp.concatenate([w_gd, w_gm], axis=1).astype(BF16)
    wuq = _pad_heads(w_uq, MLA_HEADS, MLA_QK).astype(BF16)
    ukv = w_ukv.reshape(MLA_KV_RANK, MLA_HEADS, MLA_NOPE + MLA_V)
    wuk = _pad_heads(ukv[:, :, :MLA_NOPE].reshape(MLA_KV_RANK, -1), MLA_HEADS, MLA_NOPE).astype(BF16)
    wuv = _pad_heads(ukv[:, :, MLA_NOPE:].reshape(MLA_KV_RANK, -1), MLA_HEADS, MLA_V).astype(BF16)

    row = lambda v: _pad_cols(v.reshape(1, -1).astype(F32), width=max(LANES, v.size))
    half_d = DSA_HEAD_DIM // 2
    inv_d = ROPE_THETA ** (-jnp.arange(half_d, dtype=F32) / half_d)
    fd = _pad_cols(jnp.concatenate([inv_d, inv_d]).reshape(1, -1))
    half_m = MLA_ROPE // 2
    inv_m = ROPE_THETA ** (-jnp.arange(half_m, dtype=F32) / half_m)
    fm = _pad_cols(jnp.concatenate([inv_m, inv_m]).reshape(1, -1), offset=MLA_NOPE)

    consts = [attn_g.reshape(1, d), wqa, wqi, wsm, wcq, wg, wuq, wuk, wuv, row(dsa_q_g), row(dsa_k_g),
              row(cq_g), row(ckv_g), row(mq_g), row(mk_g), fd, fm]
    wide = DSA_HEADS * LANES
    out_shapes = [
        jax.ShapeDtypeStruct((n, wide), BF16),
        jax.ShapeDtypeStruct((n, wide), BF16),
        jax.ShapeDtypeStruct((n, LANES), BF16),
        jax.ShapeDtypeStruct((n, LANES), BF16),
        jax.ShapeDtypeStruct((n, LANES), BF16),
        jax.ShapeDtypeStruct((n, LANES), F32),
        jax.ShapeDtypeStruct((n, wide), BF16),
        jax.ShapeDtypeStruct((n, wide), BF16),
        jax.ShapeDtypeStruct((n, wide), BF16),
        jax.ShapeDtypeStruct((n, d), BF16),
        jax.ShapeDtypeStruct((n, d), BF16),
    ]
    tile = lambda w: pl.BlockSpec((tm, w), lambda i: (i, 0))
    return pl.pallas_call(
        _in_proj_kernel,
        grid=(n // tm,),
        in_specs=[tile(d), tile(1)] + [_const_spec(c.shape) for c in consts],
        out_specs=[tile(s.shape[1]) for s in out_shapes],
        out_shape=out_shapes,
        compiler_params=_params(("parallel",)),
        name="in_proj",
    )(x2, pos2, *consts)


def _dsa_kernel(qa_ref, qi_ref, iw_ref, ka_ref, vt_ref, ki_ref, o_ref, score_ref, logit_ref, *, topk):
    i = pl.program_id(1)
    tq = qa_ref.shape[1]
    heads = DSA_HEADS
    npair = (i * tq + tq + 2 * KEY_CHUNK - 1) // (2 * KEY_CHUNK)
    kf = float(topk)
    shape = (KEY_CHUNK, tq)

    krow = lax.broadcasted_iota(jnp.int32, shape, 0)
    qcol = i * tq + lax.broadcasted_iota(jnp.int32, shape, 1)
    contract_last = (((1,), (1,)), ((), ()))

    def pair_loop(body, init):
        return lax.fori_loop(0, npair, lambda j, carry: body(2 * j, 2 * j + 1, carry), init)

    qi = jnp.concatenate([qi_ref[0, :, hd * LANES:(hd + 1) * LANES] for hd in range(IDX_HEADS)], axis=0)
    iw = iw_ref[0]

    def score_body(c0, c1, carry):
        mx, mn = carry
        for c in (c0, c1):
            kc = ki_ref[0, pl.ds(pl.multiple_of(c * KEY_CHUNK, KEY_CHUNK), KEY_CHUNK), :]
            sc = None
            for hp in range(IDX_HEADS // 2):
                rel = lax.dot_general(kc, qi[2 * hp * tq:(2 * hp + 2) * tq], contract_last,
                                      preferred_element_type=F32)
                rel = jnp.maximum(rel, 0.0)
                part = rel[:, :tq] * iw[2 * hp:2 * hp + 1, :] + rel[:, tq:] * iw[2 * hp + 1:2 * hp + 2, :]
                sc = part if sc is None else sc + part
            causal = (krow + c * KEY_CHUNK) <= qcol
            masked = jnp.where(causal, sc, -jnp.inf)
            score_ref[c] = masked
            mx = jnp.maximum(mx, masked)
            mn = jnp.minimum(mn, jnp.where(causal, sc, jnp.inf))
        return mx, mn

    mx, mn = pair_loop(score_body, (jnp.full(shape, -jnp.inf, F32), jnp.full(shape, jnp.inf, F32)))
    hi = jnp.max(mx, axis=0, keepdims=True)
    lo = jnp.min(mn, axis=0, keepdims=True)

    def count_gt(t):
        def body(c0, c1, acc):
            return acc + (jnp.where(score_ref[c0] > t, 1.0, 0.0) + jnp.where(score_ref[c1] > t, 1.0, 0.0))
        return jnp.sum(pair_loop(body, jnp.zeros(shape, F32)), axis=0, keepdims=True)

    def bisect(_, carry):
        lo, hi = carry
        mid = 0.5 * (lo + hi)
        below = count_gt(mid) < kf
        return jnp.where(below, lo, mid), jnp.where(below, mid, hi)

    lo, hi = lax.fori_loop(0, BISECT_ITERS, bisect, (lo, hi))

    def max_le(t):
        def body(c0, c1, acc):
            for c in (c0, c1):
                s = score_ref[c]
                acc = jnp.maximum(acc, jnp.where(s <= t, s, -jnp.inf))
            return acc
        return jnp.max(pair_loop(body, jnp.full(shape, -jnp.inf, F32)), axis=0, keepdims=True)

    n_causal = (i * tq + lax.broadcasted_iota(jnp.int32, (1, tq), 1) + 1).astype(F32)
    small = n_causal <= kf

    def refine_cond(carry):
        return jnp.min(carry[2]) < 0.5

    def refine_body(carry):
        m, thr, done, thr_cnt = carry

        def body(c0, c1, acc):
            cnt, nxt = acc
            for c in (c0, c1):
                s = score_ref[c]
                cnt = cnt + jnp.where(s >= m, 1.0, 0.0)
                nxt = jnp.maximum(nxt, jnp.where(s < m, s, -jnp.inf))
            return cnt, nxt

        cnt, nxt = pair_loop(body, (jnp.zeros(shape, F32), jnp.full(shape, -jnp.inf, F32)))
        cnt = jnp.sum(cnt, axis=0, keepdims=True)
        nxt = jnp.max(nxt, axis=0, keepdims=True)
        reached = cnt >= kf
        hit = jnp.logical_and(reached, done < 0.5)
        return nxt, jnp.where(hit, m, thr), jnp.where(reached, 1.0, done), jnp.where(hit, cnt, thr_cnt)

    init = (max_le(hi), jnp.full((1, tq), -jnp.inf, F32), jnp.where(small, 1.0, 0.0), jnp.zeros((1, tq), F32))
    _, thr, _, thr_cnt = lax.while_loop(refine_cond, refine_body, init)

    earlier = (lax.broadcasted_iota(jnp.int32, (KEY_CHUNK, KEY_CHUNK), 1)
               < lax.broadcasted_iota(jnp.int32, (KEY_CHUNK, KEY_CHUNK), 0)).astype(BF16)

    def mask_exact_k():
        def body(c0, c1, carry):
            for c in (c0, c1):
                s = score_ref[c]
                sel = jnp.logical_and(s >= thr, s > -jnp.inf)
                score_ref[c] = jnp.where(sel, 0.0, NEG_BIG)
            return carry
        pair_loop(body, 0)

    def mask_with_ties():
        quota = jnp.where(small, 0.0, kf - count_gt(thr))

        def body(c0, c1, seen):
            for c in (c0, c1):
                s = score_ref[c]
                eq = s == thr
                eqf = jnp.where(eq, 1.0, 0.0)
                before = jnp.dot(earlier, eqf.astype(BF16), preferred_element_type=F32) + seen
                sel = jnp.logical_or(s > thr, jnp.logical_and(eq, before < quota))
                seen = seen + jnp.sum(eqf, axis=0, keepdims=True)
                score_ref[c] = jnp.where(sel, 0.0, NEG_BIG)
            return seen
        pair_loop(body, jnp.zeros((1, tq), F32))

    surplus = jnp.max(jnp.where(small, 0.0, thr_cnt - kf)) > 0.5
    lax.cond(surplus, mask_with_ties, mask_exact_k)

    qa = jnp.concatenate([qa_ref[0, :, hd * LANES:(hd + 1) * LANES] for hd in range(heads)], axis=0)

    def qk_into(slot, c):
        kc = ka_ref[0, pl.ds(pl.multiple_of(c * KEY_CHUNK, KEY_CHUNK), KEY_CHUNK), :]
        logit_ref[slot] = lax.dot_general(kc, qa, contract_last, preferred_element_type=F32)

    def softmax_pv(slot, c, m_run, acc):
        bias = score_ref[c]
        vt = vt_ref[0, c]
        new_m, new_acc = [], []
        for hp in range(heads // 2):
            probs, alphas = [], []
            for hd in (2 * hp, 2 * hp + 1):
                sl = slice(hd * tq, (hd + 1) * tq)
                lh = logit_ref[slot, :, sl] + bias
                m_old = m_run[:, sl]
                m_new = jnp.maximum(m_old, jnp.max(lh, axis=0, keepdims=True))
                probs.append(jnp.exp2(lh - m_new).astype(BF16))
                alphas.append(jnp.exp2(m_old - m_new))
                new_m.append(m_new)
            pv = jnp.dot(vt, jnp.concatenate(probs, axis=1), preferred_element_type=F32)
            new_acc.append(jnp.concatenate(alphas, axis=1) * acc[:, 2 * hp * tq:(2 * hp + 2) * tq] + pv)
        return jnp.concatenate(new_m, axis=1), jnp.concatenate(new_acc, axis=1)

    def attn_body(c0, c1, carry):
        m_run, acc = carry
        qk_into(1, c1)
        m_run, acc = softmax_pv(0, c0, m_run, acc)
        qk_into(0, jnp.minimum(c0 + 2, 2 * npair - 1))
        return softmax_pv(1, c1, m_run, acc)

    qk_into(0, 0)
    init = (jnp.full((1, heads * tq), NEG_BIG, F32), jnp.zeros((LANES, heads * tq), F32))
    _, acc = pair_loop(attn_body, init)
    for hd in range(heads):
        blk = acc[:, hd * tq:(hd + 1) * tq]
        o_ref[0, hd] = (blk / blk[DSA_HEAD_DIM:DSA_HEAD_DIM + 1, :]).T.astype(o_ref.dtype)


def _dsa_attention(qa, qi, iw, ka, va, ki, topk):
    b, s, wide = qa.shape
    tq = Q_TILE_DSA
    nkc = s // KEY_CHUNK
    iw_t = jnp.swapaxes(iw[..., :IDX_HEADS], 1, 2)
    v_t = jnp.swapaxes(va.reshape(b, nkc, KEY_CHUNK, LANES), 2, 3)
    qspec = pl.BlockSpec((1, tq, wide), lambda bi, i: (bi, i, 0))
    kspec = pl.BlockSpec((1, s, LANES), lambda bi, i: (bi, 0, 0))
    return pl.pallas_call(
        functools.partial(_dsa_kernel, topk=topk),
        grid=(b, s // tq),
        in_specs=[qspec, qspec, pl.BlockSpec((1, IDX_HEADS, tq), lambda bi, i: (bi, 0, i)), kspec,
                  pl.BlockSpec((1, nkc, LANES, KEY_CHUNK), lambda bi, i: (bi, 0, 0, 0)), kspec],
        out_specs=pl.BlockSpec((1, DSA_HEADS, tq, LANES), lambda bi, i: (bi, 0, i, 0)),
        out_shape=jax.ShapeDtypeStruct((b, DSA_HEADS, s, LANES), BF16),
        scratch_shapes=[pltpu.VMEM((nkc, KEY_CHUNK, tq), F32),
                        pltpu.VMEM((2, KEY_CHUNK, DSA_HEADS * tq), F32)],
        compiler_params=_params(("parallel", "arbitrary")),
        name="dsa_attn",
    )(qa, qi, iw_t, ka, v_t, ki)


def _mla_kernel(q_ref, k_ref, v_ref, o_ref):
    i = pl.program_id(2)
    tq = q_ref.shape[1]
    group = q_ref.shape[2] // LANES
    qs = [q_ref[0, :, g * LANES:(g + 1) * LANES] for g in range(group)]
    keep = lax.broadcasted_iota(jnp.int32, (tq, tq), 1) <= lax.broadcasted_iota(jnp.int32, (tq, tq), 0)

    def step(k0, carry, masked):
        logits = [lax.dot_general(qs[g], k_ref[0, pl.ds(k0, tq), g * LANES:(g + 1) * LANES],
                                  (((1,), (1,)), ((), ())), preferred_element_type=F32) for g in range(group)]
        probs, alphas, out = [], [], []
        for g in range(group):
            m_run = carry[2 * g]
            logit = jnp.where(keep, logits[g], NEG_BIG) if masked else logits[g]
            m_new = jnp.maximum(m_run, jnp.max(logit, axis=-1, keepdims=True))
            p = jnp.exp2(logit - m_new)
            if masked:
                p = jnp.where(keep, p, 0.0)
            probs.append(p.astype(BF16))
            alphas.append(jnp.exp2(m_run - m_new))
            out.append(m_new)
        res = []
        for g in range(group):
            vc = v_ref[0, pl.ds(k0, tq), g * LANES:(g + 1) * LANES]
            acc = alphas[g] * carry[2 * g + 1] + jnp.dot(probs[g], vc, preferred_element_type=F32)
            res += [out[g], acc]
        return tuple(res)

    init = (jnp.full((tq, 1), NEG_BIG, F32), jnp.zeros((tq, LANES), F32)) * group
    carry = lax.fori_loop(0, i, lambda c, cr: step(pl.multiple_of(c * tq, tq), cr, False), init)
    carry = step(pl.multiple_of(i * tq, tq), carry, True)
    for g in range(group):
        acc = carry[2 * g + 1]
        o_ref[0, :, g * LANES:(g + 1) * LANES] = (acc / acc[:, MLA_V:MLA_V + 1]).astype(o_ref.dtype)


def _mla_attention(mq, mk, mv):
    b, s, wide = mq.shape
    tq = Q_TILE_MLA
    gw = MLA_HEAD_GROUP * LANES
    return pl.pallas_call(
        _mla_kernel,
        grid=(b, MLA_HEADS // MLA_HEAD_GROUP, s // tq),
        in_specs=[pl.BlockSpec((1, tq, gw), lambda bi, h, i: (bi, i, h)),
                  pl.BlockSpec((1, s, gw), lambda bi, h, i: (bi, 0, h)),
                  pl.BlockSpec((1, s, gw), lambda bi, h, i: (bi, 0, h))],
        out_specs=pl.BlockSpec((1, tq, gw), lambda bi, h, i: (bi, i, h)),
        out_shape=jax.ShapeDtypeStruct((b, s, wide), BF16),
        compiler_params=_params(("parallel", "parallel", "arbitrary")),
        name="mla_attn",
    )(mq, mk, mv)


def _mem_kv_kernel(mem_ref, g_ref, wkv_ref, kg_ref, k_o, v_o):
    m = mem_ref[0]
    hm = _rms(m, g_ref[...], m.shape[-1]).astype(BF16)
    kv = jnp.dot(hm, wkv_ref[...], preferred_element_type=F32)
    for hd in range(MEM_HEADS):
        sl = slice(hd * LANES, (hd + 1) * LANES)
        k_o[0, :, sl] = _rms(kv[:, sl], kg_ref[...], MEM_HEAD_DIM).astype(BF16)
    v_o[0] = kv[:, MEM_HEADS * LANES:].astype(BF16)


def _mem_kv(mem, mem_g, w_kv, k_g):
    b, m, d = mem.shape
    hw = MEM_HEADS * MEM_HEAD_DIM
    wkv = jnp.concatenate([_pad_heads(w_kv[:, :hw], MEM_HEADS, MEM_HEAD_DIM),
                           _pad_heads(w_kv[:, hw:], MEM_HEADS, MEM_HEAD_DIM)], axis=1).astype(BF16)
    consts = [mem_g.reshape(1, d), wkv, _pad_cols(k_g.reshape(1, -1))]
    wide = MEM_HEADS * LANES
    spec = pl.BlockSpec((1, m, wide), lambda bi: (bi, 0, 0))
    return pl.pallas_call(
        _mem_kv_kernel,
        grid=(b,),
        in_specs=[pl.BlockSpec((1, m, d), lambda bi: (bi, 0, 0))] + [_const_spec(c.shape) for c in consts],
        out_specs=[spec, spec],
        out_shape=[jax.ShapeDtypeStruct((b, m, wide), BF16)] * 2,
        compiler_params=_params(("parallel",)),
        name="mem_kv",
    )(mem, *consts)


def _split_dot(a, w):
    a_hi = a.astype(BF16)
    a_lo = (a - a_hi.astype(F32)).astype(BF16)
    w_hi = w.astype(BF16)
    w_lo = (w - w_hi.astype(F32)).astype(BF16)
    d = functools.partial(jnp.dot, preferred_element_type=F32)
    return d(a_hi, w_hi) + (d(a_lo, w_hi) + d(a_hi, w_lo))


def _merge_kernel(oa_ref, ob_ref, sgd_ref, sgm_ref, x_ref, wa_ref, wb_ref, wo_ref, mxg_ref, wq_ref, qg_ref,
                  km_ref, vm_ref, wmo_ref, moeg_ref, wr_ref, bias_ref, x2_o, h3_o, comb_o):
    d = functools.partial(jnp.dot, preferred_element_type=F32)
    oa = jnp.concatenate([oa_ref[0, hd] for hd in range(DSA_HEADS)], axis=-1)
    merged = (sgd_ref[0].astype(F32) * d(oa, wa_ref[...]) + sgm_ref[0].astype(F32) * d(ob_ref[0], wb_ref[...]))
    x1 = x_ref[0] + d(merged.astype(BF16), wo_ref[...])

    d_model = x1.shape[-1]
    h2 = _rms(x1, mxg_ref[...], d_model).astype(BF16)
    qm = d(h2, wq_ref[...])
    outs = []
    for hd in range(MEM_HEADS):
        sl = slice(hd * LANES, (hd + 1) * LANES)
        q = (_rms(qm[:, sl], qg_ref[...], MEM_HEAD_DIM) * MEM_HEAD_DIM ** -0.5).astype(BF16)
        logit = lax.dot_general(q, km_ref[0, :, sl], (((1,), (1,)), ((), ())), preferred_element_type=F32)
        logit = logit - jnp.max(logit, axis=-1, keepdims=True)
        p = jnp.exp(logit)
        pv = d(p.astype(BF16), vm_ref[0, :, sl])
        outs.append((pv / jnp.sum(p, axis=-1, keepdims=True)).astype(BF16))
    x2 = x1 + d(jnp.concatenate(outs, axis=-1), wmo_ref[...])
    x2_o[0] = x2

    h3 = _rms(x2, moeg_ref[...], d_model)
    h3_o[0] = h3.astype(BF16)
    logits = _split_dot(h3, wr_ref[...])
    tm = logits.shape[0]
    lane = lax.broadcasted_iota(jnp.int32, (tm, LANES), 1)
    big = jnp.int32(LANES)
    is_grp = jnp.logical_and(lane >= N_EXPERTS, lane < N_EXPERTS + N_GROUPS)
    glog = jnp.where(is_grp, logits, -jnp.inf)
    gmax = jnp.max(glog, axis=-1, keepdims=True)
    g_sel = jnp.min(jnp.where(glog == gmax, lane, big), axis=-1, keepdims=True) - N_EXPERTS
    p_g = 1.0 / jnp.sum(jnp.where(is_grp, jnp.exp(logits - gmax), 0.0), axis=-1, keepdims=True)
    aff = jax.nn.sigmoid(logits)
    in_grp = jnp.logical_and(lane >= g_sel * EXPERTS_PER_GROUP, lane < (g_sel + 1) * EXPERTS_PER_GROUP)
    val = jnp.where(in_grp, aff + bias_ref[...], -jnp.inf)
    m1 = jnp.max(val, axis=-1, keepdims=True)
    i1 = jnp.min(jnp.where(val == m1, lane, big), axis=-1, keepdims=True)
    val2 = jnp.where(lane == i1, -jnp.inf, val)
    m2 = jnp.max(val2, axis=-1, keepdims=True)
    i2 = jnp.min(jnp.where(val2 == m2, lane, big), axis=-1, keepdims=True)
    chosen = jnp.logical_or(lane == i1, lane == i2)
    a_sel = jnp.where(chosen, aff, 0.0)
    comb_o[0] = p_g * a_sel / jnp.sum(a_sel, axis=-1, keepdims=True)


def _merge_mem_router(oa, ob, sgd, sgm, x, w_br_dsa, w_br_mla, w_out, mem_x_g, mem_w_q, mem_q_g, km, vm,
                      mem_w_o, moe_g, w_group, w_expert, expert_bias, tm):
    b, s, d = x.shape
    wa = jnp.pad(w_br_dsa.reshape(DSA_HEADS, DSA_HEAD_DIM, d), ((0, 0), (0, LANES - DSA_HEAD_DIM), (0, 0)))
    wa = wa.reshape(DSA_HEADS * LANES, d).astype(BF16)
    wb = jnp.pad(w_br_mla.reshape(MLA_HEADS, MLA_V, d), ((0, 0), (0, LANES - MLA_V), (0, 0)))
    wb = wb.reshape(MLA_HEADS * LANES, d).astype(BF16)
    wq = _pad_heads(mem_w_q, MEM_HEADS, MEM_HEAD_DIM).astype(BF16)
    wmo = jnp.pad(mem_w_o.reshape(MEM_HEADS, MEM_HEAD_DIM, d), ((0, 0), (0, LANES - MEM_HEAD_DIM), (0, 0)))
    wmo = wmo.reshape(MEM_HEADS * LANES, d).astype(BF16)
    wr = _pad_cols(jnp.concatenate([w_expert, w_group], axis=1).astype(F32))
    bias = _pad_cols(expert_bias.reshape(1, -1).astype(F32))
    consts_a = [wa, wb, w_out.astype(BF16), mem_x_g.reshape(1, d), wq, _pad_cols(mem_q_g.reshape(1, -1))]
    consts_b = [wmo, moe_g.reshape(1, d), wr, bias]
    m = km.shape[1]
    mw = MEM_HEADS * LANES
    tok = lambda w: pl.BlockSpec((1, tm, w), lambda bi, i: (bi, i, 0))
    memspec = pl.BlockSpec((1, m, mw), lambda bi, i: (bi, 0, 0))
    return pl.pallas_call(
        _merge_kernel,
        grid=(b, s // tm),
        in_specs=[pl.BlockSpec((1, DSA_HEADS, tm, LANES), lambda bi, i: (bi, 0, i, 0)),
                  tok(MLA_HEADS * LANES), tok(d), tok(d), tok(d)]
                 + [_const_spec(c.shape) for c in consts_a] + [memspec, memspec]
                 + [_const_spec(c.shape) for c in consts_b],
        out_specs=[tok(d), tok(d), tok(LANES)],
        out_shape=[jax.ShapeDtypeStruct((b, s, d), F32), jax.ShapeDtypeStruct((b, s, d), BF16),
                   jax.ShapeDtypeStruct((b, s, LANES), F32)],
        compiler_params=_params(("parallel", "parallel")),
        name="merge_mem",
    )(oa, ob, sgd, sgm, x, *consts_a, km, vm, *consts_b)


def _moe_kernel(h_ref, comb_ref, x_ref, wgu_ref, wd_ref, o_ref):
    g = pl.program_id(1)
    width = EXPERTS_PER_GROUP * D_EXPERT
    gu = jnp.dot(h_ref[...], wgu_ref[0], preferred_element_type=F32)
    act = jax.nn.silu(gu[:, :width]) * gu[:, width:]
    comb = comb_ref[...]
    lane = lax.broadcasted_iota(jnp.int32, comb.shape, 1)
    parts = []
    for e in range(EXPERTS_PER_GROUP):
        c_e = jnp.sum(jnp.where(lane == g * EXPERTS_PER_GROUP + e, comb, 0.0), axis=-1, keepdims=True)
        parts.append((act[:, e * D_EXPERT:(e + 1) * D_EXPERT] * c_e).astype(BF16))
    contrib = jnp.dot(jnp.concatenate(parts, axis=1), wd_ref[0], preferred_element_type=F32)

    @pl.when(g == 0)
    def _():
        o_ref[...] = x_ref[...] + contrib

    @pl.when(g > 0)
    def _():
        o_ref[...] += contrib


def _moe(h3, comb, x2, w_gate, w_up, w_down, tm):
    n, d = x2.shape
    width = EXPERTS_PER_GROUP * D_EXPERT
    by_group = lambda w: jnp.moveaxis(w.reshape(N_GROUPS, EXPERTS_PER_GROUP, d, D_EXPERT), 1, 2).reshape(N_GROUPS, d, width)
    wgu = jnp.concatenate([by_group(w_gate), by_group(w_up)], axis=-1).astype(BF16)
    wd = w_down.reshape(N_GROUPS, width, d).astype(BF16)
    return pl.pallas_call(
        _moe_kernel,
        grid=(n // tm, N_GROUPS),
        in_specs=[pl.BlockSpec((tm, d), lambda i, g: (i, 0)),
                  pl.BlockSpec((tm, LANES), lambda i, g: (i, 0)),
                  pl.BlockSpec((tm, d), lambda i, g: (i, 0)),
                  pl.BlockSpec((1, d, 2 * width), lambda i, g: (g, 0, 0)),
                  pl.BlockSpec((1, width, d), lambda i, g: (g, 0, 0))],
        out_specs=pl.BlockSpec((tm, d), lambda i, g: (i, 0)),
        out_shape=jax.ShapeDtypeStruct((n, d), F32),
        compiler_params=_params(("parallel", "arbitrary")),
        name="moe",
    )(h3, comb, x2, wgu, wd)


def _layer(x, mem, positions, p):
    b, s, d = x.shape
    n = b * s
    assert s % (2 * KEY_CHUNK) == 0 and s % Q_TILE_MLA == 0
    topk = min(TOPK_MAX, s // 4)
    tm = min(256, s)
    qa, qi, ka, va, ki, iw, mq, mk, mv, sgd, sgm = _in_proj(
        x.reshape(n, d), positions.reshape(n, 1), p["attn_norm_g"], p["w_in"], p["dsa_q_norm_g"],
        p["dsa_k_norm_g"], p["mla_cq_norm_g"], p["mla_ckv_norm_g"], p["mla_w_uq"], p["mla_w_ukv"],
        p["mla_q_norm_g"], p["mla_k_norm_g"], tm)
    r3 = lambda a: a.reshape(b, s, a.shape[-1])
    oa = _dsa_attention(r3(qa), r3(qi), r3(iw), r3(ka), r3(va), r3(ki), topk)
    ob = _mla_attention(r3(mq), r3(mk), r3(mv))
    km, vm = _mem_kv(mem, p["mem_norm_g"], p["mem_w_kv"], p["mem_k_norm_g"])
    x2, h3, comb = _merge_mem_router(
        oa, ob, r3(sgd), r3(sgm), x, p["w_branch_dsa"], p["w_branch_mla"], p["w_out"], p["mem_x_norm_g"],
        p["mem_w_q"], p["mem_q_norm_g"], km, vm, p["mem_w_o"], p["moe_norm_g"], p["moe_w_group"],
        p["moe_w_expert"], p["moe_expert_bias"], min(512, s))
    out = _moe(h3.reshape(n, d), comb.reshape(n, LANES), x2.reshape(n, d), p["moe_w_gate"], p["moe_w_up"],
               p["moe_w_down"], min(1024, n))
    return out.reshape(b, s, d)


_PARAM_NAMES = ("attn_norm_g", "w_in", "dsa_q_norm_g", "dsa_k_norm_g", "mla_cq_norm_g", "mla_ckv_norm_g",
                "mla_w_uq", "mla_w_ukv", "mla_q_norm_g", "mla_k_norm_g", "w_branch_dsa", "w_branch_mla", "w_out",
                "mem_x_norm_g", "mem_norm_g", "mem_w_q", "mem_w_kv", "mem_q_norm_g", "mem_k_norm_g", "mem_w_o",
                "moe_norm_g", "moe_w_group", "moe_w_expert", "moe_expert_bias", "moe_w_gate", "moe_w_up",
                "moe_w_down")


def kernel(x, mem, positions, attn_norm_g, w_in, dsa_q_norm_g, dsa_k_norm_g, mla_cq_norm_g, mla_ckv_norm_g, mla_w_uq, mla_w_ukv, mla_q_norm_g, mla_k_norm_g, w_branch_dsa, w_branch_mla, w_out, mem_x_norm_g, mem_norm_g, mem_w_q, mem_w_kv, mem_q_norm_g, mem_k_norm_g, mem_w_o, moe_norm_g, moe_w_group, moe_w_expert, moe_expert_bias, moe_w_gate, moe_w_up, moe_w_down):
    stacked = (attn_norm_g, w_in, dsa_q_norm_g, dsa_k_norm_g, mla_cq_norm_g, mla_ckv_norm_g, mla_w_uq, mla_w_ukv,
               mla_q_norm_g, mla_k_norm_g, w_branch_dsa, w_branch_mla, w_out, mem_x_norm_g, mem_norm_g, mem_w_q,
               mem_w_kv, mem_q_norm_g, mem_k_norm_g, mem_w_o, moe_norm_g, moe_w_group, moe_w_expert,
               moe_expert_bias, moe_w_gate, moe_w_up, moe_w_down)
    for layer in range(attn_norm_g.shape[0]):
        p = {name: arr[layer] for name, arr in zip(_PARAM_NAMES, stacked)}
        x = _layer(x, mem, positions, p)
    return x
```

```python
import functools

import jax
import jax.numpy as jnp
from jax import lax
from jax.experimental import pallas as pl
from jax.experimental.pallas import tpu as pltpu

F32 = jnp.float32
BF16 = jnp.bfloat16

LANES = 128
SUBLANES = 8
VMEM_LIMIT = 56 * 1024 * 1024

ROPE_THETA = 10000.0
EPS = 1e-6
DSA_HEADS = 8
DSA_HEAD_DIM = 64
IDX_HEADS = 8
IDX_DIM = 64
TOPK_MAX = 256
MLA_HEADS = 8
MLA_NOPE = 64
MLA_ROPE = 32
MLA_QK = MLA_NOPE + MLA_ROPE
MLA_V = 64
MLA_Q_RANK = 256
MLA_KV_RANK = 128
MEM_HEADS = 4
MEM_HEAD_DIM = 64
N_GROUPS = 4
EXPERTS_PER_GROUP = 4
N_EXPERTS = N_GROUPS * EXPERTS_PER_GROUP
D_EXPERT = 256

NEG_BIG = -1e30
KEY_CHUNK = 256
Q_TILE_DSA = 256
Q_TILE_MLA = 512
MLA_HEAD_GROUP = 4
LOG2_E = 1.4426950408889634
BISECT_ITERS = 16


def _const_spec(shape):
    nd = len(shape)
    return pl.BlockSpec(shape, lambda *_: (0,) * nd)


def _params(sem):
    return pltpu.CompilerParams(dimension_semantics=sem, vmem_limit_bytes=VMEM_LIMIT)


def _rms(x, g, n):
    ms = jnp.sum(x * x, axis=-1, keepdims=True) * (1.0 / n)
    return (x * lax.rsqrt(ms + EPS)) * g


def _rope(x, cos, sin_signed, lo_mask, half):
    fwd = pltpu.roll(x, LANES - half, 1)
    bwd = pltpu.roll(x, half, 1)
    return x * cos + jnp.where(lo_mask, fwd, bwd) * sin_signed


def _in_proj_kernel(x_ref, pos_ref, g_ref, wqa_ref, wqi_ref, wsm_ref, wcq_ref, wg_ref, wuq_ref, wuk_ref,
                    wuv_ref, qag_ref, kag_ref, cqg_ref, ckvg_ref, mqg_ref, mkg_ref, fd_ref, fm_ref,
                    qa_o, qi_o, ka_o, va_o, ki_o, iw_o, mq_o, mk_o, mv_o, sgd_o, sgm_o):
    x = x_ref[...]
    d_model = x.shape[-1]
    h = _rms(x, g_ref[...], d_model).astype(BF16)

    pos = pos_ref[...].astype(F32)
    lane = lax.broadcasted_iota(jnp.int32, (x.shape[0], LANES), 1)
    ang_d = pos * fd_ref[...]
    lo_d = lane < DSA_HEAD_DIM // 2
    cos_d = jnp.cos(ang_d)
    sin_d = jnp.sin(ang_d)
    sin_d = jnp.where(lo_d, -sin_d, sin_d)
    ang_m = pos * fm_ref[...]
    lo_m = lane < MLA_NOPE + MLA_ROPE // 2
    cos_m = jnp.cos(ang_m)
    sin_m = jnp.sin(ang_m)
    sin_m = jnp.where(lo_m, -sin_m, sin_m)
    rope_d = functools.partial(_rope, cos=cos_d, sin_signed=sin_d, lo_mask=lo_d, half=DSA_HEAD_DIM // 2)
    rope_m = functools.partial(_rope, cos=cos_m, sin_signed=sin_m, lo_mask=lo_m, half=MLA_ROPE // 2)

    def dot(a, w_ref):
        return jnp.dot(a, w_ref[...], preferred_element_type=F32)

    att_scale = DSA_HEAD_DIM ** -0.5 * LOG2_E
    idx_scale = IDX_DIM ** -0.5 * IDX_HEADS ** -0.5
    mla_scale = MLA_QK ** -0.5 * LOG2_E

    qa = dot(h, wqa_ref)
    for hd in range(DSA_HEADS):
        sl = slice(hd * LANES, (hd + 1) * LANES)
        y = rope_d(_rms(qa[:, sl], qag_ref[...], DSA_HEAD_DIM))
        qa_o[:, sl] = (y * att_scale).astype(BF16)

    qi = dot(h, wqi_ref)
    for hd in range(IDX_HEADS):
        sl = slice(hd * LANES, (hd + 1) * LANES)
        qi_o[:, sl] = rope_d(qi[:, sl]).astype(BF16)

    sm = dot(h, wsm_ref)
    ka_o[...] = rope_d(_rms(sm[:, 0:LANES], kag_ref[...], DSA_HEAD_DIM)).astype(BF16)
    va_o[...] = (sm[:, LANES:2 * LANES] + jnp.where(lane[0:1] == DSA_HEAD_DIM, 1.0, 0.0)).astype(BF16)
    ki_o[...] = rope_d(sm[:, 2 * LANES:3 * LANES]).astype(BF16)
    iw_o[...] = sm[:, 3 * LANES:4 * LANES] * idx_scale
    ckv = _rms(sm[:, 4 * LANES:5 * LANES], ckvg_ref[...], MLA_KV_RANK).astype(BF16)
    kpe = sm[:, 5 * LANES:6 * LANES]

    kn = dot(ckv, wuk_ref)
    for hd in range(MLA_HEADS):
        sl = slice(hd * LANES, (hd + 1) * LANES)
        y = rope_m(_rms(kn[:, sl] + kpe, mkg_ref[...], MLA_QK))
        mk_o[:, sl] = y.astype(BF16)
    wide_lane = lax.broadcasted_iota(jnp.int32, (1, MLA_HEADS * LANES), 1)
    ones_col = jnp.where(wide_lane % LANES == MLA_V, 1.0, 0.0)
    mv_o[...] = (dot(ckv, wuv_ref) + ones_col).astype(BF16)

    cq = _rms(dot(h, wcq_ref), cqg_ref[...], MLA_Q_RANK).astype(BF16)
    qb = dot(cq, wuq_ref)
    for hd in range(MLA_HEADS):
        sl = slice(hd * LANES, (hd + 1) * LANES)
        y = rope_m(_rms(qb[:, sl], mqg_ref[...], MLA_QK))
        mq_o[:, sl] = (y * mla_scale).astype(BF16)

    gates = jax.nn.sigmoid(dot(h, wg_ref))
    sgd_o[...] = gates[:, :d_model].astype(BF16)
    sgm_o[...] = gates[:, d_model:].astype(BF16)


def _pad_heads(w, heads, dim):
    k = w.shape[0]
    w = w.reshape(k, heads, dim)
    w = jnp.pad(w, ((0, 0), (0, 0), (0, LANES - dim)))
    return w.reshape(k, heads * LANES)


def _pad_cols(w, width=LANES, offset=0):
    return jnp.pad(w, ((0, 0), (offset, width - offset - w.shape[1])))


def _in_proj(x2, pos2, attn_g, w_in, dsa_q_g, dsa_k_g, cq_g, ckv_g, w_uq, w_ukv, mq_g, mk_g, tm):
    n, d = x2.shape
    sizes = (DSA_HEADS * DSA_HEAD_DIM, DSA_HEAD_DIM, DSA_HEAD_DIM, IDX_HEADS * IDX_DIM, IDX_DIM, IDX_HEADS,
             MLA_Q_RANK, MLA_KV_RANK, MLA_ROPE, d, d)
    offs = [0]
    for s in sizes:
        offs.append(offs[-1] + s)
    seg = [w_in[:, offs[i]:offs[i + 1]] for i in range(len(sizes))]
    w_dq, w_dk, w_dv, w_iq, w_ik, w_iw, w_cq, w_ckv, w_kpe, w_gd, w_gm = seg

    wqa = _pad_heads(w_dq, DSA_HEADS, DSA_HEAD_DIM).astype(BF16)
    wqi = _pad_heads(w_iq, IDX_HEADS, IDX_DIM).astype(BF16)
    wsm = jnp.concatenate([_pad_cols(w_dk), _pad_cols(w_dv), _pad_cols(w_ik), _pad_cols(w_iw), w_ckv,
                           _pad_cols(w_kpe, offset=MLA_NOPE)], axis=1).astype(BF16)
    wcq = w_cq.astype(BF16)
    wg = jnp.concatenate([w_gd, w_gm], axis=1).astype(BF16)
    wuq = _pad_heads(w_uq, MLA_HEADS, MLA_QK).astype(BF16)
    ukv = w_ukv.reshape(MLA_KV_RANK, MLA_HEADS, MLA_NOPE + MLA_V)
    wuk = _pad_heads(ukv[:, :, :MLA_NOPE].reshape(MLA_KV_RANK, -1), MLA_HEADS, MLA_NOPE).astype(BF16)
    wuv = _pad_heads(ukv[:, :, MLA_NOPE:].reshape(MLA_KV_RANK, -1), MLA_HEADS, MLA_V).astype(BF16)

    row = lambda v: _pad_cols(v.reshape(1, -1).astype(F32), width=max(LANES, v.size))
    half_d = DSA_HEAD_DIM // 2
    inv_d = ROPE_THETA ** (-jnp.arange(half_d, dtype=F32) / half_d)
    fd = _pad_cols(jnp.concatenate([inv_d, inv_d]).reshape(1, -1))
    half_m = MLA_ROPE // 2
    inv_m = ROPE_THETA ** (-jnp.arange(half_m, dtype=F32) / half_m)
    fm = _pad_cols(jnp.concatenate([inv_m, inv_m]).reshape(1, -1), offset=MLA_NOPE)

    consts = [attn_g.reshape(1, d), wqa, wqi, wsm, wcq, wg, wuq, wuk, wuv, row(dsa_q_g), row(dsa_k_g),
              row(cq_g), row(ckv_g), row(mq_g), row(mk_g), fd, fm]
    wide = DSA_HEADS * LANES
    out_shapes = [
        jax.ShapeDtypeStruct((n, wide), BF16),
        jax.ShapeDtypeStruct((n, wide), BF16),
        jax.ShapeDtypeStruct((n, LANES), BF16),
        jax.ShapeDtypeStruct((n, LANES), BF16),
        jax.ShapeDtypeStruct((n, LANES), BF16),
        jax.ShapeDtypeStruct((n, LANES), F32),
        jax.ShapeDtypeStruct((n, wide), BF16),
        jax.ShapeDtypeStruct((n, wide), BF16),
        jax.ShapeDtypeStruct((n, wide), BF16),
        jax.ShapeDtypeStruct((n, d), BF16),
        jax.ShapeDtypeStruct((n, d), BF16),
    ]
    tile = lambda w: pl.BlockSpec((tm, w), lambda i: (i, 0))
    return pl.pallas_call(
        _in_proj_kernel,
        grid=(n // tm,),
        in_specs=[tile(d), tile(1)] + [_const_spec(c.shape) for c in consts],
        out_specs=[tile(s.shape[1]) for s in out_shapes],
        out_shape=out_shapes,
        compiler_params=_params(("parallel",)),
        name="in_proj",
    )(x2, pos2, *consts)


def _dsa_kernel(qa_ref, qi_ref, iw_ref, ka_ref, vt_ref, ki_ref, o_ref, score_ref, logit_ref, *, topk):
    i = pl.program_id(1)
    tq = qa_ref.shape[1]
    heads = DSA_HEADS
    npair = (i * tq + tq + 2 * KEY_CHUNK - 1) // (2 * KEY_CHUNK)
    kf = float(topk)
    shape = (KEY_CHUNK, tq)
    folded = (SUBLANES, tq)

    def fold(a, op):
        return op(a.reshape(KEY_CHUNK // SUBLANES, SUBLANES, tq), axis=0)

    krow = lax.broadcasted_iota(jnp.int32, shape, 0)
    qcol = i * tq + lax.broadcasted_iota(jnp.int32, shape, 1)
    contract_last = (((1,), (1,)), ((), ()))

    def pair_loop(body, init):
        return lax.fori_loop(0, npair, lambda j, carry: body(2 * j, 2 * j + 1, carry), init)

    qi = jnp.concatenate([qi_ref[0, :, hd * LANES:(hd + 1) * LANES] for hd in range(IDX_HEADS)], axis=0)
    iw = iw_ref[0]

    def score_body(c0, c1, carry):
        mx, mn = carry
        for c in (c0, c1):
            kc = ki_ref[0, pl.ds(pl.multiple_of(c * KEY_CHUNK, KEY_CHUNK), KEY_CHUNK), :]
            sc = None
            for hp in range(IDX_HEADS // 2):
                rel = lax.dot_general(kc, qi[2 * hp * tq:(2 * hp + 2) * tq], contract_last,
                                      preferred_element_type=F32)
                rel = jnp.maximum(rel, 0.0)
                part = rel[:, :tq] * iw[2 * hp:2 * hp + 1, :] + rel[:, tq:] * iw[2 * hp + 1:2 * hp + 2, :]
                sc = part if sc is None else sc + part
            causal = (krow + c * KEY_CHUNK) <= qcol
            masked = jnp.where(causal, sc, -jnp.inf)
            score_ref[c] = masked
            mx = jnp.maximum(mx, fold(masked, jnp.max))
            mn = jnp.minimum(mn, fold(jnp.where(causal, sc, jnp.inf), jnp.min))
        return mx, mn

    mx, mn = pair_loop(score_body, (jnp.full(folded, -jnp.inf, F32), jnp.full(folded, jnp.inf, F32)))
    hi = jnp.max(mx, axis=0, keepdims=True)
    lo = jnp.min(mn, axis=0, keepdims=True)

    def count_gt(t):
        def body(c0, c1, acc):
            for c in (c0, c1):
                acc = acc + fold(jnp.where(score_ref[c] > t, 1.0, 0.0), jnp.sum)
            return acc
        return jnp.sum(pair_loop(body, jnp.zeros(folded, F32)), axis=0, keepdims=True)

    def bisect(_, carry):
        lo, hi = carry
        mid = 0.5 * (lo + hi)
        below = count_gt(mid) < kf
        return jnp.where(below, lo, mid), jnp.where(below, mid, hi)

    lo, hi = lax.fori_loop(0, BISECT_ITERS, bisect, (lo, hi))

    def max_le(t):
        def body(c0, c1, acc):
            for c in (c0, c1):
                s = score_ref[c]
                acc = jnp.maximum(acc, fold(jnp.where(s <= t, s, -jnp.inf), jnp.max))
            return acc
        return jnp.max(pair_loop(body, jnp.full(folded, -jnp.inf, F32)), axis=0, keepdims=True)

    n_causal = (i * tq + lax.broadcasted_iota(jnp.int32, (1, tq), 1) + 1).astype(F32)
    small = n_causal <= kf

    def refine_cond(carry):
        return jnp.min(carry[2]) < 0.5

    def refine_body(carry):
        m, thr, done, thr_cnt = carry

        def body(c0, c1, acc):
            cnt, nxt = acc
            for c in (c0, c1):
                s = score_ref[c]
                cnt = cnt + fold(jnp.where(s >= m, 1.0, 0.0), jnp.sum)
                nxt = jnp.maximum(nxt, fold(jnp.where(s < m, s, -jnp.inf), jnp.max))
            return cnt, nxt

        cnt, nxt = pair_loop(body, (jnp.zeros(folded, F32), jnp.full(folded, -jnp.inf, F32)))
        cnt = jnp.sum(cnt, axis=0, keepdims=True)
        nxt = jnp.max(nxt, axis=0, keepdims=True)
        reached = cnt >= kf
        hit = jnp.logical_and(reached, done < 0.5)
        return nxt, jnp.where(hit, m, thr), jnp.where(reached, 1.0, done), jnp.where(hit, cnt, thr_cnt)

    init = (max_le(hi), jnp.full((1, tq), -jnp.inf, F32), jnp.where(small, 1.0, 0.0), jnp.zeros((1, tq), F32))
    _, thr, _, thr_cnt = lax.while_loop(refine_cond, refine_body, init)

    earlier = (lax.broadcasted_iota(jnp.int32, (KEY_CHUNK, KEY_CHUNK), 1)
               < lax.broadcasted_iota(jnp.int32, (KEY_CHUNK, KEY_CHUNK), 0)).astype(BF16)

    def mask_exact_k():
        def body(c0, c1, carry):
            for c in (c0, c1):
                s = score_ref[c]
                sel = jnp.logical_and(s >= thr, s > -jnp.inf)
                score_ref[c] = jnp.where(sel, 0.0, NEG_BIG)
            return carry
        pair_loop(body, 0)

    def mask_with_ties():
        quota = jnp.where(small, 0.0, kf - count_gt(thr))

        def body(c0, c1, seen):
            for c in (c0, c1):
                s = score_ref[c]
                eq = s == thr
                eqf = jnp.where(eq, 1.0, 0.0)
                before = jnp.dot(earlier, eqf.astype(BF16), preferred_element_type=F32) + seen
                sel = jnp.logical_or(s > thr, jnp.logical_and(eq, before < quota))
                seen = seen + jnp.sum(eqf, axis=0, keepdims=True)
                score_ref[c] = jnp.where(sel, 0.0, NEG_BIG)
            return seen
        pair_loop(body, jnp.zeros((1, tq), F32))

    surplus = jnp.max(jnp.where(small, 0.0, thr_cnt - kf)) > 0.5
    lax.cond(surplus, mask_with_ties, mask_exact_k)

    qa = jnp.concatenate([qa_ref[0, :, hd * LANES:(hd + 1) * LANES] for hd in range(heads)], axis=0)

    def qk_into(slot, c):
        kc = ka_ref[0, pl.ds(pl.multiple_of(c * KEY_CHUNK, KEY_CHUNK), KEY_CHUNK), :]
        logit_ref[slot] = lax.dot_general(kc, qa, contract_last, preferred_element_type=F32)

    def softmax_pv(slot, c, m_run, acc):
        bias = score_ref[c]
        vt = vt_ref[0, c]
        new_m, new_acc = [], []
        for hp in range(heads // 2):
            probs, alphas = [], []
            for hd in (2 * hp, 2 * hp + 1):
                sl = slice(hd * tq, (hd + 1) * tq)
                lh = logit_ref[slot, :, sl] + bias
                m_old = m_run[:, sl]
                m_new = jnp.maximum(m_old, jnp.max(lh, axis=0, keepdims=True))
                probs.append(jnp.exp2(lh - m_new).astype(BF16))
                alphas.append(jnp.exp2(m_old - m_new))
                new_m.append(m_new)
            pv = jnp.dot(vt, jnp.concatenate(probs, axis=1), preferred_element_type=F32)
            new_acc.append(jnp.concatenate(alphas, axis=1) * acc[:, 2 * hp * tq:(2 * hp + 2) * tq] + pv)
        return jnp.concatenate(new_m, axis=1), jnp.concatenate(new_acc, axis=1)

    def attn_body(c0, c1, carry):
        m_run, acc = carry
        qk_into(1, c1)
        m_run, acc = softmax_pv(0, c0, m_run, acc)
        qk_into(0, jnp.minimum(c0 + 2, 2 * npair - 1))
        return softmax_pv(1, c1, m_run, acc)

    qk_into(0, 0)
    init = (jnp.full((1, heads * tq), NEG_BIG, F32), jnp.zeros((LANES, heads * tq), F32))
    _, acc = pair_loop(attn_body, init)
    for hd in range(heads):
        blk = acc[:, hd * tq:(hd + 1) * tq]
        o_ref[0, hd] = (blk / blk[DSA_HEAD_DIM:DSA_HEAD_DIM + 1, :]).T.astype(o_ref.dtype)


def _dsa_attention(qa, qi, iw, ka, va, ki, topk):
    b, s, wide = qa.shape
    tq = Q_TILE_DSA
    nkc = s // KEY_CHUNK
    iw_t = jnp.swapaxes(iw[..., :IDX_HEADS], 1, 2)
    v_t = jnp.swapaxes(va.reshape(b, nkc, KEY_CHUNK, LANES), 2, 3)
    qspec = pl.BlockSpec((1, tq, wide), lambda bi, i: (bi, i, 0))
    kspec = pl.BlockSpec((1, s, LANES), lambda bi, i: (bi, 0, 0))
    return pl.pallas_call(
        functools.partial(_dsa_kernel, topk=topk),
        grid=(b, s // tq),
        in_specs=[qspec, qspec, pl.BlockSpec((1, IDX_HEADS, tq), lambda bi, i: (bi, 0, i)), kspec,
                  pl.BlockSpec((1, nkc, LANES, KEY_CHUNK), lambda bi, i: (bi, 0, 0, 0)), kspec],
        out_specs=pl.BlockSpec((1, DSA_HEADS, tq, LANES), lambda bi, i: (bi, 0, i, 0)),
        out_shape=jax.ShapeDtypeStruct((b, DSA_HEADS, s, LANES), BF16),
        scratch_shapes=[pltpu.VMEM((nkc, KEY_CHUNK, tq), F32),
                        pltpu.VMEM((2, KEY_CHUNK, DSA_HEADS * tq), F32)],
        compiler_params=_params(("parallel", "arbitrary")),
        name="dsa_attn",
    )(qa, qi, iw_t, ka, v_t, ki)


def _mla_kernel(q_ref, k_ref, v_ref, o_ref):
    i = pl.program_id(2)
    tq = q_ref.shape[1]
    group = q_ref.shape[2] // LANES
    qs = [q_ref[0, :, g * LANES:(g + 1) * LANES] for g in range(group)]
    keep = lax.broadcasted_iota(jnp.int32, (tq, tq), 1) <= lax.broadcasted_iota(jnp.int32, (tq, tq), 0)

    def step(k0, carry, masked):
        logits = [lax.dot_general(qs[g], k_ref[0, pl.ds(k0, tq), g * LANES:(g + 1) * LANES],
                                  (((1,), (1,)), ((), ())), preferred_element_type=F32) for g in range(group)]
        probs, alphas, out = [], [], []
        for g in range(group):
            m_run = carry[2 * g]
            logit = jnp.where(keep, logits[g], NEG_BIG) if masked else logits[g]
            m_new = jnp.maximum(m_run, jnp.max(logit, axis=-1, keepdims=True))
            p = jnp.exp2(logit - m_new)
            if masked:
                p = jnp.where(keep, p, 0.0)
            probs.append(p.astype(BF16))
            alphas.append(jnp.exp2(m_run - m_new))
            out.append(m_new)
        res = []
        for g in range(group):
            vc = v_ref[0, pl.ds(k0, tq), g * LANES:(g + 1) * LANES]
            acc = alphas[g] * carry[2 * g + 1] + jnp.dot(probs[g], vc, preferred_element_type=F32)
            res += [out[g], acc]
        return tuple(res)

    init = (jnp.full((tq, 1), NEG_BIG, F32), jnp.zeros((tq, LANES), F32)) * group
    carry = lax.fori_loop(0, i, lambda c, cr: step(pl.multiple_of(c * tq, tq), cr, False), init)
    carry = step(pl.multiple_of(i * tq, tq), carry, True)
    for g in range(group):
        acc = carry[2 * g + 1]
        o_ref[0, :, g * LANES:(g + 1) * LANES] = (acc / acc[:, MLA_V:MLA_V + 1]).astype(o_ref.dtype)


def _mla_attention(mq, mk, mv):
    b, s, wide = mq.shape
    tq = Q_TILE_MLA
    gw = MLA_HEAD_GROUP * LANES
    return pl.pallas_call(
        _mla_kernel,
        grid=(b, MLA_HEADS // MLA_HEAD_GROUP, s // tq),
        in_specs=[pl.BlockSpec((1, tq, gw), lambda bi, h, i: (bi, i, h)),
                  pl.BlockSpec((1, s, gw), lambda bi, h, i: (bi, 0, h)),
                  pl.BlockSpec((1, s, gw), lambda bi, h, i: (bi, 0, h))],
        out_specs=pl.BlockSpec((1, tq, gw), lambda bi, h, i: (bi, i, h)),
        out_shape=jax.ShapeDtypeStruct((b, s, wide), BF16),
        compiler_params=_params(("parallel", "parallel", "arbitrary")),
        name="mla_attn",
    )(mq, mk, mv)


def _mem_kv_kernel(mem_ref, g_ref, wkv_ref, kg_ref, k_o, v_o):
    m = mem_ref[0]
    hm = _rms(m, g_ref[...], m.shape[-1]).astype(BF16)
    kv = jnp.dot(hm, wkv_ref[...], preferred_element_type=F32)
    for hd in range(MEM_HEADS):
        sl = slice(hd * LANES, (hd + 1) * LANES)
        k_o[0, :, sl] = _rms(kv[:, sl], kg_ref[...], MEM_HEAD_DIM).astype(BF16)
    v_o[0] = kv[:, MEM_HEADS * LANES:].astype(BF16)


def _mem_kv(mem, mem_g, w_kv, k_g):
    b, m, d = mem.shape
    hw = MEM_HEADS * MEM_HEAD_DIM
    wkv = jnp.concatenate([_pad_heads(w_kv[:, :hw], MEM_HEADS, MEM_HEAD_DIM),
                           _pad_heads(w_kv[:, hw:], MEM_HEADS, MEM_HEAD_DIM)], axis=1).astype(BF16)
    consts = [mem_g.reshape(1, d), wkv, _pad_cols(k_g.reshape(1, -1))]
    wide = MEM_HEADS * LANES
    spec = pl.BlockSpec((1, m, wide), lambda bi: (bi, 0, 0))
    return pl.pallas_call(
        _mem_kv_kernel,
        grid=(b,),
        in_specs=[pl.BlockSpec((1, m, d), lambda bi: (bi, 0, 0))] + [_const_spec(c.shape) for c in consts],
        out_specs=[spec, spec],
        out_shape=[jax.ShapeDtypeStruct((b, m, wide), BF16)] * 2,
        compiler_params=_params(("parallel",)),
        name="mem_kv",
    )(mem, *consts)


def _split_dot(a, w):
    a_hi = a.astype(BF16)
    a_lo = (a - a_hi.astype(F32)).astype(BF16)
    w_hi = w.astype(BF16)
    w_lo = (w - w_hi.astype(F32)).astype(BF16)
    d = functools.partial(jnp.dot, preferred_element_type=F32)
    return d(a_hi, w_hi) + (d(a_lo, w_hi) + d(a_hi, w_lo))


def _merge_kernel(oa_ref, ob_ref, sgd_ref, sgm_ref, x_ref, wa_ref, wb_ref, wo_ref, mxg_ref, wq_ref, qg_ref,
                  km_ref, vm_ref, wmo_ref, moeg_ref, wr_ref, bias_ref, x2_o, h3_o, comb_o):
    d = functools.partial(jnp.dot, preferred_element_type=F32)
    oa = jnp.concatenate([oa_ref[0, hd] for hd in range(DSA_HEADS)], axis=-1)
    merged = (sgd_ref[0].astype(F32) * d(oa, wa_ref[...]) + sgm_ref[0].astype(F32) * d(ob_ref[0], wb_ref[...]))
    x1 = x_ref[0] + d(merged.astype(BF16), wo_ref[...])

    d_model = x1.shape[-1]
    h2 = _rms(x1, mxg_ref[...], d_model).astype(BF16)
    qm = d(h2, wq_ref[...])
    outs = []
    for hd in range(MEM_HEADS):
        sl = slice(hd * LANES, (hd + 1) * LANES)
        q = (_rms(qm[:, sl], qg_ref[...], MEM_HEAD_DIM) * MEM_HEAD_DIM ** -0.5).astype(BF16)
        logit = lax.dot_general(q, km_ref[0, :, sl], (((1,), (1,)), ((), ())), preferred_element_type=F32)
        logit = logit - jnp.max(logit, axis=-1, keepdims=True)
        p = jnp.exp(logit)
        pv = d(p.astype(BF16), vm_ref[0, :, sl])
        outs.append((pv / jnp.sum(p, axis=-1, keepdims=True)).astype(BF16))
    x2 = x1 + d(jnp.concatenate(outs, axis=-1), wmo_ref[...])
    x2_o[0] = x2

    h3 = _rms(x2, moeg_ref[...], d_model)
    h3_o[0] = h3.astype(BF16)
    logits = _split_dot(h3, wr_ref[...])
    tm = logits.shape[0]
    lane = lax.broadcasted_iota(jnp.int32, (tm, LANES), 1)
    big = jnp.int32(LANES)
    is_grp = jnp.logical_and(lane >= N_EXPERTS, lane < N_EXPERTS + N_GROUPS)
    glog = jnp.where(is_grp, logits, -jnp.inf)
    gmax = jnp.max(glog, axis=-1, keepdims=True)
    g_sel = jnp.min(jnp.where(glog == gmax, lane, big), axis=-1, keepdims=True) - N_EXPERTS
    p_g = 1.0 / jnp.sum(jnp.where(is_grp, jnp.exp(logits - gmax), 0.0), axis=-1, keepdims=True)
    aff = jax.nn.sigmoid(logits)
    in_grp = jnp.logical_and(lane >= g_sel * EXPERTS_PER_GROUP, lane < (g_sel + 1) * EXPERTS_PER_GROUP)
    val = jnp.where(in_grp, aff + bias_ref[...], -jnp.inf)
    m1 = jnp.max(val, axis=-1, keepdims=True)
    i1 = jnp.min(jnp.where(val == m1, lane, big), axis=-1, keepdims=True)
    val2 = jnp.where(lane == i1, -jnp.inf, val)
    m2 = jnp.max(val2, axis=-1, keepdims=True)
    i2 = jnp.min(jnp.where(val2 == m2, lane, big), axis=-1, keepdims=True)
    chosen = jnp.logical_or(lane == i1, lane == i2)
    a_sel = jnp.where(chosen, aff, 0.0)
    comb_o[0] = p_g * a_sel / jnp.sum(a_sel, axis=-1, keepdims=True)


def _merge_mem_router(oa, ob, sgd, sgm, x, w_br_dsa, w_br_mla, w_out, mem_x_g, mem_w_q, mem_q_g, km, vm,
                      mem_w_o, moe_g, w_group, w_expert, expert_bias, tm):
    b, s, d = x.shape
    wa = jnp.pad(w_br_dsa.reshape(DSA_HEADS, DSA_HEAD_DIM, d), ((0, 0), (0, LANES - DSA_HEAD_DIM), (0, 0)))
    wa = wa.reshape(DSA_HEADS * LANES, d).astype(BF16)
    wb = jnp.pad(w_br_mla.reshape(MLA_HEADS, MLA_V, d), ((0, 0), (0, LANES - MLA_V), (0, 0)))
    wb = wb.reshape(MLA_HEADS * LANES, d).astype(BF16)
    wq = _pad_heads(mem_w_q, MEM_HEADS, MEM_HEAD_DIM).astype(BF16)
    wmo = jnp.pad(mem_w_o.reshape(MEM_HEADS, MEM_HEAD_DIM, d), ((0, 0), (0, LANES - MEM_HEAD_DIM), (0, 0)))
    wmo = wmo.reshape(MEM_HEADS * LANES, d).astype(BF16)
    wr = _pad_cols(jnp.concatenate([w_expert, w_group], axis=1).astype(F32))
    bias = _pad_cols(expert_bias.reshape(1, -1).astype(F32))
    consts_a = [wa, wb, w_out.astype(BF16), mem_x_g.reshape(1, d), wq, _pad_cols(mem_q_g.reshape(1, -1))]
    consts_b = [wmo, moe_g.reshape(1, d), wr, bias]
    m = km.shape[1]
    mw = MEM_HEADS * LANES
    tok = lambda w: pl.BlockSpec((1, tm, w), lambda bi, i: (bi, i, 0))
    memspec = pl.BlockSpec((1, m, mw), lambda bi, i: (bi, 0, 0))
    return pl.pallas_call(
        _merge_kernel,
        grid=(b, s // tm),
        in_specs=[pl.BlockSpec((1, DSA_HEADS, tm, LANES), lambda bi, i: (bi, 0, i, 0)),
                  tok(MLA_HEADS * LANES), tok(d), tok(d), tok(d)]
                 + [_const_spec(c.shape) for c in consts_a] + [memspec, memspec]
                 + [_const_spec(c.shape) for c in consts_b],
        out_specs=[tok(d), tok(d), tok(LANES)],
        out_shape=[jax.ShapeDtypeStruct((b, s, d), F32), jax.ShapeDtypeStruct((b, s, d), BF16),
                   jax.ShapeDtypeStruct((b, s, LANES), F32)],
        compiler_params=_params(("parallel", "parallel")),
        name="merge_mem",
    )(oa, ob, sgd, sgm, x, *consts_a, km, vm, *consts_b)


def _moe_kernel(h_ref, comb_ref, x_ref, wgu_ref, wd_ref, o_ref):
    g = pl.program_id(1)
    width = EXPERTS_PER_GROUP * D_EXPERT
    gu = jnp.dot(h_ref[...], wgu_ref[0], preferred_element_type=F32)
    act = jax.nn.silu(gu[:, :width]) * gu[:, width:]
    comb = comb_ref[...]
    lane = lax.broadcasted_iota(jnp.int32, comb.shape, 1)
    parts = []
    for e in range(EXPERTS_PER_GROUP):
        c_e = jnp.sum(jnp.where(lane == g * EXPERTS_PER_GROUP + e, comb, 0.0), axis=-1, keepdims=True)
        parts.append((act[:, e * D_EXPERT:(e + 1) * D_EXPERT] * c_e).astype(BF16))
    contrib = jnp.dot(jnp.concatenate(parts, axis=1), wd_ref[0], preferred_element_type=F32)

    @pl.when(g == 0)
    def _():
        o_ref[...] = x_ref[...] + contrib

    @pl.when(g > 0)
    def _():
        o_ref[...] += contrib


def _moe(h3, comb, x2, w_gate, w_up, w_down, tm):
    n, d = x2.shape
    width = EXPERTS_PER_GROUP * D_EXPERT
    by_group = lambda w: jnp.moveaxis(w.reshape(N_GROUPS, EXPERTS_PER_GROUP, d, D_EXPERT), 1, 2).reshape(N_GROUPS, d, width)
    wgu = jnp.concatenate([by_group(w_gate), by_group(w_up)], axis=-1).astype(BF16)
    wd = w_down.reshape(N_GROUPS, width, d).astype(BF16)
    return pl.pallas_call(
        _moe_kernel,
        grid=(n // tm, N_GROUPS),
        in_specs=[pl.BlockSpec((tm, d), lambda i, g: (i, 0)),
                  pl.BlockSpec((tm, LANES), lambda i, g: (i, 0)),
                  pl.BlockSpec((tm, d), lambda i, g: (i, 0)),
                  pl.BlockSpec((1, d, 2 * width), lambda i, g: (g, 0, 0)),
                  pl.BlockSpec((1, width, d), lambda i, g: (g, 0, 0))],
        out_specs=pl.BlockSpec((tm, d), lambda i, g: (i, 0)),
        out_shape=jax.ShapeDtypeStruct((n, d), F32),
        compiler_params=_params(("parallel", "arbitrary")),
        name="moe",
    )(h3, comb, x2, wgu, wd)


def _layer(x, mem, positions, p):
    b, s, d = x.shape
    n = b * s
    assert s % (2 * KEY_CHUNK) == 0 and s % Q_TILE_MLA == 0
    topk = min(TOPK_MAX, s // 4)
    tm = min(256, s)
    qa, qi, ka, va, ki, iw, mq, mk, mv, sgd, sgm = _in_proj(
        x.reshape(n, d), positions.reshape(n, 1), p["attn_norm_g"], p["w_in"], p["dsa_q_norm_g"],
        p["dsa_k_norm_g"], p["mla_cq_norm_g"], p["mla_ckv_norm_g"], p["mla_w_uq"], p["mla_w_ukv"],
        p["mla_q_norm_g"], p["mla_k_norm_g"], tm)
    r3 = lambda a: a.reshape(b, s, a.shape[-1])
    oa = _dsa_attention(r3(qa), r3(qi), r3(iw), r3(ka), r3(va), r3(ki), topk)
    ob = _mla_attention(r3(mq), r3(mk), r3(mv))
    km, vm = _mem_kv(mem, p["mem_norm_g"], p["mem_w_kv"], p["mem_k_norm_g"])
    x2, h3, comb = _merge_mem_router(
        oa, ob, r3(sgd), r3(sgm), x, p["w_branch_dsa"], p["w_branch_mla"], p["w_out"], p["mem_x_norm_g"],
        p["mem_w_q"], p["mem_q_norm_g"], km, vm, p["mem_w_o"], p["moe_norm_g"], p["moe_w_group"],
        p["moe_w_expert"], p["moe_expert_bias"], min(512, s))
    out = _moe(h3.reshape(n, d), comb.reshape(n, LANES), x2.reshape(n, d), p["moe_w_gate"], p["moe_w_up"],
               p["moe_w_down"], min(1024, n))
    return out.reshape(b, s, d)


_PARAM_NAMES = ("attn_norm_g", "w_in", "dsa_q_norm_g", "dsa_k_norm_g", "mla_cq_norm_g", "mla_ckv_norm_g",
                "mla_w_uq", "mla_w_ukv", "mla_q_norm_g", "mla_k_norm_g", "w_branch_dsa", "w_branch_mla", "w_out",
                "mem_x_norm_g", "mem_norm_g", "mem_w_q", "mem_w_kv", "mem_q_norm_g", "mem_k_norm_g", "mem_w_o",
                "moe_norm_g", "moe_w_group", "moe_w_expert", "moe_expert_bias", "moe_w_gate", "moe_w_up",
                "moe_w_down")


def kernel(x, mem, positions, attn_norm_g, w_in, dsa_q_norm_g, dsa_k_norm_g, mla_cq_norm_g, mla_ckv_norm_g, mla_w_uq, mla_w_ukv, mla_q_norm_g, mla_k_norm_g, w_branch_dsa, w_branch_mla, w_out, mem_x_norm_g, mem_norm_g, mem_w_q, mem_w_kv, mem_q_norm_g, mem_k_norm_g, mem_w_o, moe_norm_g, moe_w_group, moe_w_expert, moe_expert_bias, moe_w_gate, moe_w_up, moe_w_down):
    stacked = (attn_norm_g, w_in, dsa_q_norm_g, dsa_k_norm_g, mla_cq_norm_g, mla_ckv_norm_g, mla_w_uq, mla_w_ukv,
               mla_q_norm_g, mla_k_norm_g, w_branch_dsa, w_branch_mla, w_out, mem_x_norm_g, mem_norm_g, mem_w_q,
               mem_w_kv, mem_q_norm_g, mem_k_norm_g, mem_w_o, moe_norm_g, moe_w_group, moe_w_expert,
               moe_expert_bias, moe_w_gate, moe_w_up, moe_w_down)
    for layer in range(attn_norm_g.shape[0]):
        p = {name: arr[layer] for name, arr in zip(_PARAM_NAMES, stacked)}
        x = _layer(x, mem, positions, p)
    return x
```

```python
import functools

import jax
import jax.numpy as jnp
from jax import lax
from jax.experimental import pallas as pl
from jax.experimental.pallas import tpu as pltpu

F32 = jnp.float32
BF16 = jnp.bfloat16

LANES = 128
SUBLANES = 8
VMEM_LIMIT = 56 * 1024 * 1024

ROPE_THETA = 10000.0
EPS = 1e-6
DSA_HEADS = 8
DSA_HEAD_DIM = 64
IDX_HEADS = 8
IDX_DIM = 64
TOPK_MAX = 256
MLA_HEADS = 8
MLA_NOPE = 64
MLA_ROPE = 32
MLA_QK = MLA_NOPE + MLA_ROPE
MLA_V = 64
MLA_Q_RANK = 256
MLA_KV_RANK = 128
MEM_HEADS = 4
MEM_HEAD_DIM = 64
N_GROUPS = 4
EXPERTS_PER_GROUP = 4
N_EXPERTS = N_GROUPS * EXPERTS_PER_GROUP
D_EXPERT = 256

NEG_BIG = -1e30
KEY_CHUNK = 256
Q_TILE_DSA = 256
DSA_V_ROWS = 80
MLA_V_ROWS = 80
Q_TILE_MLA = 512
MLA_HEAD_GROUP = 4
LOG2_E = 1.4426950408889634
BISECT_ITERS = 16


def _const_spec(shape):
    nd = len(shape)
    return pl.BlockSpec(shape, lambda *_: (0,) * nd)


def _params(sem):
    return pltpu.CompilerParams(dimension_semantics=sem, vmem_limit_bytes=VMEM_LIMIT)


def _rms(x, g, n):
    ms = jnp.sum(x * x, axis=-1, keepdims=True) * (1.0 / n)
    return (x * lax.rsqrt(ms + EPS)) * g


def _rope(x, cos, sin_signed, lo_mask, half):
    fwd = pltpu.roll(x, LANES - half, 1)
    bwd = pltpu.roll(x, half, 1)
    return x * cos + jnp.where(lo_mask, fwd, bwd) * sin_signed


def _in_proj_kernel(x_ref, pos_ref, g_ref, wqa_ref, wqi_ref, wsm_ref, wcq_ref, wg_ref, wuq_ref, wuk_ref,
                    wuv_ref, qag_ref, kag_ref, cqg_ref, ckvg_ref, mqg_ref, mkg_ref, fd_ref, fm_ref,
                    qa_o, qi_o, ka_o, va_o, ki_o, iw_o, mq_o, mk_o, mv_o, sgd_o, sgm_o):
    x = x_ref[...]
    d_model = x.shape[-1]
    h = _rms(x, g_ref[...], d_model).astype(BF16)

    pos = pos_ref[...].astype(F32)
    lane = lax.broadcasted_iota(jnp.int32, (x.shape[0], LANES), 1)
    ang_d = pos * fd_ref[...]
    lo_d = lane < DSA_HEAD_DIM // 2
    cos_d = jnp.cos(ang_d)
    sin_d = jnp.sin(ang_d)
    sin_d = jnp.where(lo_d, -sin_d, sin_d)
    ang_m = pos * fm_ref[...]
    lo_m = lane < MLA_NOPE + MLA_ROPE // 2
    cos_m = jnp.cos(ang_m)
    sin_m = jnp.sin(ang_m)
    sin_m = jnp.where(lo_m, -sin_m, sin_m)
    rope_d = functools.partial(_rope, cos=cos_d, sin_signed=sin_d, lo_mask=lo_d, half=DSA_HEAD_DIM // 2)
    rope_m = functools.partial(_rope, cos=cos_m, sin_signed=sin_m, lo_mask=lo_m, half=MLA_ROPE // 2)

    def dot(a, w_ref):
        return jnp.dot(a, w_ref[...], preferred_element_type=F32)

    att_scale = DSA_HEAD_DIM ** -0.5 * LOG2_E
    idx_scale = IDX_DIM ** -0.5 * IDX_HEADS ** -0.5
    mla_scale = MLA_QK ** -0.5 * LOG2_E

    qa = dot(h, wqa_ref)
    for hd in range(DSA_HEADS):
        sl = slice(hd * LANES, (hd + 1) * LANES)
        y = rope_d(_rms(qa[:, sl], qag_ref[...], DSA_HEAD_DIM))
        qa_o[:, sl] = (y * att_scale).astype(BF16)

    qi = dot(h, wqi_ref)
    for hd in range(IDX_HEADS):
        sl = slice(hd * LANES, (hd + 1) * LANES)
        qi_o[:, sl] = rope_d(qi[:, sl]).astype(BF16)

    sm = dot(h, wsm_ref)
    ka_o[...] = rope_d(_rms(sm[:, 0:LANES], kag_ref[...], DSA_HEAD_DIM)).astype(BF16)
    va_o[...] = (sm[:, LANES:2 * LANES] + jnp.where(lane[0:1] == DSA_HEAD_DIM, 1.0, 0.0)).astype(BF16)
    ki_o[...] = rope_d(sm[:, 2 * LANES:3 * LANES]).astype(BF16)
    iw_o[...] = sm[:, 3 * LANES:4 * LANES] * idx_scale
    ckv = _rms(sm[:, 4 * LANES:5 * LANES], ckvg_ref[...], MLA_KV_RANK).astype(BF16)
    kpe = sm[:, 5 * LANES:6 * LANES]

    kn = dot(ckv, wuk_ref)
    for hd in range(MLA_HEADS):
        sl = slice(hd * LANES, (hd + 1) * LANES)
        y = rope_m(_rms(kn[:, sl] + kpe, mkg_ref[...], MLA_QK))
        mk_o[:, sl] = y.astype(BF16)
    wide_lane = lax.broadcasted_iota(jnp.int32, (1, MLA_HEADS * LANES), 1)
    ones_col = jnp.where(wide_lane % LANES == MLA_V, 1.0, 0.0)
    mv_o[...] = (dot(ckv, wuv_ref) + ones_col).astype(BF16)

    cq = _rms(dot(h, wcq_ref), cqg_ref[...], MLA_Q_RANK).astype(BF16)
    qb = dot(cq, wuq_ref)
    for hd in range(MLA_HEADS):
        sl = slice(hd * LANES, (hd + 1) * LANES)
        y = rope_m(_rms(qb[:, sl], mqg_ref[...], MLA_QK))
        mq_o[:, sl] = (y * mla_scale).astype(BF16)

    gates = jax.nn.sigmoid(dot(h, wg_ref))
    sgd_o[...] = gates[:, :d_model].astype(BF16)
    sgm_o[...] = gates[:, d_model:].astype(BF16)


def _pad_heads(w, heads, dim):
    k = w.shape[0]
    w = w.reshape(k, heads, dim)
    w = jnp.pad(w, ((0, 0), (0, 0), (0, LANES - dim)))
    return w.reshape(k, heads * LANES)


def _pad_cols(w, width=LANES, offset=0):
    return jnp.pad(w, ((0, 0), (offset, width - offset - w.shape[1])))


def _in_proj(x2, pos2, attn_g, w_in, dsa_q_g, dsa_k_g, cq_g, ckv_g, w_uq, w_ukv, mq_g, mk_g, tm):
    n, d = x2.shape
    sizes = (DSA_HEADS * DSA_HEAD_DIM, DSA_HEAD_DIM, DSA_HEAD_DIM, IDX_HEADS * IDX_DIM, IDX_DIM, IDX_HEADS,
             MLA_Q_RANK, MLA_KV_RANK, MLA_ROPE, d, d)
    offs = [0]
    for s in sizes:
        offs.append(offs[-1] + s)
    seg = [w_in[:, offs[i]:offs[i + 1]] for i in range(len(sizes))]
    w_dq, w_dk, w_dv, w_iq, w_ik, w_iw, w_cq, w_ckv, w_kpe, w_gd, w_gm = seg

    wqa = _pad_heads(w_dq, DSA_HEADS, DSA_HEAD_DIM).astype(BF16)
    wqi = _pad_heads(w_iq, IDX_HEADS, IDX_DIM).astype(BF16)
    wsm = jnp.concatenate([_pad_cols(w_dk), _pad_cols(w_dv), _pad_cols(w_ik), _pad_cols(w_iw), w_ckv,
                           _pad_cols(w_kpe, offset=MLA_NOPE)], axis=1).astype(BF16)
    wcq = w_cq.astype(BF16)
    wg = jnp.concatenate([w_gd, w_gm], axis=1).astype(BF16)
    wuq = _pad_heads(w_uq, MLA_HEADS, MLA_QK).astype(BF16)
    ukv = w_ukv.reshape(MLA_KV_RANK, MLA_HEADS, MLA_NOPE + MLA_V)
    wuk = _pad_heads(ukv[:, :, :MLA_NOPE].reshape(MLA_KV_RANK, -1), MLA_HEADS, MLA_NOPE).astype(BF16)
    wuv = _pad_heads(ukv[:, :, MLA_NOPE:].reshape(MLA_KV_RANK, -1), MLA_HEADS, MLA_V).astype(BF16)

    row = lambda v: _pad_cols(v.reshape(1, -1).astype(F32), width=max(LANES, v.size))
    half_d = DSA_HEAD_DIM // 2
    inv_d = ROPE_THETA ** (-jnp.arange(half_d, dtype=F32) / half_d)
    fd = _pad_cols(jnp.concatenate([inv_d, inv_d]).reshape(1, -1))
    half_m = MLA_ROPE // 2
    inv_m = ROPE_THETA ** (-jnp.arange(half_m, dtype=F32) / half_m)
    fm = _pad_cols(jnp.concatenate([inv_m, inv_m]).reshape(1, -1), offset=MLA_NOPE)

    consts = [attn_g.reshape(1, d), wqa, wqi, wsm, wcq, wg, wuq, wuk, wuv, row(dsa_q_g), row(dsa_k_g),
              row(cq_g), row(ckv_g), row(mq_g), row(mk_g), fd, fm]
    wide = DSA_HEADS * LANES
    out_shapes = [
        jax.ShapeDtypeStruct((n, wide), BF16),
        jax.ShapeDtypeStruct((n, wide), BF16),
        jax.ShapeDtypeStruct((n, LANES), BF16),
        jax.ShapeDtypeStruct((n, LANES), BF16),
        jax.ShapeDtypeStruct((n, LANES), BF16),
        jax.ShapeDtypeStruct((n, LANES), F32),
        jax.ShapeDtypeStruct((n, wide), BF16),
        jax.ShapeDtypeStruct((n, wide), BF16),
        jax.ShapeDtypeStruct((n, wide), BF16),
        jax.ShapeDtypeStruct((n, d), BF16),
        jax.ShapeDtypeStruct((n, d), BF16),
    ]
    tile = lambda w: pl.BlockSpec((tm, w), lambda i: (i, 0))
    return pl.pallas_call(
        _in_proj_kernel,
        grid=(n // tm,),
        in_specs=[tile(d), tile(1)] + [_const_spec(c.shape) for c in consts],
        out_specs=[tile(s.shape[1]) for s in out_shapes],
        out_shape=out_shapes,
        compiler_params=_params(("parallel",)),
        name="in_proj",
    )(x2, pos2, *consts)


def _dsa_kernel(qa_ref, qi_ref, iw_ref, ka_ref, vt_ref, ki_ref, o_ref, score_ref, logit_ref, *, topk):
    i = pl.program_id(1)
    tq = qa_ref.shape[1]
    heads = DSA_HEADS
    npair = (i * tq + tq + 2 * KEY_CHUNK - 1) // (2 * KEY_CHUNK)
    kf = float(topk)
    shape = (KEY_CHUNK, tq)
    folded = (SUBLANES, tq)

    def fold(a, op):
        return op(a.reshape(KEY_CHUNK // SUBLANES, SUBLANES, tq), axis=0)

    krow = lax.broadcasted_iota(jnp.int32, shape, 0)
    qcol = i * tq + lax.broadcasted_iota(jnp.int32, shape, 1)
    contract_last = (((1,), (1,)), ((), ()))

    def pair_loop(body, init):
        return lax.fori_loop(0, npair, lambda j, carry: body(2 * j, 2 * j + 1, carry), init)

    qi = jnp.concatenate([qi_ref[0, :, hd * LANES:(hd + 1) * LANES] for hd in range(IDX_HEADS)], axis=0)
    iw = iw_ref[0]

    def score_body(c0, c1, carry):
        mx, mn = carry
        for c in (c0, c1):
            kc = ki_ref[0, pl.ds(pl.multiple_of(c * KEY_CHUNK, KEY_CHUNK), KEY_CHUNK), :]
            sc = None
            for hp in range(IDX_HEADS // 2):
                rel = lax.dot_general(kc, qi[2 * hp * tq:(2 * hp + 2) * tq], contract_last,
                                      preferred_element_type=F32)
                rel = jnp.maximum(rel, 0.0)
                part = rel[:, :tq] * iw[2 * hp:2 * hp + 1, :] + rel[:, tq:] * iw[2 * hp + 1:2 * hp + 2, :]
                sc = part if sc is None else sc + part
            causal = (krow + c * KEY_CHUNK) <= qcol
            masked = jnp.where(causal, sc, -jnp.inf)
            score_ref[c] = masked
            mx = jnp.maximum(mx, fold(masked, jnp.max))
            mn = jnp.minimum(mn, fold(jnp.where(causal, sc, jnp.inf), jnp.min))
        return mx, mn

    mx, mn = pair_loop(score_body, (jnp.full(folded, -jnp.inf, F32), jnp.full(folded, jnp.inf, F32)))
    hi = jnp.max(mx, axis=0, keepdims=True)
    lo = jnp.min(mn, axis=0, keepdims=True)

    def count_gt(t):
        def body(c0, c1, acc):
            for c in (c0, c1):
                acc = acc + fold(jnp.where(score_ref[c] > t, 1.0, 0.0), jnp.sum)
            return acc
        return jnp.sum(pair_loop(body, jnp.zeros(folded, F32)), axis=0, keepdims=True)

    def bisect(_, carry):
        lo, hi = carry
        mid = 0.5 * (lo + hi)
        below = count_gt(mid) < kf
        return jnp.where(below, lo, mid), jnp.where(below, mid, hi)

    lo, hi = lax.fori_loop(0, BISECT_ITERS, bisect, (lo, hi))

    def max_le(t):
        def body(c0, c1, acc):
            for c in (c0, c1):
                s = score_ref[c]
                acc = jnp.maximum(acc, fold(jnp.where(s <= t, s, -jnp.inf), jnp.max))
            return acc
        return jnp.max(pair_loop(body, jnp.full(folded, -jnp.inf, F32)), axis=0, keepdims=True)

    n_causal = (i * tq + lax.broadcasted_iota(jnp.int32, (1, tq), 1) + 1).astype(F32)
    small = n_causal <= kf

    def refine_cond(carry):
        return jnp.min(carry[2]) < 0.5

    def refine_body(carry):
        m, thr, done, thr_cnt = carry

        def body(c0, c1, acc):
            cnt, nxt = acc
            for c in (c0, c1):
                s = score_ref[c]
                cnt = cnt + fold(jnp.where(s >= m, 1.0, 0.0), jnp.sum)
                nxt = jnp.maximum(nxt, fold(jnp.where(s < m, s, -jnp.inf), jnp.max))
            return cnt, nxt

        cnt, nxt = pair_loop(body, (jnp.zeros(folded, F32), jnp.full(folded, -jnp.inf, F32)))
        cnt = jnp.sum(cnt, axis=0, keepdims=True)
        nxt = jnp.max(nxt, axis=0, keepdims=True)
        reached = cnt >= kf
        hit = jnp.logical_and(reached, done < 0.5)
        return nxt, jnp.where(hit, m, thr), jnp.where(reached, 1.0, done), jnp.where(hit, cnt, thr_cnt)

    init = (max_le(hi), jnp.full((1, tq), -jnp.inf, F32), jnp.where(small, 1.0, 0.0), jnp.zeros((1, tq), F32))
    _, thr, _, thr_cnt = lax.while_loop(refine_cond, refine_body, init)

    earlier = (lax.broadcasted_iota(jnp.int32, (KEY_CHUNK, KEY_CHUNK), 1)
               < lax.broadcasted_iota(jnp.int32, (KEY_CHUNK, KEY_CHUNK), 0)).astype(BF16)

    def mask_exact_k():
        def body(c0, c1, carry):
            for c in (c0, c1):
                s = score_ref[c]
                sel = jnp.logical_and(s >= thr, s > -jnp.inf)
                score_ref[c] = jnp.where(sel, 0.0, NEG_BIG)
            return carry
        pair_loop(body, 0)

    def mask_with_ties():
        quota = jnp.where(small, 0.0, kf - count_gt(thr))

        def body(c0, c1, seen):
            for c in (c0, c1):
                s = score_ref[c]
                eq = s == thr
                eqf = jnp.where(eq, 1.0, 0.0)
                before = jnp.dot(earlier, eqf.astype(BF16), preferred_element_type=F32) + seen
                sel = jnp.logical_or(s > thr, jnp.logical_and(eq, before < quota))
                seen = seen + jnp.sum(eqf, axis=0, keepdims=True)
                score_ref[c] = jnp.where(sel, 0.0, NEG_BIG)
            return seen
        pair_loop(body, jnp.zeros((1, tq), F32))

    surplus = jnp.max(jnp.where(small, 0.0, thr_cnt - kf)) > 0.5
    lax.cond(surplus, mask_with_ties, mask_exact_k)

    qa = jnp.concatenate([qa_ref[0, :, hd * LANES:(hd + 1) * LANES] for hd in range(heads)], axis=0)

    def qk_into(slot, c):
        kc = ka_ref[0, pl.ds(pl.multiple_of(c * KEY_CHUNK, KEY_CHUNK), KEY_CHUNK), :]
        logit_ref[slot] = lax.dot_general(kc, qa, contract_last, preferred_element_type=F32)

    def softmax_pv(slot, c, m_run, acc):
        bias = score_ref[c]
        vt = vt_ref[0, c]
        new_m, new_acc = [], []
        for hp in range(heads // 2):
            probs, alphas = [], []
            for hd in (2 * hp, 2 * hp + 1):
                sl = slice(hd * tq, (hd + 1) * tq)
                lh = logit_ref[slot, :, sl] + bias
                m_old = m_run[:, sl]
                m_new = jnp.maximum(m_old, jnp.max(lh, axis=0, keepdims=True))
                probs.append(jnp.exp2(lh - m_new).astype(BF16))
                alphas.append(jnp.exp2(m_old - m_new))
                new_m.append(m_new)
            pv = jnp.dot(vt, jnp.concatenate(probs, axis=1), preferred_element_type=F32)
            new_acc.append(jnp.concatenate(alphas, axis=1) * acc[:, 2 * hp * tq:(2 * hp + 2) * tq] + pv)
        return jnp.concatenate(new_m, axis=1), jnp.concatenate(new_acc, axis=1)

    def attn_body(c0, c1, carry):
        m_run, acc = carry
        qk_into(1, c1)
        m_run, acc = softmax_pv(0, c0, m_run, acc)
        qk_into(0, jnp.minimum(c0 + 2, 2 * npair - 1))
        return softmax_pv(1, c1, m_run, acc)

    qk_into(0, 0)
    d_rows = vt_ref.shape[2]
    init = (jnp.full((1, heads * tq), NEG_BIG, F32), jnp.zeros((d_rows, heads * tq), F32))
    _, acc = pair_loop(attn_body, init)
    pad_rows = jnp.zeros((LANES - d_rows, tq), F32)
    for hd in range(heads):
        blk = acc[:, hd * tq:(hd + 1) * tq]
        out_t = jnp.concatenate([blk / blk[DSA_HEAD_DIM:DSA_HEAD_DIM + 1, :], pad_rows], axis=0)
        o_ref[0, hd] = out_t.T.astype(o_ref.dtype)


def _dsa_attention(qa, qi, iw, ka, va, ki, topk):
    b, s, wide = qa.shape
    tq = Q_TILE_DSA
    nkc = s // KEY_CHUNK
    iw_t = jnp.swapaxes(iw[..., :IDX_HEADS], 1, 2)
    v_t = jnp.swapaxes(va.reshape(b, nkc, KEY_CHUNK, LANES), 2, 3)[:, :, :DSA_V_ROWS]
    qspec = pl.BlockSpec((1, tq, wide), lambda bi, i: (bi, i, 0))
    kspec = pl.BlockSpec((1, s, LANES), lambda bi, i: (bi, 0, 0))
    return pl.pallas_call(
        functools.partial(_dsa_kernel, topk=topk),
        grid=(b, s // tq),
        in_specs=[qspec, qspec, pl.BlockSpec((1, IDX_HEADS, tq), lambda bi, i: (bi, 0, i)), kspec,
                  pl.BlockSpec((1, nkc, DSA_V_ROWS, KEY_CHUNK), lambda bi, i: (bi, 0, 0, 0)), kspec],
        out_specs=pl.BlockSpec((1, DSA_HEADS, tq, LANES), lambda bi, i: (bi, 0, i, 0)),
        out_shape=jax.ShapeDtypeStruct((b, DSA_HEADS, s, LANES), BF16),
        scratch_shapes=[pltpu.VMEM((nkc, KEY_CHUNK, tq), F32),
                        pltpu.VMEM((2, KEY_CHUNK, DSA_HEADS * tq), F32)],
        compiler_params=_params(("parallel", "arbitrary")),
        name="dsa_attn",
    )(qa, qi, iw_t, ka, v_t, ki)


def _mla_kernel(q_ref, k_ref, vt_ref, o_ref):
    i = pl.program_id(2)
    tq = q_ref.shape[1]
    group = q_ref.shape[2] // LANES
    d_rows = vt_ref.shape[3]
    qs = [q_ref[0, :, g * LANES:(g + 1) * LANES] for g in range(group)]
    keep = lax.broadcasted_iota(jnp.int32, (tq, tq), 0) <= lax.broadcasted_iota(jnp.int32, (tq, tq), 1)
    contract_last = (((1,), (1,)), ((), ()))

    def step(c, carry, masked):
        k0 = pl.multiple_of(c * tq, tq)
        logits = [lax.dot_general(k_ref[0, pl.ds(k0, tq), g * LANES:(g + 1) * LANES], qs[g], contract_last,
                                  preferred_element_type=F32) for g in range(group)]
        probs, alphas, out = [], [], []
        for g in range(group):
            m_run = carry[2 * g]
            logit = jnp.where(keep, logits[g], NEG_BIG) if masked else logits[g]
            m_new = jnp.maximum(m_run, jnp.max(logit, axis=0, keepdims=True))
            p = jnp.exp2(logit - m_new)
            if masked:
                p = jnp.where(keep, p, 0.0)
            probs.append(p.astype(BF16))
            alphas.append(jnp.exp2(m_run - m_new))
            out.append(m_new)
        res = []
        for g in range(group):
            pv = jnp.dot(vt_ref[0, g, c], probs[g], preferred_element_type=F32)
            res += [out[g], alphas[g] * carry[2 * g + 1] + pv]
        return tuple(res)

    init = (jnp.full((1, tq), NEG_BIG, F32), jnp.zeros((d_rows, tq), F32)) * group
    carry = lax.fori_loop(0, i, lambda c, cr: step(c, cr, False), init)
    carry = step(i, carry, True)
    pad_rows = jnp.zeros((LANES - d_rows, tq), F32)
    for g in range(group):
        acc = carry[2 * g + 1]
        out_t = jnp.concatenate([acc / acc[MLA_V:MLA_V + 1, :], pad_rows], axis=0)
        o_ref[0, :, g * LANES:(g + 1) * LANES] = out_t.T.astype(o_ref.dtype)


def _mla_attention(mq, mk, mv):
    b, s, wide = mq.shape
    tq = Q_TILE_MLA
    nkc = s // tq
    gw = MLA_HEAD_GROUP * LANES
    v_t = mv.reshape(b, nkc, tq, MLA_HEADS, LANES)[..., :MLA_V_ROWS]
    v_t = jnp.transpose(v_t, (0, 3, 1, 4, 2))
    return pl.pallas_call(
        _mla_kernel,
        grid=(b, MLA_HEADS // MLA_HEAD_GROUP, s // tq),
        in_specs=[pl.BlockSpec((1, tq, gw), lambda bi, h, i: (bi, i, h)),
                  pl.BlockSpec((1, s, gw), lambda bi, h, i: (bi, 0, h)),
                  pl.BlockSpec((1, MLA_HEAD_GROUP, nkc, MLA_V_ROWS, tq), lambda bi, h, i: (bi, h, 0, 0, 0))],
        out_specs=pl.BlockSpec((1, tq, gw), lambda bi, h, i: (bi, i, h)),
        out_shape=jax.ShapeDtypeStruct((b, s, wide), BF16),
        compiler_params=_params(("parallel", "parallel", "arbitrary")),
        name="mla_attn",
    )(mq, mk, v_t)


def _mem_kv_kernel(mem_ref, g_ref, wkv_ref, kg_ref, k_o, v_o):
    m = mem_ref[0]
    hm = _rms(m, g_ref[...], m.shape[-1]).astype(BF16)
    kv = jnp.dot(hm, wkv_ref[...], preferred_element_type=F32)
    for hd in range(MEM_HEADS):
        sl = slice(hd * LANES, (hd + 1) * LANES)
        k_o[0, :, sl] = _rms(kv[:, sl], kg_ref[...], MEM_HEAD_DIM).astype(BF16)
    v_o[0] = kv[:, MEM_HEADS * LANES:].astype(BF16)


def _mem_kv(mem, mem_g, w_kv, k_g):
    b, m, d = mem.shape
    hw = MEM_HEADS * MEM_HEAD_DIM
    wkv = jnp.concatenate([_pad_heads(w_kv[:, :hw], MEM_HEADS, MEM_HEAD_DIM),
                           _pad_heads(w_kv[:, hw:], MEM_HEADS, MEM_HEAD_DIM)], axis=1).astype(BF16)
    consts = [mem_g.reshape(1, d), wkv, _pad_cols(k_g.reshape(1, -1))]
    wide = MEM_HEADS * LANES
    spec = pl.BlockSpec((1, m, wide), lambda bi: (bi, 0, 0))
    return pl.pallas_call(
        _mem_kv_kernel,
        grid=(b,),
        in_specs=[pl.BlockSpec((1, m, d), lambda bi: (bi, 0, 0))] + [_const_spec(c.shape) for c in consts],
        out_specs=[spec, spec],
        out_shape=[jax.ShapeDtypeStruct((b, m, wide), BF16)] * 2,
        compiler_params=_params(("parallel",)),
        name="mem_kv",
    )(mem, *consts)


def _split_dot(a, w):
    a_hi = a.astype(BF16)
    a_lo = (a - a_hi.astype(F32)).astype(BF16)
    w_hi = w.astype(BF16)
    w_lo = (w - w_hi.astype(F32)).astype(BF16)
    d = functools.partial(jnp.dot, preferred_element_type=F32)
    return d(a_hi, w_hi) + (d(a_lo, w_hi) + d(a_hi, w_lo))


def _merge_kernel(oa_ref, ob_ref, sgd_ref, sgm_ref, x_ref, wa_ref, wb_ref, wo_ref, mxg_ref, wq_ref, qg_ref,
                  km_ref, vm_ref, wmo_ref, moeg_ref, wr_ref, bias_ref, x2_o, h3_o, comb_o):
    d = functools.partial(jnp.dot, preferred_element_type=F32)
    oa = jnp.concatenate([oa_ref[0, hd] for hd in range(DSA_HEADS)], axis=-1)
    merged = (sgd_ref[0].astype(F32) * d(oa, wa_ref[...]) + sgm_ref[0].astype(F32) * d(ob_ref[0], wb_ref[...]))
    x1 = x_ref[0] + d(merged.astype(BF16), wo_ref[...])

    d_model = x1.shape[-1]
    h2 = _rms(x1, mxg_ref[...], d_model).astype(BF16)
    qm = d(h2, wq_ref[...])
    outs = []
    for hd in range(MEM_HEADS):
        sl = slice(hd * LANES, (hd + 1) * LANES)
        q = (_rms(qm[:, sl], qg_ref[...], MEM_HEAD_DIM) * MEM_HEAD_DIM ** -0.5).astype(BF16)
        logit = lax.dot_general(q, km_ref[0, :, sl], (((1,), (1,)), ((), ())), preferred_element_type=F32)
        logit = logit - jnp.max(logit, axis=-1, keepdims=True)
        p = jnp.exp(logit)
        pv = d(p.astype(BF16), vm_ref[0, :, sl])
        outs.append((pv / jnp.sum(p, axis=-1, keepdims=True)).astype(BF16))
    x2 = x1 + d(jnp.concatenate(outs, axis=-1), wmo_ref[...])
    x2_o[0] = x2

    h3 = _rms(x2, moeg_ref[...], d_model)
    h3_o[0] = h3.astype(BF16)
    logits = _split_dot(h3, wr_ref[...])
    tm = logits.shape[0]
    lane = lax.broadcasted_iota(jnp.int32, (tm, LANES), 1)
    big = jnp.int32(LANES)
    is_grp = jnp.logical_and(lane >= N_EXPERTS, lane < N_EXPERTS + N_GROUPS)
    glog = jnp.where(is_grp, logits, -jnp.inf)
    gmax = jnp.max(glog, axis=-1, keepdims=True)
    g_sel = jnp.min(jnp.where(glog == gmax, lane, big), axis=-1, keepdims=True) - N_EXPERTS
    p_g = 1.0 / jnp.sum(jnp.where(is_grp, jnp.exp(logits - gmax), 0.0), axis=-1, keepdims=True)
    aff = jax.nn.sigmoid(logits)
    in_grp = jnp.logical_and(lane >= g_sel * EXPERTS_PER_GROUP, lane < (g_sel + 1) * EXPERTS_PER_GROUP)
    val = jnp.where(in_grp, aff + bias_ref[...], -jnp.inf)
    m1 = jnp.max(val, axis=-1, keepdims=True)
    i1 = jnp.min(jnp.where(val == m1, lane, big), axis=-1, keepdims=True)
    val2 = jnp.where(lane == i1, -jnp.inf, val)
    m2 = jnp.max(val2, axis=-1, keepdims=True)
    i2 = jnp.min(jnp.where(val2 == m2, lane, big), axis=-1, keepdims=True)
    chosen = jnp.logical_or(lane == i1, lane == i2)
    a_sel = jnp.where(chosen, aff, 0.0)
    comb_o[0] = p_g * a_sel / jnp.sum(a_sel, axis=-1, keepdims=True)


def _merge_mem_router(oa, ob, sgd, sgm, x, w_br_dsa, w_br_mla, w_out, mem_x_g, mem_w_q, mem_q_g, km, vm,
                      mem_w_o, moe_g, w_group, w_expert, expert_bias, tm):
    b, s, d = x.shape
    wa = jnp.pad(w_br_dsa.reshape(DSA_HEADS, DSA_HEAD_DIM, d), ((0, 0), (0, LANES - DSA_HEAD_DIM), (0, 0)))
    wa = wa.reshape(DSA_HEADS * LANES, d).astype(BF16)
    wb = jnp.pad(w_br_mla.reshape(MLA_HEADS, MLA_V, d), ((0, 0), (0, LANES - MLA_V), (0, 0)))
    wb = wb.reshape(MLA_HEADS * LANES, d).astype(BF16)
    wq = _pad_heads(mem_w_q, MEM_HEADS, MEM_HEAD_DIM).astype(BF16)
    wmo = jnp.pad(mem_w_o.reshape(MEM_HEADS, MEM_HEAD_DIM, d), ((0, 0), (0, LANES - MEM_HEAD_DIM), (0, 0)))
    wmo = wmo.reshape(MEM_HEADS * LANES, d).astype(BF16)
    wr = _pad_cols(jnp.concatenate([w_expert, w_group], axis=1).astype(F32))
    bias = _pad_cols(expert_bias.reshape(1, -1).astype(F32))
    consts_a = [wa, wb, w_out.astype(BF16), mem_x_g.reshape(1, d), wq, _pad_cols(mem_q_g.reshape(1, -1))]
    consts_b = [wmo, moe_g.reshape(1, d), wr, bias]
    m = km.shape[1]
    mw = MEM_HEADS * LANES
    tok = lambda w: pl.BlockSpec((1, tm, w), lambda bi, i: (bi, i, 0))
    memspec = pl.BlockSpec((1, m, mw), lambda bi, i: (bi, 0, 0))
    return pl.pallas_call(
        _merge_kernel,
        grid=(b, s // tm),
        in_specs=[pl.BlockSpec((1, DSA_HEADS, tm, LANES), lambda bi, i: (bi, 0, i, 0)),
                  tok(MLA_HEADS * LANES), tok(d), tok(d), tok(d)]
                 + [_const_spec(c.shape) for c in consts_a] + [memspec, memspec]
                 + [_const_spec(c.shape) for c in consts_b],
        out_specs=[tok(d), tok(d), tok(LANES)],
        out_shape=[jax.ShapeDtypeStruct((b, s, d), F32), jax.ShapeDtypeStruct((b, s, d), BF16),
                   jax.ShapeDtypeStruct((b, s, LANES), F32)],
        compiler_params=_params(("parallel", "parallel")),
        name="merge_mem",
    )(oa, ob, sgd, sgm, x, *consts_a, km, vm, *consts_b)


def _moe_kernel(h_ref, comb_ref, x_ref, wgu_ref, wd_ref, o_ref):
    g = pl.program_id(1)
    width = EXPERTS_PER_GROUP * D_EXPERT
    gu = jnp.dot(h_ref[...], wgu_ref[0], preferred_element_type=F32)
    act = jax.nn.silu(gu[:, :width]) * gu[:, width:]
    comb = comb_ref[...]
    lane = lax.broadcasted_iota(jnp.int32, comb.shape, 1)
    parts = []
    for e in range(EXPERTS_PER_GROUP):
        c_e = jnp.sum(jnp.where(lane == g * EXPERTS_PER_GROUP + e, comb, 0.0), axis=-1, keepdims=True)
        parts.append((act[:, e * D_EXPERT:(e + 1) * D_EXPERT] * c_e).astype(BF16))
    contrib = jnp.dot(jnp.concatenate(parts, axis=1), wd_ref[0], preferred_element_type=F32)

    @pl.when(g == 0)
    def _():
        o_ref[...] = x_ref[...] + contrib

    @pl.when(g > 0)
    def _():
        o_ref[...] += contrib


def _moe(h3, comb, x2, w_gate, w_up, w_down, tm):
    n, d = x2.shape
    width = EXPERTS_PER_GROUP * D_EXPERT
    by_group = lambda w: jnp.moveaxis(w.reshape(N_GROUPS, EXPERTS_PER_GROUP, d, D_EXPERT), 1, 2).reshape(N_GROUPS, d, width)
    wgu = jnp.concatenate([by_group(w_gate), by_group(w_up)], axis=-1).astype(BF16)
    wd = w_down.reshape(N_GROUPS, width, d).astype(BF16)
    return pl.pallas_call(
        _moe_kernel,
        grid=(n // tm, N_GROUPS),
        in_specs=[pl.BlockSpec((tm, d), lambda i, g: (i, 0)),
                  pl.BlockSpec((tm, LANES), lambda i, g: (i, 0)),
                  pl.BlockSpec((tm, d), lambda i, g: (i, 0)),
                  pl.BlockSpec((1, d, 2 * width), lambda i, g: (g, 0, 0)),
                  pl.BlockSpec((1, width, d), lambda i, g: (g, 0, 0))],
        out_specs=pl.BlockSpec((tm, d), lambda i, g: (i, 0)),
        out_shape=jax.ShapeDtypeStruct((n, d), F32),
        compiler_params=_params(("parallel", "arbitrary")),
        name="moe",
    )(h3, comb, x2, wgu, wd)


def _layer(x, mem, positions, p):
    b, s, d = x.shape
    n = b * s
    assert s % (2 * KEY_CHUNK) == 0 and s % Q_TILE_MLA == 0
    topk = min(TOPK_MAX, s // 4)
    tm = min(256, s)
    qa, qi, ka, va, ki, iw, mq, mk, mv, sgd, sgm = _in_proj(
        x.reshape(n, d), positions.reshape(n, 1), p["attn_norm_g"], p["w_in"], p["dsa_q_norm_g"],
        p["dsa_k_norm_g"], p["mla_cq_norm_g"], p["mla_ckv_norm_g"], p["mla_w_uq"], p["mla_w_ukv"],
        p["mla_q_norm_g"], p["mla_k_norm_g"], tm)
    r3 = lambda a: a.reshape(b, s, a.shape[-1])
    oa = _dsa_attention(r3(qa), r3(qi), r3(iw), r3(ka), r3(va), r3(ki), topk)
    ob = _mla_attention(r3(mq), r3(mk), r3(mv))
    km, vm = _mem_kv(mem, p["mem_norm_g"], p["mem_w_kv"], p["mem_k_norm_g"])
    x2, h3, comb = _merge_mem_router(
        oa, ob, r3(sgd), r3(sgm), x, p["w_branch_dsa"], p["w_branch_mla"], p["w_out"], p["mem_x_norm_g"],
        p["mem_w_q"], p["mem_q_norm_g"], km, vm, p["mem_w_o"], p["moe_norm_g"], p["moe_w_group"],
        p["moe_w_expert"], p["moe_expert_bias"], min(512, s))
    out = _moe(h3.reshape(n, d), comb.reshape(n, LANES), x2.reshape(n, d), p["moe_w_gate"], p["moe_w_up"],
               p["moe_w_down"], min(1024, n))
    return out.reshape(b, s, d)


_PARAM_NAMES = ("attn_norm_g", "w_in", "dsa_q_norm_g", "dsa_k_norm_g", "mla_cq_norm_g", "mla_ckv_norm_g",
                "mla_w_uq", "mla_w_ukv", "mla_q_norm_g", "mla_k_norm_g", "w_branch_dsa", "w_branch_mla", "w_out",
                "mem_x_norm_g", "mem_norm_g", "mem_w_q", "mem_w_kv", "mem_q_norm_g", "mem_k_norm_g", "mem_w_o",
                "moe_norm_g", "moe_w_group", "moe_w_expert", "moe_expert_bias", "moe_w_gate", "moe_w_up",
                "moe_w_down")


def kernel(x, mem, positions, attn_norm_g, w_in, dsa_q_norm_g, dsa_k_norm_g, mla_cq_norm_g, mla_ckv_norm_g, mla_w_uq, mla_w_ukv, mla_q_norm_g, mla_k_norm_g, w_branch_dsa, w_branch_mla, w_out, mem_x_norm_g, mem_norm_g, mem_w_q, mem_w_kv, mem_q_norm_g, mem_k_norm_g, mem_w_o, moe_norm_g, moe_w_group, moe_w_expert, moe_expert_bias, moe_w_gate, moe_w_up, moe_w_down):
    stacked = (attn_norm_g, w_in, dsa_q_norm_g, dsa_k_norm_g, mla_cq_norm_g, mla_ckv_norm_g, mla_w_uq, mla_w_ukv,
               mla_q_norm_g, mla_k_norm_g, w_branch_dsa, w_branch_mla, w_out, mem_x_norm_g, mem_norm_g, mem_w_q,
               mem_w_kv, mem_q_norm_g, mem_k_norm_g, mem_w_o, moe_norm_g, moe_w_group, moe_w_expert,
               moe_expert_bias, moe_w_gate, moe_w_up, moe_w_down)
    for layer in range(attn_norm_g.shape[0]):
        p = {name: arr[layer] for name, arr in zip(_PARAM_NAMES, stacked)}
        x = _layer(x, mem, positions, p)
    return x
```

```python
import functools

import jax
import jax.numpy as jnp
from jax import lax
from jax.experimental import pallas as pl
from jax.experimental.pallas import tpu as pltpu

F32 = jnp.float32
BF16 = jnp.bfloat16

LANES = 128
SUBLANES = 8
VMEM_LIMIT = 56 * 1024 * 1024

ROPE_THETA = 10000.0
EPS = 1e-6
DSA_HEADS = 8
DSA_HEAD_DIM = 64
IDX_HEADS = 8
IDX_DIM = 64
TOPK_MAX = 256
MLA_HEADS = 8
MLA_NOPE = 64
MLA_ROPE = 32
MLA_QK = MLA_NOPE + MLA_ROPE
MLA_V = 64
MLA_Q_RANK = 256
MLA_KV_RANK = 128
MEM_HEADS = 4
MEM_HEAD_DIM = 64
N_GROUPS = 4
EXPERTS_PER_GROUP = 4
N_EXPERTS = N_GROUPS * EXPERTS_PER_GROUP
D_EXPERT = 256

NEG_BIG = -1e30
KEY_CHUNK = 256
Q_TILE_DSA = 256
DSA_V_ROWS = 80
MLA_V_ROWS = 80
Q_TILE_MLA = 512
MLA_HEAD_GROUP = 4
LOG2_E = 1.4426950408889634
BISECT_ITERS = 16


def _const_spec(shape):
    nd = len(shape)
    return pl.BlockSpec(shape, lambda *_: (0,) * nd)


def _params(sem):
    return pltpu.CompilerParams(dimension_semantics=sem, vmem_limit_bytes=VMEM_LIMIT)


def _rms(x, g, n):
    ms = jnp.sum(x * x, axis=-1, keepdims=True) * (1.0 / n)
    return (x * lax.rsqrt(ms + EPS)) * g


def _rope(x, cos, sin_signed, lo_mask, half):
    fwd = pltpu.roll(x, LANES - half, 1)
    bwd = pltpu.roll(x, half, 1)
    return x * cos + jnp.where(lo_mask, fwd, bwd) * sin_signed


def _in_proj_kernel(x_ref, pos_ref, g_ref, wqa_ref, wqi_ref, wsm_ref, wcq_ref, wg_ref, wuq_ref, wuk_ref,
                    wuv_ref, qag_ref, kag_ref, cqg_ref, ckvg_ref, mqg_ref, mkg_ref, fd_ref, fm_ref,
                    qa_o, qi_o, ka_o, va_o, ki_o, iw_o, mq_o, mk_o, mv_o, sgd_o, sgm_o):
    x = x_ref[...]
    d_model = x.shape[-1]
    h = _rms(x, g_ref[...], d_model).astype(BF16)

    pos = pos_ref[...].astype(F32)
    lane = lax.broadcasted_iota(jnp.int32, (x.shape[0], LANES), 1)
    ang_d = pos * fd_ref[...]
    lo_d = lane < DSA_HEAD_DIM // 2
    cos_d = jnp.cos(ang_d)
    sin_d = jnp.sin(ang_d)
    sin_d = jnp.where(lo_d, -sin_d, sin_d)
    ang_m = pos * fm_ref[...]
    lo_m = lane < MLA_NOPE + MLA_ROPE // 2
    cos_m = jnp.cos(ang_m)
    sin_m = jnp.sin(ang_m)
    sin_m = jnp.where(lo_m, -sin_m, sin_m)
    rope_d = functools.partial(_rope, cos=cos_d, sin_signed=sin_d, lo_mask=lo_d, half=DSA_HEAD_DIM // 2)
    rope_m = functools.partial(_rope, cos=cos_m, sin_signed=sin_m, lo_mask=lo_m, half=MLA_ROPE // 2)

    def dot(a, w_ref):
        return jnp.dot(a, w_ref[...], preferred_element_type=F32)

    att_scale = DSA_HEAD_DIM ** -0.5 * LOG2_E
    idx_scale = IDX_DIM ** -0.5 * IDX_HEADS ** -0.5
    mla_scale = MLA_QK ** -0.5 * LOG2_E

    qa = dot(h, wqa_ref)
    for hd in range(DSA_HEADS):
        sl = slice(hd * LANES, (hd + 1) * LANES)
        y = rope_d(_rms(qa[:, sl], qag_ref[...], DSA_HEAD_DIM))
        qa_o[:, sl] = (y * att_scale).astype(BF16)

    qi = dot(h, wqi_ref)
    for hd in range(IDX_HEADS):
        sl = slice(hd * LANES, (hd + 1) * LANES)
        qi_o[:, sl] = rope_d(qi[:, sl]).astype(BF16)

    sm = dot(h, wsm_ref)
    ka_o[...] = rope_d(_rms(sm[:, 0:LANES], kag_ref[...], DSA_HEAD_DIM)).astype(BF16)
    va_o[...] = (sm[:, LANES:2 * LANES] + jnp.where(lane[0:1] == DSA_HEAD_DIM, 1.0, 0.0)).astype(BF16)
    ki_o[...] = rope_d(sm[:, 2 * LANES:3 * LANES]).astype(BF16)
    iw_o[...] = sm[:, 3 * LANES:4 * LANES] * idx_scale
    ckv = _rms(sm[:, 4 * LANES:5 * LANES], ckvg_ref[...], MLA_KV_RANK).astype(BF16)
    kpe = sm[:, 5 * LANES:6 * LANES]

    kn = dot(ckv, wuk_ref)
    for hd in range(MLA_HEADS):
        sl = slice(hd * LANES, (hd + 1) * LANES)
        y = rope_m(_rms(kn[:, sl] + kpe, mkg_ref[...], MLA_QK))
        mk_o[:, sl] = y.astype(BF16)
    wide_lane = lax.broadcasted_iota(jnp.int32, (1, MLA_HEADS * LANES), 1)
    ones_col = jnp.where(wide_lane % LANES == MLA_V, 1.0, 0.0)
    mv_o[...] = (dot(ckv, wuv_ref) + ones_col).astype(BF16)

    cq = _rms(dot(h, wcq_ref), cqg_ref[...], MLA_Q_RANK).astype(BF16)
    qb = dot(cq, wuq_ref)
    for hd in range(MLA_HEADS):
        sl = slice(hd * LANES, (hd + 1) * LANES)
        y = rope_m(_rms(qb[:, sl], mqg_ref[...], MLA_QK))
        mq_o[:, sl] = (y * mla_scale).astype(BF16)

    gates = jax.nn.sigmoid(dot(h, wg_ref))
    sgd_o[...] = gates[:, :d_model].astype(BF16)
    sgm_o[...] = gates[:, d_model:].astype(BF16)


def _pad_heads(w, heads, dim):
    k = w.shape[0]
    w = w.reshape(k, heads, dim)
    w = jnp.pad(w, ((0, 0), (0, 0), (0, LANES - dim)))
    return w.reshape(k, heads * LANES)


def _pad_cols(w, width=LANES, offset=0):
    return jnp.pad(w, ((0, 0), (offset, width - offset - w.shape[1])))


def _in_proj(x2, pos2, attn_g, w_in, dsa_q_g, dsa_k_g, cq_g, ckv_g, w_uq, w_ukv, mq_g, mk_g, tm):
    n, d = x2.shape
    sizes = (DSA_HEADS * DSA_HEAD_DIM, DSA_HEAD_DIM, DSA_HEAD_DIM, IDX_HEADS * IDX_DIM, IDX_DIM, IDX_HEADS,
             MLA_Q_RANK, MLA_KV_RANK, MLA_ROPE, d, d)
    offs = [0]
    for s in sizes:
        offs.append(offs[-1] + s)
    seg = [w_in[:, offs[i]:offs[i + 1]] for i in range(len(sizes))]
    w_dq, w_dk, w_dv, w_iq, w_ik, w_iw, w_cq, w_ckv, w_kpe, w_gd, w_gm = seg

    wqa = _pad_heads(w_dq, DSA_HEADS, DSA_HEAD_DIM).astype(BF16)
    wqi = _pad_heads(w_iq, IDX_HEADS, IDX_DIM).astype(BF16)
    wsm = jnp.concatenate([_pad_cols(w_dk), _pad_cols(w_dv), _pad_cols(w_ik), _pad_cols(w_iw), w_ckv,
                           _pad_cols(w_kpe, offset=MLA_NOPE)], axis=1).astype(BF16)
    wcq = w_cq.astype(BF16)
    wg = jnp.concatenate([w_gd, w_gm], axis=1).astype(BF16)
    wuq = _pad_heads(w_uq, MLA_HEADS, MLA_QK).astype(BF16)
    ukv = w_ukv.reshape(MLA_KV_RANK, MLA_HEADS, MLA_NOPE + MLA_V)
    wuk = _pad_heads(ukv[:, :, :MLA_NOPE].reshape(MLA_KV_RANK, -1), MLA_HEADS, MLA_NOPE).astype(BF16)
    wuv = _pad_heads(ukv[:, :, MLA_NOPE:].reshape(MLA_KV_RANK, -1), MLA_HEADS, MLA_V).astype(BF16)

    row = lambda v: _pad_cols(v.reshape(1, -1).astype(F32), width=max(LANES, v.size))
    half_d = DSA_HEAD_DIM // 2
    inv_d = ROPE_THETA ** (-jnp.arange(half_d, dtype=F32) / half_d)
    fd = _pad_cols(jnp.concatenate([inv_d, inv_d]).reshape(1, -1))
    half_m = MLA_ROPE // 2
    inv_m = ROPE_THETA ** (-jnp.arange(half_m, dtype=F32) / half_m)
    fm = _pad_cols(jnp.concatenate([inv_m, inv_m]).reshape(1, -1), offset=MLA_NOPE)

    consts = [attn_g.reshape(1, d), wqa, wqi, wsm, wcq, wg, wuq, wuk, wuv, row(dsa_q_g), row(dsa_k_g),
              row(cq_g), row(ckv_g), row(mq_g), row(mk_g), fd, fm]
    wide = DSA_HEADS * LANES
    out_shapes = [
        jax.ShapeDtypeStruct((n, wide), BF16),
        jax.ShapeDtypeStruct((n, wide), BF16),
        jax.ShapeDtypeStruct((n, LANES), BF16),
        jax.ShapeDtypeStruct((n, LANES), BF16),
        jax.ShapeDtypeStruct((n, LANES), BF16),
        jax.ShapeDtypeStruct((n, LANES), F32),
        jax.ShapeDtypeStruct((n, wide), BF16),
        jax.ShapeDtypeStruct((n, wide), BF16),
        jax.ShapeDtypeStruct((n, wide), BF16),
        jax.ShapeDtypeStruct((n, d), BF16),
        jax.ShapeDtypeStruct((n, d), BF16),
    ]
    tile = lambda w: pl.BlockSpec((tm, w), lambda i: (i, 0))
    return pl.pallas_call(
        _in_proj_kernel,
        grid=(n // tm,),
        in_specs=[tile(d), tile(1)] + [_const_spec(c.shape) for c in consts],
        out_specs=[tile(s.shape[1]) for s in out_shapes],
        out_shape=out_shapes,
        compiler_params=_params(("parallel",)),
        name="in_proj",
    )(x2, pos2, *consts)


def _dsa_kernel(qa_ref, qi_ref, iw_ref, ka_ref, vt_ref, ki_ref, o_ref, score_ref, logit_ref, *, topk):
    i = pl.program_id(1)
    tq = qa_ref.shape[1]
    heads = DSA_HEADS
    npair = (i * tq + tq + 2 * KEY_CHUNK - 1) // (2 * KEY_CHUNK)
    kf = float(topk)
    shape = (KEY_CHUNK, tq)
    folded = (SUBLANES, tq)

    def fold(a, op):
        return op(a.reshape(KEY_CHUNK // SUBLANES, SUBLANES, tq), axis=0)

    krow = lax.broadcasted_iota(jnp.int32, shape, 0)
    qcol = i * tq + lax.broadcasted_iota(jnp.int32, shape, 1)
    contract_last = (((1,), (1,)), ((), ()))

    def pair_loop(body, init):
        return lax.fori_loop(0, npair, lambda j, carry: body(2 * j, 2 * j + 1, carry), init)

    qi = jnp.concatenate([qi_ref[0, :, hd * LANES:(hd + 1) * LANES] for hd in range(IDX_HEADS)], axis=0)
    iw = iw_ref[0]

    def score_body(c0, c1, carry):
        mx, mn = carry
        for c in (c0, c1):
            kc = ki_ref[0, pl.ds(pl.multiple_of(c * KEY_CHUNK, KEY_CHUNK), KEY_CHUNK), :]
            sc = None
            for hp in range(IDX_HEADS // 2):
                rel = lax.dot_general(kc, qi[2 * hp * tq:(2 * hp + 2) * tq], contract_last,
                                      preferred_element_type=F32)
                rel = jnp.maximum(rel, 0.0)
                part = rel[:, :tq] * iw[2 * hp:2 * hp + 1, :] + rel[:, tq:] * iw[2 * hp + 1:2 * hp + 2, :]
                sc = part if sc is None else sc + part
            causal = (krow + c * KEY_CHUNK) <= qcol
            masked = jnp.where(causal, sc, -jnp.inf)
            score_ref[c] = masked
            mx = jnp.maximum(mx, fold(masked, jnp.max))
            mn = jnp.minimum(mn, fold(jnp.where(causal, sc, jnp.inf), jnp.min))
        return mx, mn

    mx, mn = pair_loop(score_body, (jnp.full(folded, -jnp.inf, F32), jnp.full(folded, jnp.inf, F32)))
    hi = jnp.max(mx, axis=0, keepdims=True)
    lo = jnp.min(mn, axis=0, keepdims=True)

    def count_gt(t):
        def body(c0, c1, acc):
            for c in (c0, c1):
                acc = acc + fold(jnp.where(score_ref[c] > t, 1.0, 0.0), jnp.sum)
            return acc
        return jnp.sum(pair_loop(body, jnp.zeros(folded, F32)), axis=0, keepdims=True)

    def bisect(_, carry):
        lo, hi = carry
        mid = 0.5 * (lo + hi)
        below = count_gt(mid) < kf
        return jnp.where(below, lo, mid), jnp.where(below, mid, hi)

    lo, hi = lax.fori_loop(0, BISECT_ITERS, bisect, (lo, hi))

    def max_le(t):
        def body(c0, c1, acc):
            for c in (c0, c1):
                s = score_ref[c]
                acc = jnp.maximum(acc, fold(jnp.where(s <= t, s, -jnp.inf), jnp.max))
            return acc
        return jnp.max(pair_loop(body, jnp.full(folded, -jnp.inf, F32)), axis=0, keepdims=True)

    n_causal = (i * tq + lax.broadcasted_iota(jnp.int32, (1, tq), 1) + 1).astype(F32)
    small = n_causal <= kf

    def refine_cond(carry):
        return jnp.min(carry[2]) < 0.5

    def refine_body(carry):
        m, thr, done, thr_cnt = carry

        def body(c0, c1, acc):
            cnt, nxt = acc
            for c in (c0, c1):
                s = score_ref[c]
                cnt = cnt + fold(jnp.where(s >= m, 1.0, 0.0), jnp.sum)
                nxt = jnp.maximum(nxt, fold(jnp.where(s < m, s, -jnp.inf), jnp.max))
            return cnt, nxt

        cnt, nxt = pair_loop(body, (jnp.zeros(folded, F32), jnp.full(folded, -jnp.inf, F32)))
        cnt = jnp.sum(cnt, axis=0, keepdims=True)
        nxt = jnp.max(nxt, axis=0, keepdims=True)
        reached = cnt >= kf
        hit = jnp.logical_and(reached, done < 0.5)
        return nxt, jnp.where(hit, m, thr), jnp.where(reached, 1.0, done), jnp.where(hit, cnt, thr_cnt)

    init = (max_le(hi), jnp.full((1, tq), -jnp.inf, F32), jnp.where(small, 1.0, 0.0), jnp.zeros((1, tq), F32))
    _, thr, _, thr_cnt = lax.while_loop(refine_cond, refine_body, init)

    earlier = (lax.broadcasted_iota(jnp.int32, (KEY_CHUNK, KEY_CHUNK), 1)
               < lax.broadcasted_iota(jnp.int32, (KEY_CHUNK, KEY_CHUNK), 0)).astype(BF16)

    def mask_exact_k():
        def body(c0, c1, carry):
            for c in (c0, c1):
                s = score_ref[c]
                sel = jnp.logical_and(s >= thr, s > -jnp.inf)
                score_ref[c] = jnp.where(sel, 0.0, NEG_BIG)
            return carry
        pair_loop(body, 0)

    def mask_with_ties():
        quota = jnp.where(small, 0.0, kf - count_gt(thr))

        def body(c0, c1, seen):
            for c in (c0, c1):
                s = score_ref[c]
                eq = s == thr
                eqf = jnp.where(eq, 1.0, 0.0)
                before = jnp.dot(earlier, eqf.astype(BF16), preferred_element_type=F32) + seen
                sel = jnp.logical_or(s > thr, jnp.logical_and(eq, before < quota))
                seen = seen + jnp.sum(eqf, axis=0, keepdims=True)
                score_ref[c] = jnp.where(sel, 0.0, NEG_BIG)
            return seen
        pair_loop(body, jnp.zeros((1, tq), F32))

    surplus = jnp.max(jnp.where(small, 0.0, thr_cnt - kf)) > 0.5
    lax.cond(surplus, mask_with_ties, mask_exact_k)

    qa = jnp.concatenate([qa_ref[0, :, hd * LANES:(hd + 1) * LANES] for hd in range(heads)], axis=0)

    def qk_into(slot, c):
        kc = ka_ref[0, pl.ds(pl.multiple_of(c * KEY_CHUNK, KEY_CHUNK), KEY_CHUNK), :]
        logit_ref[slot] = lax.dot_general(kc, qa, contract_last, preferred_element_type=F32)

    def softmax_pv(slot, c, m_run, acc):
        bias = score_ref[c]
        vt = vt_ref[0, c]
        new_m, new_acc = [], []
        for hp in range(heads // 2):
            probs, alphas = [], []
            for hd in (2 * hp, 2 * hp + 1):
                sl = slice(hd * tq, (hd + 1) * tq)
                lh = logit_ref[slot, :, sl] + bias
                m_old = m_run[:, sl]
                m_new = jnp.maximum(m_old, jnp.max(lh, axis=0, keepdims=True))
                probs.append(jnp.exp2(lh - m_new).astype(BF16))
                alphas.append(jnp.exp2(m_old - m_new))
                new_m.append(m_new)
            pv = jnp.dot(vt, jnp.concatenate(probs, axis=1), preferred_element_type=F32)
            new_acc.append(jnp.concatenate(alphas, axis=1) * acc[:, 2 * hp * tq:(2 * hp + 2) * tq] + pv)
        return jnp.concatenate(new_m, axis=1), jnp.concatenate(new_acc, axis=1)

    def attn_body(c0, c1, carry):
        m_run, acc = carry
        qk_into(1, c1)
        m_run, acc = softmax_pv(0, c0, m_run, acc)
        qk_into(0, jnp.minimum(c0 + 2, 2 * npair - 1))
        return softmax_pv(1, c1, m_run, acc)

    qk_into(0, 0)
    d_rows = vt_ref.shape[2]
    init = (jnp.full((1, heads * tq), NEG_BIG, F32), jnp.zeros((d_rows, heads * tq), F32))
    _, acc = pair_loop(attn_body, init)
    pad_rows = jnp.zeros((LANES - d_rows, tq), F32)
    for hd in range(heads):
        blk = acc[:, hd * tq:(hd + 1) * tq]
        out_t = jnp.concatenate([blk / blk[DSA_HEAD_DIM:DSA_HEAD_DIM + 1, :], pad_rows], axis=0)
        o_ref[0, hd] = out_t.T.astype(o_ref.dtype)


def _dsa_attention(qa, qi, iw, ka, va, ki, topk):
    b, s, wide = qa.shape
    tq = Q_TILE_DSA
    nkc = s // KEY_CHUNK
    iw_t = jnp.swapaxes(iw[..., :IDX_HEADS], 1, 2)
    v_t = jnp.swapaxes(va.reshape(b, nkc, KEY_CHUNK, LANES), 2, 3)[:, :, :DSA_V_ROWS]
    qspec = pl.BlockSpec((1, tq, wide), lambda bi, i: (bi, i, 0))
    kspec = pl.BlockSpec((1, s, LANES), lambda bi, i: (bi, 0, 0))
    return pl.pallas_call(
        functools.partial(_dsa_kernel, topk=topk),
        grid=(b, s // tq),
        in_specs=[qspec, qspec, pl.BlockSpec((1, IDX_HEADS, tq), lambda bi, i: (bi, 0, i)), kspec,
                  pl.BlockSpec((1, nkc, DSA_V_ROWS, KEY_CHUNK), lambda bi, i: (bi, 0, 0, 0)), kspec],
        out_specs=pl.BlockSpec((1, DSA_HEADS, tq, LANES), lambda bi, i: (bi, 0, i, 0)),
        out_shape=jax.ShapeDtypeStruct((b, DSA_HEADS, s, LANES), BF16),
        scratch_shapes=[pltpu.VMEM((nkc, KEY_CHUNK, tq), F32),
                        pltpu.VMEM((2, KEY_CHUNK, DSA_HEADS * tq), F32)],
        compiler_params=_params(("parallel", "arbitrary")),
        name="dsa_attn",
    )(qa, qi, iw_t, ka, v_t, ki)


def _mla_kernel(q_ref, k_ref, vt_ref, o_ref, logit_ref):
    i = pl.program_id(2)
    tq = q_ref.shape[1]
    group = q_ref.shape[2] // LANES
    d_rows = vt_ref.shape[3]
    qs = [q_ref[0, :, g * LANES:(g + 1) * LANES] for g in range(group)]
    contract_last = (((1,), (1,)), ((), ()))

    def qk_into(slot, c):
        k0 = pl.multiple_of(c * tq, tq)
        for g in range(group):
            logit_ref[slot, g] = lax.dot_general(k_ref[0, pl.ds(k0, tq), g * LANES:(g + 1) * LANES], qs[g],
                                                 contract_last, preferred_element_type=F32)

    def softmax_pv(slot, c, carry, masked):
        if masked:
            keep = (lax.broadcasted_iota(jnp.int32, (tq, tq), 0) <= lax.broadcasted_iota(jnp.int32, (tq, tq), 1))
            bias = jnp.where(keep, 0.0, NEG_BIG)
        probs, alphas, maxes = [], [], []
        for g in range(group):
            m_run = carry[2 * g]
            logit = logit_ref[slot, g]
            if masked:
                logit = logit + bias
            m_new = jnp.maximum(m_run, jnp.max(logit, axis=0, keepdims=True))
            probs.append(jnp.exp2(logit - m_new).astype(BF16))
            alphas.append(jnp.exp2(m_run - m_new))
            maxes.append(m_new)
        res = []
        for g in range(group):
            pv = jnp.dot(vt_ref[0, g, c], probs[g], preferred_element_type=F32)
            res += [maxes[g], alphas[g] * carry[2 * g + 1] + pv]
        return tuple(res)

    def pair_body(j, carry):
        qk_into(1, 2 * j + 1)
        carry = softmax_pv(0, 2 * j, carry, False)
        qk_into(0, 2 * j + 2)
        return softmax_pv(1, 2 * j + 1, carry, False)

    def tail_even(carry):
        return softmax_pv(0, i, carry, True)

    def tail_odd(carry):
        qk_into(1, i)
        carry = softmax_pv(0, i - 1, carry, False)
        return softmax_pv(1, i, carry, True)

    qk_into(0, 0)
    init = (jnp.full((1, tq), NEG_BIG, F32), jnp.zeros((d_rows, tq), F32)) * group
    carry = lax.fori_loop(0, i // 2, pair_body, init)
    carry = lax.cond(i % 2 == 0, tail_even, tail_odd, carry)
    pad_rows = jnp.zeros((LANES - d_rows, tq), F32)
    for g in range(group):
        acc = carry[2 * g + 1]
        out_t = jnp.concatenate([acc / acc[MLA_V:MLA_V + 1, :], pad_rows], axis=0)
        o_ref[0, :, g * LANES:(g + 1) * LANES] = out_t.T.astype(o_ref.dtype)


def _mla_attention(mq, mk, mv):
    b, s, wide = mq.shape
    tq = Q_TILE_MLA
    nkc = s // tq
    gw = MLA_HEAD_GROUP * LANES
    v_t = mv.reshape(b, nkc, tq, MLA_HEADS, LANES)[..., :MLA_V_ROWS]
    v_t = jnp.transpose(v_t, (0, 3, 1, 4, 2))
    return pl.pallas_call(
        _mla_kernel,
        grid=(b, MLA_HEADS // MLA_HEAD_GROUP, s // tq),
        in_specs=[pl.BlockSpec((1, tq, gw), lambda bi, h, i: (bi, i, h)),
                  pl.BlockSpec((1, s, gw), lambda bi, h, i: (bi, 0, h)),
                  pl.BlockSpec((1, MLA_HEAD_GROUP, nkc, MLA_V_ROWS, tq), lambda bi, h, i: (bi, h, 0, 0, 0))],
        out_specs=pl.BlockSpec((1, tq, gw), lambda bi, h, i: (bi, i, h)),
        out_shape=jax.ShapeDtypeStruct((b, s, wide), BF16),
        scratch_shapes=[pltpu.VMEM((2, MLA_HEAD_GROUP, tq, tq), F32)],
        compiler_params=_params(("parallel", "parallel", "arbitrary")),
        name="mla_attn",
    )(mq, mk, v_t)


def _mem_kv_kernel(mem_ref, g_ref, wkv_ref, kg_ref, k_o, v_o):
    m = mem_ref[0]
    hm = _rms(m, g_ref[...], m.shape[-1]).astype(BF16)
    kv = jnp.dot(hm, wkv_ref[...], preferred_element_type=F32)
    for hd in range(MEM_HEADS):
        sl = slice(hd * LANES, (hd + 1) * LANES)
        k_o[0, :, sl] = _rms(kv[:, sl], kg_ref[...], MEM_HEAD_DIM).astype(BF16)
    v_o[0] = kv[:, MEM_HEADS * LANES:].astype(BF16)


def _mem_kv(mem, mem_g, w_kv, k_g):
    b, m, d = mem.shape
    hw = MEM_HEADS * MEM_HEAD_DIM
    wkv = jnp.concatenate([_pad_heads(w_kv[:, :hw], MEM_HEADS, MEM_HEAD_DIM),
                           _pad_heads(w_kv[:, hw:], MEM_HEADS, MEM_HEAD_DIM)], axis=1).astype(BF16)
    consts = [mem_g.reshape(1, d), wkv, _pad_cols(k_g.reshape(1, -1))]
    wide = MEM_HEADS * LANES
    spec = pl.BlockSpec((1, m, wide), lambda bi: (bi, 0, 0))
    return pl.pallas_call(
        _mem_kv_kernel,
        grid=(b,),
        in_specs=[pl.BlockSpec((1, m, d), lambda bi: (bi, 0, 0))] + [_const_spec(c.shape) for c in consts],
        out_specs=[spec, spec],
        out_shape=[jax.ShapeDtypeStruct((b, m, wide), BF16)] * 2,
        compiler_params=_params(("parallel",)),
        name="mem_kv",
    )(mem, *consts)


def _split_dot(a, w):
    a_hi = a.astype(BF16)
    a_lo = (a - a_hi.astype(F32)).astype(BF16)
    w_hi = w.astype(BF16)
    w_lo = (w - w_hi.astype(F32)).astype(BF16)
    w_both = jnp.concatenate([w_hi, w_lo], axis=1)
    n = w.shape[1]
    hi = jnp.dot(a_hi, w_both, preferred_element_type=F32)
    lo = jnp.dot(a_lo, w_both, preferred_element_type=F32)
    return (hi[:, :n] + hi[:, n:]) + (lo[:, :n] + lo[:, n:])


def _merge_kernel(oa_ref, ob_ref, sgd_ref, sgm_ref, x_ref, wa_ref, wb_ref, wo_ref, mxg_ref, wq_ref, qg_ref,
                  km_ref, vm_ref, wmo_ref, moeg_ref, wr_ref, bias_ref, x2_o, h3_o, comb_o):
    d = functools.partial(jnp.dot, preferred_element_type=F32)
    oa = jnp.concatenate([oa_ref[0, hd] for hd in range(DSA_HEADS)], axis=-1)
    merged = (sgd_ref[0].astype(F32) * d(oa, wa_ref[...]) + sgm_ref[0].astype(F32) * d(ob_ref[0], wb_ref[...]))
    x1 = x_ref[0] + d(merged.astype(BF16), wo_ref[...])

    d_model = x1.shape[-1]
    h2 = _rms(x1, mxg_ref[...], d_model).astype(BF16)
    qm = d(h2, wq_ref[...])
    outs = []
    for hd in range(MEM_HEADS):
        sl = slice(hd * LANES, (hd + 1) * LANES)
        q = (_rms(qm[:, sl], qg_ref[...], MEM_HEAD_DIM) * MEM_HEAD_DIM ** -0.5).astype(BF16)
        logit = lax.dot_general(q, km_ref[0, :, sl], (((1,), (1,)), ((), ())), preferred_element_type=F32)
        logit = logit - jnp.max(logit, axis=-1, keepdims=True)
        p = jnp.exp(logit)
        pv = d(p.astype(BF16), vm_ref[0, :, sl])
        outs.append((pv / jnp.sum(p, axis=-1, keepdims=True)).astype(BF16))
    x2 = x1 + d(jnp.concatenate(outs, axis=-1), wmo_ref[...])
    x2_o[0] = x2

    h3 = _rms(x2, moeg_ref[...], d_model)
    h3_o[0] = h3.astype(BF16)
    logits = _split_dot(h3, wr_ref[...])
    tm = logits.shape[0]
    lane = lax.broadcasted_iota(jnp.int32, (tm, LANES), 1)
    big = jnp.int32(LANES)
    is_grp = jnp.logical_and(lane >= N_EXPERTS, lane < N_EXPERTS + N_GROUPS)
    glog = jnp.where(is_grp, logits, -jnp.inf)
    gmax = jnp.max(glog, axis=-1, keepdims=True)
    g_sel = jnp.min(jnp.where(glog == gmax, lane, big), axis=-1, keepdims=True) - N_EXPERTS
    p_g = 1.0 / jnp.sum(jnp.where(is_grp, jnp.exp(logits - gmax), 0.0), axis=-1, keepdims=True)
    aff = jax.nn.sigmoid(logits)
    in_grp = jnp.logical_and(lane >= g_sel * EXPERTS_PER_GROUP, lane < (g_sel + 1) * EXPERTS_PER_GROUP)
    val = jnp.where(in_grp, aff + bias_ref[...], -jnp.inf)
    m1 = jnp.max(val, axis=-1, keepdims=True)
    i1 = jnp.min(jnp.where(val == m1, lane, big), axis=-1, keepdims=True)
    val2 = jnp.where(lane == i1, -jnp.inf, val)
    m2 = jnp.max(val2, axis=-1, keepdims=True)
    i2 = jnp.min(jnp.where(val2 == m2, lane, big), axis=-1, keepdims=True)
    chosen = jnp.logical_or(lane == i1, lane == i2)
    a_sel = jnp.where(chosen, aff, 0.0)
    comb_o[0] = p_g * a_sel / jnp.sum(a_sel, axis=-1, keepdims=True)


def _merge_mem_router(oa, ob, sgd, sgm, x, w_br_dsa, w_br_mla, w_out, mem_x_g, mem_w_q, mem_q_g, km, vm,
                      mem_w_o, moe_g, w_group, w_expert, expert_bias, tm):
    b, s, d = x.shape
    wa = jnp.pad(w_br_dsa.reshape(DSA_HEADS, DSA_HEAD_DIM, d), ((0, 0), (0, LANES - DSA_HEAD_DIM), (0, 0)))
    wa = wa.reshape(DSA_HEADS * LANES, d).astype(BF16)
    wb = jnp.pad(w_br_mla.reshape(MLA_HEADS, MLA_V, d), ((0, 0), (0, LANES - MLA_V), (0, 0)))
    wb = wb.reshape(MLA_HEADS * LANES, d).astype(BF16)
    wq = _pad_heads(mem_w_q, MEM_HEADS, MEM_HEAD_DIM).astype(BF16)
    wmo = jnp.pad(mem_w_o.reshape(MEM_HEADS, MEM_HEAD_DIM, d), ((0, 0), (0, LANES - MEM_HEAD_DIM), (0, 0)))
    wmo = wmo.reshape(MEM_HEADS * LANES, d).astype(BF16)
    wr = _pad_cols(jnp.concatenate([w_expert, w_group], axis=1).astype(F32))
    bias = _pad_cols(expert_bias.reshape(1, -1).astype(F32))
    consts_a = [wa, wb, w_out.astype(BF16), mem_x_g.reshape(1, d), wq, _pad_cols(mem_q_g.reshape(1, -1))]
    consts_b = [wmo, moe_g.reshape(1, d), wr, bias]
    m = km.shape[1]
    mw = MEM_HEADS * LANES
    tok = lambda w: pl.BlockSpec((1, tm, w), lambda bi, i: (bi, i, 0))
    memspec = pl.BlockSpec((1, m, mw), lambda bi, i: (bi, 0, 0))
    return pl.pallas_call(
        _merge_kernel,
        grid=(b, s // tm),
        in_specs=[pl.BlockSpec((1, DSA_HEADS, tm, LANES), lambda bi, i: (bi, 0, i, 0)),
                  tok(MLA_HEADS * LANES), tok(d), tok(d), tok(d)]
                 + [_const_spec(c.shape) for c in consts_a] + [memspec, memspec]
                 + [_const_spec(c.shape) for c in consts_b],
        out_specs=[tok(d), tok(d), tok(LANES)],
        out_shape=[jax.ShapeDtypeStruct((b, s, d), F32), jax.ShapeDtypeStruct((b, s, d), BF16),
                   jax.ShapeDtypeStruct((b, s, LANES), F32)],
        compiler_params=_params(("parallel", "parallel")),
        name="merge_mem",
    )(oa, ob, sgd, sgm, x, *consts_a, km, vm, *consts_b)


def _moe_kernel(h_ref, comb_ref, x_ref, wgu_ref, wd_ref, o_ref):
    g = pl.program_id(1)
    width = EXPERTS_PER_GROUP * D_EXPERT
    gu = jnp.dot(h_ref[...], wgu_ref[0], preferred_element_type=F32)
    act = jax.nn.silu(gu[:, :width]) * gu[:, width:]
    comb = comb_ref[...]
    lane = lax.broadcasted_iota(jnp.int32, comb.shape, 1)
    parts = []
    for e in range(EXPERTS_PER_GROUP):
        c_e = jnp.sum(jnp.where(lane == g * EXPERTS_PER_GROUP + e, comb, 0.0), axis=-1, keepdims=True)
        parts.append((act[:, e * D_EXPERT:(e + 1) * D_EXPERT] * c_e).astype(BF16))
    contrib = jnp.dot(jnp.concatenate(parts, axis=1), wd_ref[0], preferred_element_type=F32)

    @pl.when(g == 0)
    def _():
        o_ref[...] = x_ref[...] + contrib

    @pl.when(g > 0)
    def _():
        o_ref[...] += contrib


def _moe(h3, comb, x2, w_gate, w_up, w_down, tm):
    n, d = x2.shape
    width = EXPERTS_PER_GROUP * D_EXPERT
    by_group = lambda w: jnp.moveaxis(w.reshape(N_GROUPS, EXPERTS_PER_GROUP, d, D_EXPERT), 1, 2).reshape(N_GROUPS, d, width)
    wgu = jnp.concatenate([by_group(w_gate), by_group(w_up)], axis=-1).astype(BF16)
    wd = w_down.reshape(N_GROUPS, width, d).astype(BF16)
    return pl.pallas_call(
        _moe_kernel,
        grid=(n // tm, N_GROUPS),
        in_specs=[pl.BlockSpec((tm, d), lambda i, g: (i, 0)),
                  pl.BlockSpec((tm, LANES), lambda i, g: (i, 0)),
                  pl.BlockSpec((tm, d), lambda i, g: (i, 0)),
                  pl.BlockSpec((1, d, 2 * width), lambda i, g: (g, 0, 0)),
                  pl.BlockSpec((1, width, d), lambda i, g: (g, 0, 0))],
        out_specs=pl.BlockSpec((tm, d), lambda i, g: (i, 0)),
        out_shape=jax.ShapeDtypeStruct((n, d), F32),
        compiler_params=_params(("parallel", "arbitrary")),
        name="moe",
    )(h3, comb, x2, wgu, wd)


def _layer(x, mem, positions, p):
    b, s, d = x.shape
    n = b * s
    assert s % (2 * KEY_CHUNK) == 0 and s % Q_TILE_MLA == 0
    topk = min(TOPK_MAX, s // 4)
    tm = min(256, s)
    qa, qi, ka, va, ki, iw, mq, mk, mv, sgd, sgm = _in_proj(
        x.reshape(n, d), positions.reshape(n, 1), p["attn_norm_g"], p["w_in"], p["dsa_q_norm_g"],
        p["dsa_k_norm_g"], p["mla_cq_norm_g"], p["mla_ckv_norm_g"], p["mla_w_uq"], p["mla_w_ukv"],
        p["mla_q_norm_g"], p["mla_k_norm_g"], tm)
    r3 = lambda a: a.reshape(b, s, a.shape[-1])
    oa = _dsa_attention(r3(qa), r3(qi), r3(iw), r3(ka), r3(va), r3(ki), topk)
    ob = _mla_attention(r3(mq), r3(mk), r3(mv))
    km, vm = _mem_kv(mem, p["mem_norm_g"], p["mem_w_kv"], p["mem_k_norm_g"])
    x2, h3, comb = _merge_mem_router(
        oa, ob, r3(sgd), r3(sgm), x, p["w_branch_dsa"], p["w_branch_mla"], p["w_out"], p["mem_x_norm_g"],
        p["mem_w_q"], p["mem_q_norm_g"], km, vm, p["mem_w_o"], p["moe_norm_g"], p["moe_w_group"],
        p["moe_w_expert"], p["moe_expert_bias"], min(512, s))
    out = _moe(h3.reshape(n, d), comb.reshape(n, LANES), x2.reshape(n, d), p["moe_w_gate"], p["moe_w_up"],
               p["moe_w_down"], min(1024, n))
    return out.reshape(b, s, d)


_PARAM_NAMES = ("attn_norm_g", "w_in", "dsa_q_norm_g", "dsa_k_norm_g", "mla_cq_norm_g", "mla_ckv_norm_g",
                "mla_w_uq", "mla_w_ukv", "mla_q_norm_g", "mla_k_norm_g", "w_branch_dsa", "w_branch_mla", "w_out",
                "mem_x_norm_g", "mem_norm_g", "mem_w_q", "mem_w_kv", "mem_q_norm_g", "mem_k_norm_g", "mem_w_o",
                "moe_norm_g", "moe_w_group", "moe_w_expert", "moe_expert_bias", "moe_w_gate", "moe_w_up",
                "moe_w_down")


def kernel(x, mem, positions, attn_norm_g, w_in, dsa_q_norm_g, dsa_k_norm_g, mla_cq_norm_g, mla_ckv_norm_g, mla_w_uq, mla_w_ukv, mla_q_norm_g, mla_k_norm_g, w_branch_dsa, w_branch_mla, w_out, mem_x_norm_g, mem_norm_g, mem_w_q, mem_w_kv, mem_q_norm_g, mem_k_norm_g, mem_w_o, moe_norm_g, moe_w_group, moe_w_expert, moe_expert_bias, moe_w_gate, moe_w_up, moe_w_down):
    stacked = (attn_norm_g, w_in, dsa_q_norm_g, dsa_k_norm_g, mla_cq_norm_g, mla_ckv_norm_g, mla_w_uq, mla_w_ukv,
               mla_q_norm_g, mla_k_norm_g, w_branch_dsa, w_branch_mla, w_out, mem_x_norm_g, mem_norm_g, mem_w_q,
               mem_w_kv, mem_q_norm_g, mem_k_norm_g, mem_w_o, moe_norm_g, moe_w_group, moe_w_expert,
               moe_expert_bias, moe_w_gate, moe_w_up, moe_w_down)
    for layer in range(attn_norm_g.shape[0]):
        p = {name: arr[layer] for name, arr in zip(_PARAM_NAMES, stacked)}
        x = _layer(x, mem, positions, p)
    return x
```

```python
import functools

import jax
import jax.numpy as jnp
from jax import lax
from jax.experimental import pallas as pl
from jax.experimental.pallas import tpu as pltpu

F32 = jnp.float32
BF16 = jnp.bfloat16

LANES = 128
SUBLANES = 8
VMEM_LIMIT = 56 * 1024 * 1024

ROPE_THETA = 10000.0
EPS = 1e-6
DSA_HEADS = 8
DSA_HEAD_DIM = 64
IDX_HEADS = 8
IDX_DIM = 64
TOPK_MAX = 256
MLA_HEADS = 8
MLA_NOPE = 64
MLA_ROPE = 32
MLA_QK = MLA_NOPE + MLA_ROPE
MLA_V = 64
MLA_Q_RANK = 256
MLA_KV_RANK = 128
MEM_HEADS = 4
MEM_HEAD_DIM = 64
N_GROUPS = 4
EXPERTS_PER_GROUP = 4
N_EXPERTS = N_GROUPS * EXPERTS_PER_GROUP
D_EXPERT = 256

NEG_BIG = -1e30
KEY_CHUNK = 256
Q_TILE_DSA = 256
DSA_V_ROWS = 80
MLA_V_ROWS = 80
Q_TILE_MLA = 512
MLA_HEAD_GROUP = 4
LOG2_E = 1.4426950408889634
BISECT_ITERS = 15


def _const_spec(shape):
    nd = len(shape)
    return pl.BlockSpec(shape, lambda *_: (0,) * nd)


def _params(sem):
    return pltpu.CompilerParams(dimension_semantics=sem, vmem_limit_bytes=VMEM_LIMIT)


def _rms(x, g, n):
    ms = jnp.sum(x * x, axis=-1, keepdims=True) * (1.0 / n)
    return (x * lax.rsqrt(ms + EPS)) * g


def _rope(x, cos, sin_signed, lo_mask, half):
    fwd = pltpu.roll(x, LANES - half, 1)
    bwd = pltpu.roll(x, half, 1)
    return x * cos + jnp.where(lo_mask, fwd, bwd) * sin_signed


def _in_proj_kernel(x_ref, pos_ref, g_ref, wqq_ref, wsm_ref, wcq_ref, wg_ref, wuq_ref, wuk_ref,
                    wuv_ref, qag_ref, kag_ref, cqg_ref, ckvg_ref, mqg_ref, mkg_ref, fd_ref, fm_ref,
                    qq_o, ka_o, va_o, ki_o, iw_o, mq_o, mk_o, mv_o, sgd_o, sgm_o):
    x = x_ref[...]
    d_model = x.shape[-1]
    h = _rms(x, g_ref[...], d_model).astype(BF16)

    pos = pos_ref[...].astype(F32)
    lane = lax.broadcasted_iota(jnp.int32, (x.shape[0], LANES), 1)
    ang_d = pos * fd_ref[...]
    lo_d = (lane & (DSA_HEAD_DIM - 1)) < DSA_HEAD_DIM // 2
    cos_d = jnp.cos(ang_d)
    sin_d = jnp.sin(ang_d)
    sin_d = jnp.where(lo_d, -sin_d, sin_d)
    ang_m = pos * fm_ref[...]
    lo_m = lane < MLA_NOPE + MLA_ROPE // 2
    cos_m = jnp.cos(ang_m)
    sin_m = jnp.sin(ang_m)
    sin_m = jnp.where(lo_m, -sin_m, sin_m)
    rope_d = functools.partial(_rope, cos=cos_d, sin_signed=sin_d, lo_mask=lo_d, half=DSA_HEAD_DIM // 2)
    rope_m = functools.partial(_rope, cos=cos_m, sin_signed=sin_m, lo_mask=lo_m, half=MLA_ROPE // 2)

    def dot(a, w_ref):
        return jnp.dot(a, w_ref[...], preferred_element_type=F32)

    att_scale = DSA_HEAD_DIM ** -0.5 * LOG2_E
    idx_scale = IDX_DIM ** -0.5 * IDX_HEADS ** -0.5
    mla_scale = MLA_QK ** -0.5 * LOG2_E

    qq = dot(h, wqq_ref)
    is_qa = lane < DSA_HEAD_DIM
    for hd in range(DSA_HEADS):
        sl = slice(hd * LANES, (hd + 1) * LANES)
        xh = qq[:, sl]
        ms = jnp.sum(jnp.where(is_qa, xh * xh, 0.0), axis=-1, keepdims=True) * (1.0 / DSA_HEAD_DIM)
        mult = jnp.where(is_qa, (lax.rsqrt(ms + EPS) * att_scale) * qag_ref[...], 1.0)
        qq_o[:, sl] = rope_d(xh * mult).astype(BF16)

    sm = dot(h, wsm_ref)
    ka_o[...] = rope_d(_rms(sm[:, 0:LANES], kag_ref[...], DSA_HEAD_DIM)).astype(BF16)
    va_o[...] = (sm[:, LANES:2 * LANES] + jnp.where(lane[0:1] == DSA_HEAD_DIM, 1.0, 0.0)).astype(BF16)
    ki_o[...] = rope_d(sm[:, 2 * LANES:3 * LANES]).astype(BF16)
    iw_o[...] = sm[:, 3 * LANES:4 * LANES] * idx_scale
    ckv = _rms(sm[:, 4 * LANES:5 * LANES], ckvg_ref[...], MLA_KV_RANK).astype(BF16)
    kpe = sm[:, 5 * LANES:6 * LANES]

    kn = dot(ckv, wuk_ref)
    for hd in range(MLA_HEADS):
        sl = slice(hd * LANES, (hd + 1) * LANES)
        y = rope_m(_rms(kn[:, sl] + kpe, mkg_ref[...], MLA_QK))
        mk_o[:, sl] = y.astype(BF16)
    wide_lane = lax.broadcasted_iota(jnp.int32, (1, MLA_HEADS * LANES), 1)
    ones_col = jnp.where(wide_lane % LANES == MLA_V, 1.0, 0.0)
    mv_o[...] = (dot(ckv, wuv_ref) + ones_col).astype(BF16)

    cq = _rms(dot(h, wcq_ref), cqg_ref[...], MLA_Q_RANK).astype(BF16)
    qb = dot(cq, wuq_ref)
    for hd in range(MLA_HEADS):
        sl = slice(hd * LANES, (hd + 1) * LANES)
        y = rope_m(_rms(qb[:, sl], mqg_ref[...], MLA_QK))
        mq_o[:, sl] = (y * mla_scale).astype(BF16)

    gates = jax.nn.sigmoid(dot(h, wg_ref))
    sgd_o[...] = gates[:, :d_model].astype(BF16)
    sgm_o[...] = gates[:, d_model:].astype(BF16)


def _pad_heads(w, heads, dim):
    k = w.shape[0]
    w = w.reshape(k, heads, dim)
    w = jnp.pad(w, ((0, 0), (0, 0), (0, LANES - dim)))
    return w.reshape(k, heads * LANES)


def _pad_cols(w, width=LANES, offset=0):
    return jnp.pad(w, ((0, 0), (offset, width - offset - w.shape[1])))


def _in_proj(x2, pos2, attn_g, w_in, dsa_q_g, dsa_k_g, cq_g, ckv_g, w_uq, w_ukv, mq_g, mk_g, tm):
    n, d = x2.shape
    sizes = (DSA_HEADS * DSA_HEAD_DIM, DSA_HEAD_DIM, DSA_HEAD_DIM, IDX_HEADS * IDX_DIM, IDX_DIM, IDX_HEADS,
             MLA_Q_RANK, MLA_KV_RANK, MLA_ROPE, d, d)
    offs = [0]
    for s in sizes:
        offs.append(offs[-1] + s)
    seg = [w_in[:, offs[i]:offs[i + 1]] for i in range(len(sizes))]
    w_dq, w_dk, w_dv, w_iq, w_ik, w_iw, w_cq, w_ckv, w_kpe, w_gd, w_gm = seg

    assert DSA_HEADS == IDX_HEADS and DSA_HEAD_DIM + IDX_DIM == LANES
    wqq = jnp.concatenate([w_dq.reshape(d, DSA_HEADS, DSA_HEAD_DIM), w_iq.reshape(d, IDX_HEADS, IDX_DIM)], axis=-1)
    wqq = wqq.reshape(d, DSA_HEADS * LANES).astype(BF16)
    wsm = jnp.concatenate([_pad_cols(w_dk), _pad_cols(w_dv), _pad_cols(w_ik, offset=DSA_HEAD_DIM), _pad_cols(w_iw), w_ckv,
                           _pad_cols(w_kpe, offset=MLA_NOPE)], axis=1).astype(BF16)
    wcq = w_cq.astype(BF16)
    wg = jnp.concatenate([w_gd, w_gm], axis=1).astype(BF16)
    wuq = _pad_heads(w_uq, MLA_HEADS, MLA_QK).astype(BF16)
    ukv = w_ukv.reshape(MLA_KV_RANK, MLA_HEADS, MLA_NOPE + MLA_V)
    wuk = _pad_heads(ukv[:, :, :MLA_NOPE].reshape(MLA_KV_RANK, -1), MLA_HEADS, MLA_NOPE).astype(BF16)
    wuv = _pad_heads(ukv[:, :, MLA_NOPE:].reshape(MLA_KV_RANK, -1), MLA_HEADS, MLA_V).astype(BF16)

    row = lambda v: _pad_cols(v.reshape(1, -1).astype(F32), width=max(LANES, v.size))
    half_d = DSA_HEAD_DIM // 2
    inv_d = ROPE_THETA ** (-jnp.arange(half_d, dtype=F32) / half_d)
    fd = jnp.tile(inv_d, LANES // half_d).reshape(1, -1)
    half_m = MLA_ROPE // 2
    inv_m = ROPE_THETA ** (-jnp.arange(half_m, dtype=F32) / half_m)
    fm = _pad_cols(jnp.concatenate([inv_m, inv_m]).reshape(1, -1), offset=MLA_NOPE)

    consts = [attn_g.reshape(1, d), wqq, wsm, wcq, wg, wuq, wuk, wuv, row(dsa_q_g), row(dsa_k_g),
              row(cq_g), row(ckv_g), row(mq_g), row(mk_g), fd, fm]
    wide = DSA_HEADS * LANES
    out_shapes = [
        jax.ShapeDtypeStruct((n, wide), BF16),
        jax.ShapeDtypeStruct((n, LANES), BF16),
        jax.ShapeDtypeStruct((n, LANES), BF16),
        jax.ShapeDtypeStruct((n, LANES), BF16),
        jax.ShapeDtypeStruct((n, LANES), F32),
        jax.ShapeDtypeStruct((n, wide), BF16),
        jax.ShapeDtypeStruct((n, wide), BF16),
        jax.ShapeDtypeStruct((n, wide), BF16),
        jax.ShapeDtypeStruct((n, d), BF16),
        jax.ShapeDtypeStruct((n, d), BF16),
    ]
    tile = lambda w: pl.BlockSpec((tm, w), lambda i: (i, 0))
    return pl.pallas_call(
        _in_proj_kernel,
        grid=(n // tm,),
        in_specs=[tile(d), tile(1)] + [_const_spec(c.shape) for c in consts],
        out_specs=[tile(s.shape[1]) for s in out_shapes],
        out_shape=out_shapes,
        compiler_params=_params(("parallel",)),
        name="in_proj",
    )(x2, pos2, *consts)


def _dsa_kernel(qq_ref, iw_ref, ka_ref, vt_ref, ki_ref, o_ref, score_ref, logit_ref, *, topk):
    i = pl.program_id(1)
    tq = qq_ref.shape[1]
    heads = DSA_HEADS
    npair = (i * tq + tq + 2 * KEY_CHUNK - 1) // (2 * KEY_CHUNK)
    kf = float(topk)
    shape = (KEY_CHUNK, tq)
    folded = (SUBLANES, tq)

    def fold(a, op):
        return op(a.reshape(KEY_CHUNK // SUBLANES, SUBLANES, tq), axis=0)

    krow = lax.broadcasted_iota(jnp.int32, shape, 0)
    qcol = i * tq + lax.broadcasted_iota(jnp.int32, shape, 1)
    contract_last = (((1,), (1,)), ((), ()))

    def pair_loop(body, init):
        return lax.fori_loop(0, npair, lambda j, carry: body(2 * j, 2 * j + 1, carry), init)

    qq = jnp.concatenate([qq_ref[0, :, hd * LANES:(hd + 1) * LANES] for hd in range(heads)], axis=0)
    iw = iw_ref[0]

    def score_body(c0, c1, carry):
        mx, mn = carry
        for c in (c0, c1):
            kc = ki_ref[0, pl.ds(pl.multiple_of(c * KEY_CHUNK, KEY_CHUNK), KEY_CHUNK), :]
            sc = None
            for hp in range(IDX_HEADS // 2):
                rel = lax.dot_general(kc, qq[2 * hp * tq:(2 * hp + 2) * tq], contract_last,
                                      preferred_element_type=F32)
                rel = jnp.maximum(rel, 0.0)
                part = rel[:, :tq] * iw[2 * hp:2 * hp + 1, :] + rel[:, tq:] * iw[2 * hp + 1:2 * hp + 2, :]
                sc = part if sc is None else sc + part
            causal = (krow + c * KEY_CHUNK) <= qcol
            masked = jnp.where(causal, sc, -jnp.inf)
            score_ref[c] = masked
            mx = jnp.maximum(mx, fold(masked, jnp.max))
            mn = jnp.minimum(mn, fold(jnp.where(causal, sc, jnp.inf), jnp.min))
        return mx, mn

    mx, mn = pair_loop(score_body, (jnp.full(folded, -jnp.inf, F32), jnp.full(folded, jnp.inf, F32)))
    hi = jnp.max(mx, axis=0, keepdims=True)
    lo = jnp.min(mn, axis=0, keepdims=True)

    def count_gt(t):
        def body(c0, c1, acc):
            for c in (c0, c1):
                acc = acc + fold(jnp.where(score_ref[c] > t, 1.0, 0.0), jnp.sum)
            return acc
        return jnp.sum(pair_loop(body, jnp.zeros(folded, F32)), axis=0, keepdims=True)

    def bisect(_, carry):
        lo, hi = carry
        mid = 0.5 * (lo + hi)
        below = count_gt(mid) < kf
        return jnp.where(below, lo, mid), jnp.where(below, mid, hi)

    lo, hi = lax.fori_loop(0, BISECT_ITERS, bisect, (lo, hi))

    def max_le(t):
        def body(c0, c1, acc):
            for c in (c0, c1):
                s = score_ref[c]
                acc = jnp.maximum(acc, fold(jnp.where(s <= t, s, -jnp.inf), jnp.max))
            return acc
        return jnp.max(pair_loop(body, jnp.full(folded, -jnp.inf, F32)), axis=0, keepdims=True)

    n_causal = (i * tq + lax.broadcasted_iota(jnp.int32, (1, tq), 1) + 1).astype(F32)
    small = n_causal <= kf

    def refine_cond(carry):
        return jnp.min(carry[2]) < 0.5

    def refine_body(carry):
        m, thr, done, thr_cnt = carry

        def body(c0, c1, acc):
            cnt, nxt = acc
            for c in (c0, c1):
                s = score_ref[c]
                cnt = cnt + fold(jnp.where(s >= m, 1.0, 0.0), jnp.sum)
                nxt = jnp.maximum(nxt, fold(jnp.where(s < m, s, -jnp.inf), jnp.max))
            return cnt, nxt

        cnt, nxt = pair_loop(body, (jnp.zeros(folded, F32), jnp.full(folded, -jnp.inf, F32)))
        cnt = jnp.sum(cnt, axis=0, keepdims=True)
        nxt = jnp.max(nxt, axis=0, keepdims=True)
        reached = cnt >= kf
        hit = jnp.logical_and(reached, done < 0.5)
        return nxt, jnp.where(hit, m, thr), jnp.where(reached, 1.0, done), jnp.where(hit, cnt, thr_cnt)

    init = (max_le(hi), jnp.full((1, tq), -jnp.inf, F32), jnp.where(small, 1.0, 0.0), jnp.zeros((1, tq), F32))
    _, thr, _, thr_cnt = lax.while_loop(refine_cond, refine_body, init)

    def mask_exact_k():
        def body(c0, c1, carry):
            for c in (c0, c1):
                s = score_ref[c]
                sel = jnp.logical_and(s >= thr, s > -jnp.inf)
                score_ref[c] = jnp.where(sel, 0.0, NEG_BIG)
            return carry
        pair_loop(body, 0)

    def mask_with_ties():
        quota = jnp.where(small, 0.0, kf - count_gt(thr))
        earlier = (lax.broadcasted_iota(jnp.int32, (KEY_CHUNK, KEY_CHUNK), 1)
                   < lax.broadcasted_iota(jnp.int32, (KEY_CHUNK, KEY_CHUNK), 0)).astype(BF16)

        def body(c0, c1, seen):
            for c in (c0, c1):
                s = score_ref[c]
                eq = s == thr
                eqf = jnp.where(eq, 1.0, 0.0)
                before = jnp.dot(earlier, eqf.astype(BF16), preferred_element_type=F32) + seen
                sel = jnp.logical_or(s > thr, jnp.logical_and(eq, before < quota))
                seen = seen + jnp.sum(eqf, axis=0, keepdims=True)
                score_ref[c] = jnp.where(sel, 0.0, NEG_BIG)
            return seen
        pair_loop(body, jnp.zeros((1, tq), F32))

    surplus = jnp.max(jnp.where(small, 0.0, thr_cnt - kf)) > 0.5
    lax.cond(surplus, mask_with_ties, mask_exact_k)

    def qk_into(slot, c):
        kc = ka_ref[0, pl.ds(pl.multiple_of(c * KEY_CHUNK, KEY_CHUNK), KEY_CHUNK), :]
        logit_ref[slot] = lax.dot_general(kc, qq, contract_last, preferred_element_type=F32)

    def softmax_pv(slot, c, m_run, acc):
        bias = score_ref[c]
        vt = vt_ref[0, c]
        new_m, new_acc = [], []
        for hp in range(heads // 2):
            probs, alphas = [], []
            for hd in (2 * hp, 2 * hp + 1):
                sl = slice(hd * tq, (hd + 1) * tq)
                lh = logit_ref[slot, :, sl] + bias
                m_old = m_run[:, sl]
                m_new = jnp.maximum(m_old, jnp.max(lh, axis=0, keepdims=True))
                probs.append(jnp.exp2(lh - m_new).astype(BF16))
                alphas.append(jnp.exp2(m_old - m_new))
                new_m.append(m_new)
            pv = jnp.dot(vt, jnp.concatenate(probs, axis=1), preferred_element_type=F32)
            new_acc.append(jnp.concatenate(alphas, axis=1) * acc[:, 2 * hp * tq:(2 * hp + 2) * tq] + pv)
        return jnp.concatenate(new_m, axis=1), jnp.concatenate(new_acc, axis=1)

    def attn_body(c0, c1, carry):
        m_run, acc = carry
        qk_into(1, c1)
        m_run, acc = softmax_pv(0, c0, m_run, acc)
        qk_into(0, jnp.minimum(c0 + 2, 2 * npair - 1))
        return softmax_pv(1, c1, m_run, acc)

    qk_into(0, 0)
    d_rows = vt_ref.shape[2]
    init = (jnp.full((1, heads * tq), NEG_BIG, F32), jnp.zeros((d_rows, heads * tq), F32))
    _, acc = pair_loop(attn_body, init)
    pad_rows = jnp.zeros((LANES - d_rows, tq), F32)
    for hd in range(heads):
        blk = acc[:, hd * tq:(hd + 1) * tq]
        out_t = jnp.concatenate([blk / blk[DSA_HEAD_DIM:DSA_HEAD_DIM + 1, :], pad_rows], axis=0)
        o_ref[0, hd] = out_t.T.astype(o_ref.dtype)


def _dsa_attention(qq, iw, ka, va, ki, topk):
    b, s, wide = qq.shape
    tq = Q_TILE_DSA
    nkc = s // KEY_CHUNK
    iw_t = jnp.swapaxes(iw[..., :IDX_HEADS], 1, 2)
    v_t = jnp.swapaxes(va.reshape(b, nkc, KEY_CHUNK, LANES), 2, 3)[:, :, :DSA_V_ROWS]
    qspec = pl.BlockSpec((1, tq, wide), lambda bi, i: (bi, i, 0))
    kspec = pl.BlockSpec((1, s, LANES), lambda bi, i: (bi, 0, 0))
    return pl.pallas_call(
        functools.partial(_dsa_kernel, topk=topk),
        grid=(b, s // tq),
        in_specs=[qspec, pl.BlockSpec((1, IDX_HEADS, tq), lambda bi, i: (bi, 0, i)), kspec,
                  pl.BlockSpec((1, nkc, DSA_V_ROWS, KEY_CHUNK), lambda bi, i: (bi, 0, 0, 0)), kspec],
        out_specs=pl.BlockSpec((1, DSA_HEADS, tq, LANES), lambda bi, i: (bi, 0, i, 0)),
        out_shape=jax.ShapeDtypeStruct((b, DSA_HEADS, s, LANES), BF16),
        scratch_shapes=[pltpu.VMEM((nkc, KEY_CHUNK, tq), F32),
                        pltpu.VMEM((2, KEY_CHUNK, DSA_HEADS * tq), F32)],
        compiler_params=_params(("parallel", "arbitrary")),
        name="dsa_attn",
    )(qq, iw_t, ka, v_t, ki)


def _mla_kernel(q_ref, k_ref, vt_ref, o_ref, logit_ref):
    i = pl.program_id(2)
    tq = q_ref.shape[1]
    group = q_ref.shape[2] // LANES
    d_rows = vt_ref.shape[3]
    qs = [q_ref[0, :, g * LANES:(g + 1) * LANES] for g in range(group)]
    contract_last = (((1,), (1,)), ((), ()))

    def qk_into(slot, c):
        k0 = pl.multiple_of(c * tq, tq)
        for g in range(group):
            logit_ref[slot, g] = lax.dot_general(k_ref[0, pl.ds(k0, tq), g * LANES:(g + 1) * LANES], qs[g],
                                                 contract_last, preferred_element_type=F32)

    def softmax_pv(slot, c, carry, masked):
        if masked:
            keep = (lax.broadcasted_iota(jnp.int32, (tq, tq), 0) <= lax.broadcasted_iota(jnp.int32, (tq, tq), 1))
            bias = jnp.where(keep, 0.0, NEG_BIG)
        probs, alphas, maxes = [], [], []
        for g in range(group):
            m_run = carry[2 * g]
            logit = logit_ref[slot, g]
            if masked:
                logit = logit + bias
            m_new = jnp.maximum(m_run, jnp.max(logit, axis=0, keepdims=True))
            probs.append(jnp.exp2(logit - m_new).astype(BF16))
            alphas.append(jnp.exp2(m_run - m_new))
            maxes.append(m_new)
        res = []
        for g in range(group):
            pv = jnp.dot(vt_ref[0, g, c], probs[g], preferred_element_type=F32)
            res += [maxes[g], alphas[g] * carry[2 * g + 1] + pv]
        return tuple(res)

    def pair_body(j, carry):
        qk_into(1, 2 * j + 1)
        carry = softmax_pv(0, 2 * j, carry, False)
        qk_into(0, 2 * j + 2)
        return softmax_pv(1, 2 * j + 1, carry, False)

    def tail_even(carry):
        return softmax_pv(0, i, carry, True)

    def tail_odd(carry):
        qk_into(1, i)
        carry = softmax_pv(0, i - 1, carry, False)
        return softmax_pv(1, i, carry, True)

    qk_into(0, 0)
    init = (jnp.full((1, tq), NEG_BIG, F32), jnp.zeros((d_rows, tq), F32)) * group
    carry = lax.fori_loop(0, i // 2, pair_body, init)
    carry = lax.cond(i % 2 == 0, tail_even, tail_odd, carry)
    pad_rows = jnp.zeros((LANES - d_rows, tq), F32)
    for g in range(group):
        acc = carry[2 * g + 1]
        out_t = jnp.concatenate([acc / acc[MLA_V:MLA_V + 1, :], pad_rows], axis=0)
        o_ref[0, :, g * LANES:(g + 1) * LANES] = out_t.T.astype(o_ref.dtype)


def _mla_attention(mq, mk, mv):
    b, s, wide = mq.shape
    tq = Q_TILE_MLA
    nkc = s // tq
    gw = MLA_HEAD_GROUP * LANES
    v_t = mv.reshape(b, nkc, tq, MLA_HEADS, LANES)[..., :MLA_V_ROWS]
    v_t = jnp.transpose(v_t, (0, 3, 1, 4, 2))
    return pl.pallas_call(
        _mla_kernel,
        grid=(b, MLA_HEADS // MLA_HEAD_GROUP, s // tq),
        in_specs=[pl.BlockSpec((1, tq, gw), lambda bi, h, i: (bi, i, h)),
                  pl.BlockSpec((1, s, gw), lambda bi, h, i: (bi, 0, h)),
                  pl.BlockSpec((1, MLA_HEAD_GROUP, nkc, MLA_V_ROWS, tq), lambda bi, h, i: (bi, h, 0, 0, 0))],
        out_specs=pl.BlockSpec((1, tq, gw), lambda bi, h, i: (bi, i, h)),
        out_shape=jax.ShapeDtypeStruct((b, s, wide), BF16),
        scratch_shapes=[pltpu.VMEM((2, MLA_HEAD_GROUP, tq, tq), F32)],
        compiler_params=_params(("parallel", "parallel", "arbitrary")),
        name="mla_attn",
    )(mq, mk, v_t)


def _mem_kv_kernel(mem_ref, g_ref, wkv_ref, kg_ref, k_o, v_o):
    m = mem_ref[0]
    hm = _rms(m, g_ref[...], m.shape[-1]).astype(BF16)
    kv = jnp.dot(hm, wkv_ref[...], preferred_element_type=F32)
    for hd in range(MEM_HEADS):
        sl = slice(hd * LANES, (hd + 1) * LANES)
        k_o[0, :, sl] = _rms(kv[:, sl], kg_ref[...], MEM_HEAD_DIM).astype(BF16)
    v_o[0] = kv[:, MEM_HEADS * LANES:].astype(BF16)


def _mem_kv(mem, mem_g, w_kv, k_g):
    b, m, d = mem.shape
    hw = MEM_HEADS * MEM_HEAD_DIM
    wkv = jnp.concatenate([_pad_heads(w_kv[:, :hw], MEM_HEADS, MEM_HEAD_DIM),
                           _pad_heads(w_kv[:, hw:], MEM_HEADS, MEM_HEAD_DIM)], axis=1).astype(BF16)
    consts = [mem_g.reshape(1, d), wkv, _pad_cols(k_g.reshape(1, -1))]
    wide = MEM_HEADS * LANES
    spec = pl.BlockSpec((1, m, wide), lambda bi: (bi, 0, 0))
    return pl.pallas_call(
        _mem_kv_kernel,
        grid=(b,),
        in_specs=[pl.BlockSpec((1, m, d), lambda bi: (bi, 0, 0))] + [_const_spec(c.shape) for c in consts],
        out_specs=[spec, spec],
        out_shape=[jax.ShapeDtypeStruct((b, m, wide), BF16)] * 2,
        compiler_params=_params(("parallel",)),
        name="mem_kv",
    )(mem, *consts)


def _split_dot(a, w):
    a_hi = a.astype(BF16)
    a_lo = (a - a_hi.astype(F32)).astype(BF16)
    w_hi = w.astype(BF16)
    w_lo = (w - w_hi.astype(F32)).astype(BF16)
    w_both = jnp.concatenate([w_hi, w_lo], axis=1)
    n = w.shape[1]
    hi = jnp.dot(a_hi, w_both, preferred_element_type=F32)
    lo = jnp.dot(a_lo, w_both, preferred_element_type=F32)
    return (hi[:, :n] + hi[:, n:]) + (lo[:, :n] + lo[:, n:])


def _merge_kernel(oa_ref, ob_ref, sgd_ref, sgm_ref, x_ref, wa_ref, wb_ref, wo_ref, mxg_ref, wq_ref, qg_ref,
                  km_ref, vm_ref, wmo_ref, moeg_ref, wr_ref, bias_ref, x2_o, h3_o, comb_o):
    d = functools.partial(jnp.dot, preferred_element_type=F32)
    oa = jnp.concatenate([oa_ref[0, hd] for hd in range(DSA_HEADS)], axis=-1)
    merged = (sgd_ref[0].astype(F32) * d(oa, wa_ref[...]) + sgm_ref[0].astype(F32) * d(ob_ref[0], wb_ref[...]))
    x1 = x_ref[0] + d(merged.astype(BF16), wo_ref[...])

    d_model = x1.shape[-1]
    h2 = _rms(x1, mxg_ref[...], d_model).astype(BF16)
    qm = d(h2, wq_ref[...])
    outs = []
    for hd in range(MEM_HEADS):
        sl = slice(hd * LANES, (hd + 1) * LANES)
        q = (_rms(qm[:, sl], qg_ref[...], MEM_HEAD_DIM) * MEM_HEAD_DIM ** -0.5).astype(BF16)
        logit = lax.dot_general(q, km_ref[0, :, sl], (((1,), (1,)), ((), ())), preferred_element_type=F32)
        logit = logit - jnp.max(logit, axis=-1, keepdims=True)
        p = jnp.exp(logit)
        pv = d(p.astype(BF16), vm_ref[0, :, sl])
        outs.append((pv / jnp.sum(p, axis=-1, keepdims=True)).astype(BF16))
    x2 = x1 + d(jnp.concatenate(outs, axis=-1), wmo_ref[...])
    x2_o[0] = x2

    h3 = _rms(x2, moeg_ref[...], d_model)
    h3_o[0] = h3.astype(BF16)
    logits = _split_dot(h3, wr_ref[...])
    tm = logits.shape[0]
    lane = lax.broadcasted_iota(jnp.int32, (tm, LANES), 1)
    big = jnp.int32(LANES)
    is_grp = jnp.logical_and(lane >= N_EXPERTS, lane < N_EXPERTS + N_GROUPS)
    glog = jnp.where(is_grp, logits, -jnp.inf)
    gmax = jnp.max(glog, axis=-1, keepdims=True)
    g_sel = jnp.min(jnp.where(glog == gmax, lane, big), axis=-1, keepdims=True) - N_EXPERTS
    p_g = 1.0 / jnp.sum(jnp.where(is_grp, jnp.exp(logits - gmax), 0.0), axis=-1, keepdims=True)
    aff = jax.nn.sigmoid(logits)
    in_grp = jnp.logical_and(lane >= g_sel * EXPERTS_PER_GROUP, lane < (g_sel + 1) * EXPERTS_PER_GROUP)
    val = jnp.where(in_grp, aff + bias_ref[...], -jnp.inf)
    m1 = jnp.max(val, axis=-1, keepdims=True)
    i1 = jnp.min(jnp.where(val == m1, lane, big), axis=-1, keepdims=True)
    val2 = jnp.where(lane == i1, -jnp.inf, val)
    m2 = jnp.max(val2, axis=-1, keepdims=True)
    i2 = jnp.min(jnp.where(val2 == m2, lane, big), axis=-1, keepdims=True)
    chosen = jnp.logical_or(lane == i1, lane == i2)
    a_sel = jnp.where(chosen, aff, 0.0)
    comb_o[0] = p_g * a_sel / jnp.sum(a_sel, axis=-1, keepdims=True)


def _merge_mem_router(oa, ob, sgd, sgm, x, w_br_dsa, w_br_mla, w_out, mem_x_g, mem_w_q, mem_q_g, km, vm,
                      mem_w_o, moe_g, w_group, w_expert, expert_bias, tm):
    b, s, d = x.shape
    wa = jnp.pad(w_br_dsa.reshape(DSA_HEADS, DSA_HEAD_DIM, d), ((0, 0), (0, LANES - DSA_HEAD_DIM), (0, 0)))
    wa = wa.reshape(DSA_HEADS * LANES, d).astype(BF16)
    wb = jnp.pad(w_br_mla.reshape(MLA_HEADS, MLA_V, d), ((0, 0), (0, LANES - MLA_V), (0, 0)))
    wb = wb.reshape(MLA_HEADS * LANES, d).astype(BF16)
    wq = _pad_heads(mem_w_q, MEM_HEADS, MEM_HEAD_DIM).astype(BF16)
    wmo = jnp.pad(mem_w_o.reshape(MEM_HEADS, MEM_HEAD_DIM, d), ((0, 0), (0, LANES - MEM_HEAD_DIM), (0, 0)))
    wmo = wmo.reshape(MEM_HEADS * LANES, d).astype(BF16)
    wr = _pad_cols(jnp.concatenate([w_expert, w_group], axis=1).astype(F32))
    bias = _pad_cols(expert_bias.reshape(1, -1).astype(F32))
    consts_a = [wa, wb, w_out.astype(BF16), mem_x_g.reshape(1, d), wq, _pad_cols(mem_q_g.reshape(1, -1))]
    consts_b = [wmo, moe_g.reshape(1, d), wr, bias]
    m = km.shape[1]
    mw = MEM_HEADS * LANES
    tok = lambda w: pl.BlockSpec((1, tm, w), lambda bi, i: (bi, i, 0))
    memspec = pl.BlockSpec((1, m, mw), lambda bi, i: (bi, 0, 0))
    return pl.pallas_call(
        _merge_kernel,
        grid=(b, s // tm),
        in_specs=[pl.BlockSpec((1, DSA_HEADS, tm, LANES), lambda bi, i: (bi, 0, i, 0)),
                  tok(MLA_HEADS * LANES), tok(d), tok(d), tok(d)]
                 + [_const_spec(c.shape) for c in consts_a] + [memspec, memspec]
                 + [_const_spec(c.shape) for c in consts_b],
        out_specs=[tok(d), tok(d), tok(LANES)],
        out_shape=[jax.ShapeDtypeStruct((b, s, d), F32), jax.ShapeDtypeStruct((b, s, d), BF16),
                   jax.ShapeDtypeStruct((b, s, LANES), F32)],
        compiler_params=_params(("parallel", "parallel")),
        name="merge_mem",
    )(oa, ob, sgd, sgm, x, *consts_a, km, vm, *consts_b)


def _moe_kernel(h_ref, comb_ref, x_ref, wgu_ref, wd_ref, o_ref):
    g = pl.program_id(1)
    width = EXPERTS_PER_GROUP * D_EXPERT
    gu = jnp.dot(h_ref[...], wgu_ref[0], preferred_element_type=F32)
    act = jax.nn.silu(gu[:, :width]) * gu[:, width:]
    comb = comb_ref[...]
    lane = lax.broadcasted_iota(jnp.int32, comb.shape, 1)
    parts = []
    for e in range(EXPERTS_PER_GROUP):
        c_e = jnp.sum(jnp.where(lane == g * EXPERTS_PER_GROUP + e, comb, 0.0), axis=-1, keepdims=True)
        parts.append((act[:, e * D_EXPERT:(e + 1) * D_EXPERT] * c_e).astype(BF16))
    contrib = jnp.dot(jnp.concatenate(parts, axis=1), wd_ref[0], preferred_element_type=F32)

    @pl.when(g == 0)
    def _():
        o_ref[...] = x_ref[...] + contrib

    @pl.when(g > 0)
    def _():
        o_ref[...] += contrib


def _moe(h3, comb, x2, w_gate, w_up, w_down, tm):
    n, d = x2.shape
    width = EXPERTS_PER_GROUP * D_EXPERT
    by_group = lambda w: jnp.moveaxis(w.reshape(N_GROUPS, EXPERTS_PER_GROUP, d, D_EXPERT), 1, 2).reshape(N_GROUPS, d, width)
    wgu = jnp.concatenate([by_group(w_gate), by_group(w_up)], axis=-1).astype(BF16)
    wd = w_down.reshape(N_GROUPS, width, d).astype(BF16)
    return pl.pallas_call(
        _moe_kernel,
        grid=(n // tm, N_GROUPS),
        in_specs=[pl.BlockSpec((tm, d), lambda i, g: (i, 0)),
                  pl.BlockSpec((tm, LANES), lambda i, g: (i, 0)),
                  pl.BlockSpec((tm, d), lambda i, g: (i, 0)),
                  pl.BlockSpec((1, d, 2 * width), lambda i, g: (g, 0, 0)),
                  pl.BlockSpec((1, width, d), lambda i, g: (g, 0, 0))],
        out_specs=pl.BlockSpec((tm, d), lambda i, g: (i, 0)),
        out_shape=jax.ShapeDtypeStruct((n, d), F32),
        compiler_params=_params(("parallel", "arbitrary")),
        name="moe",
    )(h3, comb, x2, wgu, wd)


def _layer(x, mem, positions, p):
    b, s, d = x.shape
    n = b * s
    assert s % (2 * KEY_CHUNK) == 0 and s % Q_TILE_MLA == 0
    topk = min(TOPK_MAX, s // 4)
    tm = min(256, s)
    qq, ka, va, ki, iw, mq, mk, mv, sgd, sgm = _in_proj(
        x.reshape(n, d), positions.reshape(n, 1), p["attn_norm_g"], p["w_in"], p["dsa_q_norm_g"],
        p["dsa_k_norm_g"], p["mla_cq_norm_g"], p["mla_ckv_norm_g"], p["mla_w_uq"], p["mla_w_ukv"],
        p["mla_q_norm_g"], p["mla_k_norm_g"], tm)
    r3 = lambda a: a.reshape(b, s, a.shape[-1])
    oa = _dsa_attention(r3(qq), r3(iw), r3(ka), r3(va), r3(ki), topk)
    ob = _mla_attention(r3(mq), r3(mk), r3(mv))
    km, vm = _mem_kv(mem, p["mem_norm_g"], p["mem_w_kv"], p["mem_k_norm_g"])
    x2, h3, comb = _merge_mem_router(
        oa, ob, r3(sgd), r3(sgm), x, p["w_branch_dsa"], p["w_branch_mla"], p["w_out"], p["mem_x_norm_g"],
        p["mem_w_q"], p["mem_q_norm_g"], km, vm, p["mem_w_o"], p["moe_norm_g"], p["moe_w_group"],
        p["moe_w_expert"], p["moe_expert_bias"], min(512, s))
    out = _moe(h3.reshape(n, d), comb.reshape(n, LANES), x2.reshape(n, d), p["moe_w_gate"], p["moe_w_up"],
               p["moe_w_down"], min(1024, n))
    return out.reshape(b, s, d)


_PARAM_NAMES = ("attn_norm_g", "w_in", "dsa_q_norm_g", "dsa_k_norm_g", "mla_cq_norm_g", "mla_ckv_norm_g",
                "mla_w_uq", "mla_w_ukv", "mla_q_norm_g", "mla_k_norm_g", "w_branch_dsa", "w_branch_mla", "w_out",
                "mem_x_norm_g", "mem_norm_g", "mem_w_q", "mem_w_kv", "mem_q_norm_g", "mem_k_norm_g", "mem_w_o",
                "moe_norm_g", "moe_w_group", "moe_w_expert", "moe_expert_bias", "moe_w_gate", "moe_w_up",
                "moe_w_down")


def kernel(x, mem, positions, attn_norm_g, w_in, dsa_q_norm_g, dsa_k_norm_g, mla_cq_norm_g, mla_ckv_norm_g, mla_w_uq, mla_w_ukv, mla_q_norm_g, mla_k_norm_g, w_branch_dsa, w_branch_mla, w_out, mem_x_norm_g, mem_norm_g, mem_w_q, mem_w_kv, mem_q_norm_g, mem_k_norm_g, mem_w_o, moe_norm_g, moe_w_group, moe_w_expert, moe_expert_bias, moe_w_gate, moe_w_up, moe_w_down):
    stacked = (attn_norm_g, w_in, dsa_q_norm_g, dsa_k_norm_g, mla_cq_norm_g, mla_ckv_norm_g, mla_w_uq, mla_w_ukv,
               mla_q_norm_g, mla_k_norm_g, w_branch_dsa, w_branch_mla, w_out, mem_x_norm_g, mem_norm_g, mem_w_q,
               mem_w_kv, mem_q_norm_g, mem_k_norm_g, mem_w_o, moe_norm_g, moe_w_group, moe_w_expert,
               moe_expert_bias, moe_w_gate, moe_w_up, moe_w_down)
    for layer in range(attn_norm_g.shape[0]):
        p = {name: arr[layer] for name, arr in zip(_PARAM_NAMES, stacked)}
        x = _layer(x, mem, positions, p)
    return x
```

```python
import functools

import jax
import jax.numpy as jnp
from jax import lax
from jax.experimental import pallas as pl
from jax.experimental.pallas import tpu as pltpu

F32 = jnp.float32
BF16 = jnp.bfloat16

LANES = 128
SUBLANES = 8
VMEM_LIMIT = 56 * 1024 * 1024

ROPE_THETA = 10000.0
EPS = 1e-6
DSA_HEADS = 8
DSA_HEAD_DIM = 64
IDX_HEADS = 8
IDX_DIM = 64
TOPK_MAX = 256
MLA_HEADS = 8
MLA_NOPE = 64
MLA_ROPE = 32
MLA_QK = MLA_NOPE + MLA_ROPE
MLA_V = 64
MLA_Q_RANK = 256
MLA_KV_RANK = 128
MEM_HEADS = 4
MEM_HEAD_DIM = 64
N_GROUPS = 4
EXPERTS_PER_GROUP = 4
N_EXPERTS = N_GROUPS * EXPERTS_PER_GROUP
D_EXPERT = 256

NEG_BIG = -1e30
KEY_CHUNK = 256
Q_TILE_DSA = 256
DSA_V_ROWS = 80
MLA_V_ROWS = 80
Q_TILE_MLA = 512
MLA_HEAD_GROUP = 4
LOG2_E = 1.4426950408889634
BISECT_ITERS = 15


def _const_spec(shape):
    nd = len(shape)
    return pl.BlockSpec(shape, lambda *_: (0,) * nd)


def _params(sem):
    return pltpu.CompilerParams(dimension_semantics=sem, vmem_limit_bytes=VMEM_LIMIT)


def _rms(x, g, n):
    ms = jnp.sum(x * x, axis=-1, keepdims=True) * (1.0 / n)
    return (x * lax.rsqrt(ms + EPS)) * g


def _rope(x, cos, sin_signed, lo_mask, half):
    fwd = pltpu.roll(x, LANES - half, 1)
    bwd = pltpu.roll(x, half, 1)
    return x * cos + jnp.where(lo_mask, fwd, bwd) * sin_signed


def _in_proj_kernel(x_ref, pos_ref, g_ref, wqq_ref, wsm_ref, wcq_ref, wg_ref, wuq_ref, wuk_ref,
                    wuv_ref, qag_ref, kag_ref, cqg_ref, ckvg_ref, mqg_ref, mkg_ref, freq_ref,
                    qq_o, ka_o, va_o, ki_o, iw_o, mq_o, mk_o, mv_o, sgd_o, sgm_o):
    x = x_ref[...]
    d_model = x.shape[-1]
    h = _rms(x, g_ref[...], d_model).astype(BF16)

    pos = pos_ref[...].astype(F32)
    lane = lax.broadcasted_iota(jnp.int32, (x.shape[0], LANES), 1)
    half_d, half_m = DSA_HEAD_DIM // 2, MLA_ROPE // 2
    ang = pos * freq_ref[...]
    lo_d = (lane & (DSA_HEAD_DIM - 1)) < half_d
    lo_m = lane < MLA_NOPE + half_m
    in_m_lo = jnp.logical_and(lane >= MLA_NOPE, lo_m)
    in_m_hi = jnp.logical_and(lane >= MLA_NOPE + half_m, lane < MLA_QK)

    def tables(t, fill):
        by_half = pltpu.roll(t, half_d, 1)
        head = jnp.where(lane < half_d, t, by_half)
        dsa = jnp.where(lane < DSA_HEAD_DIM, head, pltpu.roll(head, DSA_HEAD_DIM, 1))
        mla = jnp.where(in_m_lo, by_half, jnp.where(in_m_hi, pltpu.roll(t, half_d + half_m, 1), fill))
        return dsa, mla

    cos_d, cos_m = tables(jnp.cos(ang), 1.0)
    sin_d, sin_m = tables(jnp.sin(ang), 0.0)
    sin_d = jnp.where(lo_d, -sin_d, sin_d)
    sin_m = jnp.where(lo_m, -sin_m, sin_m)
    rope_d = functools.partial(_rope, cos=cos_d, sin_signed=sin_d, lo_mask=lo_d, half=DSA_HEAD_DIM // 2)
    rope_m = functools.partial(_rope, cos=cos_m, sin_signed=sin_m, lo_mask=lo_m, half=MLA_ROPE // 2)

    def dot(a, w_ref):
        return jnp.dot(a, w_ref[...], preferred_element_type=F32)

    att_scale = DSA_HEAD_DIM ** -0.5 * LOG2_E
    idx_scale = IDX_DIM ** -0.5 * IDX_HEADS ** -0.5
    mla_scale = MLA_QK ** -0.5 * LOG2_E

    qq = dot(h, wqq_ref)
    is_qa = lane < DSA_HEAD_DIM
    for hd in range(DSA_HEADS):
        sl = slice(hd * LANES, (hd + 1) * LANES)
        xh = qq[:, sl]
        ms = jnp.sum(jnp.where(is_qa, xh * xh, 0.0), axis=-1, keepdims=True) * (1.0 / DSA_HEAD_DIM)
        mult = jnp.where(is_qa, (lax.rsqrt(ms + EPS) * att_scale) * qag_ref[...], 1.0)
        qq_o[:, sl] = rope_d(xh * mult).astype(BF16)

    sm = dot(h, wsm_ref)
    ka_o[...] = rope_d(_rms(sm[:, 0:LANES], kag_ref[...], DSA_HEAD_DIM)).astype(BF16)
    va_o[...] = (sm[:, LANES:2 * LANES] + jnp.where(lane[0:1] == DSA_HEAD_DIM, 1.0, 0.0)).astype(BF16)
    ki_o[...] = rope_d(sm[:, 2 * LANES:3 * LANES]).astype(BF16)
    iw_o[...] = sm[:, 3 * LANES:4 * LANES] * idx_scale
    ckv = _rms(sm[:, 4 * LANES:5 * LANES], ckvg_ref[...], MLA_KV_RANK).astype(BF16)
    kpe = sm[:, 5 * LANES:6 * LANES]

    kn = dot(ckv, wuk_ref)
    mkg = mkg_ref[...]
    pe_rot = rope_m(kpe * mkg)
    ss_pe = jnp.sum(kpe * kpe, axis=-1, keepdims=True)
    for hd in range(MLA_HEADS):
        sl = slice(hd * LANES, (hd + 1) * LANES)
        xh = kn[:, sl]
        ms = (jnp.sum(xh * xh, axis=-1, keepdims=True) + ss_pe) * (1.0 / MLA_QK)
        mk_o[:, sl] = ((xh * mkg + pe_rot) * lax.rsqrt(ms + EPS)).astype(BF16)
    wide_lane = lax.broadcasted_iota(jnp.int32, (1, MLA_HEADS * LANES), 1)
    ones_col = jnp.where(wide_lane % LANES == MLA_V, 1.0, 0.0)
    mv_o[...] = (dot(ckv, wuv_ref) + ones_col).astype(BF16)

    cq = _rms(dot(h, wcq_ref), cqg_ref[...], MLA_Q_RANK).astype(BF16)
    qb = dot(cq, wuq_ref)
    for hd in range(MLA_HEADS):
        sl = slice(hd * LANES, (hd + 1) * LANES)
        y = rope_m(_rms(qb[:, sl], mqg_ref[...], MLA_QK))
        mq_o[:, sl] = (y * mla_scale).astype(BF16)

    gates = jax.nn.sigmoid(dot(h, wg_ref))
    sgd_o[...] = gates[:, :d_model].astype(BF16)
    sgm_o[...] = gates[:, d_model:].astype(BF16)


def _pad_heads(w, heads, dim):
    k = w.shape[0]
    w = w.reshape(k, heads, dim)
    w = jnp.pad(w, ((0, 0), (0, 0), (0, LANES - dim)))
    return w.reshape(k, heads * LANES)


def _pad_cols(w, width=LANES, offset=0):
    return jnp.pad(w, ((0, 0), (offset, width - offset - w.shape[1])))


def _in_proj(x2, pos2, attn_g, w_in, dsa_q_g, dsa_k_g, cq_g, ckv_g, w_uq, w_ukv, mq_g, mk_g, tm):
    n, d = x2.shape
    sizes = (DSA_HEADS * DSA_HEAD_DIM, DSA_HEAD_DIM, DSA_HEAD_DIM, IDX_HEADS * IDX_DIM, IDX_DIM, IDX_HEADS,
             MLA_Q_RANK, MLA_KV_RANK, MLA_ROPE, d, d)
    offs = [0]
    for s in sizes:
        offs.append(offs[-1] + s)
    seg = [w_in[:, offs[i]:offs[i + 1]] for i in range(len(sizes))]
    w_dq, w_dk, w_dv, w_iq, w_ik, w_iw, w_cq, w_ckv, w_kpe, w_gd, w_gm = seg

    assert DSA_HEADS == IDX_HEADS and DSA_HEAD_DIM + IDX_DIM == LANES
    wqq = jnp.concatenate([w_dq.reshape(d, DSA_HEADS, DSA_HEAD_DIM), w_iq.reshape(d, IDX_HEADS, IDX_DIM)], axis=-1)
    wqq = wqq.reshape(d, DSA_HEADS * LANES).astype(BF16)
    wsm = jnp.concatenate([_pad_cols(w_dk), _pad_cols(w_dv), _pad_cols(w_ik, offset=DSA_HEAD_DIM), _pad_cols(w_iw), w_ckv,
                           _pad_cols(w_kpe, offset=MLA_NOPE)], axis=1).astype(BF16)
    wcq = w_cq.astype(BF16)
    wg = jnp.concatenate([w_gd, w_gm], axis=1).astype(BF16)
    wuq = _pad_heads(w_uq, MLA_HEADS, MLA_QK).astype(BF16)
    ukv = w_ukv.reshape(MLA_KV_RANK, MLA_HEADS, MLA_NOPE + MLA_V)
    wuk = _pad_heads(ukv[:, :, :MLA_NOPE].reshape(MLA_KV_RANK, -1), MLA_HEADS, MLA_NOPE).astype(BF16)
    wuv = _pad_heads(ukv[:, :, MLA_NOPE:].reshape(MLA_KV_RANK, -1), MLA_HEADS, MLA_V).astype(BF16)

    row = lambda v: _pad_cols(v.reshape(1, -1).astype(F32), width=max(LANES, v.size))
    half_d = DSA_HEAD_DIM // 2
    inv_d = ROPE_THETA ** (-jnp.arange(half_d, dtype=F32) / half_d)
    half_m = MLA_ROPE // 2
    inv_m = ROPE_THETA ** (-jnp.arange(half_m, dtype=F32) / half_m)
    freq = _pad_cols(jnp.concatenate([inv_d, inv_m]).reshape(1, -1))

    consts = [attn_g.reshape(1, d), wqq, wsm, wcq, wg, wuq, wuk, wuv, row(dsa_q_g), row(dsa_k_g),
              row(cq_g), row(ckv_g), row(mq_g), row(mk_g), freq]
    wide = DSA_HEADS * LANES
    out_shapes = [
        jax.ShapeDtypeStruct((n, wide), BF16),
        jax.ShapeDtypeStruct((n, LANES), BF16),
        jax.ShapeDtypeStruct((n, LANES), BF16),
        jax.ShapeDtypeStruct((n, LANES), BF16),
        jax.ShapeDtypeStruct((n, LANES), F32),
        jax.ShapeDtypeStruct((n, wide), BF16),
        jax.ShapeDtypeStruct((n, wide), BF16),
        jax.ShapeDtypeStruct((n, wide), BF16),
        jax.ShapeDtypeStruct((n, d), BF16),
        jax.ShapeDtypeStruct((n, d), BF16),
    ]
    tile = lambda w: pl.BlockSpec((tm, w), lambda i: (i, 0))
    return pl.pallas_call(
        _in_proj_kernel,
        grid=(n // tm,),
        in_specs=[tile(d), tile(1)] + [_const_spec(c.shape) for c in consts],
        out_specs=[tile(s.shape[1]) for s in out_shapes],
        out_shape=out_shapes,
        compiler_params=_params(("parallel",)),
        name="in_proj",
    )(x2, pos2, *consts)


def _dsa_kernel(qq_ref, iw_ref, ka_ref, vt_ref, ki_ref, o_ref, score_ref, logit_ref, *, topk):
    i = pl.program_id(1)
    tq = qq_ref.shape[1]
    heads = DSA_HEADS
    npair = (i * tq + tq + 2 * KEY_CHUNK - 1) // (2 * KEY_CHUNK)
    kf = float(topk)
    shape = (KEY_CHUNK, tq)
    folded = (SUBLANES, tq)

    def fold(a, op):
        return op(a.reshape(KEY_CHUNK // SUBLANES, SUBLANES, tq), axis=0)

    krow = lax.broadcasted_iota(jnp.int32, shape, 0)
    qcol = i * tq + lax.broadcasted_iota(jnp.int32, shape, 1)
    contract_last = (((1,), (1,)), ((), ()))

    def pair_loop(body, init):
        return lax.fori_loop(0, npair, lambda j, carry: body(2 * j, 2 * j + 1, carry), init)

    qq = jnp.concatenate([qq_ref[0, :, hd * LANES:(hd + 1) * LANES] for hd in range(heads)], axis=0)
    iw = iw_ref[0]

    def score_body(c0, c1, carry):
        mx, mn = carry
        for c in (c0, c1):
            kc = ki_ref[0, pl.ds(pl.multiple_of(c * KEY_CHUNK, KEY_CHUNK), KEY_CHUNK), :]
            sc = None
            for hp in range(IDX_HEADS // 2):
                rel = lax.dot_general(kc, qq[2 * hp * tq:(2 * hp + 2) * tq], contract_last,
                                      preferred_element_type=F32)
                rel = jnp.maximum(rel, 0.0)
                part = rel[:, :tq] * iw[2 * hp:2 * hp + 1, :] + rel[:, tq:] * iw[2 * hp + 1:2 * hp + 2, :]
                sc = part if sc is None else sc + part
            causal = (krow + c * KEY_CHUNK) <= qcol
            masked = jnp.where(causal, sc, -jnp.inf)
            score_ref[c] = masked
            mx = jnp.maximum(mx, fold(masked, jnp.max))
            mn = jnp.minimum(mn, fold(jnp.where(causal, sc, jnp.inf), jnp.min))
        return mx, mn

    mx, mn = pair_loop(score_body, (jnp.full(folded, -jnp.inf, F32), jnp.full(folded, jnp.inf, F32)))
    hi = jnp.max(mx, axis=0, keepdims=True)
    lo = jnp.min(mn, axis=0, keepdims=True)

    def count_gt(t):
        def body(c0, c1, acc):
            for c in (c0, c1):
                acc = acc + fold(jnp.where(score_ref[c] > t, 1.0, 0.0), jnp.sum)
            return acc
        return jnp.sum(pair_loop(body, jnp.zeros(folded, F32)), axis=0, keepdims=True)

    def bisect(_, carry):
        lo, hi = carry
        mid = 0.5 * (lo + hi)
        below = count_gt(mid) < kf
        return jnp.where(below, lo, mid), jnp.where(below, mid, hi)

    lo, hi = lax.fori_loop(0, BISECT_ITERS, bisect, (lo, hi))

    def max_le(t):
        def body(c0, c1, acc):
            for c in (c0, c1):
                s = score_ref[c]
                acc = jnp.maximum(acc, fold(jnp.where(s <= t, s, -jnp.inf), jnp.max))
            return acc
        return jnp.max(pair_loop(body, jnp.full(folded, -jnp.inf, F32)), axis=0, keepdims=True)

    n_causal = (i * tq + lax.broadcasted_iota(jnp.int32, (1, tq), 1) + 1).astype(F32)
    small = n_causal <= kf

    def refine_cond(carry):
        return jnp.min(carry[2]) < 0.5

    def refine_body(carry):
        m, thr, done, thr_cnt = carry

        def body(c0, c1, acc):
            cnt, nxt = acc
            for c in (c0, c1):
                s = score_ref[c]
                cnt = cnt + fold(jnp.where(s >= m, 1.0, 0.0), jnp.sum)
                nxt = jnp.maximum(nxt, fold(jnp.where(s < m, s, -jnp.inf), jnp.max))
            return cnt, nxt

        cnt, nxt = pair_loop(body, (jnp.zeros(folded, F32), jnp.full(folded, -jnp.inf, F32)))
        cnt = jnp.sum(cnt, axis=0, keepdims=True)
        nxt = jnp.max(nxt, axis=0, keepdims=True)
        reached = cnt >= kf
        hit = jnp.logical_and(reached, done < 0.5)
        return nxt, jnp.where(hit, m, thr), jnp.where(reached, 1.0, done), jnp.where(hit, cnt, thr_cnt)

    init = (max_le(hi), jnp.full((1, tq), -jnp.inf, F32), jnp.where(small, 1.0, 0.0), jnp.zeros((1, tq), F32))
    _, thr, _, thr_cnt = lax.while_loop(refine_cond, refine_body, init)

    def mask_exact_k():
        def body(c0, c1, carry):
            for c in (c0, c1):
                s = score_ref[c]
                sel = jnp.logical_and(s >= thr, s > -jnp.inf)
                score_ref[c] = jnp.where(sel, 0.0, NEG_BIG)
            return carry
        pair_loop(body, 0)

    def mask_with_ties():
        quota = jnp.where(small, 0.0, kf - count_gt(thr))
        earlier = (lax.broadcasted_iota(jnp.int32, (KEY_CHUNK, KEY_CHUNK), 1)
                   < lax.broadcasted_iota(jnp.int32, (KEY_CHUNK, KEY_CHUNK), 0)).astype(BF16)

        def body(c0, c1, seen):
            for c in (c0, c1):
                s = score_ref[c]
                eq = s == thr
                eqf = jnp.where(eq, 1.0, 0.0)
                before = jnp.dot(earlier, eqf.astype(BF16), preferred_element_type=F32) + seen
                sel = jnp.logical_or(s > thr, jnp.logical_and(eq, before < quota))
                seen = seen + jnp.sum(eqf, axis=0, keepdims=True)
                score_ref[c] = jnp.where(sel, 0.0, NEG_BIG)
            return seen
        pair_loop(body, jnp.zeros((1, tq), F32))

    surplus = jnp.max(jnp.where(small, 0.0, thr_cnt - kf)) > 0.5
    lax.cond(surplus, mask_with_ties, mask_exact_k)

    def qk_into(slot, c):
        kc = ka_ref[0, pl.ds(pl.multiple_of(c * KEY_CHUNK, KEY_CHUNK), KEY_CHUNK), :]
        logit_ref[slot] = lax.dot_general(kc, qq, contract_last, preferred_element_type=F32)

    def softmax_pv(slot, c, m_run, acc):
        bias = score_ref[c]
        vt = vt_ref[0, c]
        new_m, new_acc = [], []
        for hp in range(heads // 2):
            probs, alphas = [], []
            for hd in (2 * hp, 2 * hp + 1):
                sl = slice(hd * tq, (hd + 1) * tq)
                lh = logit_ref[slot, :, sl] + bias
                m_old = m_run[:, sl]
                m_new = jnp.maximum(m_old, jnp.max(lh, axis=0, keepdims=True))
                probs.append(jnp.exp2(lh - m_new).astype(BF16))
                alphas.append(jnp.exp2(m_old - m_new))
                new_m.append(m_new)
            pv = jnp.dot(vt, jnp.concatenate(probs, axis=1), preferred_element_type=F32)
            new_acc.append(jnp.concatenate(alphas, axis=1) * acc[:, 2 * hp * tq:(2 * hp + 2) * tq] + pv)
        return jnp.concatenate(new_m, axis=1), jnp.concatenate(new_acc, axis=1)

    def attn_body(c0, c1, carry):
        m_run, acc = carry
        qk_into(1, c1)
        m_run, acc = softmax_pv(0, c0, m_run, acc)
        qk_into(0, jnp.minimum(c0 + 2, 2 * npair - 1))
        return softmax_pv(1, c1, m_run, acc)

    qk_into(0, 0)
    d_rows = vt_ref.shape[2]
    init = (jnp.full((1, heads * tq), NEG_BIG, F32), jnp.zeros((d_rows, heads * tq), F32))
    _, acc = pair_loop(attn_body, init)
    pad_rows = jnp.zeros((LANES - d_rows, tq), F32)
    for hd in range(heads):
        blk = acc[:, hd * tq:(hd + 1) * tq]
        out_t = jnp.concatenate([blk / blk[DSA_HEAD_DIM:DSA_HEAD_DIM + 1, :], pad_rows], axis=0)
        o_ref[0, hd] = out_t.T.astype(o_ref.dtype)


def _dsa_attention(qq, iw, ka, va, ki, topk):
    b, s, wide = qq.shape
    tq = Q_TILE_DSA
    nkc = s // KEY_CHUNK
    iw_t = jnp.swapaxes(iw[..., :IDX_HEADS], 1, 2)
    v_t = jnp.swapaxes(va.reshape(b, nkc, KEY_CHUNK, LANES), 2, 3)[:, :, :DSA_V_ROWS]
    qspec = pl.BlockSpec((1, tq, wide), lambda bi, i: (bi, i, 0))
    kspec = pl.BlockSpec((1, s, LANES), lambda bi, i: (bi, 0, 0))
    return pl.pallas_call(
        functools.partial(_dsa_kernel, topk=topk),
        grid=(b, s // tq),
        in_specs=[qspec, pl.BlockSpec((1, IDX_HEADS, tq), lambda bi, i: (bi, 0, i)), kspec,
                  pl.BlockSpec((1, nkc, DSA_V_ROWS, KEY_CHUNK), lambda bi, i: (bi, 0, 0, 0)), kspec],
        out_specs=pl.BlockSpec((1, DSA_HEADS, tq, LANES), lambda bi, i: (bi, 0, i, 0)),
        out_shape=jax.ShapeDtypeStruct((b, DSA_HEADS, s, LANES), BF16),
        scratch_shapes=[pltpu.VMEM((nkc, KEY_CHUNK, tq), F32),
                        pltpu.VMEM((2, KEY_CHUNK, DSA_HEADS * tq), F32)],
        compiler_params=_params(("parallel", "arbitrary")),
        name="dsa_attn",
    )(qq, iw_t, ka, v_t, ki)


def _mla_kernel(q_ref, k_ref, vt_ref, o_ref, logit_ref):
    i = pl.program_id(2)
    tq = q_ref.shape[1]
    group = q_ref.shape[2] // LANES
    d_rows = vt_ref.shape[3]
    qs = [q_ref[0, :, g * LANES:(g + 1) * LANES] for g in range(group)]
    contract_last = (((1,), (1,)), ((), ()))

    def qk_into(slot, c):
        k0 = pl.multiple_of(c * tq, tq)
        for g in range(group):
            logit_ref[slot, g] = lax.dot_general(k_ref[0, pl.ds(k0, tq), g * LANES:(g + 1) * LANES], qs[g],
                                                 contract_last, preferred_element_type=F32)

    def softmax_pv(slot, c, carry, masked):
        if masked:
            keep = (lax.broadcasted_iota(jnp.int32, (tq, tq), 0) <= lax.broadcasted_iota(jnp.int32, (tq, tq), 1))
            bias = jnp.where(keep, 0.0, NEG_BIG)
        probs, alphas, maxes = [], [], []
        for g in range(group):
            m_run = carry[2 * g]
            logit = logit_ref[slot, g]
            if masked:
                logit = logit + bias
            m_new = jnp.maximum(m_run, jnp.max(logit, axis=0, keepdims=True))
            probs.append(jnp.exp2(logit - m_new).astype(BF16))
            alphas.append(jnp.exp2(m_run - m_new))
            maxes.append(m_new)
        res = []
        for g in range(group):
            pv = jnp.dot(vt_ref[0, g, c], probs[g], preferred_element_type=F32)
            res += [maxes[g], alphas[g] * carry[2 * g + 1] + pv]
        return tuple(res)

    def pair_body(j, carry):
        qk_into(1, 2 * j + 1)
        carry = softmax_pv(0, 2 * j, carry, False)
        qk_into(0, 2 * j + 2)
        return softmax_pv(1, 2 * j + 1, carry, False)

    def tail_even(carry):
        return softmax_pv(0, i, carry, True)

    def tail_odd(carry):
        qk_into(1, i)
        carry = softmax_pv(0, i - 1, carry, False)
        return softmax_pv(1, i, carry, True)

    qk_into(0, 0)
    init = (jnp.full((1, tq), NEG_BIG, F32), jnp.zeros((d_rows, tq), F32)) * group
    carry = lax.fori_loop(0, i // 2, pair_body, init)
    carry = lax.cond(i % 2 == 0, tail_even, tail_odd, carry)
    pad_rows = jnp.zeros((LANES - d_rows, tq), F32)
    for g in range(group):
        acc = carry[2 * g + 1]
        out_t = jnp.concatenate([acc / acc[MLA_V:MLA_V + 1, :], pad_rows], axis=0)
        o_ref[0, :, g * LANES:(g + 1) * LANES] = out_t.T.astype(o_ref.dtype)


def _mla_attention(mq, mk, mv):
    b, s, wide = mq.shape
    tq = Q_TILE_MLA
    nkc = s // tq
    gw = MLA_HEAD_GROUP * LANES
    v_t = mv.reshape(b, nkc, tq, MLA_HEADS, LANES)[..., :MLA_V_ROWS]
    v_t = jnp.transpose(v_t, (0, 3, 1, 4, 2))
    return pl.pallas_call(
        _mla_kernel,
        grid=(b, MLA_HEADS // MLA_HEAD_GROUP, s // tq),
        in_specs=[pl.BlockSpec((1, tq, gw), lambda bi, h, i: (bi, i, h)),
                  pl.BlockSpec((1, s, gw), lambda bi, h, i: (bi, 0, h)),
                  pl.BlockSpec((1, MLA_HEAD_GROUP, nkc, MLA_V_ROWS, tq), lambda bi, h, i: (bi, h, 0, 0, 0))],
        out_specs=pl.BlockSpec((1, tq, gw), lambda bi, h, i: (bi, i, h)),
        out_shape=jax.ShapeDtypeStruct((b, s, wide), BF16),
        scratch_shapes=[pltpu.VMEM((2, MLA_HEAD_GROUP, tq, tq), F32)],
        compiler_params=_params(("parallel", "parallel", "arbitrary")),
        name="mla_attn",
    )(mq, mk, v_t)


def _mem_kv_kernel(mem_ref, g_ref, wkv_ref, kg_ref, k_o, v_o):
    m = mem_ref[0]
    hm = _rms(m, g_ref[...], m.shape[-1]).astype(BF16)
    kv = jnp.dot(hm, wkv_ref[...], preferred_element_type=F32)
    for hd in range(MEM_HEADS):
        sl = slice(hd * LANES, (hd + 1) * LANES)
        k_o[0, :, sl] = _rms(kv[:, sl], kg_ref[...], MEM_HEAD_DIM).astype(BF16)
    v_o[0] = kv[:, MEM_HEADS * LANES:].astype(BF16)


def _mem_kv(mem, mem_g, w_kv, k_g):
    b, m, d = mem.shape
    hw = MEM_HEADS * MEM_HEAD_DIM
    wkv = jnp.concatenate([_pad_heads(w_kv[:, :hw], MEM_HEADS, MEM_HEAD_DIM),
                           _pad_heads(w_kv[:, hw:], MEM_HEADS, MEM_HEAD_DIM)], axis=1).astype(BF16)
    consts = [mem_g.reshape(1, d), wkv, _pad_cols(k_g.reshape(1, -1))]
    wide = MEM_HEADS * LANES
    spec = pl.BlockSpec((1, m, wide), lambda bi: (bi, 0, 0))
    return pl.pallas_call(
        _mem_kv_kernel,
        grid=(b,),
        in_specs=[pl.BlockSpec((1, m, d), lambda bi: (bi, 0, 0))] + [_const_spec(c.shape) for c in consts],
        out_specs=[spec, spec],
        out_shape=[jax.ShapeDtypeStruct((b, m, wide), BF16)] * 2,
        compiler_params=_params(("parallel",)),
        name="mem_kv",
    )(mem, *consts)


def _split_dot(a, w):
    a_hi = a.astype(BF16)
    a_lo = (a - a_hi.astype(F32)).astype(BF16)
    w_hi = w.astype(BF16)
    w_lo = (w - w_hi.astype(F32)).astype(BF16)
    w_both = jnp.concatenate([w_hi, w_lo], axis=1)
    n = w.shape[1]
    hi = jnp.dot(a_hi, w_both, preferred_element_type=F32)
    lo = jnp.dot(a_lo, w_both, preferred_element_type=F32)
    return (hi[:, :n] + hi[:, n:]) + (lo[:, :n] + lo[:, n:])


def _merge_kernel(oa_ref, ob_ref, sgd_ref, sgm_ref, x_ref, wa_ref, wb_ref, wo_ref, mxg_ref, wq_ref, qg_ref,
                  km_ref, vm_ref, wmo_ref, moeg_ref, wr_ref, bias_ref, x2_o, h3_o, comb_o):
    d = functools.partial(jnp.dot, preferred_element_type=F32)
    oa = jnp.concatenate([oa_ref[0, hd] for hd in range(DSA_HEADS)], axis=-1)
    merged = (sgd_ref[0].astype(F32) * d(oa, wa_ref[...]) + sgm_ref[0].astype(F32) * d(ob_ref[0], wb_ref[...]))
    x1 = x_ref[0] + d(merged.astype(BF16), wo_ref[...])

    d_model = x1.shape[-1]
    h2 = _rms(x1, mxg_ref[...], d_model).astype(BF16)
    qm = d(h2, wq_ref[...])
    outs = []
    for hd in range(MEM_HEADS):
        sl = slice(hd * LANES, (hd + 1) * LANES)
        q = (_rms(qm[:, sl], qg_ref[...], MEM_HEAD_DIM) * MEM_HEAD_DIM ** -0.5).astype(BF16)
        logit = lax.dot_general(q, km_ref[0, :, sl], (((1,), (1,)), ((), ())), preferred_element_type=F32)
        logit = logit - jnp.max(logit, axis=-1, keepdims=True)
        p = jnp.exp(logit)
        pv = d(p.astype(BF16), vm_ref[0, :, sl])
        outs.append((pv / jnp.sum(p, axis=-1, keepdims=True)).astype(BF16))
    x2 = x1 + d(jnp.concatenate(outs, axis=-1), wmo_ref[...])
    x2_o[0] = x2

    h3 = _rms(x2, moeg_ref[...], d_model)
    h3_o[0] = h3.astype(BF16)
    logits = _split_dot(h3, wr_ref[...])
    tm = logits.shape[0]
    lane = lax.broadcasted_iota(jnp.int32, (tm, LANES), 1)
    big = jnp.int32(LANES)
    is_grp = jnp.logical_and(lane >= N_EXPERTS, lane < N_EXPERTS + N_GROUPS)
    glog = jnp.where(is_grp, logits, -jnp.inf)
    gmax = jnp.max(glog, axis=-1, keepdims=True)
    g_sel = jnp.min(jnp.where(glog == gmax, lane, big), axis=-1, keepdims=True) - N_EXPERTS
    p_g = 1.0 / jnp.sum(jnp.where(is_grp, jnp.exp(logits - gmax), 0.0), axis=-1, keepdims=True)
    aff = jax.nn.sigmoid(logits)
    in_grp = jnp.logical_and(lane >= g_sel * EXPERTS_PER_GROUP, lane < (g_sel + 1) * EXPERTS_PER_GROUP)
    val = jnp.where(in_grp, aff + bias_ref[...], -jnp.inf)
    m1 = jnp.max(val, axis=-1, keepdims=True)
    i1 = jnp.min(jnp.where(val == m1, lane, big), axis=-1, keepdims=True)
    val2 = jnp.where(lane == i1, -jnp.inf, val)
    m2 = jnp.max(val2, axis=-1, keepdims=True)
    i2 = jnp.min(jnp.where(val2 == m2, lane, big), axis=-1, keepdims=True)
    chosen = jnp.logical_or(lane == i1, lane == i2)
    a_sel = jnp.where(chosen, aff, 0.0)
    comb_o[0] = p_g * a_sel / jnp.sum(a_sel, axis=-1, keepdims=True)


def _merge_mem_router(oa, ob, sgd, sgm, x, w_br_dsa, w_br_mla, w_out, mem_x_g, mem_w_q, mem_q_g, km, vm,
                      mem_w_o, moe_g, w_group, w_expert, expert_bias, tm):
    b, s, d = x.shape
    wa = jnp.pad(w_br_dsa.reshape(DSA_HEADS, DSA_HEAD_DIM, d), ((0, 0), (0, LANES - DSA_HEAD_DIM), (0, 0)))
    wa = wa.reshape(DSA_HEADS * LANES, d).astype(BF16)
    wb = jnp.pad(w_br_mla.reshape(MLA_HEADS, MLA_V, d), ((0, 0), (0, LANES - MLA_V), (0, 0)))
    wb = wb.reshape(MLA_HEADS * LANES, d).astype(BF16)
    wq = _pad_heads(mem_w_q, MEM_HEADS, MEM_HEAD_DIM).astype(BF16)
    wmo = jnp.pad(mem_w_o.reshape(MEM_HEADS, MEM_HEAD_DIM, d), ((0, 0), (0, LANES - MEM_HEAD_DIM), (0, 0)))
    wmo = wmo.reshape(MEM_HEADS * LANES, d).astype(BF16)
    wr = _pad_cols(jnp.concatenate([w_expert, w_group], axis=1).astype(F32))
    bias = _pad_cols(expert_bias.reshape(1, -1).astype(F32))
    consts_a = [wa, wb, w_out.astype(BF16), mem_x_g.reshape(1, d), wq, _pad_cols(mem_q_g.reshape(1, -1))]
    consts_b = [wmo, moe_g.reshape(1, d), wr, bias]
    m = km.shape[1]
    mw = MEM_HEADS * LANES
    tok = lambda w: pl.BlockSpec((1, tm, w), lambda bi, i: (bi, i, 0))
    memspec = pl.BlockSpec((1, m, mw), lambda bi, i: (bi, 0, 0))
    return pl.pallas_call(
        _merge_kernel,
        grid=(b, s // tm),
        in_specs=[pl.BlockSpec((1, DSA_HEADS, tm, LANES), lambda bi, i: (bi, 0, i, 0)),
                  tok(MLA_HEADS * LANES), tok(d), tok(d), tok(d)]
                 + [_const_spec(c.shape) for c in consts_a] + [memspec, memspec]
                 + [_const_spec(c.shape) for c in consts_b],
        out_specs=[tok(d), tok(d), tok(LANES)],
        out_shape=[jax.ShapeDtypeStruct((b, s, d), F32), jax.ShapeDtypeStruct((b, s, d), BF16),
                   jax.ShapeDtypeStruct((b, s, LANES), F32)],
        compiler_params=_params(("parallel", "parallel")),
        name="merge_mem",
    )(oa, ob, sgd, sgm, x, *consts_a, km, vm, *consts_b)


def _moe_kernel(h_ref, comb_ref, x_ref, wgu_ref, wd_ref, o_ref):
    g = pl.program_id(1)
    width = EXPERTS_PER_GROUP * D_EXPERT
    gu = jnp.dot(h_ref[...], wgu_ref[0], preferred_element_type=F32)
    act = jax.nn.silu(gu[:, :width]) * gu[:, width:]
    comb = comb_ref[...]
    lane = lax.broadcasted_iota(jnp.int32, comb.shape, 1)
    parts = []
    for e in range(EXPERTS_PER_GROUP):
        c_e = jnp.sum(jnp.where(lane == g * EXPERTS_PER_GROUP + e, comb, 0.0), axis=-1, keepdims=True)
        parts.append((act[:, e * D_EXPERT:(e + 1) * D_EXPERT] * c_e).astype(BF16))
    contrib = jnp.dot(jnp.concatenate(parts, axis=1), wd_ref[0], preferred_element_type=F32)

    @pl.when(g == 0)
    def _():
        o_ref[...] = x_ref[...] + contrib

    @pl.when(g > 0)
    def _():
        o_ref[...] += contrib


def _moe(h3, comb, x2, w_gate, w_up, w_down, tm):
    n, d = x2.shape
    width = EXPERTS_PER_GROUP * D_EXPERT
    by_group = lambda w: jnp.moveaxis(w.reshape(N_GROUPS, EXPERTS_PER_GROUP, d, D_EXPERT), 1, 2).reshape(N_GROUPS, d, width)
    wgu = jnp.concatenate([by_group(w_gate), by_group(w_up)], axis=-1).astype(BF16)
    wd = w_down.reshape(N_GROUPS, width, d).astype(BF16)
    return pl.pallas_call(
        _moe_kernel,
        grid=(n // tm, N_GROUPS),
        in_specs=[pl.BlockSpec((tm, d), lambda i, g: (i, 0)),
                  pl.BlockSpec((tm, LANES), lambda i, g: (i, 0)),
                  pl.BlockSpec((tm, d), lambda i, g: (i, 0)),
                  pl.BlockSpec((1, d, 2 * width), lambda i, g: (g, 0, 0)),
                  pl.BlockSpec((1, width, d), lambda i, g: (g, 0, 0))],
        out_specs=pl.BlockSpec((tm, d), lambda i, g: (i, 0)),
        out_shape=jax.ShapeDtypeStruct((n, d), F32),
        compiler_params=_params(("parallel", "arbitrary")),
        name="moe",
    )(h3, comb, x2, wgu, wd)


def _layer(x, mem, positions, p):
    b, s, d = x.shape
    n = b * s
    assert s % (2 * KEY_CHUNK) == 0 and s % Q_TILE_MLA == 0
    topk = min(TOPK_MAX, s // 4)
    tm = min(256, s)
    qq, ka, va, ki, iw, mq, mk, mv, sgd, sgm = _in_proj(
        x.reshape(n, d), positions.reshape(n, 1), p["attn_norm_g"], p["w_in"], p["dsa_q_norm_g"],
        p["dsa_k_norm_g"], p["mla_cq_norm_g"], p["mla_ckv_norm_g"], p["mla_w_uq"], p["mla_w_ukv"],
        p["mla_q_norm_g"], p["mla_k_norm_g"], tm)
    r3 = lambda a: a.reshape(b, s, a.shape[-1])
    oa = _dsa_attention(r3(qq), r3(iw), r3(ka), r3(va), r3(ki), topk)
    ob = _mla_attention(r3(mq), r3(mk), r3(mv))
    km, vm = _mem_kv(mem, p["mem_norm_g"], p["mem_w_kv"], p["mem_k_norm_g"])
    x2, h3, comb = _merge_mem_router(
        oa, ob, r3(sgd), r3(sgm), x, p["w_branch_dsa"], p["w_branch_mla"], p["w_out"], p["mem_x_norm_g"],
        p["mem_w_q"], p["mem_q_norm_g"], km, vm, p["mem_w_o"], p["moe_norm_g"], p["moe_w_group"],
        p["moe_w_expert"], p["moe_expert_bias"], min(512, s))
    out = _moe(h3.reshape(n, d), comb.reshape(n, LANES), x2.reshape(n, d), p["moe_w_gate"], p["moe_w_up"],
               p["moe_w_down"], min(1024, n))
    return out.reshape(b, s, d)


_PARAM_NAMES = ("attn_norm_g", "w_in", "dsa_q_norm_g", "dsa_k_norm_g", "mla_cq_norm_g", "mla_ckv_norm_g",
                "mla_w_uq", "mla_w_ukv", "mla_q_norm_g", "mla_k_norm_g", "w_branch_dsa", "w_branch_mla", "w_out",
                "mem_x_norm_g", "mem_norm_g", "mem_w_q", "mem_w_kv", "mem_q_norm_g", "mem_k_norm_g", "mem_w_o",
                "moe_norm_g", "moe_w_group", "moe_w_expert", "moe_expert_bias", "moe_w_gate", "moe_w_up",
                "moe_w_down")


def kernel(x, mem, positions, attn_norm_g, w_in, dsa_q_norm_g, dsa_k_norm_g, mla_cq_norm_g, mla_ckv_norm_g, mla_w_uq, mla_w_ukv, mla_q_norm_g, mla_k_norm_g, w_branch_dsa, w_branch_mla, w_out, mem_x_norm_g, mem_norm_g, mem_w_q, mem_w_kv, mem_q_norm_g, mem_k_norm_g, mem_w_o, moe_norm_g, moe_w_group, moe_w_expert, moe_expert_bias, moe_w_gate, moe_w_up, moe_w_down):
    stacked = (attn_norm_g, w_in, dsa_q_norm_g, dsa_k_norm_g, mla_cq_norm_g, mla_ckv_norm_g, mla_w_uq, mla_w_ukv,
               mla_q_norm_g, mla_k_norm_g, w_branch_dsa, w_branch_mla, w_out, mem_x_norm_g, mem_norm_g, mem_w_q,
               mem_w_kv, mem_q_norm_g, mem_k_norm_g, mem_w_o, moe_norm_g, moe_w_group, moe_w_expert,
               moe_expert_bias, moe_w_gate, moe_w_up, moe_w_down)
    for layer in range(attn_norm_g.shape[0]):
        p = {name: arr[layer] for name, arr in zip(_PARAM_NAMES, stacked)}
        x = _layer(x, mem, positions, p)
    return x
```

```python
import functools

import jax
import jax.numpy as jnp
from jax import lax
from jax.experimental import pallas as pl
from jax.experimental.pallas import tpu as pltpu

F32 = jnp.float32
BF16 = jnp.bfloat16

LANES = 128
SUBLANES = 8
VMEM_LIMIT = 56 * 1024 * 1024

ROPE_THETA = 10000.0
EPS = 1e-6
DSA_HEADS = 8
DSA_HEAD_DIM = 64
IDX_HEADS = 8
IDX_DIM = 64
TOPK_MAX = 256
MLA_HEADS = 8
MLA_NOPE = 64
MLA_ROPE = 32
MLA_QK = MLA_NOPE + MLA_ROPE
MLA_V = 64
MLA_Q_RANK = 256
MLA_KV_RANK = 128
MEM_HEADS = 4
MEM_HEAD_DIM = 64
N_GROUPS = 4
EXPERTS_PER_GROUP = 4
N_EXPERTS = N_GROUPS * EXPERTS_PER_GROUP
D_EXPERT = 256

NEG_BIG = -1e30
KEY_CHUNK = 256
Q_TILE_DSA = 256
DSA_V_ROWS = 80
MLA_V_ROWS = 80
Q_TILE_MLA = 512
MLA_HEAD_GROUP = 4
LOG2_E = 1.4426950408889634
BISECT_ITERS = 15
TOKENS_PER_STEP_PROJ = 256
TOKENS_PER_STEP_MERGE = 512
TOKENS_PER_STEP_MOE = 1024


def _const_spec(shape):
    nd = len(shape)
    return pl.BlockSpec(shape, lambda *_: (0,) * nd)


def _params(sem):
    return pltpu.CompilerParams(dimension_semantics=sem, vmem_limit_bytes=VMEM_LIMIT)


def _rms(x, g, n):
    ms = jnp.sum(x * x, axis=-1, keepdims=True) * (1.0 / n)
    return (x * lax.rsqrt(ms + EPS)) * g


def _rope(x, cos, sin_signed, lo_mask, half):
    fwd = pltpu.roll(x, LANES - half, 1)
    bwd = pltpu.roll(x, half, 1)
    return x * cos + jnp.where(lo_mask, fwd, bwd) * sin_signed


def _in_proj_kernel(x_ref, pos_ref, g_ref, wqq_ref, wsm_ref, wcq_ref, wg_ref, wuq_ref, wuk_ref,
                    wuv_ref, qag_ref, kag_ref, cqg_ref, ckvg_ref, mqg_ref, mkg_ref, freq_ref,
                    qq_o, ka_o, va_o, ki_o, iw_o, mq_o, mk_o, mv_o, sgd_o, sgm_o):
    x = x_ref[...]
    d_model = x.shape[-1]
    h = _rms(x, g_ref[...], d_model).astype(BF16)

    pos = pos_ref[...].astype(F32)
    lane = lax.broadcasted_iota(jnp.int32, (x.shape[0], LANES), 1)
    half_d, half_m = DSA_HEAD_DIM // 2, MLA_ROPE // 2
    ang = pos * freq_ref[...]
    lo_d = (lane & (DSA_HEAD_DIM - 1)) < half_d
    lo_m = lane < MLA_NOPE + half_m
    in_m_lo = jnp.logical_and(lane >= MLA_NOPE, lo_m)
    in_m_hi = jnp.logical_and(lane >= MLA_NOPE + half_m, lane < MLA_QK)

    def tables(t, fill):
        by_half = pltpu.roll(t, half_d, 1)
        head = jnp.where(lane < half_d, t, by_half)
        dsa = jnp.where(lane < DSA_HEAD_DIM, head, pltpu.roll(head, DSA_HEAD_DIM, 1))
        mla = jnp.where(in_m_lo, by_half, jnp.where(in_m_hi, pltpu.roll(t, half_d + half_m, 1), fill))
        return dsa, mla

    cos_d, cos_m = tables(jnp.cos(ang), 1.0)
    sin_d, sin_m = tables(jnp.sin(ang), 0.0)
    sin_d = jnp.where(lo_d, -sin_d, sin_d)
    sin_m = jnp.where(lo_m, -sin_m, sin_m)
    rope_d = functools.partial(_rope, cos=cos_d, sin_signed=sin_d, lo_mask=lo_d, half=DSA_HEAD_DIM // 2)
    rope_m = functools.partial(_rope, cos=cos_m, sin_signed=sin_m, lo_mask=lo_m, half=MLA_ROPE // 2)

    def dot(a, w_ref):
        return jnp.dot(a, w_ref[...], preferred_element_type=F32)

    att_scale = DSA_HEAD_DIM ** -0.5 * LOG2_E
    idx_scale = IDX_DIM ** -0.5 * IDX_HEADS ** -0.5
    mla_scale = MLA_QK ** -0.5 * LOG2_E

    qq = dot(h, wqq_ref)
    is_qa = lane < DSA_HEAD_DIM
    for hd in range(DSA_HEADS):
        sl = slice(hd * LANES, (hd + 1) * LANES)
        xh = qq[:, sl]
        ms = jnp.sum(jnp.where(is_qa, xh * xh, 0.0), axis=-1, keepdims=True) * (1.0 / DSA_HEAD_DIM)
        mult = jnp.where(is_qa, (lax.rsqrt(ms + EPS) * att_scale) * qag_ref[...], 1.0)
        qq_o[:, sl] = rope_d(xh * mult).astype(BF16)

    sm = dot(h, wsm_ref)
    ka_o[...] = rope_d(_rms(sm[:, 0:LANES], kag_ref[...], DSA_HEAD_DIM)).astype(BF16)
    va = sm[:, LANES:2 * LANES] + jnp.where(lane[0:1] == DSA_HEAD_DIM, 1.0, 0.0)
    va_o[0] = va.T[:DSA_V_ROWS].astype(BF16)
    ki_o[...] = rope_d(sm[:, 2 * LANES:3 * LANES]).astype(BF16)
    iw_o[0] = (sm[:, 3 * LANES:4 * LANES] * idx_scale).T[:IDX_HEADS]
    ckv = _rms(sm[:, 4 * LANES:5 * LANES], ckvg_ref[...], MLA_KV_RANK).astype(BF16)
    kpe = sm[:, 5 * LANES:6 * LANES]

    kn = dot(ckv, wuk_ref)
    mkg = mkg_ref[...]
    pe_rot = rope_m(kpe * mkg)
    ss_pe = jnp.sum(kpe * kpe, axis=-1, keepdims=True)
    for hd in range(MLA_HEADS):
        sl = slice(hd * LANES, (hd + 1) * LANES)
        xh = kn[:, sl]
        ms = (jnp.sum(xh * xh, axis=-1, keepdims=True) + ss_pe) * (1.0 / MLA_QK)
        mk_o[:, sl] = ((xh * mkg + pe_rot) * lax.rsqrt(ms + EPS)).astype(BF16)
    wide_lane = lax.broadcasted_iota(jnp.int32, (1, MLA_HEADS * LANES), 1)
    ones_col = jnp.where(wide_lane % LANES == MLA_V, 1.0, 0.0)
    mv = dot(ckv, wuv_ref) + ones_col
    for hd in range(MLA_HEADS):
        mv_o[0, hd] = mv[:, hd * LANES:(hd + 1) * LANES].T[:MLA_V_ROWS].astype(BF16)

    cq = _rms(dot(h, wcq_ref), cqg_ref[...], MLA_Q_RANK).astype(BF16)
    qb = dot(cq, wuq_ref)
    for hd in range(MLA_HEADS):
        sl = slice(hd * LANES, (hd + 1) * LANES)
        y = rope_m(_rms(qb[:, sl], mqg_ref[...], MLA_QK))
        mq_o[:, sl] = (y * mla_scale).astype(BF16)

    gates = jax.nn.sigmoid(dot(h, wg_ref))
    sgd_o[...] = gates[:, :d_model].astype(BF16)
    sgm_o[...] = gates[:, d_model:].astype(BF16)


def _pad_heads(w, heads, dim):
    k = w.shape[0]
    w = w.reshape(k, heads, dim)
    w = jnp.pad(w, ((0, 0), (0, 0), (0, LANES - dim)))
    return w.reshape(k, heads * LANES)


def _pad_cols(w, width=LANES, offset=0):
    return jnp.pad(w, ((0, 0), (offset, width - offset - w.shape[1])))


def _in_proj(x2, pos2, attn_g, w_in, dsa_q_g, dsa_k_g, cq_g, ckv_g, w_uq, w_ukv, mq_g, mk_g, tm):
    n, d = x2.shape
    sizes = (DSA_HEADS * DSA_HEAD_DIM, DSA_HEAD_DIM, DSA_HEAD_DIM, IDX_HEADS * IDX_DIM, IDX_DIM, IDX_HEADS,
             MLA_Q_RANK, MLA_KV_RANK, MLA_ROPE, d, d)
    offs = [0]
    for s in sizes:
        offs.append(offs[-1] + s)
    seg = [w_in[:, offs[i]:offs[i + 1]] for i in range(len(sizes))]
    w_dq, w_dk, w_dv, w_iq, w_ik, w_iw, w_cq, w_ckv, w_kpe, w_gd, w_gm = seg

    assert DSA_HEADS == IDX_HEADS and DSA_HEAD_DIM + IDX_DIM == LANES
    wqq = jnp.concatenate([w_dq.reshape(d, DSA_HEADS, DSA_HEAD_DIM), w_iq.reshape(d, IDX_HEADS, IDX_DIM)], axis=-1)
    wqq = wqq.reshape(d, DSA_HEADS * LANES).astype(BF16)
    wsm = jnp.concatenate([_pad_cols(w_dk), _pad_cols(w_dv), _pad_cols(w_ik, offset=DSA_HEAD_DIM), _pad_cols(w_iw), w_ckv,
                           _pad_cols(w_kpe, offset=MLA_NOPE)], axis=1).astype(BF16)
    wcq = w_cq.astype(BF16)
    wg = jnp.concatenate([w_gd, w_gm], axis=1).astype(BF16)
    wuq = _pad_heads(w_uq, MLA_HEADS, MLA_QK).astype(BF16)
    ukv = w_ukv.reshape(MLA_KV_RANK, MLA_HEADS, MLA_NOPE + MLA_V)
    wuk = _pad_heads(ukv[:, :, :MLA_NOPE].reshape(MLA_KV_RANK, -1), MLA_HEADS, MLA_NOPE).astype(BF16)
    wuv = _pad_heads(ukv[:, :, MLA_NOPE:].reshape(MLA_KV_RANK, -1), MLA_HEADS, MLA_V).astype(BF16)

    row = lambda v: _pad_cols(v.reshape(1, -1).astype(F32), width=max(LANES, v.size))
    half_d = DSA_HEAD_DIM // 2
    inv_d = ROPE_THETA ** (-jnp.arange(half_d, dtype=F32) / half_d)
    half_m = MLA_ROPE // 2
    inv_m = ROPE_THETA ** (-jnp.arange(half_m, dtype=F32) / half_m)
    freq = _pad_cols(jnp.concatenate([inv_d, inv_m]).reshape(1, -1))

    consts = [attn_g.reshape(1, d), wqq, wsm, wcq, wg, wuq, wuk, wuv, row(dsa_q_g), row(dsa_k_g),
              row(cq_g), row(ckv_g), row(mq_g), row(mk_g), freq]
    wide = DSA_HEADS * LANES
    out_shapes = [
        jax.ShapeDtypeStruct((n, wide), BF16),
        jax.ShapeDtypeStruct((n, LANES), BF16),
        jax.ShapeDtypeStruct((n // tm, DSA_V_ROWS, tm), BF16),
        jax.ShapeDtypeStruct((n, LANES), BF16),
        jax.ShapeDtypeStruct((n // tm, IDX_HEADS, tm), F32),
        jax.ShapeDtypeStruct((n, wide), BF16),
        jax.ShapeDtypeStruct((n, wide), BF16),
        jax.ShapeDtypeStruct((n // tm, MLA_HEADS, MLA_V_ROWS, tm), BF16),
        jax.ShapeDtypeStruct((n, d), BF16),
        jax.ShapeDtypeStruct((n, d), BF16),
    ]
    tile = lambda w: pl.BlockSpec((tm, w), lambda i: (i, 0))
    return pl.pallas_call(
        _in_proj_kernel,
        grid=(n // tm,),
        in_specs=[tile(d), tile(1)] + [_const_spec(c.shape) for c in consts],
        out_specs=[tile(s.shape[1]) if len(s.shape) == 2 else
                   pl.BlockSpec((1,) + s.shape[1:], lambda i, nd=len(s.shape): (i,) + (0,) * (nd - 1))
                   for s in out_shapes],
        out_shape=out_shapes,
        compiler_params=_params(("parallel",)),
        name="in_proj",
    )(x2, pos2, *consts)


def _dsa_kernel(qq_ref, iw_ref, ka_ref, vt_ref, ki_ref, o_ref, score_ref, logit_ref, *, topk):
    i = pl.program_id(1)
    tq = qq_ref.shape[1]
    heads = DSA_HEADS
    npair = (i * tq + tq + 2 * KEY_CHUNK - 1) // (2 * KEY_CHUNK)
    kf = float(topk)
    shape = (KEY_CHUNK, tq)
    folded = (SUBLANES, tq)

    def fold(a, op):
        return op(a.reshape(KEY_CHUNK // SUBLANES, SUBLANES, tq), axis=0)

    krow = lax.broadcasted_iota(jnp.int32, shape, 0)
    qcol = i * tq + lax.broadcasted_iota(jnp.int32, shape, 1)
    contract_last = (((1,), (1,)), ((), ()))

    def pair_loop(body, init):
        return lax.fori_loop(0, npair, lambda j, carry: body(2 * j, 2 * j + 1, carry), init)

    qq = jnp.concatenate([qq_ref[0, :, hd * LANES:(hd + 1) * LANES] for hd in range(heads)], axis=0)
    iw = iw_ref[0]

    def score_body(c0, c1, carry):
        mx, mn = carry
        for c in (c0, c1):
            kc = ki_ref[0, pl.ds(pl.multiple_of(c * KEY_CHUNK, KEY_CHUNK), KEY_CHUNK), :]
            sc = None
            for hp in range(IDX_HEADS // 2):
                rel = lax.dot_general(kc, qq[2 * hp * tq:(2 * hp + 2) * tq], contract_last,
                                      preferred_element_type=F32)
                rel = jnp.maximum(rel, 0.0)
                part = rel[:, :tq] * iw[2 * hp:2 * hp + 1, :] + rel[:, tq:] * iw[2 * hp + 1:2 * hp + 2, :]
                sc = part if sc is None else sc + part
            causal = (krow + c * KEY_CHUNK) <= qcol
            masked = jnp.where(causal, sc, -jnp.inf)
            score_ref[c] = masked
            mx = jnp.maximum(mx, fold(masked, jnp.max))
            mn = jnp.minimum(mn, fold(jnp.where(causal, sc, jnp.inf), jnp.min))
        return mx, mn

    mx, mn = pair_loop(score_body, (jnp.full(folded, -jnp.inf, F32), jnp.full(folded, jnp.inf, F32)))
    hi = jnp.max(mx, axis=0, keepdims=True)
    lo = jnp.min(mn, axis=0, keepdims=True)

    def count_gt(t):
        def body(c0, c1, acc):
            for c in (c0, c1):
                acc = acc + fold(jnp.where(score_ref[c] > t, 1.0, 0.0), jnp.sum)
            return acc
        return jnp.sum(pair_loop(body, jnp.zeros(folded, F32)), axis=0, keepdims=True)

    def bisect(_, carry):
        lo, hi = carry
        mid = 0.5 * (lo + hi)
        below = count_gt(mid) < kf
        return jnp.where(below, lo, mid), jnp.where(below, mid, hi)

    lo, hi = lax.fori_loop(0, BISECT_ITERS, bisect, (lo, hi))

    def max_le(t):
        def body(c0, c1, acc):
            for c in (c0, c1):
                s = score_ref[c]
                acc = jnp.maximum(acc, fold(jnp.where(s <= t, s, -jnp.inf), jnp.max))
            return acc
        return jnp.max(pair_loop(body, jnp.full(folded, -jnp.inf, F32)), axis=0, keepdims=True)

    n_causal = (i * tq + lax.broadcasted_iota(jnp.int32, (1, tq), 1) + 1).astype(F32)
    small = n_causal <= kf

    def refine_cond(carry):
        return jnp.min(carry[2]) < 0.5

    def refine_body(carry):
        m, thr, done, thr_cnt = carry

        def body(c0, c1, acc):
            cnt, nxt = acc
            for c in (c0, c1):
                s = score_ref[c]
                cnt = cnt + fold(jnp.where(s >= m, 1.0, 0.0), jnp.sum)
                nxt = jnp.maximum(nxt, fold(jnp.where(s < m, s, -jnp.inf), jnp.max))
            return cnt, nxt

        cnt, nxt = pair_loop(body, (jnp.zeros(folded, F32), jnp.full(folded, -jnp.inf, F32)))
        cnt = jnp.sum(cnt, axis=0, keepdims=True)
        nxt = jnp.max(nxt, axis=0, keepdims=True)
        reached = cnt >= kf
        hit = jnp.logical_and(reached, done < 0.5)
        return nxt, jnp.where(hit, m, thr), jnp.where(reached, 1.0, done), jnp.where(hit, cnt, thr_cnt)

    init = (max_le(hi), jnp.full((1, tq), -jnp.inf, F32), jnp.where(small, 1.0, 0.0), jnp.zeros((1, tq), F32))
    _, thr, _, thr_cnt = lax.while_loop(refine_cond, refine_body, init)

    def mask_exact_k():
        def body(c0, c1, carry):
            for c in (c0, c1):
                s = score_ref[c]
                sel = jnp.logical_and(s >= thr, s > -jnp.inf)
                score_ref[c] = jnp.where(sel, 0.0, NEG_BIG)
            return carry
        pair_loop(body, 0)

    def mask_with_ties():
        quota = jnp.where(small, 0.0, kf - count_gt(thr))
        earlier = (lax.broadcasted_iota(jnp.int32, (KEY_CHUNK, KEY_CHUNK), 1)
                   < lax.broadcasted_iota(jnp.int32, (KEY_CHUNK, KEY_CHUNK), 0)).astype(BF16)

        def body(c0, c1, seen):
            for c in (c0, c1):
                s = score_ref[c]
                eq = s == thr
                eqf = jnp.where(eq, 1.0, 0.0)
                before = jnp.dot(earlier, eqf.astype(BF16), preferred_element_type=F32) + seen
                sel = jnp.logical_or(s > thr, jnp.logical_and(eq, before < quota))
                seen = seen + jnp.sum(eqf, axis=0, keepdims=True)
                score_ref[c] = jnp.where(sel, 0.0, NEG_BIG)
            return seen
        pair_loop(body, jnp.zeros((1, tq), F32))

    surplus = jnp.max(jnp.where(small, 0.0, thr_cnt - kf)) > 0.5
    lax.cond(surplus, mask_with_ties, mask_exact_k)

    def qk_into(slot, c):
        kc = ka_ref[0, pl.ds(pl.multiple_of(c * KEY_CHUNK, KEY_CHUNK), KEY_CHUNK), :]
        logit_ref[slot] = lax.dot_general(kc, qq, contract_last, preferred_element_type=F32)

    def softmax_pv(slot, c, m_run, acc):
        bias = score_ref[c]
        vt = vt_ref[0, c]
        new_m, new_acc = [], []
        for hp in range(heads // 2):
            probs, alphas = [], []
            for hd in (2 * hp, 2 * hp + 1):
                sl = slice(hd * tq, (hd + 1) * tq)
                lh = logit_ref[slot, :, sl] + bias
                m_old = m_run[:, sl]
                m_new = jnp.maximum(m_old, jnp.max(lh, axis=0, keepdims=True))
                probs.append(jnp.exp2(lh - m_new).astype(BF16))
                alphas.append(jnp.exp2(m_old - m_new))
                new_m.append(m_new)
            pv = jnp.dot(vt, jnp.concatenate(probs, axis=1), preferred_element_type=F32)
            new_acc.append(jnp.concatenate(alphas, axis=1) * acc[:, 2 * hp * tq:(2 * hp + 2) * tq] + pv)
        return jnp.concatenate(new_m, axis=1), jnp.concatenate(new_acc, axis=1)

    def attn_body(c0, c1, carry):
        m_run, acc = carry
        qk_into(1, c1)
        m_run, acc = softmax_pv(0, c0, m_run, acc)
        qk_into(0, jnp.minimum(c0 + 2, 2 * npair - 1))
        return softmax_pv(1, c1, m_run, acc)

    qk_into(0, 0)
    d_rows = vt_ref.shape[2]
    init = (jnp.full((1, heads * tq), NEG_BIG, F32), jnp.zeros((d_rows, heads * tq), F32))
    _, acc = pair_loop(attn_body, init)
    pad_rows = jnp.zeros((LANES - d_rows, tq), F32)
    for hd in range(heads):
        blk = acc[:, hd * tq:(hd + 1) * tq]
        out_t = jnp.concatenate([blk / blk[DSA_HEAD_DIM:DSA_HEAD_DIM + 1, :], pad_rows], axis=0)
        o_ref[0, hd] = out_t.T.astype(o_ref.dtype)


def _dsa_attention(qq, iw_t, ka, va_t, ki, topk):
    b, s, wide = qq.shape
    tq = Q_TILE_DSA
    nkc = s // KEY_CHUNK
    assert iw_t.shape[-1] == tq and va_t.shape[-1] == KEY_CHUNK
    v_t = va_t.reshape(b, nkc, DSA_V_ROWS, KEY_CHUNK)
    qspec = pl.BlockSpec((1, tq, wide), lambda bi, i: (bi, i, 0))
    kspec = pl.BlockSpec((1, s, LANES), lambda bi, i: (bi, 0, 0))
    return pl.pallas_call(
        functools.partial(_dsa_kernel, topk=topk),
        grid=(b, s // tq),
        in_specs=[qspec, pl.BlockSpec((1, IDX_HEADS, tq), lambda bi, i: (bi * (s // tq) + i, 0, 0)), kspec,
                  pl.BlockSpec((1, nkc, DSA_V_ROWS, KEY_CHUNK), lambda bi, i: (bi, 0, 0, 0)), kspec],
        out_specs=pl.BlockSpec((1, DSA_HEADS, tq, LANES), lambda bi, i: (bi, 0, i, 0)),
        out_shape=jax.ShapeDtypeStruct((b, DSA_HEADS, s, LANES), BF16),
        scratch_shapes=[pltpu.VMEM((nkc, KEY_CHUNK, tq), F32),
                        pltpu.VMEM((2, KEY_CHUNK, DSA_HEADS * tq), F32)],
        compiler_params=_params(("parallel", "arbitrary")),
        name="dsa_attn",
    )(qq, iw_t, ka, v_t, ki)


def _mla_kernel(q_ref, k_ref, vt_ref, o_ref, logit_ref):
    i = pl.program_id(2)
    tq = q_ref.shape[1]
    group = q_ref.shape[2] // LANES
    d_rows = vt_ref.shape[3]
    per_block = tq // vt_ref.shape[4]
    qs = [q_ref[0, :, g * LANES:(g + 1) * LANES] for g in range(group)]
    contract_last = (((1,), (1,)), ((), ()))

    def qk_into(slot, c):
        k0 = pl.multiple_of(c * tq, tq)
        for g in range(group):
            logit_ref[slot, g] = lax.dot_general(k_ref[0, pl.ds(k0, tq), g * LANES:(g + 1) * LANES], qs[g],
                                                 contract_last, preferred_element_type=F32)

    def softmax_pv(slot, c, carry, masked):
        if masked:
            keep = (lax.broadcasted_iota(jnp.int32, (tq, tq), 0) <= lax.broadcasted_iota(jnp.int32, (tq, tq), 1))
            bias = jnp.where(keep, 0.0, NEG_BIG)
        probs, alphas, maxes = [], [], []
        for g in range(group):
            m_run = carry[2 * g]
            logit = logit_ref[slot, g]
            if masked:
                logit = logit + bias
            m_new = jnp.maximum(m_run, jnp.max(logit, axis=0, keepdims=True))
            probs.append(jnp.exp2(logit - m_new).astype(BF16))
            alphas.append(jnp.exp2(m_run - m_new))
            maxes.append(m_new)
        res = []
        for g in range(group):
            vt = jnp.concatenate([vt_ref[0, c * per_block + t, g] for t in range(per_block)], axis=1)
            pv = jnp.dot(vt, probs[g], preferred_element_type=F32)
            res += [maxes[g], alphas[g] * carry[2 * g + 1] + pv]
        return tuple(res)

    def pair_body(j, carry):
        qk_into(1, 2 * j + 1)
        carry = softmax_pv(0, 2 * j, carry, False)
        qk_into(0, 2 * j + 2)
        return softmax_pv(1, 2 * j + 1, carry, False)

    def tail_even(carry):
        return softmax_pv(0, i, carry, True)

    def tail_odd(carry):
        qk_into(1, i)
        carry = softmax_pv(0, i - 1, carry, False)
        return softmax_pv(1, i, carry, True)

    qk_into(0, 0)
    init = (jnp.full((1, tq), NEG_BIG, F32), jnp.zeros((d_rows, tq), F32)) * group
    carry = lax.fori_loop(0, i // 2, pair_body, init)
    carry = lax.cond(i % 2 == 0, tail_even, tail_odd, carry)
    pad_rows = jnp.zeros((LANES - d_rows, tq), F32)
    for g in range(group):
        acc = carry[2 * g + 1]
        out_t = jnp.concatenate([acc / acc[MLA_V:MLA_V + 1, :], pad_rows], axis=0)
        o_ref[0, :, g * LANES:(g + 1) * LANES] = out_t.T.astype(o_ref.dtype)


def _mla_attention(mq, mk, mv_t):
    b, s, wide = mq.shape
    tq = Q_TILE_MLA
    nkc = s // tq
    gw = MLA_HEAD_GROUP * LANES
    nt, _, _, tm = mv_t.shape
    nt //= b
    assert tq % tm == 0
    v_t = mv_t.reshape(b, nt, MLA_HEADS, MLA_V_ROWS, tm)
    return pl.pallas_call(
        _mla_kernel,
        grid=(b, MLA_HEADS // MLA_HEAD_GROUP, s // tq),
        in_specs=[pl.BlockSpec((1, tq, gw), lambda bi, h, i: (bi, i, h)),
                  pl.BlockSpec((1, s, gw), lambda bi, h, i: (bi, 0, h)),
                  pl.BlockSpec((1, nt, MLA_HEAD_GROUP, MLA_V_ROWS, tm), lambda bi, h, i: (bi, 0, h, 0, 0))],
        out_specs=pl.BlockSpec((1, tq, gw), lambda bi, h, i: (bi, i, h)),
        out_shape=jax.ShapeDtypeStruct((b, s, wide), BF16),
        scratch_shapes=[pltpu.VMEM((2, MLA_HEAD_GROUP, tq, tq), F32)],
        compiler_params=_params(("parallel", "parallel", "arbitrary")),
        name="mla_attn",
    )(mq, mk, v_t)


def _mem_kv_kernel(mem_ref, g_ref, wkv_ref, kg_ref, k_o, v_o):
    m = mem_ref[0]
    hm = _rms(m, g_ref[...], m.shape[-1]).astype(BF16)
    kv = jnp.dot(hm, wkv_ref[...], preferred_element_type=F32)
    for hd in range(MEM_HEADS):
        sl = slice(hd * LANES, (hd + 1) * LANES)
        k_o[0, :, sl] = _rms(kv[:, sl], kg_ref[...], MEM_HEAD_DIM).astype(BF16)
    v_o[0] = kv[:, MEM_HEADS * LANES:].astype(BF16)


def _mem_kv(mem, mem_g, w_kv, k_g):
    b, m, d = mem.shape
    hw = MEM_HEADS * MEM_HEAD_DIM
    wkv = jnp.concatenate([_pad_heads(w_kv[:, :hw], MEM_HEADS, MEM_HEAD_DIM),
                           _pad_heads(w_kv[:, hw:], MEM_HEADS, MEM_HEAD_DIM)], axis=1).astype(BF16)
    consts = [mem_g.reshape(1, d), wkv, _pad_cols(k_g.reshape(1, -1))]
    wide = MEM_HEADS * LANES
    spec = pl.BlockSpec((1, m, wide), lambda bi: (bi, 0, 0))
    return pl.pallas_call(
        _mem_kv_kernel,
        grid=(b,),
        in_specs=[pl.BlockSpec((1, m, d), lambda bi: (bi, 0, 0))] + [_const_spec(c.shape) for c in consts],
        out_specs=[spec, spec],
        out_shape=[jax.ShapeDtypeStruct((b, m, wide), BF16)] * 2,
        compiler_params=_params(("parallel",)),
        name="mem_kv",
    )(mem, *consts)


def _split_dot(a, w):
    a_hi = a.astype(BF16)
    a_lo = (a - a_hi.astype(F32)).astype(BF16)
    w_hi = w.astype(BF16)
    w_lo = (w - w_hi.astype(F32)).astype(BF16)
    w_both = jnp.concatenate([w_hi, w_lo], axis=1)
    n = w.shape[1]
    hi = jnp.dot(a_hi, w_both, preferred_element_type=F32)
    lo = jnp.dot(a_lo, w_both, preferred_element_type=F32)
    return (hi[:, :n] + hi[:, n:]) + (lo[:, :n] + lo[:, n:])


def _merge_kernel(oa_ref, ob_ref, sgd_ref, sgm_ref, x_ref, wa_ref, wb_ref, wo_ref, mxg_ref, wq_ref, qg_ref,
                  km_ref, vm_ref, wmo_ref, moeg_ref, wr_ref, bias_ref, x2_o, h3_o, comb_o):
    d = functools.partial(jnp.dot, preferred_element_type=F32)
    oa = jnp.concatenate([oa_ref[0, hd] for hd in range(DSA_HEADS)], axis=-1)
    merged = (sgd_ref[0].astype(F32) * d(oa, wa_ref[...]) + sgm_ref[0].astype(F32) * d(ob_ref[0], wb_ref[...]))
    x1 = x_ref[0] + d(merged.astype(BF16), wo_ref[...])

    d_model = x1.shape[-1]
    h2 = _rms(x1, mxg_ref[...], d_model).astype(BF16)
    qm = d(h2, wq_ref[...])
    outs = []
    for hd in range(MEM_HEADS):
        sl = slice(hd * LANES, (hd + 1) * LANES)
        q = (_rms(qm[:, sl], qg_ref[...], MEM_HEAD_DIM) * MEM_HEAD_DIM ** -0.5).astype(BF16)
        logit = lax.dot_general(q, km_ref[0, :, sl], (((1,), (1,)), ((), ())), preferred_element_type=F32)
        logit = logit - jnp.max(logit, axis=-1, keepdims=True)
        p = jnp.exp(logit)
        pv = d(p.astype(BF16), vm_ref[0, :, sl])
        outs.append((pv / jnp.sum(p, axis=-1, keepdims=True)).astype(BF16))
    x2 = x1 + d(jnp.concatenate(outs, axis=-1), wmo_ref[...])
    x2_o[0] = x2

    h3 = _rms(x2, moeg_ref[...], d_model)
    h3_o[0] = h3.astype(BF16)
    logits = _split_dot(h3, wr_ref[...])
    tm = logits.shape[0]
    lane = lax.broadcasted_iota(jnp.int32, (tm, LANES), 1)
    big = jnp.int32(LANES)
    is_grp = jnp.logical_and(lane >= N_EXPERTS, lane < N_EXPERTS + N_GROUPS)
    glog = jnp.where(is_grp, logits, -jnp.inf)
    gmax = jnp.max(glog, axis=-1, keepdims=True)
    g_sel = jnp.min(jnp.where(glog == gmax, lane, big), axis=-1, keepdims=True) - N_EXPERTS
    p_g = 1.0 / jnp.sum(jnp.where(is_grp, jnp.exp(logits - gmax), 0.0), axis=-1, keepdims=True)
    aff = jax.nn.sigmoid(logits)
    in_grp = jnp.logical_and(lane >= g_sel * EXPERTS_PER_GROUP, lane < (g_sel + 1) * EXPERTS_PER_GROUP)
    val = jnp.where(in_grp, aff + bias_ref[...], -jnp.inf)
    m1 = jnp.max(val, axis=-1, keepdims=True)
    i1 = jnp.min(jnp.where(val == m1, lane, big), axis=-1, keepdims=True)
    val2 = jnp.where(lane == i1, -jnp.inf, val)
    m2 = jnp.max(val2, axis=-1, keepdims=True)
    i2 = jnp.min(jnp.where(val2 == m2, lane, big), axis=-1, keepdims=True)
    chosen = jnp.logical_or(lane == i1, lane == i2)
    a_sel = jnp.where(chosen, aff, 0.0)
    comb_o[0] = p_g * a_sel / jnp.sum(a_sel, axis=-1, keepdims=True)


def _merge_mem_router(oa, ob, sgd, sgm, x, w_br_dsa, w_br_mla, w_out, mem_x_g, mem_w_q, mem_q_g, km, vm,
                      mem_w_o, moe_g, w_group, w_expert, expert_bias, tm):
    b, s, d = x.shape
    wa = jnp.pad(w_br_dsa.reshape(DSA_HEADS, DSA_HEAD_DIM, d), ((0, 0), (0, LANES - DSA_HEAD_DIM), (0, 0)))
    wa = wa.reshape(DSA_HEADS * LANES, d).astype(BF16)
    wb = jnp.pad(w_br_mla.reshape(MLA_HEADS, MLA_V, d), ((0, 0), (0, LANES - MLA_V), (0, 0)))
    wb = wb.reshape(MLA_HEADS * LANES, d).astype(BF16)
    wq = _pad_heads(mem_w_q, MEM_HEADS, MEM_HEAD_DIM).astype(BF16)
    wmo = jnp.pad(mem_w_o.reshape(MEM_HEADS, MEM_HEAD_DIM, d), ((0, 0), (0, LANES - MEM_HEAD_DIM), (0, 0)))
    wmo = wmo.reshape(MEM_HEADS * LANES, d).astype(BF16)
    wr = _pad_cols(jnp.concatenate([w_expert, w_group], axis=1).astype(F32))
    bias = _pad_cols(expert_bias.reshape(1, -1).astype(F32))
    consts_a = [wa, wb, w_out.astype(BF16), mem_x_g.reshape(1, d), wq, _pad_cols(mem_q_g.reshape(1, -1))]
    consts_b = [wmo, moe_g.reshape(1, d), wr, bias]
    m = km.shape[1]
    mw = MEM_HEADS * LANES
    tok = lambda w: pl.BlockSpec((1, tm, w), lambda bi, i: (bi, i, 0))
    memspec = pl.BlockSpec((1, m, mw), lambda bi, i: (bi, 0, 0))
    return pl.pallas_call(
        _merge_kernel,
        grid=(b, s // tm),
        in_specs=[pl.BlockSpec((1, DSA_HEADS, tm, LANES), lambda bi, i: (bi, 0, i, 0)),
                  tok(MLA_HEADS * LANES), tok(d), tok(d), tok(d)]
                 + [_const_spec(c.shape) for c in consts_a] + [memspec, memspec]
                 + [_const_spec(c.shape) for c in consts_b],
        out_specs=[tok(d), tok(d), tok(LANES)],
        out_shape=[jax.ShapeDtypeStruct((b, s, d), F32), jax.ShapeDtypeStruct((b, s, d), BF16),
                   jax.ShapeDtypeStruct((b, s, LANES), F32)],
        compiler_params=_params(("parallel", "parallel")),
        name="merge_mem",
    )(oa, ob, sgd, sgm, x, *consts_a, km, vm, *consts_b)


def _moe_kernel(h_ref, comb_ref, x_ref, wgu_ref, wd_ref, o_ref):
    g = pl.program_id(1)
    width = EXPERTS_PER_GROUP * D_EXPERT
    gu = jnp.dot(h_ref[...], wgu_ref[0], preferred_element_type=F32)
    act = jax.nn.silu(gu[:, :width]) * gu[:, width:]
    comb = comb_ref[...]
    lane = lax.broadcasted_iota(jnp.int32, comb.shape, 1)
    parts = []
    for e in range(EXPERTS_PER_GROUP):
        c_e = jnp.sum(jnp.where(lane == g * EXPERTS_PER_GROUP + e, comb, 0.0), axis=-1, keepdims=True)
        parts.append((act[:, e * D_EXPERT:(e + 1) * D_EXPERT] * c_e).astype(BF16))
    contrib = jnp.dot(jnp.concatenate(parts, axis=1), wd_ref[0], preferred_element_type=F32)

    @pl.when(g == 0)
    def _():
        o_ref[...] = x_ref[...] + contrib

    @pl.when(g > 0)
    def _():
        o_ref[...] += contrib


def _moe(h3, comb, x2, w_gate, w_up, w_down, tm):
    n, d = x2.shape
    width = EXPERTS_PER_GROUP * D_EXPERT
    by_group = lambda w: jnp.moveaxis(w.reshape(N_GROUPS, EXPERTS_PER_GROUP, d, D_EXPERT), 1, 2).reshape(N_GROUPS, d, width)
    wgu = jnp.concatenate([by_group(w_gate), by_group(w_up)], axis=-1).astype(BF16)
    wd = w_down.reshape(N_GROUPS, width, d).astype(BF16)
    return pl.pallas_call(
        _moe_kernel,
        grid=(n // tm, N_GROUPS),
        in_specs=[pl.BlockSpec((tm, d), lambda i, g: (i, 0)),
                  pl.BlockSpec((tm, LANES), lambda i, g: (i, 0)),
                  pl.BlockSpec((tm, d), lambda i, g: (i, 0)),
                  pl.BlockSpec((1, d, 2 * width), lambda i, g: (g, 0, 0)),
                  pl.BlockSpec((1, width, d), lambda i, g: (g, 0, 0))],
        out_specs=pl.BlockSpec((tm, d), lambda i, g: (i, 0)),
        out_shape=jax.ShapeDtypeStruct((n, d), F32),
        compiler_params=_params(("parallel", "arbitrary")),
        name="moe",
    )(h3, comb, x2, wgu, wd)


def _layer(x, mem, positions, p):
    b, s, d = x.shape
    n = b * s
    assert s % (2 * KEY_CHUNK) == 0 and s % Q_TILE_MLA == 0
    topk = min(TOPK_MAX, s // 4)
    tm = min(TOKENS_PER_STEP_PROJ, s)
    qq, ka, va, ki, iw, mq, mk, mv, sgd, sgm = _in_proj(
        x.reshape(n, d), positions.reshape(n, 1), p["attn_norm_g"], p["w_in"], p["dsa_q_norm_g"],
        p["dsa_k_norm_g"], p["mla_cq_norm_g"], p["mla_ckv_norm_g"], p["mla_w_uq"], p["mla_w_ukv"],
        p["mla_q_norm_g"], p["mla_k_norm_g"], tm)
    r3 = lambda a: a.reshape(b, s, a.shape[-1])
    oa = _dsa_attention(r3(qq), iw, r3(ka), va, r3(ki), topk)
    ob = _mla_attention(r3(mq), r3(mk), mv)
    km, vm = _mem_kv(mem, p["mem_norm_g"], p["mem_w_kv"], p["mem_k_norm_g"])
    x2, h3, comb = _merge_mem_router(
        oa, ob, r3(sgd), r3(sgm), x, p["w_branch_dsa"], p["w_branch_mla"], p["w_out"], p["mem_x_norm_g"],
        p["mem_w_q"], p["mem_q_norm_g"], km, vm, p["mem_w_o"], p["moe_norm_g"], p["moe_w_group"],
        p["moe_w_expert"], p["moe_expert_bias"], min(TOKENS_PER_STEP_MERGE, s))
    out = _moe(h3.reshape(n, d), comb.reshape(n, LANES), x2.reshape(n, d), p["moe_w_gate"], p["moe_w_up"],
               p["moe_w_down"], min(TOKENS_PER_STEP_MOE, n))
    return out.reshape(b, s, d)


_PARAM_NAMES = ("attn_norm_g", "w_in", "dsa_q_norm_g", "dsa_k_norm_g", "mla_cq_norm_g", "mla_ckv_norm_g",
                "mla_w_uq", "mla_w_ukv", "mla_q_norm_g", "mla_k_norm_g", "w_branch_dsa", "w_branch_mla", "w_out",
                "mem_x_norm_g", "mem_norm_g", "mem_w_q", "mem_w_kv", "mem_q_norm_g", "mem_k_norm_g", "mem_w_o",
                "moe_norm_g", "moe_w_group", "moe_w_expert", "moe_expert_bias", "moe_w_gate", "moe_w_up",
                "moe_w_down")


def kernel(x, mem, positions, attn_norm_g, w_in, dsa_q_norm_g, dsa_k_norm_g, mla_cq_norm_g, mla_ckv_norm_g, mla_w_uq, mla_w_ukv, mla_q_norm_g, mla_k_norm_g, w_branch_dsa, w_branch_mla, w_out, mem_x_norm_g, mem_norm_g, mem_w_q, mem_w_kv, mem_q_norm_g, mem_k_norm_g, mem_w_o, moe_norm_g, moe_w_group, moe_w_expert, moe_expert_bias, moe_w_gate, moe_w_up, moe_w_down):
    stacked = (attn_norm_g, w_in, dsa_q_norm_g, dsa_k_norm_g, mla_cq_norm_g, mla_ckv_norm_g, mla_w_uq, mla_w_ukv,
               mla_q_norm_g, mla_k_norm_g, w_branch_dsa, w_branch_mla, w_out, mem_x_norm_g, mem_norm_g, mem_w_q,
               mem_w_kv, mem_q_norm_g, mem_k_norm_g, mem_w_o, moe_norm_g, moe_w_group, moe_w_expert,
               moe_expert_bias, moe_w_gate, moe_w_up, moe_w_down)
    for layer in range(attn_norm_g.shape[0]):
        p = {name: arr[layer] for name, arr in zip(_PARAM_NAMES, stacked)}
        x = _layer(x, mem, positions, p)
    return x
```

```python
import functools

import jax
import jax.numpy as jnp
from jax import lax
from jax.experimental import pallas as pl
from jax.experimental.pallas import tpu as pltpu

F32 = jnp.float32
BF16 = jnp.bfloat16

LANES = 128
SUBLANES = 8
VMEM_LIMIT = 56 * 1024 * 1024

ROPE_THETA = 10000.0
EPS = 1e-6
DSA_HEADS = 8
DSA_HEAD_DIM = 64
IDX_HEADS = 8
IDX_DIM = 64
TOPK_MAX = 256
MLA_HEADS = 8
MLA_NOPE = 64
MLA_ROPE = 32
MLA_QK = MLA_NOPE + MLA_ROPE
MLA_V = 64
MLA_Q_RANK = 256
MLA_KV_RANK = 128
MEM_HEADS = 4
MEM_HEAD_DIM = 64
N_GROUPS = 4
EXPERTS_PER_GROUP = 4
N_EXPERTS = N_GROUPS * EXPERTS_PER_GROUP
D_EXPERT = 256

NEG_BIG = -1e30
KEY_CHUNK = 256
Q_TILE_DSA = 256
DSA_V_ROWS = 80
MLA_V_ROWS = 80
Q_TILE_MLA = 512
MLA_HEAD_GROUP = 4
LOG2_E = 1.4426950408889634
BISECT_ITERS = 15
TOKENS_PER_STEP_PROJ = 256
TOKENS_PER_STEP_MERGE = 512
TOKENS_PER_STEP_MOE = 1024


def _const_spec(shape):
    nd = len(shape)
    return pl.BlockSpec(shape, lambda *_: (0,) * nd)


def _params(sem):
    return pltpu.CompilerParams(dimension_semantics=sem, vmem_limit_bytes=VMEM_LIMIT)


def _rms(x, g, n):
    ms = jnp.sum(x * x, axis=-1, keepdims=True) * (1.0 / n)
    return (x * lax.rsqrt(ms + EPS)) * g


def _rope(x, cos, sin_signed, lo_mask, half):
    fwd = pltpu.roll(x, LANES - half, 1)
    bwd = pltpu.roll(x, half, 1)
    return x * cos + jnp.where(lo_mask, fwd, bwd) * sin_signed


def _in_proj_kernel(x_ref, pos_ref, g_ref, wqq_ref, wsm_ref, wcq_ref, wg_ref, wuq_ref, wuk_ref,
                    wuv_ref, qag_ref, kag_ref, cqg_ref, ckvg_ref, mqg_ref, mkg_ref, freq_ref,
                    qq_o, ka_o, va_o, ki_o, iw_o, mq_o, mk_o, mv_o, sgd_o, sgm_o):
    x = x_ref[...]
    d_model = x.shape[-1]
    h = _rms(x, g_ref[...], d_model).astype(BF16)

    pos = pos_ref[...].astype(F32)
    lane = lax.broadcasted_iota(jnp.int32, (x.shape[0], LANES), 1)
    half_d, half_m = DSA_HEAD_DIM // 2, MLA_ROPE // 2
    ang = pos * freq_ref[...]
    lo_d = (lane & (DSA_HEAD_DIM - 1)) < half_d
    lo_m = lane < MLA_NOPE + half_m
    in_m_lo = jnp.logical_and(lane >= MLA_NOPE, lo_m)
    in_m_hi = jnp.logical_and(lane >= MLA_NOPE + half_m, lane < MLA_QK)

    def tables(t, fill):
        by_half = pltpu.roll(t, half_d, 1)
        head = jnp.where(lane < half_d, t, by_half)
        dsa = jnp.where(lane < DSA_HEAD_DIM, head, pltpu.roll(head, DSA_HEAD_DIM, 1))
        mla = jnp.where(in_m_lo, by_half, jnp.where(in_m_hi, pltpu.roll(t, half_d + half_m, 1), fill))
        return dsa, mla

    cos_d, cos_m = tables(jnp.cos(ang), 1.0)
    sin_d, sin_m = tables(jnp.sin(ang), 0.0)
    sin_d = jnp.where(lo_d, -sin_d, sin_d)
    sin_m = jnp.where(lo_m, -sin_m, sin_m)
    rope_d = functools.partial(_rope, cos=cos_d, sin_signed=sin_d, lo_mask=lo_d, half=DSA_HEAD_DIM // 2)
    rope_m = functools.partial(_rope, cos=cos_m, sin_signed=sin_m, lo_mask=lo_m, half=MLA_ROPE // 2)

    def dot(a, w_ref):
        return jnp.dot(a, w_ref[...], preferred_element_type=F32)

    att_scale = DSA_HEAD_DIM ** -0.5 * LOG2_E
    idx_scale = IDX_DIM ** -0.5 * IDX_HEADS ** -0.5
    mla_scale = MLA_QK ** -0.5 * LOG2_E

    qq = dot(h, wqq_ref)
    is_qa = lane < DSA_HEAD_DIM
    for hd in range(DSA_HEADS):
        sl = slice(hd * LANES, (hd + 1) * LANES)
        xh = qq[:, sl]
        ms = jnp.sum(jnp.where(is_qa, xh * xh, 0.0), axis=-1, keepdims=True) * (1.0 / DSA_HEAD_DIM)
        mult = jnp.where(is_qa, (lax.rsqrt(ms + EPS) * att_scale) * qag_ref[...], 1.0)
        qq_o[:, sl] = rope_d(xh * mult).astype(BF16)

    sm = dot(h, wsm_ref)
    ka_o[...] = rope_d(_rms(sm[:, 0:LANES], kag_ref[...], DSA_HEAD_DIM)).astype(BF16)
    va = sm[:, LANES:2 * LANES] + jnp.where(lane[0:1] == DSA_HEAD_DIM, 1.0, 0.0)
    va_o[0] = va.T[:DSA_V_ROWS].astype(BF16)
    ki_o[...] = rope_d(sm[:, 2 * LANES:3 * LANES]).astype(BF16)
    iw_o[0] = (sm[:, 3 * LANES:4 * LANES] * idx_scale).T[:IDX_HEADS]
    ckv = _rms(sm[:, 4 * LANES:5 * LANES], ckvg_ref[...], MLA_KV_RANK).astype(BF16)
    kpe = sm[:, 5 * LANES:6 * LANES]

    kn = dot(ckv, wuk_ref)
    mkg = mkg_ref[...]
    pe_rot = rope_m(kpe * mkg)
    ss_pe = jnp.sum(kpe * kpe, axis=-1, keepdims=True)
    for hd in range(MLA_HEADS):
        sl = slice(hd * LANES, (hd + 1) * LANES)
        xh = kn[:, sl]
        ms = (jnp.sum(xh * xh, axis=-1, keepdims=True) + ss_pe) * (1.0 / MLA_QK)
        mk_o[:, sl] = ((xh * mkg + pe_rot) * lax.rsqrt(ms + EPS)).astype(BF16)
    wide_lane = lax.broadcasted_iota(jnp.int32, (1, MLA_HEADS * LANES), 1)
    ones_col = jnp.where(wide_lane % LANES == MLA_V, 1.0, 0.0)
    mv = dot(ckv, wuv_ref) + ones_col
    for hd in range(MLA_HEADS):
        mv_o[0, hd] = mv[:, hd * LANES:(hd + 1) * LANES].T[:MLA_V_ROWS].astype(BF16)

    cq = _rms(dot(h, wcq_ref), cqg_ref[...], MLA_Q_RANK).astype(BF16)
    qb = dot(cq, wuq_ref)
    for hd in range(MLA_HEADS):
        sl = slice(hd * LANES, (hd + 1) * LANES)
        y = rope_m(_rms(qb[:, sl], mqg_ref[...], MLA_QK))
        mq_o[:, sl] = (y * mla_scale).astype(BF16)

    gates = jax.nn.sigmoid(dot(h, wg_ref))
    sgd_o[...] = gates[:, :d_model].astype(BF16)
    sgm_o[...] = gates[:, d_model:].astype(BF16)


def _pad_heads(w, heads, dim):
    k = w.shape[0]
    w = w.reshape(k, heads, dim)
    w = jnp.pad(w, ((0, 0), (0, 0), (0, LANES - dim)))
    return w.reshape(k, heads * LANES)


def _pad_cols(w, width=LANES, offset=0):
    return jnp.pad(w, ((0, 0), (offset, width - offset - w.shape[1])))


def _in_proj(x2, pos2, attn_g, w_in, dsa_q_g, dsa_k_g, cq_g, ckv_g, w_uq, w_ukv, mq_g, mk_g, tm):
    n, d = x2.shape
    sizes = (DSA_HEADS * DSA_HEAD_DIM, DSA_HEAD_DIM, DSA_HEAD_DIM, IDX_HEADS * IDX_DIM, IDX_DIM, IDX_HEADS,
             MLA_Q_RANK, MLA_KV_RANK, MLA_ROPE, d, d)
    offs = [0]
    for s in sizes:
        offs.append(offs[-1] + s)
    seg = [w_in[:, offs[i]:offs[i + 1]] for i in range(len(sizes))]
    w_dq, w_dk, w_dv, w_iq, w_ik, w_iw, w_cq, w_ckv, w_kpe, w_gd, w_gm = seg

    assert DSA_HEADS == IDX_HEADS and DSA_HEAD_DIM + IDX_DIM == LANES
    wqq = jnp.concatenate([w_dq.reshape(d, DSA_HEADS, DSA_HEAD_DIM), w_iq.reshape(d, IDX_HEADS, IDX_DIM)], axis=-1)
    wqq = wqq.reshape(d, DSA_HEADS * LANES).astype(BF16)
    wsm = jnp.concatenate([_pad_cols(w_dk), _pad_cols(w_dv), _pad_cols(w_ik, offset=DSA_HEAD_DIM), _pad_cols(w_iw), w_ckv,
                           _pad_cols(w_kpe, offset=MLA_NOPE)], axis=1).astype(BF16)
    wcq = w_cq.astype(BF16)
    wg = jnp.concatenate([w_gd, w_gm], axis=1).astype(BF16)
    wuq = _pad_heads(w_uq, MLA_HEADS, MLA_QK).astype(BF16)
    ukv = w_ukv.reshape(MLA_KV_RANK, MLA_HEADS, MLA_NOPE + MLA_V)
    wuk = _pad_heads(ukv[:, :, :MLA_NOPE].reshape(MLA_KV_RANK, -1), MLA_HEADS, MLA_NOPE).astype(BF16)
    wuv = _pad_heads(ukv[:, :, MLA_NOPE:].reshape(MLA_KV_RANK, -1), MLA_HEADS, MLA_V).astype(BF16)

    row = lambda v: _pad_cols(v.reshape(1, -1).astype(F32), width=max(LANES, v.size))
    half_d = DSA_HEAD_DIM // 2
    inv_d = ROPE_THETA ** (-jnp.arange(half_d, dtype=F32) / half_d)
    half_m = MLA_ROPE // 2
    inv_m = ROPE_THETA ** (-jnp.arange(half_m, dtype=F32) / half_m)
    freq = _pad_cols(jnp.concatenate([inv_d, inv_m]).reshape(1, -1))

    consts = [attn_g.reshape(1, d), wqq, wsm, wcq, wg, wuq, wuk, wuv, row(dsa_q_g), row(dsa_k_g),
              row(cq_g), row(ckv_g), row(mq_g), row(mk_g), freq]
    wide = DSA_HEADS * LANES
    out_shapes = [
        jax.ShapeDtypeStruct((n, wide), BF16),
        jax.ShapeDtypeStruct((n, LANES), BF16),
        jax.ShapeDtypeStruct((n // tm, DSA_V_ROWS, tm), BF16),
        jax.ShapeDtypeStruct((n, LANES), BF16),
        jax.ShapeDtypeStruct((n // tm, IDX_HEADS, tm), F32),
        jax.ShapeDtypeStruct((n, wide), BF16),
        jax.ShapeDtypeStruct((n, wide), BF16),
        jax.ShapeDtypeStruct((n // tm, MLA_HEADS, MLA_V_ROWS, tm), BF16),
        jax.ShapeDtypeStruct((n, d), BF16),
        jax.ShapeDtypeStruct((n, d), BF16),
    ]
    tile = lambda w: pl.BlockSpec((tm, w), lambda i: (i, 0))
    return pl.pallas_call(
        _in_proj_kernel,
        grid=(n // tm,),
        in_specs=[tile(d), tile(1)] + [_const_spec(c.shape) for c in consts],
        out_specs=[tile(s.shape[1]) if len(s.shape) == 2 else
                   pl.BlockSpec((1,) + s.shape[1:], lambda i, nd=len(s.shape): (i,) + (0,) * (nd - 1))
                   for s in out_shapes],
        out_shape=out_shapes,
        compiler_params=_params(("parallel",)),
        name="in_proj",
    )(x2, pos2, *consts)


def _dsa_kernel(qq_ref, iw_ref, ka_ref, vt_ref, ki_ref, o_ref, score_ref, logit_ref, *, topk):
    i = pl.program_id(1)
    tq = qq_ref.shape[1]
    heads = DSA_HEADS
    npair = (i * tq + tq + 2 * KEY_CHUNK - 1) // (2 * KEY_CHUNK)
    kf = float(topk)
    shape = (KEY_CHUNK, tq)
    folded = (SUBLANES, tq)

    def fold(a, op):
        return op(a.reshape(KEY_CHUNK // SUBLANES, SUBLANES, tq), axis=0)

    krow = lax.broadcasted_iota(jnp.int32, shape, 0)
    qcol = i * tq + lax.broadcasted_iota(jnp.int32, shape, 1)
    contract_last = (((1,), (1,)), ((), ()))

    def pair_loop(body, init):
        return lax.fori_loop(0, npair, lambda j, carry: body(2 * j, 2 * j + 1, carry), init)

    qq = jnp.concatenate([qq_ref[0, :, hd * LANES:(hd + 1) * LANES] for hd in range(heads)], axis=0)
    iw = iw_ref[0]

    def score_body(c0, c1, carry):
        mx, mn = carry
        for c in (c0, c1):
            kc = ki_ref[0, pl.ds(pl.multiple_of(c * KEY_CHUNK, KEY_CHUNK), KEY_CHUNK), :]
            sc = None
            for hp in range(IDX_HEADS // 2):
                rel = lax.dot_general(kc, qq[2 * hp * tq:(2 * hp + 2) * tq], contract_last,
                                      preferred_element_type=F32)
                rel = jnp.maximum(rel, 0.0)
                part = rel[:, :tq] * iw[2 * hp:2 * hp + 1, :] + rel[:, tq:] * iw[2 * hp + 1:2 * hp + 2, :]
                sc = part if sc is None else sc + part
            causal = (krow + c * KEY_CHUNK) <= qcol
            masked = jnp.where(causal, sc, -jnp.inf)
            score_ref[c] = masked
            mx = jnp.maximum(mx, fold(masked, jnp.max))
            mn = jnp.minimum(mn, fold(jnp.where(causal, sc, jnp.inf), jnp.min))
        return mx, mn

    mx, mn = pair_loop(score_body, (jnp.full(folded, -jnp.inf, F32), jnp.full(folded, jnp.inf, F32)))
    hi = jnp.max(mx, axis=0, keepdims=True)
    lo = jnp.min(mn, axis=0, keepdims=True)

    def count_gt(t):
        def body(c0, c1, acc):
            for c in (c0, c1):
                acc = acc + fold(jnp.where(score_ref[c] > t, 1.0, 0.0), jnp.sum)
            return acc
        return jnp.sum(pair_loop(body, jnp.zeros(folded, F32)), axis=0, keepdims=True)

    def bisect(_, carry):
        lo, hi = carry
        mid = 0.5 * (lo + hi)
        below = count_gt(mid) < kf
        return jnp.where(below, lo, mid), jnp.where(below, mid, hi)

    lo, hi = lax.fori_loop(0, BISECT_ITERS, bisect, (lo, hi))

    def max_le(t):
        def body(c0, c1, acc):
            for c in (c0, c1):
                s = score_ref[c]
                acc = jnp.maximum(acc, fold(jnp.where(s <= t, s, -jnp.inf), jnp.max))
            return acc
        return jnp.max(pair_loop(body, jnp.full(folded, -jnp.inf, F32)), axis=0, keepdims=True)

    n_causal = (i * tq + lax.broadcasted_iota(jnp.int32, (1, tq), 1) + 1).astype(F32)
    small = n_causal <= kf

    def refine_cond(carry):
        return jnp.min(carry[2]) < 0.5

    def refine_body(carry):
        m, thr, done, thr_cnt = carry

        def body(c0, c1, acc):
            cnt, nxt = acc
            for c in (c0, c1):
                s = score_ref[c]
                cnt = cnt + fold(jnp.where(s >= m, 1.0, 0.0), jnp.sum)
                nxt = jnp.maximum(nxt, fold(jnp.where(s < m, s, -jnp.inf), jnp.max))
            return cnt, nxt

        cnt, nxt = pair_loop(body, (jnp.zeros(folded, F32), jnp.full(folded, -jnp.inf, F32)))
        cnt = jnp.sum(cnt, axis=0, keepdims=True)
        nxt = jnp.max(nxt, axis=0, keepdims=True)
        reached = cnt >= kf
        hit = jnp.logical_and(reached, done < 0.5)
        return nxt, jnp.where(hit, m, thr), jnp.where(reached, 1.0, done), jnp.where(hit, cnt, thr_cnt)

    init = (max_le(hi), jnp.full((1, tq), -jnp.inf, F32), jnp.where(small, 1.0, 0.0), jnp.zeros((1, tq), F32))
    _, thr, _, thr_cnt = lax.while_loop(refine_cond, refine_body, init)

    def mask_exact_k():
        def body(c0, c1, carry):
            for c in (c0, c1):
                s = score_ref[c]
                sel = jnp.logical_and(s >= thr, s > -jnp.inf)
                score_ref[c] = jnp.where(sel, 0.0, NEG_BIG)
            return carry
        pair_loop(body, 0)

    def mask_with_ties():
        quota = jnp.where(small, 0.0, kf - count_gt(thr))
        earlier = (lax.broadcasted_iota(jnp.int32, (KEY_CHUNK, KEY_CHUNK), 1)
                   < lax.broadcasted_iota(jnp.int32, (KEY_CHUNK, KEY_CHUNK), 0)).astype(BF16)

        def body(c0, c1, seen):
            for c in (c0, c1):
                s = score_ref[c]
                eq = s == thr
                eqf = jnp.where(eq, 1.0, 0.0)
                before = jnp.dot(earlier, eqf.astype(BF16), preferred_element_type=F32) + seen
                sel = jnp.logical_or(s > thr, jnp.logical_and(eq, before < quota))
                seen = seen + jnp.sum(eqf, axis=0, keepdims=True)
                score_ref[c] = jnp.where(sel, 0.0, NEG_BIG)
            return seen
        pair_loop(body, jnp.zeros((1, tq), F32))

    surplus = jnp.max(jnp.where(small, 0.0, thr_cnt - kf)) > 0.5
    lax.cond(surplus, mask_with_ties, mask_exact_k)

    def qk_into(slot, c):
        kc = ka_ref[0, pl.ds(pl.multiple_of(c * KEY_CHUNK, KEY_CHUNK), KEY_CHUNK), :]
        logit_ref[slot] = lax.dot_general(kc, qq, contract_last, preferred_element_type=F32)

    def softmax_pv(slot, c, m_run, acc):
        bias = score_ref[c]
        vt = vt_ref[0, c]
        new_m, new_acc = [], []
        for hp in range(heads // 2):
            probs, alphas = [], []
            for hd in (2 * hp, 2 * hp + 1):
                sl = slice(hd * tq, (hd + 1) * tq)
                lh = logit_ref[slot, :, sl] + bias
                m_old = m_run[:, sl]
                m_new = jnp.maximum(m_old, jnp.max(lh, axis=0, keepdims=True))
                probs.append(jnp.exp2(lh - m_new).astype(BF16))
                alphas.append(jnp.exp2(m_old - m_new))
                new_m.append(m_new)
            pv = jnp.dot(vt, jnp.concatenate(probs, axis=1), preferred_element_type=F32)
            new_acc.append(jnp.concatenate(alphas, axis=1) * acc[:, 2 * hp * tq:(2 * hp + 2) * tq] + pv)
        return jnp.concatenate(new_m, axis=1), jnp.concatenate(new_acc, axis=1)

    def attn_body(c0, c1, carry):
        m_run, acc = carry
        qk_into(1, c1)
        m_run, acc = softmax_pv(0, c0, m_run, acc)
        qk_into(0, jnp.minimum(c0 + 2, 2 * npair - 1))
        return softmax_pv(1, c1, m_run, acc)

    qk_into(0, 0)
    d_rows = vt_ref.shape[2]
    init = (jnp.full((1, heads * tq), NEG_BIG, F32), jnp.zeros((d_rows, heads * tq), F32))
    _, acc = pair_loop(attn_body, init)
    pad_rows = jnp.zeros((LANES - d_rows, tq), F32)
    for hd in range(heads):
        blk = acc[:, hd * tq:(hd + 1) * tq]
        out_t = jnp.concatenate([blk / blk[DSA_HEAD_DIM:DSA_HEAD_DIM + 1, :], pad_rows], axis=0)
        o_ref[0, hd] = out_t.T.astype(o_ref.dtype)


def _dsa_attention(qq, iw_t, ka, va_t, ki, topk):
    b, s, wide = qq.shape
    tq = Q_TILE_DSA
    nkc = s // KEY_CHUNK
    assert iw_t.shape[-1] == tq and va_t.shape[-1] == KEY_CHUNK
    v_t = va_t.reshape(b, nkc, DSA_V_ROWS, KEY_CHUNK)
    qspec = pl.BlockSpec((1, tq, wide), lambda bi, i: (bi, i, 0))
    kspec = pl.BlockSpec((1, s, LANES), lambda bi, i: (bi, 0, 0))
    return pl.pallas_call(
        functools.partial(_dsa_kernel, topk=topk),
        grid=(b, s // tq),
        in_specs=[qspec, pl.BlockSpec((1, IDX_HEADS, tq), lambda bi, i: (bi * (s // tq) + i, 0, 0)), kspec,
                  pl.BlockSpec((1, nkc, DSA_V_ROWS, KEY_CHUNK), lambda bi, i: (bi, 0, 0, 0)), kspec],
        out_specs=pl.BlockSpec((1, DSA_HEADS, tq, LANES), lambda bi, i: (bi, 0, i, 0)),
        out_shape=jax.ShapeDtypeStruct((b, DSA_HEADS, s, LANES), BF16),
        scratch_shapes=[pltpu.VMEM((nkc, KEY_CHUNK, tq), F32),
                        pltpu.VMEM((2, KEY_CHUNK, DSA_HEADS * tq), F32)],
        compiler_params=_params(("parallel", "arbitrary")),
        name="dsa_attn",
    )(qq, iw_t, ka, v_t, ki)


def _mla_kernel(q_ref, k_ref, vt_ref, o_ref, logit_ref):
    i = pl.program_id(2)
    tq = q_ref.shape[1]
    group = q_ref.shape[2] // LANES
    d_rows = vt_ref.shape[3]
    per_block = tq // vt_ref.shape[4]
    qs = [q_ref[0, :, g * LANES:(g + 1) * LANES] for g in range(group)]
    contract_last = (((1,), (1,)), ((), ()))

    def qk_into(slot, c):
        k0 = pl.multiple_of(c * tq, tq)
        for g in range(group):
            logit_ref[slot, g] = lax.dot_general(k_ref[0, pl.ds(k0, tq), g * LANES:(g + 1) * LANES], qs[g],
                                                 contract_last, preferred_element_type=F32)

    def softmax_pv(slot, c, carry, masked):
        if masked:
            keep = (lax.broadcasted_iota(jnp.int32, (tq, tq), 0) <= lax.broadcasted_iota(jnp.int32, (tq, tq), 1))
            bias = jnp.where(keep, 0.0, NEG_BIG)
        probs, alphas, maxes = [], [], []
        for g in range(group):
            m_run = carry[2 * g]
            logit = logit_ref[slot, g]
            if masked:
                logit = logit + bias
            m_new = jnp.maximum(m_run, jnp.max(logit, axis=0, keepdims=True))
            probs.append(jnp.exp2(logit - m_new).astype(BF16))
            alphas.append(jnp.exp2(m_run - m_new))
            maxes.append(m_new)
        res = []
        for g in range(group):
            vt = jnp.concatenate([vt_ref[0, c * per_block + t, g] for t in range(per_block)], axis=1)
            pv = jnp.dot(vt, probs[g], preferred_element_type=F32)
            res += [maxes[g], alphas[g] * carry[2 * g + 1] + pv]
        return tuple(res)

    def pair_body(j, carry):
        qk_into(1, 2 * j + 1)
        carry = softmax_pv(0, 2 * j, carry, False)
        qk_into(0, 2 * j + 2)
        return softmax_pv(1, 2 * j + 1, carry, False)

    def tail_even(carry):
        return softmax_pv(0, i, carry, True)

    def tail_odd(carry):
        qk_into(1, i)
        carry = softmax_pv(0, i - 1, carry, False)
        return softmax_pv(1, i, carry, True)

    qk_into(0, 0)
    init = (jnp.full((1, tq), NEG_BIG, F32), jnp.zeros((d_rows, tq), F32)) * group
    carry = lax.fori_loop(0, i // 2, pair_body, init)
    carry = lax.cond(i % 2 == 0, tail_even, tail_odd, carry)
    pad_rows = jnp.zeros((LANES - d_rows, tq), F32)
    for g in range(group):
        acc = carry[2 * g + 1]
        out_t = jnp.concatenate([acc / acc[MLA_V:MLA_V + 1, :], pad_rows], axis=0)
        o_ref[0, :, g * LANES:(g + 1) * LANES] = out_t.T.astype(o_ref.dtype)


def _mla_attention(mq, mk, mv_t):
    b, s, wide = mq.shape
    tq = Q_TILE_MLA
    nkc = s // tq
    gw = MLA_HEAD_GROUP * LANES
    nt, _, _, tm = mv_t.shape
    nt //= b
    assert tq % tm == 0
    v_t = mv_t.reshape(b, nt, MLA_HEADS, MLA_V_ROWS, tm)
    return pl.pallas_call(
        _mla_kernel,
        grid=(b, MLA_HEADS // MLA_HEAD_GROUP, s // tq),
        in_specs=[pl.BlockSpec((1, tq, gw), lambda bi, h, i: (bi, i, h)),
                  pl.BlockSpec((1, s, gw), lambda bi, h, i: (bi, 0, h)),
                  pl.BlockSpec((1, nt, MLA_HEAD_GROUP, MLA_V_ROWS, tm), lambda bi, h, i: (bi, 0, h, 0, 0))],
        out_specs=pl.BlockSpec((1, tq, gw), lambda bi, h, i: (bi, i, h)),
        out_shape=jax.ShapeDtypeStruct((b, s, wide), BF16),
        scratch_shapes=[pltpu.VMEM((2, MLA_HEAD_GROUP, tq, tq), F32)],
        compiler_params=_params(("parallel", "parallel", "arbitrary")),
        name="mla_attn",
    )(mq, mk, v_t)


def _mem_kv_kernel(mem_ref, g_ref, wkv_ref, kg_ref, k_o, v_o):
    m = mem_ref[0]
    hm = _rms(m, g_ref[...], m.shape[-1]).astype(BF16)
    kv = jnp.dot(hm, wkv_ref[...], preferred_element_type=F32)
    for hd in range(MEM_HEADS):
        sl = slice(hd * LANES, (hd + 1) * LANES)
        k_o[0, :, sl] = _rms(kv[:, sl], kg_ref[...], MEM_HEAD_DIM).astype(BF16)
    v_o[0] = kv[:, MEM_HEADS * LANES:].astype(BF16)


def _mem_kv(mem, mem_g, w_kv, k_g):
    b, m, d = mem.shape
    hw = MEM_HEADS * MEM_HEAD_DIM
    wkv = jnp.concatenate([_pad_heads(w_kv[:, :hw], MEM_HEADS, MEM_HEAD_DIM),
                           _pad_heads(w_kv[:, hw:], MEM_HEADS, MEM_HEAD_DIM)], axis=1).astype(BF16)
    consts = [mem_g.reshape(1, d), wkv, _pad_cols(k_g.reshape(1, -1))]
    wide = MEM_HEADS * LANES
    spec = pl.BlockSpec((1, m, wide), lambda bi: (bi, 0, 0))
    return pl.pallas_call(
        _mem_kv_kernel,
        grid=(b,),
        in_specs=[pl.BlockSpec((1, m, d), lambda bi: (bi, 0, 0))] + [_const_spec(c.shape) for c in consts],
        out_specs=[spec, spec],
        out_shape=[jax.ShapeDtypeStruct((b, m, wide), BF16)] * 2,
        compiler_params=_params(("parallel",)),
        name="mem_kv",
    )(mem, *consts)


def _split_dot(a, w):
    a_hi = a.astype(BF16)
    a_lo = (a - a_hi.astype(F32)).astype(BF16)
    w_hi = w.astype(BF16)
    w_lo = (w - w_hi.astype(F32)).astype(BF16)
    w_both = jnp.concatenate([w_hi, w_lo], axis=1)
    n = w.shape[1]
    hi = jnp.dot(a_hi, w_both, preferred_element_type=F32)
    lo = jnp.dot(a_lo, w_both, preferred_element_type=F32)
    return (hi[:, :n] + hi[:, n:]) + (lo[:, :n] + lo[:, n:])


def _merge_kernel(oa_ref, ob_ref, sgd_ref, sgm_ref, x_ref, wa_ref, wb_ref, wo_ref, mxg_ref, wq_ref, qg_ref,
                  km_ref, vm_ref, wmo_ref, moeg_ref, wr_ref, bias_ref, x2_o, h3_o, comb_o):
    d = functools.partial(jnp.dot, preferred_element_type=F32)
    oa = jnp.concatenate([oa_ref[0, hd] for hd in range(DSA_HEADS)], axis=-1)
    merged = (sgd_ref[0].astype(F32) * d(oa, wa_ref[...]) + sgm_ref[0].astype(F32) * d(ob_ref[0], wb_ref[...]))
    x1 = x_ref[0] + d(merged.astype(BF16), wo_ref[...])

    d_model = x1.shape[-1]
    h2 = _rms(x1, mxg_ref[...], d_model).astype(BF16)
    qm = d(h2, wq_ref[...])
    outs = []
    for hd in range(MEM_HEADS):
        sl = slice(hd * LANES, (hd + 1) * LANES)
        q = (_rms(qm[:, sl], qg_ref[...], MEM_HEAD_DIM) * MEM_HEAD_DIM ** -0.5).astype(BF16)
        logit = lax.dot_general(q, km_ref[0, :, sl], (((1,), (1,)), ((), ())), preferred_element_type=F32)
        logit = logit - jnp.max(logit, axis=-1, keepdims=True)
        p = jnp.exp(logit)
        pv = d(p.astype(BF16), vm_ref[0, :, sl])
        outs.append((pv / jnp.sum(p, axis=-1, keepdims=True)).astype(BF16))
    x2 = x1 + d(jnp.concatenate(outs, axis=-1), wmo_ref[...])
    x2_o[0] = x2

    h3 = _rms(x2, moeg_ref[...], d_model)
    h3_o[0] = h3.astype(BF16)
    logits = _split_dot(h3, wr_ref[...])
    tm = logits.shape[0]
    lane = lax.broadcasted_iota(jnp.int32, (tm, LANES), 1)
    big = jnp.int32(LANES)
    is_grp = jnp.logical_and(lane >= N_EXPERTS, lane < N_EXPERTS + N_GROUPS)
    glog = jnp.where(is_grp, logits, -jnp.inf)
    gmax = jnp.max(glog, axis=-1, keepdims=True)
    g_sel = jnp.min(jnp.where(glog == gmax, lane, big), axis=-1, keepdims=True) - N_EXPERTS
    p_g = 1.0 / jnp.sum(jnp.where(is_grp, jnp.exp(logits - gmax), 0.0), axis=-1, keepdims=True)
    aff = jax.nn.sigmoid(logits)
    in_grp = jnp.logical_and(lane >= g_sel * EXPERTS_PER_GROUP, lane < (g_sel + 1) * EXPERTS_PER_GROUP)
    val = jnp.where(in_grp, aff + bias_ref[...], -jnp.inf)
    m1 = jnp.max(val, axis=-1, keepdims=True)
    i1 = jnp.min(jnp.where(val == m1, lane, big), axis=-1, keepdims=True)
    val2 = jnp.where(lane == i1, -jnp.inf, val)
    m2 = jnp.max(val2, axis=-1, keepdims=True)
    i2 = jnp.min(jnp.where(val2 == m2, lane, big), axis=-1, keepdims=True)
    chosen = jnp.logical_or(lane == i1, lane == i2)
    a_sel = jnp.where(chosen, aff, 0.0)
    comb_o[0] = p_g * a_sel / jnp.sum(a_sel, axis=-1, keepdims=True)


def _merge_mem_router(oa, ob, sgd, sgm, x, w_br_dsa, w_br_mla, w_out, mem_x_g, mem_w_q, mem_q_g, km, vm,
                      mem_w_o, moe_g, w_group, w_expert, expert_bias, tm):
    b, s, d = x.shape
    wa = jnp.pad(w_br_dsa.reshape(DSA_HEADS, DSA_HEAD_DIM, d), ((0, 0), (0, LANES - DSA_HEAD_DIM), (0, 0)))
    wa = wa.reshape(DSA_HEADS * LANES, d).astype(BF16)
    wb = jnp.pad(w_br_mla.reshape(MLA_HEADS, MLA_V, d), ((0, 0), (0, LANES - MLA_V), (0, 0)))
    wb = wb.reshape(MLA_HEADS * LANES, d).astype(BF16)
    wq = _pad_heads(mem_w_q, MEM_HEADS, MEM_HEAD_DIM).astype(BF16)
    wmo = jnp.pad(mem_w_o.reshape(MEM_HEADS, MEM_HEAD_DIM, d), ((0, 0), (0, LANES - MEM_HEAD_DIM), (0, 0)))
    wmo = wmo.reshape(MEM_HEADS * LANES, d).astype(BF16)
    wr = _pad_cols(jnp.concatenate([w_expert, w_group], axis=1).astype(F32))
    bias = _pad_cols(expert_bias.reshape(1, -1).astype(F32))
    consts_a = [wa, wb, w_out.astype(BF16), mem_x_g.reshape(1, d), wq, _pad_cols(mem_q_g.reshape(1, -1))]
    consts_b = [wmo, moe_g.reshape(1, d), wr, bias]
    m = km.shape[1]
    mw = MEM_HEADS * LANES
    tok = lambda w: pl.BlockSpec((1, tm, w), lambda bi, i: (bi, i, 0))
    memspec = pl.BlockSpec((1, m, mw), lambda bi, i: (bi, 0, 0))
    return pl.pallas_call(
        _merge_kernel,
        grid=(b, s // tm),
        in_specs=[pl.BlockSpec((1, DSA_HEADS, tm, LANES), lambda bi, i: (bi, 0, i, 0)),
                  tok(MLA_HEADS * LANES), tok(d), tok(d), tok(d)]
                 + [_const_spec(c.shape) for c in consts_a] + [memspec, memspec]
                 + [_const_spec(c.shape) for c in consts_b],
        out_specs=[tok(d), tok(d), tok(LANES)],
        out_shape=[jax.ShapeDtypeStruct((b, s, d), F32), jax.ShapeDtypeStruct((b, s, d), BF16),
                   jax.ShapeDtypeStruct((b, s, LANES), F32)],
        compiler_params=_params(("parallel", "parallel")),
        name="merge_mem",
    )(oa, ob, sgd, sgm, x, *consts_a, km, vm, *consts_b)


def _moe_kernel(h_ref, comb_ref, x_ref, wg_ref, wu_ref, wd_ref, o_ref):
    g = pl.program_id(1)
    h = h_ref[...]
    comb = comb_ref[...]
    lane = lax.broadcasted_iota(jnp.int32, comb.shape, 1)
    parts = []
    for e in range(EXPERTS_PER_GROUP):
        gate = jnp.dot(h, wg_ref[e], preferred_element_type=F32)
        up = jnp.dot(h, wu_ref[e], preferred_element_type=F32)
        c_e = jnp.sum(jnp.where(lane == g * EXPERTS_PER_GROUP + e, comb, 0.0), axis=-1, keepdims=True)
        parts.append(((jax.nn.silu(gate) * up) * c_e).astype(BF16))
    w_down = wd_ref[...].reshape(EXPERTS_PER_GROUP * D_EXPERT, wd_ref.shape[-1])
    contrib = jnp.dot(jnp.concatenate(parts, axis=1), w_down, preferred_element_type=F32)

    @pl.when(g == 0)
    def _():
        o_ref[...] = x_ref[...] + contrib

    @pl.when(g > 0)
    def _():
        o_ref[...] += contrib


def _moe(h3, comb, x2, w_gate, w_up, w_down, tm):
    n, d = x2.shape
    per = EXPERTS_PER_GROUP
    return pl.pallas_call(
        _moe_kernel,
        grid=(n // tm, N_GROUPS),
        in_specs=[pl.BlockSpec((tm, d), lambda i, g: (i, 0)),
                  pl.BlockSpec((tm, LANES), lambda i, g: (i, 0)),
                  pl.BlockSpec((tm, d), lambda i, g: (i, 0)),
                  pl.BlockSpec((per, d, D_EXPERT), lambda i, g: (g, 0, 0)),
                  pl.BlockSpec((per, d, D_EXPERT), lambda i, g: (g, 0, 0)),
                  pl.BlockSpec((per, D_EXPERT, d), lambda i, g: (g, 0, 0))],
        out_specs=pl.BlockSpec((tm, d), lambda i, g: (i, 0)),
        out_shape=jax.ShapeDtypeStruct((n, d), F32),
        compiler_params=_params(("parallel", "arbitrary")),
        name="moe",
    )(h3, comb, x2, w_gate.astype(BF16), w_up.astype(BF16), w_down.astype(BF16))


def _layer(x, mem, positions, p):
    b, s, d = x.shape
    n = b * s
    assert s % (2 * KEY_CHUNK) == 0 and s % Q_TILE_MLA == 0
    topk = min(TOPK_MAX, s // 4)
    tm = min(TOKENS_PER_STEP_PROJ, s)
    qq, ka, va, ki, iw, mq, mk, mv, sgd, sgm = _in_proj(
        x.reshape(n, d), positions.reshape(n, 1), p["attn_norm_g"], p["w_in"], p["dsa_q_norm_g"],
        p["dsa_k_norm_g"], p["mla_cq_norm_g"], p["mla_ckv_norm_g"], p["mla_w_uq"], p["mla_w_ukv"],
        p["mla_q_norm_g"], p["mla_k_norm_g"], tm)
    r3 = lambda a: a.reshape(b, s, a.shape[-1])
    oa = _dsa_attention(r3(qq), iw, r3(ka), va, r3(ki), topk)
    ob = _mla_attention(r3(mq), r3(mk), mv)
    km, vm = _mem_kv(mem, p["mem_norm_g"], p["mem_w_kv"], p["mem_k_norm_g"])
    x2, h3, comb = _merge_mem_router(
        oa, ob, r3(sgd), r3(sgm), x, p["w_branch_dsa"], p["w_branch_mla"], p["w_out"], p["mem_x_norm_g"],
        p["mem_w_q"], p["mem_q_norm_g"], km, vm, p["mem_w_o"], p["moe_norm_g"], p["moe_w_group"],
        p["moe_w_expert"], p["moe_expert_bias"], min(TOKENS_PER_STEP_MERGE, s))
    out = _moe(h3.reshape(n, d), comb.reshape(n, LANES), x2.reshape(n, d), p["moe_w_gate"], p["moe_w_up"],
               p["moe_w_down"], min(TOKENS_PER_STEP_MOE, n))
    return out.reshape(b, s, d)


_PARAM_NAMES = ("attn_norm_g", "w_in", "dsa_q_norm_g", "dsa_k_norm_g", "mla_cq_norm_g", "mla_ckv_norm_g",
                "mla_w_uq", "mla_w_ukv", "mla_q_norm_g", "mla_k_norm_g", "w_branch_dsa", "w_branch_mla", "w_out",
                "mem_x_norm_g", "mem_norm_g", "mem_w_q", "mem_w_kv", "mem_q_norm_g", "mem_k_norm_g", "mem_w_o",
                "moe_norm_g", "moe_w_group", "moe_w_expert", "moe_expert_bias", "moe_w_gate", "moe_w_up",
                "moe_w_down")


def kernel(x, mem, positions, attn_norm_g, w_in, dsa_q_norm_g, dsa_k_norm_g, mla_cq_norm_g, mla_ckv_norm_g, mla_w_uq, mla_w_ukv, mla_q_norm_g, mla_k_norm_g, w_branch_dsa, w_branch_mla, w_out, mem_x_norm_g, mem_norm_g, mem_w_q, mem_w_kv, mem_q_norm_g, mem_k_norm_g, mem_w_o, moe_norm_g, moe_w_group, moe_w_expert, moe_expert_bias, moe_w_gate, moe_w_up, moe_w_down):
    stacked = (attn_norm_g, w_in, dsa_q_norm_g, dsa_k_norm_g, mla_cq_norm_g, mla_ckv_norm_g, mla_w_uq, mla_w_ukv,
               mla_q_norm_g, mla_k_norm_g, w_branch_dsa, w_branch_mla, w_out, mem_x_norm_g, mem_norm_g, mem_w_q,
               mem_w_kv, mem_q_norm_g, mem_k_norm_g, mem_w_o, moe_norm_g, moe_w_group, moe_w_expert,
               moe_expert_bias, moe_w_gate, moe_w_up, moe_w_down)
    for layer in range(attn_norm_g.shape[0]):
        p = {name: arr[layer] for name, arr in zip(_PARAM_NAMES, stacked)}
        x = _layer(x, mem, positions, p)
    return x
```

```python
import functools

import jax
import jax.numpy as jnp
from jax import lax
from jax.experimental import pallas as pl
from jax.experimental.pallas import tpu as pltpu

F32 = jnp.float32
BF16 = jnp.bfloat16

LANES = 128
SUBLANES = 8
VMEM_LIMIT = 56 * 1024 * 1024

ROPE_THETA = 10000.0
EPS = 1e-6
DSA_HEADS = 8
DSA_HEAD_DIM = 64
IDX_HEADS = 8
IDX_DIM = 64
TOPK_MAX = 256
MLA_HEADS = 8
MLA_NOPE = 64
MLA_ROPE = 32
MLA_QK = MLA_NOPE + MLA_ROPE
MLA_V = 64
MLA_Q_RANK = 256
MLA_KV_RANK = 128
MEM_HEADS = 4
MEM_HEAD_DIM = 64
N_GROUPS = 4
EXPERTS_PER_GROUP = 4
N_EXPERTS = N_GROUPS * EXPERTS_PER_GROUP
D_EXPERT = 256

NEG_BIG = -1e30
KEY_CHUNK = 256
Q_TILE_DSA = 256
DSA_V_ROWS = 80
MLA_V_ROWS = 80
Q_TILE_MLA = 512
MLA_HEAD_GROUP = 4
LOG2_E = 1.4426950408889634
BISECT_ITERS = 15
TOKENS_PER_STEP_PROJ = 256
TOKENS_PER_STEP_MERGE = 512
TOKENS_PER_STEP_MOE = 1024


def _const_spec(shape):
    nd = len(shape)
    return pl.BlockSpec(shape, lambda *_: (0,) * nd)


def _params(sem):
    return pltpu.CompilerParams(dimension_semantics=sem, vmem_limit_bytes=VMEM_LIMIT)


def _rms(x, g, n):
    ms = jnp.sum(x * x, axis=-1, keepdims=True) * (1.0 / n)
    return (x * lax.rsqrt(ms + EPS)) * g


def _rope(x, cos, sin_signed, lo_mask, half):
    fwd = pltpu.roll(x, LANES - half, 1)
    bwd = pltpu.roll(x, half, 1)
    return x * cos + jnp.where(lo_mask, fwd, bwd) * sin_signed


def _in_proj_kernel(x_ref, pos_ref, g_ref, wqq_ref, wsm_ref, wcq_ref, wg_ref, wuq_ref, wuk_ref,
                    wuv_ref, qag_ref, kag_ref, cqg_ref, ckvg_ref, mqg_ref, mkg_ref, freq_ref,
                    qq_o, ka_o, va_o, ki_o, iw_o, mq_o, mk_o, mv_o, sgd_o, sgm_o):
    x = x_ref[...]
    d_model = x.shape[-1]
    h = _rms(x, g_ref[...], d_model).astype(BF16)

    pos = pos_ref[...].astype(F32)
    lane = lax.broadcasted_iota(jnp.int32, (x.shape[0], LANES), 1)
    half_d, half_m = DSA_HEAD_DIM // 2, MLA_ROPE // 2
    ang = pos * freq_ref[...]
    lo_d = (lane & (DSA_HEAD_DIM - 1)) < half_d
    lo_m = lane < MLA_NOPE + half_m
    in_m_lo = jnp.logical_and(lane >= MLA_NOPE, lo_m)
    in_m_hi = jnp.logical_and(lane >= MLA_NOPE + half_m, lane < MLA_QK)

    def tables(t, fill):
        by_half = pltpu.roll(t, half_d, 1)
        head = jnp.where(lane < half_d, t, by_half)
        dsa = jnp.where(lane < DSA_HEAD_DIM, head, pltpu.roll(head, DSA_HEAD_DIM, 1))
        mla = jnp.where(in_m_lo, by_half, jnp.where(in_m_hi, pltpu.roll(t, half_d + half_m, 1), fill))
        return dsa, mla

    cos_d, cos_m = tables(jnp.cos(ang), 1.0)
    sin_d, sin_m = tables(jnp.sin(ang), 0.0)
    sin_d = jnp.where(lo_d, -sin_d, sin_d)
    sin_m = jnp.where(lo_m, -sin_m, sin_m)
    rope_d = functools.partial(_rope, cos=cos_d, sin_signed=sin_d, lo_mask=lo_d, half=DSA_HEAD_DIM // 2)
    rope_m = functools.partial(_rope, cos=cos_m, sin_signed=sin_m, lo_mask=lo_m, half=MLA_ROPE // 2)

    def dot(a, w_ref):
        return jnp.dot(a, w_ref[...], preferred_element_type=F32)

    att_scale = DSA_HEAD_DIM ** -0.5 * LOG2_E
    idx_scale = IDX_DIM ** -0.5 * IDX_HEADS ** -0.5
    mla_scale = MLA_QK ** -0.5 * LOG2_E

    qq = dot(h, wqq_ref)
    is_qa = lane < DSA_HEAD_DIM
    for hd in range(DSA_HEADS):
        sl = slice(hd * LANES, (hd + 1) * LANES)
        xh = qq[:, sl]
        ms = jnp.sum(jnp.where(is_qa, xh * xh, 0.0), axis=-1, keepdims=True) * (1.0 / DSA_HEAD_DIM)
        mult = jnp.where(is_qa, (lax.rsqrt(ms + EPS) * att_scale) * qag_ref[...], 1.0)
        qq_o[:, sl] = rope_d(xh * mult).astype(BF16)

    sm = dot(h, wsm_ref)
    ka_o[...] = rope_d(_rms(sm[:, 0:LANES], kag_ref[...], DSA_HEAD_DIM)).astype(BF16)
    va = sm[:, LANES:2 * LANES] + jnp.where(lane[0:1] == DSA_HEAD_DIM, 1.0, 0.0)
    va_o[0] = va.T[:DSA_V_ROWS].astype(BF16)
    ki_o[...] = rope_d(sm[:, 2 * LANES:3 * LANES]).astype(BF16)
    iw_o[0] = (sm[:, 3 * LANES:4 * LANES] * idx_scale).T[:IDX_HEADS]
    ckv = _rms(sm[:, 4 * LANES:5 * LANES], ckvg_ref[...], MLA_KV_RANK).astype(BF16)
    kpe = sm[:, 5 * LANES:6 * LANES]

    kn = dot(ckv, wuk_ref)
    mkg = mkg_ref[...]
    pe_rot = rope_m(kpe * mkg)
    ss_pe = jnp.sum(kpe * kpe, axis=-1, keepdims=True)
    for hd in range(MLA_HEADS):
        sl = slice(hd * LANES, (hd + 1) * LANES)
        xh = kn[:, sl]
        ms = (jnp.sum(xh * xh, axis=-1, keepdims=True) + ss_pe) * (1.0 / MLA_QK)
        mk_o[:, sl] = ((xh * mkg + pe_rot) * lax.rsqrt(ms + EPS)).astype(BF16)
    wide_lane = lax.broadcasted_iota(jnp.int32, (1, MLA_HEADS * LANES), 1)
    ones_col = jnp.where(wide_lane % LANES == MLA_V, 1.0, 0.0)
    mv = dot(ckv, wuv_ref) + ones_col
    for hd in range(MLA_HEADS):
        mv_o[0, hd] = mv[:, hd * LANES:(hd + 1) * LANES].T[:MLA_V_ROWS].astype(BF16)

    cq = _rms(dot(h, wcq_ref), cqg_ref[...], MLA_Q_RANK).astype(BF16)
    qb = dot(cq, wuq_ref)
    for hd in range(MLA_HEADS):
        sl = slice(hd * LANES, (hd + 1) * LANES)
        y = rope_m(_rms(qb[:, sl], mqg_ref[...], MLA_QK))
        mq_o[:, sl] = (y * mla_scale).astype(BF16)

    gates = jax.nn.sigmoid(dot(h, wg_ref))
    sgd_o[...] = gates[:, :d_model].astype(BF16)
    sgm_o[...] = gates[:, d_model:].astype(BF16)


def _pad_heads(w, heads, dim):
    k = w.shape[0]
    w = w.reshape(k, heads, dim)
    w = jnp.pad(w, ((0, 0), (0, 0), (0, LANES - dim)))
    return w.reshape(k, heads * LANES)


def _pad_cols(w, width=LANES, offset=0):
    return jnp.pad(w, ((0, 0), (offset, width - offset - w.shape[1])))


def _in_proj(x2, pos2, attn_g, w_in, dsa_q_g, dsa_k_g, cq_g, ckv_g, w_uq, w_ukv, mq_g, mk_g, tm):
    n, d = x2.shape
    sizes = (DSA_HEADS * DSA_HEAD_DIM, DSA_HEAD_DIM, DSA_HEAD_DIM, IDX_HEADS * IDX_DIM, IDX_DIM, IDX_HEADS,
             MLA_Q_RANK, MLA_KV_RANK, MLA_ROPE, d, d)
    offs = [0]
    for s in sizes:
        offs.append(offs[-1] + s)
    seg = [w_in[:, offs[i]:offs[i + 1]] for i in range(len(sizes))]
    w_dq, w_dk, w_dv, w_iq, w_ik, w_iw, w_cq, w_ckv, w_kpe, w_gd, w_gm = seg

    assert DSA_HEADS == IDX_HEADS and DSA_HEAD_DIM + IDX_DIM == LANES
    wqq = jnp.concatenate([w_dq.reshape(d, DSA_HEADS, DSA_HEAD_DIM), w_iq.reshape(d, IDX_HEADS, IDX_DIM)], axis=-1)
    wqq = wqq.reshape(d, DSA_HEADS * LANES).astype(BF16)
    wsm = jnp.concatenate([_pad_cols(w_dk), _pad_cols(w_dv), _pad_cols(w_ik, offset=DSA_HEAD_DIM), _pad_cols(w_iw), w_ckv,
                           _pad_cols(w_kpe, offset=MLA_NOPE)], axis=1).astype(BF16)
    wcq = w_cq.astype(BF16)
    wg = jnp.concatenate([w_gd, w_gm], axis=1).astype(BF16)
    wuq = _pad_heads(w_uq, MLA_HEADS, MLA_QK).astype(BF16)
    ukv = w_ukv.reshape(MLA_KV_RANK, MLA_HEADS, MLA_NOPE + MLA_V)
    wuk = _pad_heads(ukv[:, :, :MLA_NOPE].reshape(MLA_KV_RANK, -1), MLA_HEADS, MLA_NOPE).astype(BF16)
    wuv = _pad_heads(ukv[:, :, MLA_NOPE:].reshape(MLA_KV_RANK, -1), MLA_HEADS, MLA_V).astype(BF16)

    row = lambda v: _pad_cols(v.reshape(1, -1).astype(F32), width=max(LANES, v.size))
    half_d = DSA_HEAD_DIM // 2
    inv_d = ROPE_THETA ** (-jnp.arange(half_d, dtype=F32) / half_d)
    half_m = MLA_ROPE // 2
    inv_m = ROPE_THETA ** (-jnp.arange(half_m, dtype=F32) / half_m)
    freq = _pad_cols(jnp.concatenate([inv_d, inv_m]).reshape(1, -1))

    consts = [attn_g.reshape(1, d), wqq, wsm, wcq, wg, wuq, wuk, wuv, row(dsa_q_g), row(dsa_k_g),
              row(cq_g), row(ckv_g), row(mq_g), row(mk_g), freq]
    wide = DSA_HEADS * LANES
    out_shapes = [
        jax.ShapeDtypeStruct((n, wide), BF16),
        jax.ShapeDtypeStruct((n, LANES), BF16),
        jax.ShapeDtypeStruct((n // tm, DSA_V_ROWS, tm), BF16),
        jax.ShapeDtypeStruct((n, LANES), BF16),
        jax.ShapeDtypeStruct((n // tm, IDX_HEADS, tm), F32),
        jax.ShapeDtypeStruct((n, wide), BF16),
        jax.ShapeDtypeStruct((n, wide), BF16),
        jax.ShapeDtypeStruct((n // tm, MLA_HEADS, MLA_V_ROWS, tm), BF16),
        jax.ShapeDtypeStruct((n, d), BF16),
        jax.ShapeDtypeStruct((n, d), BF16),
    ]
    tile = lambda w: pl.BlockSpec((tm, w), lambda i: (i, 0))
    return pl.pallas_call(
        _in_proj_kernel,
        grid=(n // tm,),
        in_specs=[tile(d), tile(1)] + [_const_spec(c.shape) for c in consts],
        out_specs=[tile(s.shape[1]) if len(s.shape) == 2 else
                   pl.BlockSpec((1,) + s.shape[1:], lambda i, nd=len(s.shape): (i,) + (0,) * (nd - 1))
                   for s in out_shapes],
        out_shape=out_shapes,
        compiler_params=_params(("parallel",)),
        name="in_proj",
    )(x2, pos2, *consts)


def _dsa_kernel(qq_ref, iw_ref, ka_ref, vt_ref, ki_ref, o_ref, score_ref, logit_ref, *, topk):
    i = pl.program_id(1)
    tq = qq_ref.shape[1]
    heads = DSA_HEADS
    npair = (i * tq + tq + 2 * KEY_CHUNK - 1) // (2 * KEY_CHUNK)
    kf = float(topk)
    shape = (KEY_CHUNK, tq)
    folded = (SUBLANES, tq)

    def fold(a, op):
        return op(a.reshape(KEY_CHUNK // SUBLANES, SUBLANES, tq), axis=0)

    krow = lax.broadcasted_iota(jnp.int32, shape, 0)
    qcol = i * tq + lax.broadcasted_iota(jnp.int32, shape, 1)
    contract_last = (((1,), (1,)), ((), ()))

    def pair_loop(body, init):
        return lax.fori_loop(0, npair, lambda j, carry: body(2 * j, 2 * j + 1, carry), init)

    qq = jnp.concatenate([qq_ref[0, :, hd * LANES:(hd + 1) * LANES] for hd in range(heads)], axis=0)
    iw = iw_ref[0]

    def score_body(c0, c1, carry):
        mx, mn = carry
        for c in (c0, c1):
            kc = ki_ref[0, pl.ds(pl.multiple_of(c * KEY_CHUNK, KEY_CHUNK), KEY_CHUNK), :]
            sc = None
            for hp in range(IDX_HEADS // 2):
                rel = lax.dot_general(kc, qq[2 * hp * tq:(2 * hp + 2) * tq], contract_last,
                                      preferred_element_type=F32)
                rel = jnp.maximum(rel, 0.0)
                part = rel[:, :tq] * iw[2 * hp:2 * hp + 1, :] + rel[:, tq:] * iw[2 * hp + 1:2 * hp + 2, :]
                sc = part if sc is None else sc + part
            causal = (krow + c * KEY_CHUNK) <= qcol
            masked = jnp.where(causal, sc, -jnp.inf)
            score_ref[c] = masked
            mx = jnp.maximum(mx, fold(masked, jnp.max))
            mn = jnp.minimum(mn, fold(jnp.where(causal, sc, jnp.inf), jnp.min))
        return mx, mn

    mx, mn = pair_loop(score_body, (jnp.full(folded, -jnp.inf, F32), jnp.full(folded, jnp.inf, F32)))
    hi = jnp.max(mx, axis=0, keepdims=True)
    lo = jnp.min(mn, axis=0, keepdims=True)

    def count_gt(t):
        def body(c0, c1, acc):
            for c in (c0, c1):
                acc = acc + fold(jnp.where(score_ref[c] > t, 1.0, 0.0), jnp.sum)
            return acc
        return jnp.sum(pair_loop(body, jnp.zeros(folded, F32)), axis=0, keepdims=True)

    def bisect(_, carry):
        lo, hi = carry
        mid = 0.5 * (lo + hi)
        below = count_gt(mid) < kf
        return jnp.where(below, lo, mid), jnp.where(below, mid, hi)

    lo, hi = lax.fori_loop(0, BISECT_ITERS, bisect, (lo, hi))

    def max_le(t):
        def body(c0, c1, acc):
            for c in (c0, c1):
                s = score_ref[c]
                acc = jnp.maximum(acc, fold(jnp.where(s <= t, s, -jnp.inf), jnp.max))
            return acc
        return jnp.max(pair_loop(body, jnp.full(folded, -jnp.inf, F32)), axis=0, keepdims=True)

    n_causal = (i * tq + lax.broadcasted_iota(jnp.int32, (1, tq), 1) + 1).astype(F32)
    small = n_causal <= kf

    def refine_cond(carry):
        return jnp.min(carry[2]) < 0.5

    def refine_body(carry):
        m, thr, done, thr_cnt = carry

        def body(c0, c1, acc):
            cnt, nxt = acc
            for c in (c0, c1):
                s = score_ref[c]
                cnt = cnt + fold(jnp.where(s >= m, 1.0, 0.0), jnp.sum)
                nxt = jnp.maximum(nxt, fold(jnp.where(s < m, s, -jnp.inf), jnp.max))
            return cnt, nxt

        cnt, nxt = pair_loop(body, (jnp.zeros(folded, F32), jnp.full(folded, -jnp.inf, F32)))
        cnt = jnp.sum(cnt, axis=0, keepdims=True)
        nxt = jnp.max(nxt, axis=0, keepdims=True)
        reached = cnt >= kf
        hit = jnp.logical_and(reached, done < 0.5)
        return nxt, jnp.where(hit, m, thr), jnp.where(reached, 1.0, done), jnp.where(hit, cnt, thr_cnt)

    init = (max_le(hi), jnp.full((1, tq), -jnp.inf, F32), jnp.where(small, 1.0, 0.0), jnp.zeros((1, tq), F32))
    _, thr, _, thr_cnt = lax.while_loop(refine_cond, refine_body, init)

    def mask_exact_k():
        def body(c0, c1, carry):
            for c in (c0, c1):
                s = score_ref[c]
                sel = jnp.logical_and(s >= thr, s > -jnp.inf)
                score_ref[c] = jnp.where(sel, 0.0, NEG_BIG)
            return carry
        pair_loop(body, 0)

    def mask_with_ties():
        quota = jnp.where(small, 0.0, kf - count_gt(thr))
        earlier = (lax.broadcasted_iota(jnp.int32, (KEY_CHUNK, KEY_CHUNK), 1)
                   < lax.broadcasted_iota(jnp.int32, (KEY_CHUNK, KEY_CHUNK), 0)).astype(BF16)

        def body(c0, c1, seen):
            for c in (c0, c1):
                s = score_ref[c]
                eq = s == thr
                eqf = jnp.where(eq, 1.0, 0.0)
                before = jnp.dot(earlier, eqf.astype(BF16), preferred_element_type=F32) + seen
                sel = jnp.logical_or(s > thr, jnp.logical_and(eq, before < quota))
                seen = seen + jnp.sum(eqf, axis=0, keepdims=True)
                score_ref[c] = jnp.where(sel, 0.0, NEG_BIG)
            return seen
        pair_loop(body, jnp.zeros((1, tq), F32))

    surplus = jnp.max(jnp.where(small, 0.0, thr_cnt - kf)) > 0.5
    lax.cond(surplus, mask_with_ties, mask_exact_k)

    npairs_h = heads // 2

    def qk_pair(slot, c, hp):
        kc = ka_ref[0, pl.ds(pl.multiple_of(c * KEY_CHUNK, KEY_CHUNK), KEY_CHUNK), :]
        cols = slice(2 * hp * tq, (2 * hp + 2) * tq)
        logit_ref[slot, :, cols] = lax.dot_general(kc, qq[cols], contract_last, preferred_element_type=F32)

    def step(slot, c, slot_next, c_next, m_run, acc):
        bias = score_ref[c]
        vt = vt_ref[0, c]
        new_m, new_acc = [], []
        qk_pair(slot_next, c_next, 0)
        for hp in range(npairs_h):
            if hp + 1 < npairs_h:
                qk_pair(slot_next, c_next, hp + 1)
            probs, alphas = [], []
            for hd in (2 * hp, 2 * hp + 1):
                sl = slice(hd * tq, (hd + 1) * tq)
                lh = logit_ref[slot, :, sl] + bias
                m_old = m_run[:, sl]
                m_new = jnp.maximum(m_old, jnp.max(lh, axis=0, keepdims=True))
                probs.append(jnp.exp2(lh - m_new).astype(BF16))
                alphas.append(jnp.exp2(m_old - m_new))
                new_m.append(m_new)
            pv = jnp.dot(vt, jnp.concatenate(probs, axis=1), preferred_element_type=F32)
            new_acc.append(jnp.concatenate(alphas, axis=1) * acc[:, 2 * hp * tq:(2 * hp + 2) * tq] + pv)
        return jnp.concatenate(new_m, axis=1), jnp.concatenate(new_acc, axis=1)

    def attn_body(c0, c1, carry):
        m_run, acc = carry
        m_run, acc = step(0, c0, 1, c1, m_run, acc)
        return step(1, c1, 0, jnp.minimum(c0 + 2, 2 * npair - 1), m_run, acc)

    for hp in range(npairs_h):
        qk_pair(0, 0, hp)
    d_rows = vt_ref.shape[2]
    init = (jnp.full((1, heads * tq), NEG_BIG, F32), jnp.zeros((d_rows, heads * tq), F32))
    _, acc = pair_loop(attn_body, init)
    pad_rows = jnp.zeros((LANES - d_rows, tq), F32)
    for hd in range(heads):
        blk = acc[:, hd * tq:(hd + 1) * tq]
        out_t = jnp.concatenate([blk / blk[DSA_HEAD_DIM:DSA_HEAD_DIM + 1, :], pad_rows], axis=0)
        o_ref[0, hd] = out_t.T.astype(o_ref.dtype)


def _dsa_attention(qq, iw_t, ka, va_t, ki, topk):
    b, s, wide = qq.shape
    tq = Q_TILE_DSA
    nkc = s // KEY_CHUNK
    assert iw_t.shape[-1] == tq and va_t.shape[-1] == KEY_CHUNK
    v_t = va_t.reshape(b, nkc, DSA_V_ROWS, KEY_CHUNK)
    qspec = pl.BlockSpec((1, tq, wide), lambda bi, i: (bi, i, 0))
    kspec = pl.BlockSpec((1, s, LANES), lambda bi, i: (bi, 0, 0))
    return pl.pallas_call(
        functools.partial(_dsa_kernel, topk=topk),
        grid=(b, s // tq),
        in_specs=[qspec, pl.BlockSpec((1, IDX_HEADS, tq), lambda bi, i: (bi * (s // tq) + i, 0, 0)), kspec,
                  pl.BlockSpec((1, nkc, DSA_V_ROWS, KEY_CHUNK), lambda bi, i: (bi, 0, 0, 0)), kspec],
        out_specs=pl.BlockSpec((1, DSA_HEADS, tq, LANES), lambda bi, i: (bi, 0, i, 0)),
        out_shape=jax.ShapeDtypeStruct((b, DSA_HEADS, s, LANES), BF16),
        scratch_shapes=[pltpu.VMEM((nkc, KEY_CHUNK, tq), F32),
                        pltpu.VMEM((2, KEY_CHUNK, DSA_HEADS * tq), F32)],
        compiler_params=_params(("parallel", "arbitrary")),
        name="dsa_attn",
    )(qq, iw_t, ka, v_t, ki)


def _mla_kernel(q_ref, k_ref, vt_ref, o_ref, logit_ref):
    i = pl.program_id(2)
    tq = q_ref.shape[1]
    group = q_ref.shape[2] // LANES
    d_rows = vt_ref.shape[3]
    per_block = tq // vt_ref.shape[4]
    qs = [q_ref[0, :, g * LANES:(g + 1) * LANES] for g in range(group)]
    contract_last = (((1,), (1,)), ((), ()))

    def qk_head(slot, c, g):
        k0 = pl.multiple_of(c * tq, tq)
        logit_ref[slot, g] = lax.dot_general(k_ref[0, pl.ds(k0, tq), g * LANES:(g + 1) * LANES], qs[g],
                                             contract_last, preferred_element_type=F32)

    def step(slot, c, carry, masked, slot_next=None, c_next=None):
        if masked:
            keep = (lax.broadcasted_iota(jnp.int32, (tq, tq), 0) <= lax.broadcasted_iota(jnp.int32, (tq, tq), 1))
            bias = jnp.where(keep, 0.0, NEG_BIG)
        ahead = c_next is not None
        if ahead:
            qk_head(slot_next, c_next, 0)
        res = []
        for g in range(group):
            if ahead and g + 1 < group:
                qk_head(slot_next, c_next, g + 1)
            m_run = carry[2 * g]
            logit = logit_ref[slot, g]
            if masked:
                logit = logit + bias
            m_new = jnp.maximum(m_run, jnp.max(logit, axis=0, keepdims=True))
            p = jnp.exp2(logit - m_new).astype(BF16)
            alpha = jnp.exp2(m_run - m_new)
            vt = jnp.concatenate([vt_ref[0, c * per_block + t, g] for t in range(per_block)], axis=1)
            pv = jnp.dot(vt, p, preferred_element_type=F32)
            res += [m_new, alpha * carry[2 * g + 1] + pv]
        return tuple(res)

    def pair_body(j, carry):
        carry = step(0, 2 * j, carry, False, 1, 2 * j + 1)
        return step(1, 2 * j + 1, carry, False, 0, 2 * j + 2)

    def tail_even(carry):
        return step(0, i, carry, True)

    def tail_odd(carry):
        carry = step(0, i - 1, carry, False, 1, i)
        return step(1, i, carry, True)

    for g in range(group):
        qk_head(0, 0, g)
    init = (jnp.full((1, tq), NEG_BIG, F32), jnp.zeros((d_rows, tq), F32)) * group
    carry = lax.fori_loop(0, i // 2, pair_body, init)
    carry = lax.cond(i % 2 == 0, tail_even, tail_odd, carry)
    pad_rows = jnp.zeros((LANES - d_rows, tq), F32)
    for g in range(group):
        acc = carry[2 * g + 1]
        out_t = jnp.concatenate([acc / acc[MLA_V:MLA_V + 1, :], pad_rows], axis=0)
        o_ref[0, :, g * LANES:(g + 1) * LANES] = out_t.T.astype(o_ref.dtype)


def _mla_attention(mq, mk, mv_t):
    b, s, wide = mq.shape
    tq = Q_TILE_MLA
    nkc = s // tq
    gw = MLA_HEAD_GROUP * LANES
    nt, _, _, tm = mv_t.shape
    nt //= b
    assert tq % tm == 0
    v_t = mv_t.reshape(b, nt, MLA_HEADS, MLA_V_ROWS, tm)
    return pl.pallas_call(
        _mla_kernel,
        grid=(b, MLA_HEADS // MLA_HEAD_GROUP, s // tq),
        in_specs=[pl.BlockSpec((1, tq, gw), lambda bi, h, i: (bi, i, h)),
                  pl.BlockSpec((1, s, gw), lambda bi, h, i: (bi, 0, h)),
                  pl.BlockSpec((1, nt, MLA_HEAD_GROUP, MLA_V_ROWS, tm), lambda bi, h, i: (bi, 0, h, 0, 0))],
        out_specs=pl.BlockSpec((1, tq, gw), lambda bi, h, i: (bi, i, h)),
        out_shape=jax.ShapeDtypeStruct((b, s, wide), BF16),
        scratch_shapes=[pltpu.VMEM((2, MLA_HEAD_GROUP, tq, tq), F32)],
        compiler_params=_params(("parallel", "parallel", "arbitrary")),
        name="mla_attn",
    )(mq, mk, v_t)


def _mem_kv_kernel(mem_ref, g_ref, wkv_ref, kg_ref, k_o, v_o):
    m = mem_ref[0]
    hm = _rms(m, g_ref[...], m.shape[-1]).astype(BF16)
    kv = jnp.dot(hm, wkv_ref[...], preferred_element_type=F32)
    for hd in range(MEM_HEADS):
        sl = slice(hd * LANES, (hd + 1) * LANES)
        k_o[0, :, sl] = _rms(kv[:, sl], kg_ref[...], MEM_HEAD_DIM).astype(BF16)
    v_o[0] = kv[:, MEM_HEADS * LANES:].astype(BF16)


def _mem_kv(mem, mem_g, w_kv, k_g):
    b, m, d = mem.shape
    hw = MEM_HEADS * MEM_HEAD_DIM
    wkv = jnp.concatenate([_pad_heads(w_kv[:, :hw], MEM_HEADS, MEM_HEAD_DIM),
                           _pad_heads(w_kv[:, hw:], MEM_HEADS, MEM_HEAD_DIM)], axis=1).astype(BF16)
    consts = [mem_g.reshape(1, d), wkv, _pad_cols(k_g.reshape(1, -1))]
    wide = MEM_HEADS * LANES
    spec = pl.BlockSpec((1, m, wide), lambda bi: (bi, 0, 0))
    return pl.pallas_call(
        _mem_kv_kernel,
        grid=(b,),
        in_specs=[pl.BlockSpec((1, m, d), lambda bi: (bi, 0, 0))] + [_const_spec(c.shape) for c in consts],
        out_specs=[spec, spec],
        out_shape=[jax.ShapeDtypeStruct((b, m, wide), BF16)] * 2,
        compiler_params=_params(("parallel",)),
        name="mem_kv",
    )(mem, *consts)


def _split_dot(a, w):
    a_hi = a.astype(BF16)
    a_lo = (a - a_hi.astype(F32)).astype(BF16)
    w_hi = w.astype(BF16)
    w_lo = (w - w_hi.astype(F32)).astype(BF16)
    w_both = jnp.concatenate([w_hi, w_lo], axis=1)
    n = w.shape[1]
    hi = jnp.dot(a_hi, w_both, preferred_element_type=F32)
    lo = jnp.dot(a_lo, w_both, preferred_element_type=F32)
    return (hi[:, :n] + hi[:, n:]) + (lo[:, :n] + lo[:, n:])


def _merge_kernel(oa_ref, ob_ref, sgd_ref, sgm_ref, x_ref, wa_ref, wb_ref, wo_ref, mxg_ref, wq_ref, qg_ref,
                  km_ref, vm_ref, wmo_ref, moeg_ref, wr_ref, bias_ref, x2_o, h3_o, comb_o):
    d = functools.partial(jnp.dot, preferred_element_type=F32)
    oa = jnp.concatenate([oa_ref[0, hd] for hd in range(DSA_HEADS)], axis=-1)
    merged = (sgd_ref[0].astype(F32) * d(oa, wa_ref[...]) + sgm_ref[0].astype(F32) * d(ob_ref[0], wb_ref[...]))
    x1 = x_ref[0] + d(merged.astype(BF16), wo_ref[...])

    d_model = x1.shape[-1]
    h2 = _rms(x1, mxg_ref[...], d_model).astype(BF16)
    qm = d(h2, wq_ref[...])
    outs = []
    for hd in range(MEM_HEADS):
        sl = slice(hd * LANES, (hd + 1) * LANES)
        q = (_rms(qm[:, sl], qg_ref[...], MEM_HEAD_DIM) * MEM_HEAD_DIM ** -0.5).astype(BF16)
        logit = lax.dot_general(q, km_ref[0, :, sl], (((1,), (1,)), ((), ())), preferred_element_type=F32)
        logit = logit - jnp.max(logit, axis=-1, keepdims=True)
        p = jnp.exp(logit)
        pv = d(p.astype(BF16), vm_ref[0, :, sl])
        outs.append((pv / jnp.sum(p, axis=-1, keepdims=True)).astype(BF16))
    x2 = x1 + d(jnp.concatenate(outs, axis=-1), wmo_ref[...])
    x2_o[0] = x2

    h3 = _rms(x2, moeg_ref[...], d_model)
    h3_o[0] = h3.astype(BF16)
    logits = _split_dot(h3, wr_ref[...])
    tm = logits.shape[0]
    lane = lax.broadcasted_iota(jnp.int32, (tm, LANES), 1)
    big = jnp.int32(LANES)
    is_grp = jnp.logical_and(lane >= N_EXPERTS, lane < N_EXPERTS + N_GROUPS)
    glog = jnp.where(is_grp, logits, -jnp.inf)
    gmax = jnp.max(glog, axis=-1, keepdims=True)
    g_sel = jnp.min(jnp.where(glog == gmax, lane, big), axis=-1, keepdims=True) - N_EXPERTS
    p_g = 1.0 / jnp.sum(jnp.where(is_grp, jnp.exp(logits - gmax), 0.0), axis=-1, keepdims=True)
    aff = jax.nn.sigmoid(logits)
    in_grp = jnp.logical_and(lane >= g_sel * EXPERTS_PER_GROUP, lane < (g_sel + 1) * EXPERTS_PER_GROUP)
    val = jnp.where(in_grp, aff + bias_ref[...], -jnp.inf)
    m1 = jnp.max(val, axis=-1, keepdims=True)
    i1 = jnp.min(jnp.where(val == m1, lane, big), axis=-1, keepdims=True)
    val2 = jnp.where(lane == i1, -jnp.inf, val)
    m2 = jnp.max(val2, axis=-1, keepdims=True)
    i2 = jnp.min(jnp.where(val2 == m2, lane, big), axis=-1, keepdims=True)
    chosen = jnp.logical_or(lane == i1, lane == i2)
    a_sel = jnp.where(chosen, aff, 0.0)
    comb_o[0] = p_g * a_sel / jnp.sum(a_sel, axis=-1, keepdims=True)


def _merge_mem_router(oa, ob, sgd, sgm, x, w_br_dsa, w_br_mla, w_out, mem_x_g, mem_w_q, mem_q_g, km, vm,
                      mem_w_o, moe_g, w_group, w_expert, expert_bias, tm):
    b, s, d = x.shape
    wa = jnp.pad(w_br_dsa.reshape(DSA_HEADS, DSA_HEAD_DIM, d), ((0, 0), (0, LANES - DSA_HEAD_DIM), (0, 0)))
    wa = wa.reshape(DSA_HEADS * LANES, d).astype(BF16)
    wb = jnp.pad(w_br_mla.reshape(MLA_HEADS, MLA_V, d), ((0, 0), (0, LANES - MLA_V), (0, 0)))
    wb = wb.reshape(MLA_HEADS * LANES, d).astype(BF16)
    wq = _pad_heads(mem_w_q, MEM_HEADS, MEM_HEAD_DIM).astype(BF16)
    wmo = jnp.pad(mem_w_o.reshape(MEM_HEADS, MEM_HEAD_DIM, d), ((0, 0), (0, LANES - MEM_HEAD_DIM), (0, 0)))
    wmo = wmo.reshape(MEM_HEADS * LANES, d).astype(BF16)
    wr = _pad_cols(jnp.concatenate([w_expert, w_group], axis=1).astype(F32))
    bias = _pad_cols(expert_bias.reshape(1, -1).astype(F32))
    consts_a = [wa, wb, w_out.astype(BF16), mem_x_g.reshape(1, d), wq, _pad_cols(mem_q_g.reshape(1, -1))]
    consts_b = [wmo, moe_g.reshape(1, d), wr, bias]
    m = km.shape[1]
    mw = MEM_HEADS * LANES
    tok = lambda w: pl.BlockSpec((1, tm, w), lambda bi, i: (bi, i, 0))
    memspec = pl.BlockSpec((1, m, mw), lambda bi, i: (bi, 0, 0))
    return pl.pallas_call(
        _merge_kernel,
        grid=(b, s // tm),
        in_specs=[pl.BlockSpec((1, DSA_HEADS, tm, LANES), lambda bi, i: (bi, 0, i, 0)),
                  tok(MLA_HEADS * LANES), tok(d), tok(d), tok(d)]
                 + [_const_spec(c.shape) for c in consts_a] + [memspec, memspec]
                 + [_const_spec(c.shape) for c in consts_b],
        out_specs=[tok(d), tok(d), tok(LANES)],
        out_shape=[jax.ShapeDtypeStruct((b, s, d), F32), jax.ShapeDtypeStruct((b, s, d), BF16),
                   jax.ShapeDtypeStruct((b, s, LANES), F32)],
        compiler_params=_params(("parallel", "parallel")),
        name="merge_mem",
    )(oa, ob, sgd, sgm, x, *consts_a, km, vm, *consts_b)


def _moe_kernel(h_ref, comb_ref, x_ref, wg_ref, wu_ref, wd_ref, o_ref):
    g = pl.program_id(1)
    h = h_ref[...]
    comb = comb_ref[...]
    lane = lax.broadcasted_iota(jnp.int32, comb.shape, 1)
    parts = []
    for e in range(EXPERTS_PER_GROUP):
        gate = jnp.dot(h, wg_ref[e], preferred_element_type=F32)
        up = jnp.dot(h, wu_ref[e], preferred_element_type=F32)
        c_e = jnp.sum(jnp.where(lane == g * EXPERTS_PER_GROUP + e, comb, 0.0), axis=-1, keepdims=True)
        parts.append(((jax.nn.silu(gate) * up) * c_e).astype(BF16))
    w_down = wd_ref[...].reshape(EXPERTS_PER_GROUP * D_EXPERT, wd_ref.shape[-1])
    contrib = jnp.dot(jnp.concatenate(parts, axis=1), w_down, preferred_element_type=F32)

    @pl.when(g == 0)
    def _():
        o_ref[...] = x_ref[...] + contrib

    @pl.when(g > 0)
    def _():
        o_ref[...] += contrib


def _moe(h3, comb, x2, w_gate, w_up, w_down, tm):
    n, d = x2.shape
    per = EXPERTS_PER_GROUP
    return pl.pallas_call(
        _moe_kernel,
        grid=(n // tm, N_GROUPS),
        in_specs=[pl.BlockSpec((tm, d), lambda i, g: (i, 0)),
                  pl.BlockSpec((tm, LANES), lambda i, g: (i, 0)),
                  pl.BlockSpec((tm, d), lambda i, g: (i, 0)),
                  pl.BlockSpec((per, d, D_EXPERT), lambda i, g: (g, 0, 0)),
                  pl.BlockSpec((per, d, D_EXPERT), lambda i, g: (g, 0, 0)),
                  pl.BlockSpec((per, D_EXPERT, d), lambda i, g: (g, 0, 0))],
        out_specs=pl.BlockSpec((tm, d), lambda i, g: (i, 0)),
        out_shape=jax.ShapeDtypeStruct((n, d), F32),
        compiler_params=_params(("parallel", "arbitrary")),
        name="moe",
    )(h3, comb, x2, w_gate.astype(BF16), w_up.astype(BF16), w_down.astype(BF16))


def _layer(x, mem, positions, p):
    b, s, d = x.shape
    n = b * s
    assert s % (2 * KEY_CHUNK) == 0 and s % Q_TILE_MLA == 0
    topk = min(TOPK_MAX, s // 4)
    tm = min(TOKENS_PER_STEP_PROJ, s)
    qq, ka, va, ki, iw, mq, mk, mv, sgd, sgm = _in_proj(
        x.reshape(n, d), positions.reshape(n, 1), p["attn_norm_g"], p["w_in"], p["dsa_q_norm_g"],
        p["dsa_k_norm_g"], p["mla_cq_norm_g"], p["mla_ckv_norm_g"], p["mla_w_uq"], p["mla_w_ukv"],
        p["mla_q_norm_g"], p["mla_k_norm_g"], tm)
    r3 = lambda a: a.reshape(b, s, a.shape[-1])
    oa = _dsa_attention(r3(qq), iw, r3(ka), va, r3(ki), topk)
    ob = _mla_attention(r3(mq), r3(mk), mv)
    km, vm = _mem_kv(mem, p["mem_norm_g"], p["mem_w_kv"], p["mem_k_norm_g"])
    x2, h3, comb = _merge_mem_router(
        oa, ob, r3(sgd), r3(sgm), x, p["w_branch_dsa"], p["w_branch_mla"], p["w_out"], p["mem_x_norm_g"],
        p["mem_w_q"], p["mem_q_norm_g"], km, vm, p["mem_w_o"], p["moe_norm_g"], p["moe_w_group"],
        p["moe_w_expert"], p["moe_expert_bias"], min(TOKENS_PER_STEP_MERGE, s))
    out = _moe(h3.reshape(n, d), comb.reshape(n, LANES), x2.reshape(n, d), p["moe_w_gate"], p["moe_w_up"],
               p["moe_w_down"], min(TOKENS_PER_STEP_MOE, n))
    return out.reshape(b, s, d)


_PARAM_NAMES = ("attn_norm_g", "w_in", "dsa_q_norm_g", "dsa_k_norm_g", "mla_cq_norm_g", "mla_ckv_norm_g",
                "mla_w_uq", "mla_w_ukv", "mla_q_norm_g", "mla_k_norm_g", "w_branch_dsa", "w_branch_mla", "w_out",
                "mem_x_norm_g", "mem_norm_g", "mem_w_q", "mem_w_kv", "mem_q_norm_g", "mem_k_norm_g", "mem_w_o",
                "moe_norm_g", "moe_w_group", "moe_w_expert", "moe_expert_bias", "moe_w_gate", "moe_w_up",
                "moe_w_down")


def kernel(x, mem, positions, attn_norm_g, w_in, dsa_q_norm_g, dsa_k_norm_g, mla_cq_norm_g, mla_ckv_norm_g, mla_w_uq, mla_w_ukv, mla_q_norm_g, mla_k_norm_g, w_branch_dsa, w_branch_mla, w_out, mem_x_norm_g, mem_norm_g, mem_w_q, mem_w_kv, mem_q_norm_g, mem_k_norm_g, mem_w_o, moe_norm_g, moe_w_group, moe_w_expert, moe_expert_bias, moe_w_gate, moe_w_up, moe_w_down):
    stacked = (attn_norm_g, w_in, dsa_q_norm_g, dsa_k_norm_g, mla_cq_norm_g, mla_ckv_norm_g, mla_w_uq, mla_w_ukv,
               mla_q_norm_g, mla_k_norm_g, w_branch_dsa, w_branch_mla, w_out, mem_x_norm_g, mem_norm_g, mem_w_q,
               mem_w_kv, mem_q_norm_g, mem_k_norm_g, mem_w_o, moe_norm_g, moe_w_group, moe_w_expert,
               moe_expert_bias, moe_w_gate, moe_w_up, moe_w_down)
    for layer in range(attn_norm_g.shape[0]):
        p = {name: arr[layer] for name, arr in zip(_PARAM_NAMES, stacked)}
        x = _layer(x, mem, positions, p)
    return x
```

```python
import functools

import jax
import jax.numpy as jnp
from jax import lax
from jax.experimental import pallas as pl
from jax.experimental.pallas import tpu as pltpu

F32 = jnp.float32
BF16 = jnp.bfloat16

LANES = 128
SUBLANES = 8
VMEM_LIMIT = 56 * 1024 * 1024

ROPE_THETA = 10000.0
EPS = 1e-6
DSA_HEADS = 8
DSA_HEAD_DIM = 64
IDX_HEADS = 8
IDX_DIM = 64
TOPK_MAX = 256
MLA_HEADS = 8
MLA_NOPE = 64
MLA_ROPE = 32
MLA_QK = MLA_NOPE + MLA_ROPE
MLA_V = 64
MLA_Q_RANK = 256
MLA_KV_RANK = 128
MEM_HEADS = 4
MEM_HEAD_DIM = 64
N_GROUPS = 4
EXPERTS_PER_GROUP = 4
N_EXPERTS = N_GROUPS * EXPERTS_PER_GROUP
D_EXPERT = 256

NEG_BIG = -1e30
KEY_CHUNK = 256
Q_TILE_DSA = 256
DSA_V_ROWS = 80
MLA_V_ROWS = 80
Q_TILE_MLA = 512
MLA_HEAD_GROUP = 4
LOG2_E = 1.4426950408889634
BISECT_ITERS = 15
TOKENS_PER_STEP_PROJ = 256
TOKENS_PER_STEP_MERGE = 512
TOKENS_PER_STEP_MOE = 1024


def _const_spec(shape):
    nd = len(shape)
    return pl.BlockSpec(shape, lambda *_: (0,) * nd)


def _params(sem):
    return pltpu.CompilerParams(dimension_semantics=sem, vmem_limit_bytes=VMEM_LIMIT)


def _rms(x, g, n):
    ms = jnp.sum(x * x, axis=-1, keepdims=True) * (1.0 / n)
    return (x * lax.rsqrt(ms + EPS)) * g


def _rope(x, cos, sin_signed, lo_mask, half):
    fwd = pltpu.roll(x, LANES - half, 1)
    bwd = pltpu.roll(x, half, 1)
    return x * cos + jnp.where(lo_mask, fwd, bwd) * sin_signed


def _in_proj_kernel(x_ref, pos_ref, g_ref, wqq_ref, wsm_ref, wcq_ref, wg_ref, wuq_ref, wuk_ref,
                    wuv_ref, qag_ref, kag_ref, cqg_ref, ckvg_ref, mqg_ref, mkg_ref, freq_ref,
                    qq_o, ka_o, va_o, ki_o, iw_o, mq_o, mk_o, mv_o, sgd_o, sgm_o):
    x = x_ref[...]
    d_model = x.shape[-1]
    h = _rms(x, g_ref[...], d_model).astype(BF16)

    pos = pos_ref[...].astype(F32)
    lane = lax.broadcasted_iota(jnp.int32, (x.shape[0], LANES), 1)
    half_d, half_m = DSA_HEAD_DIM // 2, MLA_ROPE // 2
    ang = pos * freq_ref[...]
    lo_d = (lane & (DSA_HEAD_DIM - 1)) < half_d
    lo_m = lane < MLA_NOPE + half_m
    in_m_lo = jnp.logical_and(lane >= MLA_NOPE, lo_m)
    in_m_hi = jnp.logical_and(lane >= MLA_NOPE + half_m, lane < MLA_QK)

    def tables(t, fill):
        by_half = pltpu.roll(t, half_d, 1)
        head = jnp.where(lane < half_d, t, by_half)
        dsa = jnp.where(lane < DSA_HEAD_DIM, head, pltpu.roll(head, DSA_HEAD_DIM, 1))
        mla = jnp.where(in_m_lo, by_half, jnp.where(in_m_hi, pltpu.roll(t, half_d + half_m, 1), fill))
        return dsa, mla

    cos_d, cos_m = tables(jnp.cos(ang), 1.0)
    sin_d, sin_m = tables(jnp.sin(ang), 0.0)
    sin_d = jnp.where(lo_d, -sin_d, sin_d)
    sin_m = jnp.where(lo_m, -sin_m, sin_m)
    rope_d = functools.partial(_rope, cos=cos_d, sin_signed=sin_d, lo_mask=lo_d, half=DSA_HEAD_DIM // 2)
    rope_m = functools.partial(_rope, cos=cos_m, sin_signed=sin_m, lo_mask=lo_m, half=MLA_ROPE // 2)

    def dot(a, w_ref):
        return jnp.dot(a, w_ref[...], preferred_element_type=F32)

    att_scale = DSA_HEAD_DIM ** -0.5 * LOG2_E
    idx_scale = IDX_DIM ** -0.5 * IDX_HEADS ** -0.5
    mla_scale = MLA_QK ** -0.5 * LOG2_E

    qq = dot(h, wqq_ref)
    is_qa = lane < DSA_HEAD_DIM
    for hd in range(DSA_HEADS):
        sl = slice(hd * LANES, (hd + 1) * LANES)
        xh = qq[:, sl]
        ms = jnp.sum(jnp.where(is_qa, xh * xh, 0.0), axis=-1, keepdims=True) * (1.0 / DSA_HEAD_DIM)
        mult = jnp.where(is_qa, (lax.rsqrt(ms + EPS) * att_scale) * qag_ref[...], 1.0)
        qq_o[:, sl] = rope_d(xh * mult).astype(BF16)

    sm = dot(h, wsm_ref)
    ka_o[...] = rope_d(_rms(sm[:, 0:LANES], kag_ref[...], DSA_HEAD_DIM)).astype(BF16)
    va = sm[:, LANES:2 * LANES] + jnp.where(lane[0:1] == DSA_HEAD_DIM, 1.0, 0.0)
    va_o[0] = va.T[:DSA_V_ROWS].astype(BF16)
    ki_o[...] = rope_d(sm[:, 2 * LANES:3 * LANES]).astype(BF16)
    iw_o[0] = (sm[:, 3 * LANES:4 * LANES] * idx_scale).T[:IDX_HEADS]
    ckv = _rms(sm[:, 4 * LANES:5 * LANES], ckvg_ref[...], MLA_KV_RANK).astype(BF16)
    kpe = sm[:, 5 * LANES:6 * LANES]

    kn = dot(ckv, wuk_ref)
    mkg = mkg_ref[...]
    pe_rot = rope_m(kpe * mkg)
    ss_pe = jnp.sum(kpe * kpe, axis=-1, keepdims=True)
    for hd in range(MLA_HEADS):
        sl = slice(hd * LANES, (hd + 1) * LANES)
        xh = kn[:, sl]
        ms = (jnp.sum(xh * xh, axis=-1, keepdims=True) + ss_pe) * (1.0 / MLA_QK)
        mk_o[:, sl] = ((xh * mkg + pe_rot) * lax.rsqrt(ms + EPS)).astype(BF16)
    wide_lane = lax.broadcasted_iota(jnp.int32, (1, MLA_HEADS * LANES), 1)
    ones_col = jnp.where(wide_lane % LANES == MLA_V, 1.0, 0.0)
    mv = dot(ckv, wuv_ref) + ones_col
    for hd in range(MLA_HEADS):
        mv_o[0, hd] = mv[:, hd * LANES:(hd + 1) * LANES].T[:MLA_V_ROWS].astype(BF16)

    cq = _rms(dot(h, wcq_ref), cqg_ref[...], MLA_Q_RANK).astype(BF16)
    qb = dot(cq, wuq_ref)
    for hd in range(MLA_HEADS):
        sl = slice(hd * LANES, (hd + 1) * LANES)
        y = rope_m(_rms(qb[:, sl], mqg_ref[...], MLA_QK))
        mq_o[:, sl] = (y * mla_scale).astype(BF16)

    gates = jax.nn.sigmoid(dot(h, wg_ref))
    sgd_o[...] = gates[:, :d_model].astype(BF16)
    sgm_o[...] = gates[:, d_model:].astype(BF16)


def _pad_heads(w, heads, dim):
    k = w.shape[0]
    w = w.reshape(k, heads, dim)
    w = jnp.pad(w, ((0, 0), (0, 0), (0, LANES - dim)))
    return w.reshape(k, heads * LANES)


def _pad_cols(w, width=LANES, offset=0):
    return jnp.pad(w, ((0, 0), (offset, width - offset - w.shape[1])))


def _in_proj(x2, pos2, attn_g, w_in, dsa_q_g, dsa_k_g, cq_g, ckv_g, w_uq, w_ukv, mq_g, mk_g, tm):
    n, d = x2.shape
    sizes = (DSA_HEADS * DSA_HEAD_DIM, DSA_HEAD_DIM, DSA_HEAD_DIM, IDX_HEADS * IDX_DIM, IDX_DIM, IDX_HEADS,
             MLA_Q_RANK, MLA_KV_RANK, MLA_ROPE, d, d)
    offs = [0]
    for s in sizes:
        offs.append(offs[-1] + s)
    seg = [w_in[:, offs[i]:offs[i + 1]] for i in range(len(sizes))]
    w_dq, w_dk, w_dv, w_iq, w_ik, w_iw, w_cq, w_ckv, w_kpe, w_gd, w_gm = seg

    assert DSA_HEADS == IDX_HEADS and DSA_HEAD_DIM + IDX_DIM == LANES
    wqq = jnp.concatenate([w_dq.reshape(d, DSA_HEADS, DSA_HEAD_DIM), w_iq.reshape(d, IDX_HEADS, IDX_DIM)], axis=-1)
    wqq = wqq.reshape(d, DSA_HEADS * LANES).astype(BF16)
    wsm = jnp.concatenate([_pad_cols(w_dk), _pad_cols(w_dv), _pad_cols(w_ik, offset=DSA_HEAD_DIM), _pad_cols(w_iw), w_ckv,
                           _pad_cols(w_kpe, offset=MLA_NOPE)], axis=1).astype(BF16)
    wcq = w_cq.astype(BF16)
    wg = jnp.concatenate([w_gd, w_gm], axis=1).astype(BF16)
    wuq = _pad_heads(w_uq, MLA_HEADS, MLA_QK).astype(BF16)
    ukv = w_ukv.reshape(MLA_KV_RANK, MLA_HEADS, MLA_NOPE + MLA_V)
    wuk = _pad_heads(ukv[:, :, :MLA_NOPE].reshape(MLA_KV_RANK, -1), MLA_HEADS, MLA_NOPE).astype(BF16)
    wuv = _pad_heads(ukv[:, :, MLA_NOPE:].reshape(MLA_KV_RANK, -1), MLA_HEADS, MLA_V).astype(BF16)

    row = lambda v: _pad_cols(v.reshape(1, -1).astype(F32), width=max(LANES, v.size))
    half_d = DSA_HEAD_DIM // 2
    inv_d = ROPE_THETA ** (-jnp.arange(half_d, dtype=F32) / half_d)
    half_m = MLA_ROPE // 2
    inv_m = ROPE_THETA ** (-jnp.arange(half_m, dtype=F32) / half_m)
    freq = _pad_cols(jnp.concatenate([inv_d, inv_m]).reshape(1, -1))

    consts = [attn_g.reshape(1, d), wqq, wsm, wcq, wg, wuq, wuk, wuv, row(dsa_q_g), row(dsa_k_g),
              row(cq_g), row(ckv_g), row(mq_g), row(mk_g), freq]
    wide = DSA_HEADS * LANES
    out_shapes = [
        jax.ShapeDtypeStruct((n, wide), BF16),
        jax.ShapeDtypeStruct((n, LANES), BF16),
        jax.ShapeDtypeStruct((n // tm, DSA_V_ROWS, tm), BF16),
        jax.ShapeDtypeStruct((n, LANES), BF16),
        jax.ShapeDtypeStruct((n // tm, IDX_HEADS, tm), F32),
        jax.ShapeDtypeStruct((n, wide), BF16),
        jax.ShapeDtypeStruct((n, wide), BF16),
        jax.ShapeDtypeStruct((n // tm, MLA_HEADS, MLA_V_ROWS, tm), BF16),
        jax.ShapeDtypeStruct((n, d), BF16),
        jax.ShapeDtypeStruct((n, d), BF16),
    ]
    tile = lambda w: pl.BlockSpec((tm, w), lambda i: (i, 0))
    return pl.pallas_call(
        _in_proj_kernel,
        grid=(n // tm,),
        in_specs=[tile(d), tile(1)] + [_const_spec(c.shape) for c in consts],
        out_specs=[tile(s.shape[1]) if len(s.shape) == 2 else
                   pl.BlockSpec((1,) + s.shape[1:], lambda i, nd=len(s.shape): (i,) + (0,) * (nd - 1))
                   for s in out_shapes],
        out_shape=out_shapes,
        compiler_params=_params(("parallel",)),
        name="in_proj",
    )(x2, pos2, *consts)


def _dsa_kernel(qq_ref, iw_ref, ka_ref, vt_ref, ki_ref, o_ref, score_ref, logit_ref, *, topk):
    i = pl.program_id(1)
    tq = qq_ref.shape[1]
    heads = DSA_HEADS
    nch = (i * tq + tq + KEY_CHUNK - 1) // KEY_CHUNK
    npair = (nch + 1) // 2
    nfull = nch // 2
    odd = nch % 2 == 1
    kf = float(topk)
    shape = (KEY_CHUNK, tq)
    folded = (SUBLANES, tq)

    def fold(a, op):
        return op(a.reshape(KEY_CHUNK // SUBLANES, SUBLANES, tq), axis=0)

    krow = lax.broadcasted_iota(jnp.int32, shape, 0)
    qcol = i * tq + lax.broadcasted_iota(jnp.int32, shape, 1)
    contract_last = (((1,), (1,)), ((), ()))

    def pair_loop(body, init):
        return lax.fori_loop(0, npair, lambda j, carry: body(2 * j, 2 * j + 1, carry), init)

    qq = jnp.concatenate([qq_ref[0, :, hd * LANES:(hd + 1) * LANES] for hd in range(heads)], axis=0)
    iw = iw_ref[0]

    def score_chunk(c, mx, mn):
        kc = ki_ref[0, pl.ds(pl.multiple_of(c * KEY_CHUNK, KEY_CHUNK), KEY_CHUNK), :]
        sc = None
        for hp in range(IDX_HEADS // 2):
            rel = lax.dot_general(kc, qq[2 * hp * tq:(2 * hp + 2) * tq], contract_last,
                                  preferred_element_type=F32)
            rel = jnp.maximum(rel, 0.0)
            part = rel[:, :tq] * iw[2 * hp:2 * hp + 1, :] + rel[:, tq:] * iw[2 * hp + 1:2 * hp + 2, :]
            sc = part if sc is None else sc + part
        causal = (krow + c * KEY_CHUNK) <= qcol
        masked = jnp.where(causal, sc, -jnp.inf)
        score_ref[c] = masked
        return (jnp.maximum(mx, fold(masked, jnp.max)),
                jnp.minimum(mn, fold(jnp.where(causal, sc, jnp.inf), jnp.min)))

    def score_body(j, carry):
        return score_chunk(2 * j + 1, *score_chunk(2 * j, *carry))

    def score_tail(carry):
        score_ref[nch] = jnp.full(shape, -jnp.inf, F32)
        return score_chunk(nch - 1, *carry)

    carry = lax.fori_loop(0, nfull, score_body, (jnp.full(folded, -jnp.inf, F32), jnp.full(folded, jnp.inf, F32)))
    mx, mn = lax.cond(odd, score_tail, lambda cr: cr, carry)
    hi = jnp.max(mx, axis=0, keepdims=True)
    lo = jnp.min(mn, axis=0, keepdims=True)

    def count_gt(t):
        def body(c0, c1, acc):
            for c in (c0, c1):
                acc = acc + fold(jnp.where(score_ref[c] > t, 1.0, 0.0), jnp.sum)
            return acc
        return jnp.sum(pair_loop(body, jnp.zeros(folded, F32)), axis=0, keepdims=True)

    def bisect(_, carry):
        lo, hi = carry
        mid = 0.5 * (lo + hi)
        below = count_gt(mid) < kf
        return jnp.where(below, lo, mid), jnp.where(below, mid, hi)

    lo, hi = lax.fori_loop(0, BISECT_ITERS, bisect, (lo, hi))

    def max_le(t):
        def body(c0, c1, acc):
            for c in (c0, c1):
                s = score_ref[c]
                acc = jnp.maximum(acc, fold(jnp.where(s <= t, s, -jnp.inf), jnp.max))
            return acc
        return jnp.max(pair_loop(body, jnp.full(folded, -jnp.inf, F32)), axis=0, keepdims=True)

    n_causal = (i * tq + lax.broadcasted_iota(jnp.int32, (1, tq), 1) + 1).astype(F32)
    small = n_causal <= kf

    def refine_cond(carry):
        return jnp.min(carry[2]) < 0.5

    def refine_body(carry):
        m, thr, done, thr_cnt = carry

        def body(c0, c1, acc):
            cnt, nxt = acc
            for c in (c0, c1):
                s = score_ref[c]
                cnt = cnt + fold(jnp.where(s >= m, 1.0, 0.0), jnp.sum)
                nxt = jnp.maximum(nxt, fold(jnp.where(s < m, s, -jnp.inf), jnp.max))
            return cnt, nxt

        cnt, nxt = pair_loop(body, (jnp.zeros(folded, F32), jnp.full(folded, -jnp.inf, F32)))
        cnt = jnp.sum(cnt, axis=0, keepdims=True)
        nxt = jnp.max(nxt, axis=0, keepdims=True)
        reached = cnt >= kf
        hit = jnp.logical_and(reached, done < 0.5)
        return nxt, jnp.where(hit, m, thr), jnp.where(reached, 1.0, done), jnp.where(hit, cnt, thr_cnt)

    init = (max_le(hi), jnp.full((1, tq), -jnp.inf, F32), jnp.where(small, 1.0, 0.0), jnp.zeros((1, tq), F32))
    _, thr, _, thr_cnt = lax.while_loop(refine_cond, refine_body, init)

    def mask_exact_k():
        def body(c0, c1, carry):
            for c in (c0, c1):
                s = score_ref[c]
                sel = jnp.logical_and(s >= thr, s > -jnp.inf)
                score_ref[c] = jnp.where(sel, 0.0, NEG_BIG)
            return carry
        pair_loop(body, 0)

    def mask_with_ties():
        quota = jnp.where(small, 0.0, kf - count_gt(thr))
        earlier = (lax.broadcasted_iota(jnp.int32, (KEY_CHUNK, KEY_CHUNK), 1)
                   < lax.broadcasted_iota(jnp.int32, (KEY_CHUNK, KEY_CHUNK), 0)).astype(BF16)

        def body(c0, c1, seen):
            for c in (c0, c1):
                s = score_ref[c]
                eq = s == thr
                eqf = jnp.where(eq, 1.0, 0.0)
                before = jnp.dot(earlier, eqf.astype(BF16), preferred_element_type=F32) + seen
                sel = jnp.logical_or(s > thr, jnp.logical_and(eq, before < quota))
                seen = seen + jnp.sum(eqf, axis=0, keepdims=True)
                score_ref[c] = jnp.where(sel, 0.0, NEG_BIG)
            return seen
        pair_loop(body, jnp.zeros((1, tq), F32))

    surplus = jnp.max(jnp.where(small, 0.0, thr_cnt - kf)) > 0.5
    lax.cond(surplus, mask_with_ties, mask_exact_k)

    def qk_into(slot, c):
        kc = ka_ref[0, pl.ds(pl.multiple_of(c * KEY_CHUNK, KEY_CHUNK), KEY_CHUNK), :]
        logit_ref[slot] = lax.dot_general(kc, qq, contract_last, preferred_element_type=F32)

    def softmax_pv(slot, c, m_run, acc):
        bias = score_ref[c]
        vt = vt_ref[0, c]
        new_m, new_acc = [], []
        for hp in range(heads // 2):
            probs, alphas = [], []
            for hd in (2 * hp, 2 * hp + 1):
                sl = slice(hd * tq, (hd + 1) * tq)
                lh = logit_ref[slot, :, sl] + bias
                m_old = m_run[:, sl]
                m_new = jnp.maximum(m_old, jnp.max(lh, axis=0, keepdims=True))
                probs.append(jnp.exp2(lh - m_new).astype(BF16))
                alphas.append(jnp.exp2(m_old - m_new))
                new_m.append(m_new)
            pv = jnp.dot(vt, jnp.concatenate(probs, axis=1), preferred_element_type=F32)
            new_acc.append(jnp.concatenate(alphas, axis=1) * acc[:, 2 * hp * tq:(2 * hp + 2) * tq] + pv)
        return jnp.concatenate(new_m, axis=1), jnp.concatenate(new_acc, axis=1)

    def attn_body(j, carry):
        m_run, acc = carry
        qk_into(1, 2 * j + 1)
        m_run, acc = softmax_pv(0, 2 * j, m_run, acc)
        qk_into(0, jnp.minimum(2 * j + 2, nch - 1))
        return softmax_pv(1, 2 * j + 1, m_run, acc)

    qk_into(0, 0)
    d_rows = vt_ref.shape[2]
    init = (jnp.full((1, heads * tq), NEG_BIG, F32), jnp.zeros((d_rows, heads * tq), F32))
    carry = lax.fori_loop(0, nfull, attn_body, init)
    _, acc = lax.cond(odd, lambda cr: softmax_pv(0, nch - 1, *cr), lambda cr: cr, carry)
    pad_rows = jnp.zeros((LANES - d_rows, tq), F32)
    for hd in range(heads):
        blk = acc[:, hd * tq:(hd + 1) * tq]
        out_t = jnp.concatenate([blk / blk[DSA_HEAD_DIM:DSA_HEAD_DIM + 1, :], pad_rows], axis=0)
        o_ref[0, hd] = out_t.T.astype(o_ref.dtype)


def _dsa_attention(qq, iw_t, ka, va_t, ki, topk):
    b, s, wide = qq.shape
    tq = Q_TILE_DSA
    nkc = s // KEY_CHUNK
    assert iw_t.shape[-1] == tq and va_t.shape[-1] == KEY_CHUNK
    v_t = va_t.reshape(b, nkc, DSA_V_ROWS, KEY_CHUNK)
    qspec = pl.BlockSpec((1, tq, wide), lambda bi, i: (bi, i, 0))
    kspec = pl.BlockSpec((1, s, LANES), lambda bi, i: (bi, 0, 0))
    return pl.pallas_call(
        functools.partial(_dsa_kernel, topk=topk),
        grid=(b, s // tq),
        in_specs=[qspec, pl.BlockSpec((1, IDX_HEADS, tq), lambda bi, i: (bi * (s // tq) + i, 0, 0)), kspec,
                  pl.BlockSpec((1, nkc, DSA_V_ROWS, KEY_CHUNK), lambda bi, i: (bi, 0, 0, 0)), kspec],
        out_specs=pl.BlockSpec((1, DSA_HEADS, tq, LANES), lambda bi, i: (bi, 0, i, 0)),
        out_shape=jax.ShapeDtypeStruct((b, DSA_HEADS, s, LANES), BF16),
        scratch_shapes=[pltpu.VMEM((nkc, KEY_CHUNK, tq), F32),
                        pltpu.VMEM((2, KEY_CHUNK, DSA_HEADS * tq), F32)],
        compiler_params=_params(("parallel", "arbitrary")),
        name="dsa_attn",
    )(qq, iw_t, ka, v_t, ki)


def _mla_kernel(q_ref, k_ref, vt_ref, o_ref, logit_ref):
    i = pl.program_id(2)
    tq = q_ref.shape[1]
    group = q_ref.shape[2] // LANES
    d_rows = vt_ref.shape[3]
    per_block = tq // vt_ref.shape[4]
    qs = [q_ref[0, :, g * LANES:(g + 1) * LANES] for g in range(group)]
    contract_last = (((1,), (1,)), ((), ()))

    def qk_into(slot, c):
        k0 = pl.multiple_of(c * tq, tq)
        for g in range(group):
            logit_ref[slot, g] = lax.dot_general(k_ref[0, pl.ds(k0, tq), g * LANES:(g + 1) * LANES], qs[g],
                                                 contract_last, preferred_element_type=F32)

    def softmax_pv(slot, c, carry, masked):
        if masked:
            keep = (lax.broadcasted_iota(jnp.int32, (tq, tq), 0) <= lax.broadcasted_iota(jnp.int32, (tq, tq), 1))
            bias = jnp.where(keep, 0.0, NEG_BIG)
        probs, alphas, maxes = [], [], []
        for g in range(group):
            m_run = carry[2 * g]
            logit = logit_ref[slot, g]
            if masked:
                logit = logit + bias
            m_new = jnp.maximum(m_run, jnp.max(logit, axis=0, keepdims=True))
            probs.append(jnp.exp2(logit - m_new).astype(BF16))
            alphas.append(jnp.exp2(m_run - m_new))
            maxes.append(m_new)
        res = []
        for g in range(group):
            vt = jnp.concatenate([vt_ref[0, c * per_block + t, g] for t in range(per_block)], axis=1)
            pv = jnp.dot(vt, probs[g], preferred_element_type=F32)
            res += [maxes[g], alphas[g] * carry[2 * g + 1] + pv]
        return tuple(res)

    def pair_body(j, carry):
        qk_into(1, 2 * j + 1)
        carry = softmax_pv(0, 2 * j, carry, False)
        qk_into(0, 2 * j + 2)
        return softmax_pv(1, 2 * j + 1, carry, False)

    def tail_even(carry):
        return softmax_pv(0, i, carry, True)

    def tail_odd(carry):
        qk_into(1, i)
        carry = softmax_pv(0, i - 1, carry, False)
        return softmax_pv(1, i, carry, True)

    qk_into(0, 0)
    init = (jnp.full((1, tq), NEG_BIG, F32), jnp.zeros((d_rows, tq), F32)) * group
    carry = lax.fori_loop(0, i // 2, pair_body, init)
    carry = lax.cond(i % 2 == 0, tail_even, tail_odd, carry)
    pad_rows = jnp.zeros((LANES - d_rows, tq), F32)
    for g in range(group):
        acc = carry[2 * g + 1]
        out_t = jnp.concatenate([acc / acc[MLA_V:MLA_V + 1, :], pad_rows], axis=0)
        o_ref[0, :, g * LANES:(g + 1) * LANES] = out_t.T.astype(o_ref.dtype)


def _mla_attention(mq, mk, mv_t):
    b, s, wide = mq.shape
    tq = Q_TILE_MLA
    nkc = s // tq
    gw = MLA_HEAD_GROUP * LANES
    nt, _, _, tm = mv_t.shape
    nt //= b
    assert tq % tm == 0
    v_t = mv_t.reshape(b, nt, MLA_HEADS, MLA_V_ROWS, tm)
    return pl.pallas_call(
        _mla_kernel,
        grid=(b, MLA_HEADS // MLA_HEAD_GROUP, s // tq),
        in_specs=[pl.BlockSpec((1, tq, gw), lambda bi, h, i: (bi, i, h)),
                  pl.BlockSpec((1, s, gw), lambda bi, h, i: (bi, 0, h)),
                  pl.BlockSpec((1, nt, MLA_HEAD_GROUP, MLA_V_ROWS, tm), lambda bi, h, i: (bi, 0, h, 0, 0))],
        out_specs=pl.BlockSpec((1, tq, gw), lambda bi, h, i: (bi, i, h)),
        out_shape=jax.ShapeDtypeStruct((b, s, wide), BF16),
        scratch_shapes=[pltpu.VMEM((2, MLA_HEAD_GROUP, tq, tq), F32)],
        compiler_params=_params(("parallel", "parallel", "arbitrary")),
        name="mla_attn",
    )(mq, mk, v_t)


def _mem_kv_kernel(mem_ref, g_ref, wkv_ref, kg_ref, k_o, v_o):
    m = mem_ref[0]
    hm = _rms(m, g_ref[...], m.shape[-1]).astype(BF16)
    kv = jnp.dot(hm, wkv_ref[...], preferred_element_type=F32)
    for hd in range(MEM_HEADS):
        sl = slice(hd * LANES, (hd + 1) * LANES)
        k_o[0, :, sl] = _rms(kv[:, sl], kg_ref[...], MEM_HEAD_DIM).astype(BF16)
    v_o[0] = kv[:, MEM_HEADS * LANES:].astype(BF16)


def _mem_kv(mem, mem_g, w_kv, k_g):
    b, m, d = mem.shape
    hw = MEM_HEADS * MEM_HEAD_DIM
    wkv = jnp.concatenate([_pad_heads(w_kv[:, :hw], MEM_HEADS, MEM_HEAD_DIM),
                           _pad_heads(w_kv[:, hw:], MEM_HEADS, MEM_HEAD_DIM)], axis=1).astype(BF16)
    consts = [mem_g.reshape(1, d), wkv, _pad_cols(k_g.reshape(1, -1))]
    wide = MEM_HEADS * LANES
    spec = pl.BlockSpec((1, m, wide), lambda bi: (bi, 0, 0))
    return pl.pallas_call(
        _mem_kv_kernel,
        grid=(b,),
        in_specs=[pl.BlockSpec((1, m, d), lambda bi: (bi, 0, 0))] + [_const_spec(c.shape) for c in consts],
        out_specs=[spec, spec],
        out_shape=[jax.ShapeDtypeStruct((b, m, wide), BF16)] * 2,
        compiler_params=_params(("parallel",)),
        name="mem_kv",
    )(mem, *consts)


def _split_dot(a, w):
    a_hi = a.astype(BF16)
    a_lo = (a - a_hi.astype(F32)).astype(BF16)
    w_hi = w.astype(BF16)
    w_lo = (w - w_hi.astype(F32)).astype(BF16)
    w_both = jnp.concatenate([w_hi, w_lo], axis=1)
    n = w.shape[1]
    hi = jnp.dot(a_hi, w_both, preferred_element_type=F32)
    lo = jnp.dot(a_lo, w_both, preferred_element_type=F32)
    return (hi[:, :n] + hi[:, n:]) + (lo[:, :n] + lo[:, n:])


def _merge_kernel(oa_ref, ob_ref, sgd_ref, sgm_ref, x_ref, wa_ref, wb_ref, wo_ref, mxg_ref, wq_ref, qg_ref,
                  km_ref, vm_ref, wmo_ref, moeg_ref, wr_ref, bias_ref, x2_o, h3_o, comb_o):
    d = functools.partial(jnp.dot, preferred_element_type=F32)
    oa = jnp.concatenate([oa_ref[0, hd] for hd in range(DSA_HEADS)], axis=-1)
    merged = (sgd_ref[0].astype(F32) * d(oa, wa_ref[...]) + sgm_ref[0].astype(F32) * d(ob_ref[0], wb_ref[...]))
    x1 = x_ref[0] + d(merged.astype(BF16), wo_ref[...])

    d_model = x1.shape[-1]
    h2 = _rms(x1, mxg_ref[...], d_model).astype(BF16)
    qm = d(h2, wq_ref[...])
    outs = []
    for hd in range(MEM_HEADS):
        sl = slice(hd * LANES, (hd + 1) * LANES)
        q = (_rms(qm[:, sl], qg_ref[...], MEM_HEAD_DIM) * MEM_HEAD_DIM ** -0.5).astype(BF16)
        logit = lax.dot_general(q, km_ref[0, :, sl], (((1,), (1,)), ((), ())), preferred_element_type=F32)
        logit = logit - jnp.max(logit, axis=-1, keepdims=True)
        p = jnp.exp(logit)
        pv = d(p.astype(BF16), vm_ref[0, :, sl])
        outs.append((pv / jnp.sum(p, axis=-1, keepdims=True)).astype(BF16))
    x2 = x1 + d(jnp.concatenate(outs, axis=-1), wmo_ref[...])
    x2_o[0] = x2

    h3 = _rms(x2, moeg_ref[...], d_model)
    h3_o[0] = h3.astype(BF16)
    logits = _split_dot(h3, wr_ref[...])
    tm = logits.shape[0]
    lane = lax.broadcasted_iota(jnp.int32, (tm, LANES), 1)
    big = jnp.int32(LANES)
    is_grp = jnp.logical_and(lane >= N_EXPERTS, lane < N_EXPERTS + N_GROUPS)
    glog = jnp.where(is_grp, logits, -jnp.inf)
    gmax = jnp.max(glog, axis=-1, keepdims=True)
    g_sel = jnp.min(jnp.where(glog == gmax, lane, big), axis=-1, keepdims=True) - N_EXPERTS
    p_g = 1.0 / jnp.sum(jnp.where(is_grp, jnp.exp(logits - gmax), 0.0), axis=-1, keepdims=True)
    aff = jax.nn.sigmoid(logits)
    in_grp = jnp.logical_and(lane >= g_sel * EXPERTS_PER_GROUP, lane < (g_sel + 1) * EXPERTS_PER_GROUP)
    val = jnp.where(in_grp, aff + bias_ref[...], -jnp.inf)
    m1 = jnp.max(val, axis=-1, keepdims=True)
    i1 = jnp.min(jnp.where(val == m1, lane, big), axis=-1, keepdims=True)
    val2 = jnp.where(lane == i1, -jnp.inf, val)
    m2 = jnp.max(val2, axis=-1, keepdims=True)
    i2 = jnp.min(jnp.where(val2 == m2, lane, big), axis=-1, keepdims=True)
    chosen = jnp.logical_or(lane == i1, lane == i2)
    a_sel = jnp.where(chosen, aff, 0.0)
    comb_o[0] = p_g * a_sel / jnp.sum(a_sel, axis=-1, keepdims=True)


def _merge_mem_router(oa, ob, sgd, sgm, x, w_br_dsa, w_br_mla, w_out, mem_x_g, mem_w_q, mem_q_g, km, vm,
                      mem_w_o, moe_g, w_group, w_expert, expert_bias, tm):
    b, s, d = x.shape
    wa = jnp.pad(w_br_dsa.reshape(DSA_HEADS, DSA_HEAD_DIM, d), ((0, 0), (0, LANES - DSA_HEAD_DIM), (0, 0)))
    wa = wa.reshape(DSA_HEADS * LANES, d).astype(BF16)
    wb = jnp.pad(w_br_mla.reshape(MLA_HEADS, MLA_V, d), ((0, 0), (0, LANES - MLA_V), (0, 0)))
    wb = wb.reshape(MLA_HEADS * LANES, d).astype(BF16)
    wq = _pad_heads(mem_w_q, MEM_HEADS, MEM_HEAD_DIM).astype(BF16)
    wmo = jnp.pad(mem_w_o.reshape(MEM_HEADS, MEM_HEAD_DIM, d), ((0, 0), (0, LANES - MEM_HEAD_DIM), (0, 0)))
    wmo = wmo.reshape(MEM_HEADS * LANES, d).astype(BF16)
    wr = _pad_cols(jnp.concatenate([w_expert, w_group], axis=1).astype(F32))
    bias = _pad_cols(expert_bias.reshape(1, -1).astype(F32))
    consts_a = [wa, wb, w_out.astype(BF16), mem_x_g.reshape(1, d), wq, _pad_cols(mem_q_g.reshape(1, -1))]
    consts_b = [wmo, moe_g.reshape(1, d), wr, bias]
    m = km.shape[1]
    mw = MEM_HEADS * LANES
    tok = lambda w: pl.BlockSpec((1, tm, w), lambda bi, i: (bi, i, 0))
    memspec = pl.BlockSpec((1, m, mw), lambda bi, i: (bi, 0, 0))
    return pl.pallas_call(
        _merge_kernel,
        grid=(b, s // tm),
        in_specs=[pl.BlockSpec((1, DSA_HEADS, tm, LANES), lambda bi, i: (bi, 0, i, 0)),
                  tok(MLA_HEADS * LANES), tok(d), tok(d), tok(d)]
                 + [_const_spec(c.shape) for c in consts_a] + [memspec, memspec]
                 + [_const_spec(c.shape) for c in consts_b],
        out_specs=[tok(d), tok(d), tok(LANES)],
        out_shape=[jax.ShapeDtypeStruct((b, s, d), F32), jax.ShapeDtypeStruct((b, s, d), BF16),
                   jax.ShapeDtypeStruct((b, s, LANES), F32)],
        compiler_params=_params(("parallel", "parallel")),
        name="merge_mem",
    )(oa, ob, sgd, sgm, x, *consts_a, km, vm, *consts_b)


def _moe_kernel(h_ref, comb_ref, x_ref, wg_ref, wu_ref, wd_ref, o_ref):
    g = pl.program_id(1)
    h = h_ref[...]
    comb = comb_ref[...]
    lane = lax.broadcasted_iota(jnp.int32, comb.shape, 1)
    parts = []
    for e in range(EXPERTS_PER_GROUP):
        gate = jnp.dot(h, wg_ref[e], preferred_element_type=F32)
        up = jnp.dot(h, wu_ref[e], preferred_element_type=F32)
        c_e = jnp.sum(jnp.where(lane == g * EXPERTS_PER_GROUP + e, comb, 0.0), axis=-1, keepdims=True)
        parts.append(((jax.nn.silu(gate) * up) * c_e).astype(BF16))
    w_down = wd_ref[...].reshape(EXPERTS_PER_GROUP * D_EXPERT, wd_ref.shape[-1])
    contrib = jnp.dot(jnp.concatenate(parts, axis=1), w_down, preferred_element_type=F32)

    @pl.when(g == 0)
    def _():
        o_ref[...] = x_ref[...] + contrib

    @pl.when(g > 0)
    def _():
        o_ref[...] += contrib


def _moe(h3, comb, x2, w_gate, w_up, w_down, tm):
    n, d = x2.shape
    per = EXPERTS_PER_GROUP
    return pl.pallas_call(
        _moe_kernel,
        grid=(n // tm, N_GROUPS),
        in_specs=[pl.BlockSpec((tm, d), lambda i, g: (i, 0)),
                  pl.BlockSpec((tm, LANES), lambda i, g: (i, 0)),
                  pl.BlockSpec((tm, d), lambda i, g: (i, 0)),
                  pl.BlockSpec((per, d, D_EXPERT), lambda i, g: (g, 0, 0)),
                  pl.BlockSpec((per, d, D_EXPERT), lambda i, g: (g, 0, 0)),
                  pl.BlockSpec((per, D_EXPERT, d), lambda i, g: (g, 0, 0))],
        out_specs=pl.BlockSpec((tm, d), lambda i, g: (i, 0)),
        out_shape=jax.ShapeDtypeStruct((n, d), F32),
        compiler_params=_params(("parallel", "arbitrary")),
        name="moe",
    )(h3, comb, x2, w_gate.astype(BF16), w_up.astype(BF16), w_down.astype(BF16))


def _layer(x, mem, positions, p):
    b, s, d = x.shape
    n = b * s
    assert s % (2 * KEY_CHUNK) == 0 and s % Q_TILE_MLA == 0
    topk = min(TOPK_MAX, s // 4)
    tm = min(TOKENS_PER_STEP_PROJ, s)
    qq, ka, va, ki, iw, mq, mk, mv, sgd, sgm = _in_proj(
        x.reshape(n, d), positions.reshape(n, 1), p["attn_norm_g"], p["w_in"], p["dsa_q_norm_g"],
        p["dsa_k_norm_g"], p["mla_cq_norm_g"], p["mla_ckv_norm_g"], p["mla_w_uq"], p["mla_w_ukv"],
        p["mla_q_norm_g"], p["mla_k_norm_g"], tm)
    r3 = lambda a: a.reshape(b, s, a.shape[-1])
    oa = _dsa_attention(r3(qq), iw, r3(ka), va, r3(ki), topk)
    ob = _mla_attention(r3(mq), r3(mk), mv)
    km, vm = _mem_kv(mem, p["mem_norm_g"], p["mem_w_kv"], p["mem_k_norm_g"])
    x2, h3, comb = _merge_mem_router(
        oa, ob, r3(sgd), r3(sgm), x, p["w_branch_dsa"], p["w_branch_mla"], p["w_out"], p["mem_x_norm_g"],
        p["mem_w_q"], p["mem_q_norm_g"], km, vm, p["mem_w_o"], p["moe_norm_g"], p["moe_w_group"],
        p["moe_w_expert"], p["moe_expert_bias"], min(TOKENS_PER_STEP_MERGE, s))
    out = _moe(h3.reshape(n, d), comb.reshape(n, LANES), x2.reshape(n, d), p["moe_w_gate"], p["moe_w_up"],
               p["moe_w_down"], min(TOKENS_PER_STEP_MOE, n))
    return out.reshape(b, s, d)


_PARAM_NAMES = ("attn_norm_g", "w_in", "dsa_q_norm_g", "dsa_k_norm_g", "mla_cq_norm_g", "mla_ckv_norm_g",
                "mla_w_uq", "mla_w_ukv", "mla_q_norm_g", "mla_k_norm_g", "w_branch_dsa", "w_branch_mla", "w_out",
                "mem_x_norm_g", "mem_norm_g", "mem_w_q", "mem_w_kv", "mem_q_norm_g", "mem_k_norm_g", "mem_w_o",
                "moe_norm_g", "moe_w_group", "moe_w_expert", "moe_expert_bias", "moe_w_gate", "moe_w_up",
                "moe_w_down")


def kernel(x, mem, positions, attn_norm_g, w_in, dsa_q_norm_g, dsa_k_norm_g, mla_cq_norm_g, mla_ckv_norm_g, mla_w_uq, mla_w_ukv, mla_q_norm_g, mla_k_norm_g, w_branch_dsa, w_branch_mla, w_out, mem_x_norm_g, mem_norm_g, mem_w_q, mem_w_kv, mem_q_norm_g, mem_k_norm_g, mem_w_o, moe_norm_g, moe_w_group, moe_w_expert, moe_expert_bias, moe_w_gate, moe_w_up, moe_w_down):
    stacked = (attn_norm_g, w_in, dsa_q_norm_g, dsa_k_norm_g, mla_cq_norm_g, mla_ckv_norm_g, mla_w_uq, mla_w_ukv,
               mla_q_norm_g, mla_k_norm_g, w_branch_dsa, w_branch_mla, w_out, mem_x_norm_g, mem_norm_g, mem_w_q,
               mem_w_kv, mem_q_norm_g, mem_k_norm_g, mem_w_o, moe_norm_g, moe_w_group, moe_w_expert,
               moe_expert_bias, moe_w_gate, moe_w_up, moe_w_down)
    for layer in range(attn_norm_g.shape[0]):
        p = {name: arr[layer] for name, arr in zip(_PARAM_NAMES, stacked)}
        x = _layer(x, mem, positions, p)
    return x
```

```python
import functools

import jax
import jax.numpy as jnp
from jax import lax
from jax.experimental import pallas as pl
from jax.experimental.pallas import tpu as pltpu

F32 = jnp.float32
BF16 = jnp.bfloat16

LANES = 128
SUBLANES = 8
VMEM_LIMIT = 56 * 1024 * 1024

ROPE_THETA = 10000.0
EPS = 1e-6
DSA_HEADS = 8
DSA_HEAD_DIM = 64
IDX_HEADS = 8
IDX_DIM = 64
TOPK_MAX = 256
MLA_HEADS = 8
MLA_NOPE = 64
MLA_ROPE = 32
MLA_QK = MLA_NOPE + MLA_ROPE
MLA_V = 64
MLA_Q_RANK = 256
MLA_KV_RANK = 128
MEM_HEADS = 4
MEM_HEAD_DIM = 64
N_GROUPS = 4
EXPERTS_PER_GROUP = 4
N_EXPERTS = N_GROUPS * EXPERTS_PER_GROUP
D_EXPERT = 256

NEG_BIG = -1e30
KEY_CHUNK = 256
Q_TILE_DSA = 256
DSA_V_ROWS = 80
MLA_V_ROWS = 80
Q_TILE_MLA = 512
MLA_HEAD_GROUP = 4
LOG2_E = 1.4426950408889634
BISECT_ITERS = 15
TOKENS_PER_STEP_PROJ = 256
TOKENS_PER_STEP_MERGE = 512
TOKENS_PER_STEP_MOE = 1024


def _const_spec(shape):
    nd = len(shape)
    return pl.BlockSpec(shape, lambda *_: (0,) * nd)


def _params(sem):
    return pltpu.CompilerParams(dimension_semantics=sem, vmem_limit_bytes=VMEM_LIMIT)


def _rms(x, g, n):
    ms = jnp.sum(x * x, axis=-1, keepdims=True) * (1.0 / n)
    return (x * lax.rsqrt(ms + EPS)) * g


def _rope(x, cos, sin_signed, lo_mask, half):
    fwd = pltpu.roll(x, LANES - half, 1)
    bwd = pltpu.roll(x, half, 1)
    return x * cos + jnp.where(lo_mask, fwd, bwd) * sin_signed


def _in_proj_kernel(x_ref, pos_ref, g_ref, wqq_ref, wsm_ref, wcq_ref, wg_ref, wuq_ref, wuk_ref,
                    wuv_ref, qag_ref, kag_ref, cqg_ref, ckvg_ref, mqg_ref, mkg_ref, freq_ref,
                    qq_o, ka_o, va_o, ki_o, iw_o, mq_o, mk_o, mv_o, sgd_o, sgm_o):
    x = x_ref[...]
    d_model = x.shape[-1]
    h = _rms(x, g_ref[...], d_model).astype(BF16)

    pos = pos_ref[...].astype(F32)
    lane = lax.broadcasted_iota(jnp.int32, (x.shape[0], LANES), 1)
    half_d, half_m = DSA_HEAD_DIM // 2, MLA_ROPE // 2
    ang = pos * freq_ref[...]
    lo_d = (lane & (DSA_HEAD_DIM - 1)) < half_d
    lo_m = lane < MLA_NOPE + half_m
    in_m_lo = jnp.logical_and(lane >= MLA_NOPE, lo_m)
    in_m_hi = jnp.logical_and(lane >= MLA_NOPE + half_m, lane < MLA_QK)

    def tables(t, fill):
        by_half = pltpu.roll(t, half_d, 1)
        head = jnp.where(lane < half_d, t, by_half)
        dsa = jnp.where(lane < DSA_HEAD_DIM, head, pltpu.roll(head, DSA_HEAD_DIM, 1))
        mla = jnp.where(in_m_lo, by_half, jnp.where(in_m_hi, pltpu.roll(t, half_d + half_m, 1), fill))
        return dsa, mla

    cos_d, cos_m = tables(jnp.cos(ang), 1.0)
    sin_d, sin_m = tables(jnp.sin(ang), 0.0)
    sin_d = jnp.where(lo_d, -sin_d, sin_d)
    sin_m = jnp.where(lo_m, -sin_m, sin_m)
    rope_d = functools.partial(_rope, cos=cos_d, sin_signed=sin_d, lo_mask=lo_d, half=DSA_HEAD_DIM // 2)
    rope_m = functools.partial(_rope, cos=cos_m, sin_signed=sin_m, lo_mask=lo_m, half=MLA_ROPE // 2)

    def dot(a, w_ref):
        return jnp.dot(a, w_ref[...], preferred_element_type=F32)

    att_scale = DSA_HEAD_DIM ** -0.5 * LOG2_E
    idx_scale = IDX_DIM ** -0.5 * IDX_HEADS ** -0.5
    mla_scale = MLA_QK ** -0.5 * LOG2_E

    qq = dot(h, wqq_ref)
    is_qa = lane < DSA_HEAD_DIM
    for hd in range(DSA_HEADS):
        sl = slice(hd * LANES, (hd + 1) * LANES)
        xh = qq[:, sl]
        ms = jnp.sum(jnp.where(is_qa, xh * xh, 0.0), axis=-1, keepdims=True) * (1.0 / DSA_HEAD_DIM)
        mult = jnp.where(is_qa, (lax.rsqrt(ms + EPS) * att_scale) * qag_ref[...], 1.0)
        qq_o[:, sl] = rope_d(xh * mult).astype(BF16)

    sm = dot(h, wsm_ref)
    ka_o[...] = rope_d(_rms(sm[:, 0:LANES], kag_ref[...], DSA_HEAD_DIM)).astype(BF16)
    va = sm[:, LANES:2 * LANES] + jnp.where(lane[0:1] == DSA_HEAD_DIM, 1.0, 0.0)
    va_o[0] = va.T[:DSA_V_ROWS].astype(BF16)
    ki_o[...] = rope_d(sm[:, 2 * LANES:3 * LANES]).astype(BF16)
    iw_o[0] = (sm[:, 3 * LANES:4 * LANES] * idx_scale).T[:IDX_HEADS]
    ckv = _rms(sm[:, 4 * LANES:5 * LANES], ckvg_ref[...], MLA_KV_RANK).astype(BF16)
    kpe = sm[:, 5 * LANES:6 * LANES]

    kn = dot(ckv, wuk_ref)
    mkg = mkg_ref[...]
    pe_rot = rope_m(kpe * mkg)
    ss_pe = jnp.sum(kpe * kpe, axis=-1, keepdims=True)
    for hd in range(MLA_HEADS):
        sl = slice(hd * LANES, (hd + 1) * LANES)
        xh = kn[:, sl]
        ms = (jnp.sum(xh * xh, axis=-1, keepdims=True) + ss_pe) * (1.0 / MLA_QK)
        mk_o[:, sl] = ((xh * mkg + pe_rot) * lax.rsqrt(ms + EPS)).astype(BF16)
    wide_lane = lax.broadcasted_iota(jnp.int32, (1, MLA_HEADS * LANES), 1)
    ones_col = jnp.where(wide_lane % LANES == MLA_V, 1.0, 0.0)
    mv = dot(ckv, wuv_ref) + ones_col
    for hd in range(MLA_HEADS):
        mv_o[0, hd] = mv[:, hd * LANES:(hd + 1) * LANES].T[:MLA_V_ROWS].astype(BF16)

    cq = _rms(dot(h, wcq_ref), cqg_ref[...], MLA_Q_RANK).astype(BF16)
    qb = dot(cq, wuq_ref)
    for hd in range(MLA_HEADS):
        sl = slice(hd * LANES, (hd + 1) * LANES)
        y = rope_m(_rms(qb[:, sl], mqg_ref[...], MLA_QK))
        mq_o[:, sl] = (y * mla_scale).astype(BF16)

    gates = jax.nn.sigmoid(dot(h, wg_ref))
    sgd_o[...] = gates[:, :d_model].astype(BF16)
    sgm_o[...] = gates[:, d_model:].astype(BF16)


def _pad_heads(w, heads, dim):
    k = w.shape[0]
    w = w.reshape(k, heads, dim)
    w = jnp.pad(w, ((0, 0), (0, 0), (0, LANES - dim)))
    return w.reshape(k, heads * LANES)


def _pad_cols(w, width=LANES, offset=0):
    return jnp.pad(w, ((0, 0), (offset, width - offset - w.shape[1])))


def _in_proj(x2, pos2, attn_g, w_in, dsa_q_g, dsa_k_g, cq_g, ckv_g, w_uq, w_ukv, mq_g, mk_g, tm):
    n, d = x2.shape
    sizes = (DSA_HEADS * DSA_HEAD_DIM, DSA_HEAD_DIM, DSA_HEAD_DIM, IDX_HEADS * IDX_DIM, IDX_DIM, IDX_HEADS,
             MLA_Q_RANK, MLA_KV_RANK, MLA_ROPE, d, d)
    offs = [0]
    for s in sizes:
        offs.append(offs[-1] + s)
    seg = [w_in[:, offs[i]:offs[i + 1]] for i in range(len(sizes))]
    w_dq, w_dk, w_dv, w_iq, w_ik, w_iw, w_cq, w_ckv, w_kpe, w_gd, w_gm = seg

    assert DSA_HEADS == IDX_HEADS and DSA_HEAD_DIM + IDX_DIM == LANES
    wqq = jnp.concatenate([w_dq.reshape(d, DSA_HEADS, DSA_HEAD_DIM), w_iq.reshape(d, IDX_HEADS, IDX_DIM)], axis=-1)
    wqq = wqq.reshape(d, DSA_HEADS * LANES).astype(BF16)
    wsm = jnp.concatenate([_pad_cols(w_dk), _pad_cols(w_dv), _pad_cols(w_ik, offset=DSA_HEAD_DIM), _pad_cols(w_iw), w_ckv,
                           _pad_cols(w_kpe, offset=MLA_NOPE)], axis=1).astype(BF16)
    wcq = w_cq.astype(BF16)
    wg = jnp.concatenate([w_gd, w_gm], axis=1).astype(BF16)
    wuq = _pad_heads(w_uq, MLA_HEADS, MLA_QK).astype(BF16)
    ukv = w_ukv.reshape(MLA_KV_RANK, MLA_HEADS, MLA_NOPE + MLA_V)
    wuk = _pad_heads(ukv[:, :, :MLA_NOPE].reshape(MLA_KV_RANK, -1), MLA_HEADS, MLA_NOPE).astype(BF16)
    wuv = _pad_heads(ukv[:, :, MLA_NOPE:].reshape(MLA_KV_RANK, -1), MLA_HEADS, MLA_V).astype(BF16)

    row = lambda v: _pad_cols(v.reshape(1, -1).astype(F32), width=max(LANES, v.size))
    half_d = DSA_HEAD_DIM // 2
    inv_d = ROPE_THETA ** (-jnp.arange(half_d, dtype=F32) / half_d)
    half_m = MLA_ROPE // 2
    inv_m = ROPE_THETA ** (-jnp.arange(half_m, dtype=F32) / half_m)
    freq = _pad_cols(jnp.concatenate([inv_d, inv_m]).reshape(1, -1))

    consts = [attn_g.reshape(1, d), wqq, wsm, wcq, wg, wuq, wuk, wuv, row(dsa_q_g), row(dsa_k_g),
              row(cq_g), row(ckv_g), row(mq_g), row(mk_g), freq]
    wide = DSA_HEADS * LANES
    out_shapes = [
        jax.ShapeDtypeStruct((n, wide), BF16),
        jax.ShapeDtypeStruct((n, LANES), BF16),
        jax.ShapeDtypeStruct((n // tm, DSA_V_ROWS, tm), BF16),
        jax.ShapeDtypeStruct((n, LANES), BF16),
        jax.ShapeDtypeStruct((n // tm, IDX_HEADS, tm), F32),
        jax.ShapeDtypeStruct((n, wide), BF16),
        jax.ShapeDtypeStruct((n, wide), BF16),
        jax.ShapeDtypeStruct((n // tm, MLA_HEADS, MLA_V_ROWS, tm), BF16),
        jax.ShapeDtypeStruct((n, d), BF16),
        jax.ShapeDtypeStruct((n, d), BF16),
    ]
    tile = lambda w: pl.BlockSpec((tm, w), lambda i: (i, 0))
    return pl.pallas_call(
        _in_proj_kernel,
        grid=(n // tm,),
        in_specs=[tile(d), tile(1)] + [_const_spec(c.shape) for c in consts],
        out_specs=[tile(s.shape[1]) if len(s.shape) == 2 else
                   pl.BlockSpec((1,) + s.shape[1:], lambda i, nd=len(s.shape): (i,) + (0,) * (nd - 1))
                   for s in out_shapes],
        out_shape=out_shapes,
        compiler_params=_params(("parallel",)),
        name="in_proj",
    )(x2, pos2, *consts)


def _dsa_kernel(qq_ref, iw_ref, ka_ref, vt_ref, ki_ref, o_ref, score_ref, logit_ref, *, topk):
    i = pl.program_id(1)
    tq = qq_ref.shape[1]
    heads = DSA_HEADS
    nch = (i * tq + tq + KEY_CHUNK - 1) // KEY_CHUNK
    npair = (nch + 1) // 2
    nfull = nch // 2
    odd = nch % 2 == 1
    kf = float(topk)
    shape = (KEY_CHUNK, tq)
    folded = (SUBLANES, tq)

    def fold(a, op):
        return op(a.reshape(KEY_CHUNK // SUBLANES, SUBLANES, tq), axis=0)

    krow = lax.broadcasted_iota(jnp.int32, shape, 0)
    qcol = i * tq + lax.broadcasted_iota(jnp.int32, shape, 1)
    contract_last = (((1,), (1,)), ((), ()))

    def pair_loop(body, init):
        return lax.fori_loop(0, npair, lambda j, carry: body(2 * j, 2 * j + 1, carry), init)

    qq = jnp.concatenate([qq_ref[0, :, hd * LANES:(hd + 1) * LANES] for hd in range(heads)], axis=0)
    iw = iw_ref[0]

    def score_chunk(c, mx, mn):
        kc = ki_ref[0, pl.ds(pl.multiple_of(c * KEY_CHUNK, KEY_CHUNK), KEY_CHUNK), :]
        sc = None
        for hp in range(IDX_HEADS // 2):
            rel = lax.dot_general(kc, qq[2 * hp * tq:(2 * hp + 2) * tq], contract_last,
                                  preferred_element_type=F32)
            rel = jnp.maximum(rel, 0.0)
            part = rel[:, :tq] * iw[2 * hp:2 * hp + 1, :] + rel[:, tq:] * iw[2 * hp + 1:2 * hp + 2, :]
            sc = part if sc is None else sc + part
        causal = (krow + c * KEY_CHUNK) <= qcol
        masked = jnp.where(causal, sc, -jnp.inf)
        score_ref[c] = masked
        return (jnp.maximum(mx, fold(masked, jnp.max)),
                jnp.minimum(mn, fold(jnp.where(causal, sc, jnp.inf), jnp.min)))

    def score_body(j, carry):
        return score_chunk(2 * j + 1, *score_chunk(2 * j, *carry))

    def score_tail(carry):
        score_ref[nch] = jnp.full(shape, -jnp.inf, F32)
        return score_chunk(nch - 1, *carry)

    carry = lax.fori_loop(0, nfull, score_body, (jnp.full(folded, -jnp.inf, F32), jnp.full(folded, jnp.inf, F32)))
    mx, mn = lax.cond(odd, score_tail, lambda cr: cr, carry)
    hi = jnp.max(mx, axis=0, keepdims=True)
    lo = jnp.min(mn, axis=0, keepdims=True)

    def count_gt(t):
        def body(c0, c1, acc):
            for c in (c0, c1):
                acc = acc + fold(jnp.where(score_ref[c] > t, 1.0, 0.0), jnp.sum)
            return acc
        return jnp.sum(pair_loop(body, jnp.zeros(folded, F32)), axis=0, keepdims=True)

    def bisect(_, carry):
        lo, hi = carry
        mid = 0.5 * (lo + hi)
        below = count_gt(mid) < kf
        return jnp.where(below, lo, mid), jnp.where(below, mid, hi)

    lo, hi = lax.fori_loop(0, BISECT_ITERS, bisect, (lo, hi))

    def max_le(t):
        def body(c0, c1, acc):
            for c in (c0, c1):
                s = score_ref[c]
                acc = jnp.maximum(acc, fold(jnp.where(s <= t, s, -jnp.inf), jnp.max))
            return acc
        return jnp.max(pair_loop(body, jnp.full(folded, -jnp.inf, F32)), axis=0, keepdims=True)

    n_causal = (i * tq + lax.broadcasted_iota(jnp.int32, (1, tq), 1) + 1).astype(F32)
    small = n_causal <= kf

    def refine_cond(carry):
        return jnp.min(carry[2]) < 0.5

    def refine_body(carry):
        m, thr, done, thr_cnt = carry

        def body(c0, c1, acc):
            cnt, nxt = acc
            for c in (c0, c1):
                s = score_ref[c]
                cnt = cnt + fold(jnp.where(s >= m, 1.0, 0.0), jnp.sum)
                nxt = jnp.maximum(nxt, fold(jnp.where(s < m, s, -jnp.inf), jnp.max))
            return cnt, nxt

        cnt, nxt = pair_loop(body, (jnp.zeros(folded, F32), jnp.full(folded, -jnp.inf, F32)))
        cnt = jnp.sum(cnt, axis=0, keepdims=True)
        nxt = jnp.max(nxt, axis=0, keepdims=True)
        reached = cnt >= kf
        hit = jnp.logical_and(reached, done < 0.5)
        return nxt, jnp.where(hit, m, thr), jnp.where(reached, 1.0, done), jnp.where(hit, cnt, thr_cnt)

    init = (max_le(hi), jnp.full((1, tq), -jnp.inf, F32), jnp.where(small, 1.0, 0.0), jnp.zeros((1, tq), F32))
    _, thr, _, thr_cnt = lax.while_loop(refine_cond, refine_body, init)

    def mask_with_ties():
        quota = jnp.where(small, 0.0, kf - count_gt(thr))
        earlier = (lax.broadcasted_iota(jnp.int32, (KEY_CHUNK, KEY_CHUNK), 1)
                   < lax.broadcasted_iota(jnp.int32, (KEY_CHUNK, KEY_CHUNK), 0)).astype(BF16)

        def body(c0, c1, seen):
            for c in (c0, c1):
                s = score_ref[c]
                eq = s == thr
                eqf = jnp.where(eq, 1.0, 0.0)
                before = jnp.dot(earlier, eqf.astype(BF16), preferred_element_type=F32) + seen
                sel = jnp.logical_or(s > thr, jnp.logical_and(eq, before < quota))
                seen = seen + jnp.sum(eqf, axis=0, keepdims=True)
                score_ref[c] = jnp.where(sel, jnp.inf, -jnp.inf)
            return seen
        pair_loop(body, jnp.zeros((1, tq), F32))

    surplus = jnp.max(jnp.where(small, 0.0, thr_cnt - kf)) > 0.5
    lax.cond(surplus, mask_with_ties, lambda: None)

    def qk_into(slot, c):
        kc = ka_ref[0, pl.ds(pl.multiple_of(c * KEY_CHUNK, KEY_CHUNK), KEY_CHUNK), :]
        logit_ref[slot] = lax.dot_general(kc, qq, contract_last, preferred_element_type=F32)

    def softmax_pv(slot, c, m_run, acc):
        s = score_ref[c]
        bias = jnp.where(jnp.logical_and(s >= thr, s > -jnp.inf), 0.0, NEG_BIG)
        vt = vt_ref[0, c]
        new_m, new_acc = [], []
        for hp in range(heads // 2):
            probs, alphas = [], []
            for hd in (2 * hp, 2 * hp + 1):
                sl = slice(hd * tq, (hd + 1) * tq)
                lh = logit_ref[slot, :, sl] + bias
                m_old = m_run[:, sl]
                m_new = jnp.maximum(m_old, jnp.max(lh, axis=0, keepdims=True))
                probs.append(jnp.exp2(lh - m_new).astype(BF16))
                alphas.append(jnp.exp2(m_old - m_new))
                new_m.append(m_new)
            pv = jnp.dot(vt, jnp.concatenate(probs, axis=1), preferred_element_type=F32)
            new_acc.append(jnp.concatenate(alphas, axis=1) * acc[:, 2 * hp * tq:(2 * hp + 2) * tq] + pv)
        return jnp.concatenate(new_m, axis=1), jnp.concatenate(new_acc, axis=1)

    def attn_body(j, carry):
        m_run, acc = carry
        qk_into(1, 2 * j + 1)
        m_run, acc = softmax_pv(0, 2 * j, m_run, acc)
        qk_into(0, jnp.minimum(2 * j + 2, nch - 1))
        return softmax_pv(1, 2 * j + 1, m_run, acc)

    qk_into(0, 0)
    d_rows = vt_ref.shape[2]
    init = (jnp.full((1, heads * tq), NEG_BIG, F32), jnp.zeros((d_rows, heads * tq), F32))
    carry = lax.fori_loop(0, nfull, attn_body, init)
    _, acc = lax.cond(odd, lambda cr: softmax_pv(0, nch - 1, *cr), lambda cr: cr, carry)
    pad_rows = jnp.zeros((LANES - d_rows, tq), F32)
    for hd in range(heads):
        blk = acc[:, hd * tq:(hd + 1) * tq]
        out_t = jnp.concatenate([blk / blk[DSA_HEAD_DIM:DSA_HEAD_DIM + 1, :], pad_rows], axis=0)
        o_ref[0, hd] = out_t.T.astype(o_ref.dtype)


def _dsa_attention(qq, iw_t, ka, va_t, ki, topk):
    b, s, wide = qq.shape
    tq = Q_TILE_DSA
    nkc = s // KEY_CHUNK
    assert iw_t.shape[-1] == tq and va_t.shape[-1] == KEY_CHUNK
    v_t = va_t.reshape(b, nkc, DSA_V_ROWS, KEY_CHUNK)
    qspec = pl.BlockSpec((1, tq, wide), lambda bi, i: (bi, i, 0))
    kspec = pl.BlockSpec((1, s, LANES), lambda bi, i: (bi, 0, 0))
    return pl.pallas_call(
        functools.partial(_dsa_kernel, topk=topk),
        grid=(b, s // tq),
        in_specs=[qspec, pl.BlockSpec((1, IDX_HEADS, tq), lambda bi, i: (bi * (s // tq) + i, 0, 0)), kspec,
                  pl.BlockSpec((1, nkc, DSA_V_ROWS, KEY_CHUNK), lambda bi, i: (bi, 0, 0, 0)), kspec],
        out_specs=pl.BlockSpec((1, DSA_HEADS, tq, LANES), lambda bi, i: (bi, 0, i, 0)),
        out_shape=jax.ShapeDtypeStruct((b, DSA_HEADS, s, LANES), BF16),
        scratch_shapes=[pltpu.VMEM((nkc, KEY_CHUNK, tq), F32),
                        pltpu.VMEM((2, KEY_CHUNK, DSA_HEADS * tq), F32)],
        compiler_params=_params(("parallel", "arbitrary")),
        name="dsa_attn",
    )(qq, iw_t, ka, v_t, ki)


def _mla_kernel(q_ref, k_ref, vt_ref, o_ref, logit_ref):
    i = pl.program_id(2)
    tq = q_ref.shape[1]
    group = q_ref.shape[2] // LANES
    d_rows = vt_ref.shape[3]
    per_block = tq // vt_ref.shape[4]
    qs = [q_ref[0, :, g * LANES:(g + 1) * LANES] for g in range(group)]
    contract_last = (((1,), (1,)), ((), ()))

    def qk_into(slot, c):
        k0 = pl.multiple_of(c * tq, tq)
        for g in range(group):
            logit_ref[slot, g] = lax.dot_general(k_ref[0, pl.ds(k0, tq), g * LANES:(g + 1) * LANES], qs[g],
                                                 contract_last, preferred_element_type=F32)

    def softmax_pv(slot, c, carry, masked):
        if masked:
            keep = (lax.broadcasted_iota(jnp.int32, (tq, tq), 0) <= lax.broadcasted_iota(jnp.int32, (tq, tq), 1))
            bias = jnp.where(keep, 0.0, NEG_BIG)
        probs, alphas, maxes = [], [], []
        for g in range(group):
            m_run = carry[2 * g]
            logit = logit_ref[slot, g]
            if masked:
                logit = logit + bias
            m_new = jnp.maximum(m_run, jnp.max(logit, axis=0, keepdims=True))
            probs.append(jnp.exp2(logit - m_new).astype(BF16))
            alphas.append(jnp.exp2(m_run - m_new))
            maxes.append(m_new)
        res = []
        for g in range(group):
            vt = jnp.concatenate([vt_ref[0, c * per_block + t, g] for t in range(per_block)], axis=1)
            pv = jnp.dot(vt, probs[g], preferred_element_type=F32)
            res += [maxes[g], alphas[g] * carry[2 * g + 1] + pv]
        return tuple(res)

    def pair_body(j, carry):
        qk_into(1, 2 * j + 1)
        carry = softmax_pv(0, 2 * j, carry, False)
        qk_into(0, 2 * j + 2)
        return softmax_pv(1, 2 * j + 1, carry, False)

    def tail_even(carry):
        return softmax_pv(0, i, carry, True)

    def tail_odd(carry):
        qk_into(1, i)
        carry = softmax_pv(0, i - 1, carry, False)
        return softmax_pv(1, i, carry, True)

    qk_into(0, 0)
    init = (jnp.full((1, tq), NEG_BIG, F32), jnp.zeros((d_rows, tq), F32)) * group
    carry = lax.fori_loop(0, i // 2, pair_body, init)
    carry = lax.cond(i % 2 == 0, tail_even, tail_odd, carry)
    pad_rows = jnp.zeros((LANES - d_rows, tq), F32)
    for g in range(group):
        acc = carry[2 * g + 1]
        out_t = jnp.concatenate([acc / acc[MLA_V:MLA_V + 1, :], pad_rows], axis=0)
        o_ref[0, :, g * LANES:(g + 1) * LANES] = out_t.T.astype(o_ref.dtype)


def _mla_attention(mq, mk, mv_t):
    b, s, wide = mq.shape
    tq = Q_TILE_MLA
    nkc = s // tq
    gw = MLA_HEAD_GROUP * LANES
    nt, _, _, tm = mv_t.shape
    nt //= b
    assert tq % tm == 0
    v_t = mv_t.reshape(b, nt, MLA_HEADS, MLA_V_ROWS, tm)
    return pl.pallas_call(
        _mla_kernel,
        grid=(b, MLA_HEADS // MLA_HEAD_GROUP, s // tq),
        in_specs=[pl.BlockSpec((1, tq, gw), lambda bi, h, i: (bi, i, h)),
                  pl.BlockSpec((1, s, gw), lambda bi, h, i: (bi, 0, h)),
                  pl.BlockSpec((1, nt, MLA_HEAD_GROUP, MLA_V_ROWS, tm), lambda bi, h, i: (bi, 0, h, 0, 0))],
        out_specs=pl.BlockSpec((1, tq, gw), lambda bi, h, i: (bi, i, h)),
        out_shape=jax.ShapeDtypeStruct((b, s, wide), BF16),
        scratch_shapes=[pltpu.VMEM((2, MLA_HEAD_GROUP, tq, tq), F32)],
        compiler_params=_params(("parallel", "parallel", "arbitrary")),
        name="mla_attn",
    )(mq, mk, v_t)


def _mem_kv_kernel(mem_ref, g_ref, wkv_ref, kg_ref, k_o, v_o):
    m = mem_ref[0]
    hm = _rms(m, g_ref[...], m.shape[-1]).astype(BF16)
    kv = jnp.dot(hm, wkv_ref[...], preferred_element_type=F32)
    for hd in range(MEM_HEADS):
        sl = slice(hd * LANES, (hd + 1) * LANES)
        k_o[0, :, sl] = _rms(kv[:, sl], kg_ref[...], MEM_HEAD_DIM).astype(BF16)
    v_o[0] = kv[:, MEM_HEADS * LANES:].astype(BF16)


def _mem_kv(mem, mem_g, w_kv, k_g):
    b, m, d = mem.shape
    hw = MEM_HEADS * MEM_HEAD_DIM
    wkv = jnp.concatenate([_pad_heads(w_kv[:, :hw], MEM_HEADS, MEM_HEAD_DIM),
                           _pad_heads(w_kv[:, hw:], MEM_HEADS, MEM_HEAD_DIM)], axis=1).astype(BF16)
    consts = [mem_g.reshape(1, d), wkv, _pad_cols(k_g.reshape(1, -1))]
    wide = MEM_HEADS * LANES
    spec = pl.BlockSpec((1, m, wide), lambda bi: (bi, 0, 0))
    return pl.pallas_call(
        _mem_kv_kernel,
        grid=(b,),
        in_specs=[pl.BlockSpec((1, m, d), lambda bi: (bi, 0, 0))] + [_const_spec(c.shape) for c in consts],
        out_specs=[spec, spec],
        out_shape=[jax.ShapeDtypeStruct((b, m, wide), BF16)] * 2,
        compiler_params=_params(("parallel",)),
        name="mem_kv",
    )(mem, *consts)


def _split_dot(a, w):
    a_hi = a.astype(BF16)
    a_lo = (a - a_hi.astype(F32)).astype(BF16)
    w_hi = w.astype(BF16)
    w_lo = (w - w_hi.astype(F32)).astype(BF16)
    w_both = jnp.concatenate([w_hi, w_lo], axis=1)
    n = w.shape[1]
    hi = jnp.dot(a_hi, w_both, preferred_element_type=F32)
    lo = jnp.dot(a_lo, w_both, preferred_element_type=F32)
    return (hi[:, :n] + hi[:, n:]) + (lo[:, :n] + lo[:, n:])


def _merge_kernel(oa_ref, ob_ref, sgd_ref, sgm_ref, x_ref, wa_ref, wb_ref, wo_ref, mxg_ref, wq_ref, qg_ref,
                  km_ref, vm_ref, wmo_ref, moeg_ref, wr_ref, bias_ref, x2_o, h3_o, comb_o):
    d = functools.partial(jnp.dot, preferred_element_type=F32)
    oa = jnp.concatenate([oa_ref[0, hd] for hd in range(DSA_HEADS)], axis=-1)
    merged = (sgd_ref[0].astype(F32) * d(oa, wa_ref[...]) + sgm_ref[0].astype(F32) * d(ob_ref[0], wb_ref[...]))
    x1 = x_ref[0] + d(merged.astype(BF16), wo_ref[...])

    d_model = x1.shape[-1]
    h2 = _rms(x1, mxg_ref[...], d_model).astype(BF16)
    qm = d(h2, wq_ref[...])
    outs = []
    for hd in range(MEM_HEADS):
        sl = slice(hd * LANES, (hd + 1) * LANES)
        q = (_rms(qm[:, sl], qg_ref[...], MEM_HEAD_DIM) * MEM_HEAD_DIM ** -0.5).astype(BF16)
        logit = lax.dot_general(q, km_ref[0, :, sl], (((1,), (1,)), ((), ())), preferred_element_type=F32)
        logit = logit - jnp.max(logit, axis=-1, keepdims=True)
        p = jnp.exp(logit)
        pv = d(p.astype(BF16), vm_ref[0, :, sl])
        outs.append((pv / jnp.sum(p, axis=-1, keepdims=True)).astype(BF16))
    x2 = x1 + d(jnp.concatenate(outs, axis=-1), wmo_ref[...])
    x2_o[0] = x2

    h3 = _rms(x2, moeg_ref[...], d_model)
    h3_o[0] = h3.astype(BF16)
    logits = _split_dot(h3, wr_ref[...])
    tm = logits.shape[0]
    lane = lax.broadcasted_iota(jnp.int32, (tm, LANES), 1)
    big = jnp.int32(LANES)
    is_grp = jnp.logical_and(lane >= N_EXPERTS, lane < N_EXPERTS + N_GROUPS)
    glog = jnp.where(is_grp, logits, -jnp.inf)
    gmax = jnp.max(glog, axis=-1, keepdims=True)
    g_sel = jnp.min(jnp.where(glog == gmax, lane, big), axis=-1, keepdims=True) - N_EXPERTS
    p_g = 1.0 / jnp.sum(jnp.where(is_grp, jnp.exp(logits - gmax), 0.0), axis=-1, keepdims=True)
    aff = jax.nn.sigmoid(logits)
    in_grp = jnp.logical_and(lane >= g_sel * EXPERTS_PER_GROUP, lane < (g_sel + 1) * EXPERTS_PER_GROUP)
    val = jnp.where(in_grp, aff + bias_ref[...], -jnp.inf)
    m1 = jnp.max(val, axis=-1, keepdims=True)
    i1 = jnp.min(jnp.where(val == m1, lane, big), axis=-1, keepdims=True)
    val2 = jnp.where(lane == i1, -jnp.inf, val)
    m2 = jnp.max(val2, axis=-1, keepdims=True)
    i2 = jnp.min(jnp.where(val2 == m2, lane, big), axis=-1, keepdims=True)
    chosen = jnp.logical_or(lane == i1, lane == i2)
    a_sel = jnp.where(chosen, aff, 0.0)
    comb_o[0] = p_g * a_sel / jnp.sum(a_sel, axis=-1, keepdims=True)


def _merge_mem_router(oa, ob, sgd, sgm, x, w_br_dsa, w_br_mla, w_out, mem_x_g, mem_w_q, mem_q_g, km, vm,
                      mem_w_o, moe_g, w_group, w_expert, expert_bias, tm):
    b, s, d = x.shape
    wa = jnp.pad(w_br_dsa.reshape(DSA_HEADS, DSA_HEAD_DIM, d), ((0, 0), (0, LANES - DSA_HEAD_DIM), (0, 0)))
    wa = wa.reshape(DSA_HEADS * LANES, d).astype(BF16)
    wb = jnp.pad(w_br_mla.reshape(MLA_HEADS, MLA_V, d), ((0, 0), (0, LANES - MLA_V), (0, 0)))
    wb = wb.reshape(MLA_HEADS * LANES, d).astype(BF16)
    wq = _pad_heads(mem_w_q, MEM_HEADS, MEM_HEAD_DIM).astype(BF16)
    wmo = jnp.pad(mem_w_o.reshape(MEM_HEADS, MEM_HEAD_DIM, d), ((0, 0), (0, LANES - MEM_HEAD_DIM), (0, 0)))
    wmo = wmo.reshape(MEM_HEADS * LANES, d).astype(BF16)
    wr = _pad_cols(jnp.concatenate([w_expert, w_group], axis=1).astype(F32))
    bias = _pad_cols(expert_bias.reshape(1, -1).astype(F32))
    consts_a = [wa, wb, w_out.astype(BF16), mem_x_g.reshape(1, d), wq, _pad_cols(mem_q_g.reshape(1, -1))]
    consts_b = [wmo, moe_g.reshape(1, d), wr, bias]
    m = km.shape[1]
    mw = MEM_HEADS * LANES
    tok = lambda w: pl.BlockSpec((1, tm, w), lambda bi, i: (bi, i, 0))
    memspec = pl.BlockSpec((1, m, mw), lambda bi, i: (bi, 0, 0))
    return pl.pallas_call(
        _merge_kernel,
        grid=(b, s // tm),
        in_specs=[pl.BlockSpec((1, DSA_HEADS, tm, LANES), lambda bi, i: (bi, 0, i, 0)),
                  tok(MLA_HEADS * LANES), tok(d), tok(d), tok(d)]
                 + [_const_spec(c.shape) for c in consts_a] + [memspec, memspec]
                 + [_const_spec(c.shape) for c in consts_b],
        out_specs=[tok(d), tok(d), tok(LANES)],
        out_shape=[jax.ShapeDtypeStruct((b, s, d), F32), jax.ShapeDtypeStruct((b, s, d), BF16),
                   jax.ShapeDtypeStruct((b, s, LANES), F32)],
        compiler_params=_params(("parallel", "parallel")),
        name="merge_mem",
    )(oa, ob, sgd, sgm, x, *consts_a, km, vm, *consts_b)


def _moe_kernel(h_ref, comb_ref, x_ref, wg_ref, wu_ref, wd_ref, o_ref):
    g = pl.program_id(1)
    h = h_ref[...]
    comb = comb_ref[...]
    lane = lax.broadcasted_iota(jnp.int32, comb.shape, 1)
    parts = []
    for e in range(EXPERTS_PER_GROUP):
        gate = jnp.dot(h, wg_ref[e], preferred_element_type=F32)
        up = jnp.dot(h, wu_ref[e], preferred_element_type=F32)
        c_e = jnp.sum(jnp.where(lane == g * EXPERTS_PER_GROUP + e, comb, 0.0), axis=-1, keepdims=True)
        parts.append(((jax.nn.silu(gate) * up) * c_e).astype(BF16))
    w_down = wd_ref[...].reshape(EXPERTS_PER_GROUP * D_EXPERT, wd_ref.shape[-1])
    contrib = jnp.dot(jnp.concatenate(parts, axis=1), w_down, preferred_element_type=F32)

    @pl.when(g == 0)
    def _():
        o_ref[...] = x_ref[...] + contrib

    @pl.when(g > 0)
    def _():
        o_ref[...] += contrib


def _moe(h3, comb, x2, w_gate, w_up, w_down, tm):
    n, d = x2.shape
    per = EXPERTS_PER_GROUP
    return pl.pallas_call(
        _moe_kernel,
        grid=(n // tm, N_GROUPS),
        in_specs=[pl.BlockSpec((tm, d), lambda i, g: (i, 0)),
                  pl.BlockSpec((tm, LANES), lambda i, g: (i, 0)),
                  pl.BlockSpec((tm, d), lambda i, g: (i, 0)),
                  pl.BlockSpec((per, d, D_EXPERT), lambda i, g: (g, 0, 0)),
                  pl.BlockSpec((per, d, D_EXPERT), lambda i, g: (g, 0, 0)),
                  pl.BlockSpec((per, D_EXPERT, d), lambda i, g: (g, 0, 0))],
        out_specs=pl.BlockSpec((tm, d), lambda i, g: (i, 0)),
        out_shape=jax.ShapeDtypeStruct((n, d), F32),
        compiler_params=_params(("parallel", "arbitrary")),
        name="moe",
    )(h3, comb, x2, w_gate.astype(BF16), w_up.astype(BF16), w_down.astype(BF16))


def _layer(x, mem, positions, p):
    b, s, d = x.shape
    n = b * s
    assert s % (2 * KEY_CHUNK) == 0 and s % Q_TILE_MLA == 0
    topk = min(TOPK_MAX, s // 4)
    tm = min(TOKENS_PER_STEP_PROJ, s)
    qq, ka, va, ki, iw, mq, mk, mv, sgd, sgm = _in_proj(
        x.reshape(n, d), positions.reshape(n, 1), p["attn_norm_g"], p["w_in"], p["dsa_q_norm_g"],
        p["dsa_k_norm_g"], p["mla_cq_norm_g"], p["mla_ckv_norm_g"], p["mla_w_uq"], p["mla_w_ukv"],
        p["mla_q_norm_g"], p["mla_k_norm_g"], tm)
    r3 = lambda a: a.reshape(b, s, a.shape[-1])
    oa = _dsa_attention(r3(qq), iw, r3(ka), va, r3(ki), topk)
    ob = _mla_attention(r3(mq), r3(mk), mv)
    km, vm = _mem_kv(mem, p["mem_norm_g"], p["mem_w_kv"], p["mem_k_norm_g"])
    x2, h3, comb = _merge_mem_router(
        oa, ob, r3(sgd), r3(sgm), x, p["w_branch_dsa"], p["w_branch_mla"], p["w_out"], p["mem_x_norm_g"],
        p["mem_w_q"], p["mem_q_norm_g"], km, vm, p["mem_w_o"], p["moe_norm_g"], p["moe_w_group"],
        p["moe_w_expert"], p["moe_expert_bias"], min(TOKENS_PER_STEP_MERGE, s))
    out = _moe(h3.reshape(n, d), comb.reshape(n, LANES), x2.reshape(n, d), p["moe_w_gate"], p["moe_w_up"],
               p["moe_w_down"], min(TOKENS_PER_STEP_MOE, n))
    return out.reshape(b, s, d)


_PARAM_NAMES = ("attn_norm_g", "w_in", "dsa_q_norm_g", "dsa_k_norm_g", "mla_cq_norm_g", "mla_ckv_norm_g",
                "mla_w_uq", "mla_w_ukv", "mla_q_norm_g", "mla_k_norm_g", "w_branch_dsa", "w_branch_mla", "w_out",
                "mem_x_norm_g", "mem_norm_g", "mem_w_q", "mem_w_kv", "mem_q_norm_g", "mem_k_norm_g", "mem_w_o",
                "moe_norm_g", "moe_w_group", "moe_w_expert", "moe_expert_bias", "moe_w_gate", "moe_w_up",
                "moe_w_down")


def kernel(x, mem, positions, attn_norm_g, w_in, dsa_q_norm_g, dsa_k_norm_g, mla_cq_norm_g, mla_ckv_norm_g, mla_w_uq, mla_w_ukv, mla_q_norm_g, mla_k_norm_g, w_branch_dsa, w_branch_mla, w_out, mem_x_norm_g, mem_norm_g, mem_w_q, mem_w_kv, mem_q_norm_g, mem_k_norm_g, mem_w_o, moe_norm_g, moe_w_group, moe_w_expert, moe_expert_bias, moe_w_gate, moe_w_up, moe_w_down):
    stacked = (attn_norm_g, w_in, dsa_q_norm_g, dsa_k_norm_g, mla_cq_norm_g, mla_ckv_norm_g, mla_w_uq, mla_w_ukv,
               mla_q_norm_g, mla_k_norm_g, w_branch_dsa, w_branch_mla, w_out, mem_x_norm_g, mem_norm_g, mem_w_q,
               mem_w_kv, mem_q_norm_g, mem_k_norm_g, mem_w_o, moe_norm_g, moe_w_group, moe_w_expert,
               moe_expert_bias, moe_w_gate, moe_w_up, moe_w_down)
    for layer in range(attn_norm_g.shape[0]):
        p = {name: arr[layer] for name, arr in zip(_PARAM_NAMES, stacked)}
        x = _layer(x, mem, positions, p)
    return x
```

```python
import functools

import jax
import jax.numpy as jnp
from jax import lax
from jax.experimental import pallas as pl
from jax.experimental.pallas import tpu as pltpu

F32 = jnp.float32
BF16 = jnp.bfloat16

LANES = 128
SUBLANES = 8
VMEM_LIMIT = 56 * 1024 * 1024

ROPE_THETA = 10000.0
EPS = 1e-6
DSA_HEADS = 8
DSA_HEAD_DIM = 64
IDX_HEADS = 8
IDX_DIM = 64
TOPK_MAX = 256
MLA_HEADS = 8
MLA_NOPE = 64
MLA_ROPE = 32
MLA_QK = MLA_NOPE + MLA_ROPE
MLA_V = 64
MLA_Q_RANK = 256
MLA_KV_RANK = 128
MEM_HEADS = 4
MEM_HEAD_DIM = 64
N_GROUPS = 4
EXPERTS_PER_GROUP = 4
N_EXPERTS = N_GROUPS * EXPERTS_PER_GROUP
D_EXPERT = 256

NEG_BIG = -1e30
KEY_CHUNK = 256
Q_TILE_DSA = 256
DSA_V_ROWS = 80
MLA_V_ROWS = 80
Q_TILE_MLA = 512
MLA_HEAD_GROUP = 4
LOG2_E = 1.4426950408889634
BISECT_ITERS = 15
TOKENS_PER_STEP_PROJ = 256
TOKENS_PER_STEP_MERGE = 512
TOKENS_PER_STEP_MOE = 1024


def _const_spec(shape):
    nd = len(shape)
    return pl.BlockSpec(shape, lambda *_: (0,) * nd)


def _params(sem):
    return pltpu.CompilerParams(dimension_semantics=sem, vmem_limit_bytes=VMEM_LIMIT)


def _rms(x, g, n):
    ms = jnp.sum(x * x, axis=-1, keepdims=True) * (1.0 / n)
    return (x * lax.rsqrt(ms + EPS)) * g


def _rope(x, cos, sin_signed, lo_mask, half):
    fwd = pltpu.roll(x, LANES - half, 1)
    bwd = pltpu.roll(x, half, 1)
    return x * cos + jnp.where(lo_mask, fwd, bwd) * sin_signed


def _in_proj_kernel(x_ref, pos_ref, g_ref, wqq_ref, wsm_ref, wcq_ref, wg_ref, wuq_ref, wuk_ref,
                    wuv_ref, qag_ref, kag_ref, cqg_ref, ckvg_ref, mqg_ref, mkg_ref, freq_ref,
                    qq_o, ka_o, va_o, ki_o, iw_o, mq_o, mk_o, mv_o, sgd_o, sgm_o):
    x = x_ref[...]
    d_model = x.shape[-1]
    h = _rms(x, g_ref[...], d_model).astype(BF16)

    pos = pos_ref[...].astype(F32)
    lane = lax.broadcasted_iota(jnp.int32, (x.shape[0], LANES), 1)
    half_d, half_m = DSA_HEAD_DIM // 2, MLA_ROPE // 2
    ang = pos * freq_ref[...]
    lo_d = (lane & (DSA_HEAD_DIM - 1)) < half_d
    lo_m = lane < MLA_NOPE + half_m
    in_m_lo = jnp.logical_and(lane >= MLA_NOPE, lo_m)
    in_m_hi = jnp.logical_and(lane >= MLA_NOPE + half_m, lane < MLA_QK)

    def tables(t, fill):
        by_half = pltpu.roll(t, half_d, 1)
        head = jnp.where(lane < half_d, t, by_half)
        dsa = jnp.where(lane < DSA_HEAD_DIM, head, pltpu.roll(head, DSA_HEAD_DIM, 1))
        mla = jnp.where(in_m_lo, by_half, jnp.where(in_m_hi, pltpu.roll(t, half_d + half_m, 1), fill))
        return dsa, mla

    cos_d, cos_m = tables(jnp.cos(ang), 1.0)
    sin_d, sin_m = tables(jnp.sin(ang), 0.0)
    sin_d = jnp.where(lo_d, -sin_d, sin_d)
    sin_m = jnp.where(lo_m, -sin_m, sin_m)
    rope_d = functools.partial(_rope, cos=cos_d, sin_signed=sin_d, lo_mask=lo_d, half=DSA_HEAD_DIM // 2)
    rope_m = functools.partial(_rope, cos=cos_m, sin_signed=sin_m, lo_mask=lo_m, half=MLA_ROPE // 2)

    def dot(a, w_ref):
        return jnp.dot(a, w_ref[...], preferred_element_type=F32)

    att_scale = DSA_HEAD_DIM ** -0.5 * LOG2_E
    idx_scale = IDX_DIM ** -0.5 * IDX_HEADS ** -0.5
    mla_scale = MLA_QK ** -0.5 * LOG2_E

    qq = dot(h, wqq_ref)
    is_qa = lane < DSA_HEAD_DIM
    for hd in range(DSA_HEADS):
        sl = slice(hd * LANES, (hd + 1) * LANES)
        xh = qq[:, sl]
        ms = jnp.sum(jnp.where(is_qa, xh * xh, 0.0), axis=-1, keepdims=True) * (1.0 / DSA_HEAD_DIM)
        mult = jnp.where(is_qa, (lax.rsqrt(ms + EPS) * att_scale) * qag_ref[...], 1.0)
        qq_o[:, sl] = rope_d(xh * mult).astype(BF16)

    sm = dot(h, wsm_ref)
    ka_o[...] = rope_d(_rms(sm[:, 0:LANES], kag_ref[...], DSA_HEAD_DIM)).astype(BF16)
    va = sm[:, LANES:2 * LANES] + jnp.where(lane[0:1] == DSA_HEAD_DIM, 1.0, 0.0)
    va_o[0] = va.T[:DSA_V_ROWS].astype(BF16)
    ki_o[...] = rope_d(sm[:, 2 * LANES:3 * LANES]).astype(BF16)
    iw_o[0] = (sm[:, 3 * LANES:4 * LANES] * idx_scale).T[:IDX_HEADS]
    ckv = _rms(sm[:, 4 * LANES:5 * LANES], ckvg_ref[...], MLA_KV_RANK).astype(BF16)
    kpe = sm[:, 5 * LANES:6 * LANES]

    kn = dot(ckv, wuk_ref)
    mkg = mkg_ref[...]
    pe_rot = rope_m(kpe * mkg)
    ss_pe = jnp.sum(kpe * kpe, axis=-1, keepdims=True)
    for hd in range(MLA_HEADS):
        sl = slice(hd * LANES, (hd + 1) * LANES)
        xh = kn[:, sl]
        ms = (jnp.sum(xh * xh, axis=-1, keepdims=True) + ss_pe) * (1.0 / MLA_QK)
        mk_o[:, sl] = ((xh * mkg + pe_rot) * lax.rsqrt(ms + EPS)).astype(BF16)
    wide_lane = lax.broadcasted_iota(jnp.int32, (1, MLA_HEADS * LANES), 1)
    ones_col = jnp.where(wide_lane % LANES == MLA_V, 1.0, 0.0)
    mv = dot(ckv, wuv_ref) + ones_col
    for hd in range(MLA_HEADS):
        mv_o[0, hd] = mv[:, hd * LANES:(hd + 1) * LANES].T[:MLA_V_ROWS].astype(BF16)

    cq = _rms(dot(h, wcq_ref), cqg_ref[...], MLA_Q_RANK).astype(BF16)
    qb = dot(cq, wuq_ref)
    for hd in range(MLA_HEADS):
        sl = slice(hd * LANES, (hd + 1) * LANES)
        y = rope_m(_rms(qb[:, sl], mqg_ref[...], MLA_QK))
        mq_o[:, sl] = (y * mla_scale).astype(BF16)

    gates = jax.nn.sigmoid(dot(h, wg_ref))
    sgd_o[...] = gates[:, :d_model].astype(BF16)
    sgm_o[...] = gates[:, d_model:].astype(BF16)


def _pad_heads(w, heads, dim):
    k = w.shape[0]
    w = w.reshape(k, heads, dim)
    w = jnp.pad(w, ((0, 0), (0, 0), (0, LANES - dim)))
    return w.reshape(k, heads * LANES)


def _pad_cols(w, width=LANES, offset=0):
    return jnp.pad(w, ((0, 0), (offset, width - offset - w.shape[1])))


def _in_proj(x2, pos2, attn_g, w_in, dsa_q_g, dsa_k_g, cq_g, ckv_g, w_uq, w_ukv, mq_g, mk_g, tm):
    n, d = x2.shape
    sizes = (DSA_HEADS * DSA_HEAD_DIM, DSA_HEAD_DIM, DSA_HEAD_DIM, IDX_HEADS * IDX_DIM, IDX_DIM, IDX_HEADS,
             MLA_Q_RANK, MLA_KV_RANK, MLA_ROPE, d, d)
    offs = [0]
    for s in sizes:
        offs.append(offs[-1] + s)
    seg = [w_in[:, offs[i]:offs[i + 1]] for i in range(len(sizes))]
    w_dq, w_dk, w_dv, w_iq, w_ik, w_iw, w_cq, w_ckv, w_kpe, w_gd, w_gm = seg

    assert DSA_HEADS == IDX_HEADS and DSA_HEAD_DIM + IDX_DIM == LANES
    wqq = jnp.concatenate([w_dq.reshape(d, DSA_HEADS, DSA_HEAD_DIM), w_iq.reshape(d, IDX_HEADS, IDX_DIM)], axis=-1)
    wqq = wqq.reshape(d, DSA_HEADS * LANES).astype(BF16)
    wsm = jnp.concatenate([_pad_cols(w_dk), _pad_cols(w_dv), _pad_cols(w_ik, offset=DSA_HEAD_DIM), _pad_cols(w_iw), w_ckv,
                           _pad_cols(w_kpe, offset=MLA_NOPE)], axis=1).astype(BF16)
    wcq = w_cq.astype(BF16)
    wg = jnp.concatenate([w_gd, w_gm], axis=1).astype(BF16)
    wuq = _pad_heads(w_uq, MLA_HEADS, MLA_QK).astype(BF16)
    ukv = w_ukv.reshape(MLA_KV_RANK, MLA_HEADS, MLA_NOPE + MLA_V)
    wuk = _pad_heads(ukv[:, :, :MLA_NOPE].reshape(MLA_KV_RANK, -1), MLA_HEADS, MLA_NOPE).astype(BF16)
    wuv = _pad_heads(ukv[:, :, MLA_NOPE:].reshape(MLA_KV_RANK, -1), MLA_HEADS, MLA_V).astype(BF16)

    row = lambda v: _pad_cols(v.reshape(1, -1).astype(F32), width=max(LANES, v.size))
    half_d = DSA_HEAD_DIM // 2
    inv_d = ROPE_THETA ** (-jnp.arange(half_d, dtype=F32) / half_d)
    half_m = MLA_ROPE // 2
    inv_m = ROPE_THETA ** (-jnp.arange(half_m, dtype=F32) / half_m)
    freq = _pad_cols(jnp.concatenate([inv_d, inv_m]).reshape(1, -1))

    consts = [attn_g.reshape(1, d), wqq, wsm, wcq, wg, wuq, wuk, wuv, row(dsa_q_g), row(dsa_k_g),
              row(cq_g), row(ckv_g), row(mq_g), row(mk_g), freq]
    wide = DSA_HEADS * LANES
    out_shapes = [
        jax.ShapeDtypeStruct((n, wide), BF16),
        jax.ShapeDtypeStruct((n, LANES), BF16),
        jax.ShapeDtypeStruct((n // tm, DSA_V_ROWS, tm), BF16),
        jax.ShapeDtypeStruct((n, LANES), BF16),
        jax.ShapeDtypeStruct((n // tm, IDX_HEADS, tm), F32),
        jax.ShapeDtypeStruct((n, wide), BF16),
        jax.ShapeDtypeStruct((n, wide), BF16),
        jax.ShapeDtypeStruct((n // tm, MLA_HEADS, MLA_V_ROWS, tm), BF16),
        jax.ShapeDtypeStruct((n, d), BF16),
        jax.ShapeDtypeStruct((n, d), BF16),
    ]
    tile = lambda w: pl.BlockSpec((tm, w), lambda i: (i, 0))
    return pl.pallas_call(
        _in_proj_kernel,
        grid=(n // tm,),
        in_specs=[tile(d), tile(1)] + [_const_spec(c.shape) for c in consts],
        out_specs=[tile(s.shape[1]) if len(s.shape) == 2 else
                   pl.BlockSpec((1,) + s.shape[1:], lambda i, nd=len(s.shape): (i,) + (0,) * (nd - 1))
                   for s in out_shapes],
        out_shape=out_shapes,
        compiler_params=_params(("parallel",)),
        name="in_proj",
    )(x2, pos2, *consts)


def _dsa_kernel(qq_ref, iw_ref, ka_ref, vt_ref, ki_ref, o_ref, score_ref, logit_ref, *, topk):
    i = pl.program_id(1)
    tq = qq_ref.shape[1]
    heads = DSA_HEADS
    nch = (i * tq + tq + KEY_CHUNK - 1) // KEY_CHUNK
    npair = (nch + 1) // 2
    nfull = nch // 2
    odd = nch % 2 == 1
    kf = float(topk)
    shape = (KEY_CHUNK, tq)
    folded = (SUBLANES, tq)

    def fold(a, op):
        return op(a.reshape(KEY_CHUNK // SUBLANES, SUBLANES, tq), axis=0)

    krow = lax.broadcasted_iota(jnp.int32, shape, 0)
    qcol = i * tq + lax.broadcasted_iota(jnp.int32, shape, 1)
    contract_last = (((1,), (1,)), ((), ()))

    def pair_loop(body, init):
        return lax.fori_loop(0, npair, lambda j, carry: body(2 * j, 2 * j + 1, carry), init)

    qq = jnp.concatenate([qq_ref[0, :, hd * LANES:(hd + 1) * LANES] for hd in range(heads)], axis=0)
    iw = iw_ref[0]

    def score_chunk(c, mx, mn):
        kc = ki_ref[0, pl.ds(pl.multiple_of(c * KEY_CHUNK, KEY_CHUNK), KEY_CHUNK), :]
        sc = None
        for hp in range(IDX_HEADS // 2):
            rel = lax.dot_general(kc, qq[2 * hp * tq:(2 * hp + 2) * tq], contract_last,
                                  preferred_element_type=F32)
            rel = jnp.maximum(rel, 0.0)
            part = rel[:, :tq] * iw[2 * hp:2 * hp + 1, :] + rel[:, tq:] * iw[2 * hp + 1:2 * hp + 2, :]
            sc = part if sc is None else sc + part
        causal = (krow + c * KEY_CHUNK) <= qcol
        masked = jnp.where(causal, sc, -jnp.inf)
        score_ref[c] = masked
        return (jnp.maximum(mx, fold(masked, jnp.max)),
                jnp.minimum(mn, fold(jnp.where(causal, sc, jnp.inf), jnp.min)))

    def score_body(j, carry):
        return score_chunk(2 * j + 1, *score_chunk(2 * j, *carry))

    def score_tail(carry):
        score_ref[nch] = jnp.full(shape, -jnp.inf, F32)
        return score_chunk(nch - 1, *carry)

    carry = lax.fori_loop(0, nfull, score_body, (jnp.full(folded, -jnp.inf, F32), jnp.full(folded, jnp.inf, F32)))
    mx, mn = lax.cond(odd, score_tail, lambda cr: cr, carry)
    hi = jnp.max(mx, axis=0, keepdims=True)
    lo = jnp.min(mn, axis=0, keepdims=True)

    def count_gt(t):
        def body(c0, c1, acc):
            for c in (c0, c1):
                acc = acc + fold(jnp.where(score_ref[c] > t, 1.0, 0.0), jnp.sum)
            return acc
        return jnp.sum(pair_loop(body, jnp.zeros(folded, F32)), axis=0, keepdims=True)

    def bisect(_, carry):
        lo, hi = carry
        mid = 0.5 * (lo + hi)
        below = count_gt(mid) < kf
        return jnp.where(below, lo, mid), jnp.where(below, mid, hi)

    lo, hi = lax.fori_loop(0, BISECT_ITERS, bisect, (lo, hi))

    def max_le(t):
        def body(c0, c1, acc):
            for c in (c0, c1):
                s = score_ref[c]
                acc = jnp.maximum(acc, fold(jnp.where(s <= t, s, -jnp.inf), jnp.max))
            return acc
        return jnp.max(pair_loop(body, jnp.full(folded, -jnp.inf, F32)), axis=0, keepdims=True)

    n_causal = (i * tq + lax.broadcasted_iota(jnp.int32, (1, tq), 1) + 1).astype(F32)
    small = n_causal <= kf

    def refine_cond(carry):
        return jnp.min(carry[2]) < 0.5

    def refine_body(carry):
        m, thr, done, thr_cnt = carry

        def body(c0, c1, acc):
            cnt, nxt = acc
            for c in (c0, c1):
                s = score_ref[c]
                cnt = cnt + fold(jnp.where(s >= m, 1.0, 0.0), jnp.sum)
                nxt = jnp.maximum(nxt, fold(jnp.where(s < m, s, -jnp.inf), jnp.max))
            return cnt, nxt

        cnt, nxt = pair_loop(body, (jnp.zeros(folded, F32), jnp.full(folded, -jnp.inf, F32)))
        cnt = jnp.sum(cnt, axis=0, keepdims=True)
        nxt = jnp.max(nxt, axis=0, keepdims=True)
        reached = cnt >= kf
        hit = jnp.logical_and(reached, done < 0.5)
        return nxt, jnp.where(hit, m, thr), jnp.where(reached, 1.0, done), jnp.where(hit, cnt, thr_cnt)

    init = (max_le(hi), jnp.full((1, tq), -jnp.inf, F32), jnp.where(small, 1.0, 0.0), jnp.zeros((1, tq), F32))
    _, thr, _, thr_cnt = lax.while_loop(refine_cond, refine_body, init)

    def mask_with_ties():
        quota = jnp.where(small, 0.0, kf - count_gt(thr))
        earlier = (lax.broadcasted_iota(jnp.int32, (KEY_CHUNK, KEY_CHUNK), 1)
                   < lax.broadcasted_iota(jnp.int32, (KEY_CHUNK, KEY_CHUNK), 0)).astype(BF16)

        def body(c0, c1, seen):
            for c in (c0, c1):
                s = score_ref[c]
                eq = s == thr
                eqf = jnp.where(eq, 1.0, 0.0)
                before = jnp.dot(earlier, eqf.astype(BF16), preferred_element_type=F32) + seen
                sel = jnp.logical_or(s > thr, jnp.logical_and(eq, before < quota))
                seen = seen + jnp.sum(eqf, axis=0, keepdims=True)
                score_ref[c] = jnp.where(sel, jnp.inf, -jnp.inf)
            return seen
        pair_loop(body, jnp.zeros((1, tq), F32))

    surplus = jnp.max(jnp.where(small, 0.0, thr_cnt - kf)) > 0.5
    lax.cond(surplus, mask_with_ties, lambda: None)

    def qk_into(slot, c):
        kc = ka_ref[0, pl.ds(pl.multiple_of(c * KEY_CHUNK, KEY_CHUNK), KEY_CHUNK), :]
        logit_ref[slot] = lax.dot_general(kc, qq, contract_last, preferred_element_type=F32)

    def softmax_pv(slot, c, m_run, acc):
        s = score_ref[c]
        bias = jnp.where(jnp.logical_and(s >= thr, s > -jnp.inf), 0.0, NEG_BIG)
        vt = vt_ref[0, c]
        new_m, new_acc = [], []
        for hp in range(heads // 2):
            probs, alphas = [], []
            for hd in (2 * hp, 2 * hp + 1):
                sl = slice(hd * tq, (hd + 1) * tq)
                lh = logit_ref[slot, :, sl] + bias
                m_old = m_run[:, sl]
                m_new = jnp.maximum(m_old, jnp.max(lh, axis=0, keepdims=True))
                probs.append(jnp.exp2(lh - m_new).astype(BF16))
                alphas.append(jnp.exp2(m_old - m_new))
                new_m.append(m_new)
            pv = jnp.dot(vt, jnp.concatenate(probs, axis=1), preferred_element_type=F32)
            new_acc.append(jnp.concatenate(alphas, axis=1) * acc[:, 2 * hp * tq:(2 * hp + 2) * tq] + pv)
        return jnp.concatenate(new_m, axis=1), jnp.concatenate(new_acc, axis=1)

    def attn_body(j, carry):
        m_run, acc = carry
        qk_into(1, 2 * j + 1)
        m_run, acc = softmax_pv(0, 2 * j, m_run, acc)
        qk_into(0, jnp.minimum(2 * j + 2, nch - 1))
        return softmax_pv(1, 2 * j + 1, m_run, acc)

    qk_into(0, 0)
    d_rows = vt_ref.shape[2]
    init = (jnp.full((1, heads * tq), NEG_BIG, F32), jnp.zeros((d_rows, heads * tq), F32))
    carry = lax.fori_loop(0, nfull, attn_body, init)
    _, acc = lax.cond(odd, lambda cr: softmax_pv(0, nch - 1, *cr), lambda cr: cr, carry)
    for hp in range(heads // 2):
        halves = []
        for hd in (2 * hp, 2 * hp + 1):
            blk = acc[:, hd * tq:(hd + 1) * tq]
            halves.append(blk[:DSA_HEAD_DIM] / blk[DSA_HEAD_DIM:DSA_HEAD_DIM + 1, :])
        o_ref[0, hp] = jnp.concatenate(halves, axis=0).T.astype(o_ref.dtype)


def _dsa_attention(qq, iw_t, ka, va_t, ki, topk):
    b, s, wide = qq.shape
    tq = Q_TILE_DSA
    nkc = s // KEY_CHUNK
    assert iw_t.shape[-1] == tq and va_t.shape[-1] == KEY_CHUNK
    v_t = va_t.reshape(b, nkc, DSA_V_ROWS, KEY_CHUNK)
    qspec = pl.BlockSpec((1, tq, wide), lambda bi, i: (bi, i, 0))
    kspec = pl.BlockSpec((1, s, LANES), lambda bi, i: (bi, 0, 0))
    return pl.pallas_call(
        functools.partial(_dsa_kernel, topk=topk),
        grid=(b, s // tq),
        in_specs=[qspec, pl.BlockSpec((1, IDX_HEADS, tq), lambda bi, i: (bi * (s // tq) + i, 0, 0)), kspec,
                  pl.BlockSpec((1, nkc, DSA_V_ROWS, KEY_CHUNK), lambda bi, i: (bi, 0, 0, 0)), kspec],
        out_specs=pl.BlockSpec((1, DSA_HEADS // 2, tq, LANES), lambda bi, i: (bi, 0, i, 0)),
        out_shape=jax.ShapeDtypeStruct((b, DSA_HEADS // 2, s, LANES), BF16),
        scratch_shapes=[pltpu.VMEM((nkc, KEY_CHUNK, tq), F32),
                        pltpu.VMEM((2, KEY_CHUNK, DSA_HEADS * tq), F32)],
        compiler_params=_params(("parallel", "arbitrary")),
        name="dsa_attn",
    )(qq, iw_t, ka, v_t, ki)


def _mla_kernel(q_ref, k_ref, vt_ref, o_ref, logit_ref):
    i = pl.program_id(2)
    tq = q_ref.shape[1]
    group = q_ref.shape[2] // LANES
    d_rows = vt_ref.shape[3]
    per_block = tq // vt_ref.shape[4]
    qs = [q_ref[0, :, g * LANES:(g + 1) * LANES] for g in range(group)]
    contract_last = (((1,), (1,)), ((), ()))

    def qk_into(slot, c):
        k0 = pl.multiple_of(c * tq, tq)
        for g in range(group):
            logit_ref[slot, g] = lax.dot_general(k_ref[0, pl.ds(k0, tq), g * LANES:(g + 1) * LANES], qs[g],
                                                 contract_last, preferred_element_type=F32)

    def softmax_pv(slot, c, carry, masked):
        if masked:
            keep = (lax.broadcasted_iota(jnp.int32, (tq, tq), 0) <= lax.broadcasted_iota(jnp.int32, (tq, tq), 1))
            bias = jnp.where(keep, 0.0, NEG_BIG)
        probs, alphas, maxes = [], [], []
        for g in range(group):
            m_run = carry[2 * g]
            logit = logit_ref[slot, g]
            if masked:
                logit = logit + bias
            m_new = jnp.maximum(m_run, jnp.max(logit, axis=0, keepdims=True))
            probs.append(jnp.exp2(logit - m_new).astype(BF16))
            alphas.append(jnp.exp2(m_run - m_new))
            maxes.append(m_new)
        res = []
        for g in range(group):
            vt = jnp.concatenate([vt_ref[0, c * per_block + t, g] for t in range(per_block)], axis=1)
            pv = jnp.dot(vt, probs[g], preferred_element_type=F32)
            res += [maxes[g], alphas[g] * carry[2 * g + 1] + pv]
        return tuple(res)

    def pair_body(j, carry):
        qk_into(1, 2 * j + 1)
        carry = softmax_pv(0, 2 * j, carry, False)
        qk_into(0, 2 * j + 2)
        return softmax_pv(1, 2 * j + 1, carry, False)

    def tail_even(carry):
        return softmax_pv(0, i, carry, True)

    def tail_odd(carry):
        qk_into(1, i)
        carry = softmax_pv(0, i - 1, carry, False)
        return softmax_pv(1, i, carry, True)

    qk_into(0, 0)
    init = (jnp.full((1, tq), NEG_BIG, F32), jnp.zeros((d_rows, tq), F32)) * group
    carry = lax.fori_loop(0, i // 2, pair_body, init)
    carry = lax.cond(i % 2 == 0, tail_even, tail_odd, carry)
    for gp in range(group // 2):
        halves = [carry[2 * g + 1][:MLA_V] / carry[2 * g + 1][MLA_V:MLA_V + 1, :] for g in (2 * gp, 2 * gp + 1)]
        o_ref[0, :, gp * LANES:(gp + 1) * LANES] = jnp.concatenate(halves, axis=0).T.astype(o_ref.dtype)


def _mla_attention(mq, mk, mv_t):
    b, s, wide = mq.shape
    tq = Q_TILE_MLA
    nkc = s // tq
    gw = MLA_HEAD_GROUP * LANES
    nt, _, _, tm = mv_t.shape
    nt //= b
    assert tq % tm == 0
    v_t = mv_t.reshape(b, nt, MLA_HEADS, MLA_V_ROWS, tm)
    return pl.pallas_call(
        _mla_kernel,
        grid=(b, MLA_HEADS // MLA_HEAD_GROUP, s // tq),
        in_specs=[pl.BlockSpec((1, tq, gw), lambda bi, h, i: (bi, i, h)),
                  pl.BlockSpec((1, s, gw), lambda bi, h, i: (bi, 0, h)),
                  pl.BlockSpec((1, nt, MLA_HEAD_GROUP, MLA_V_ROWS, tm), lambda bi, h, i: (bi, 0, h, 0, 0))],
        out_specs=pl.BlockSpec((1, tq, gw // 2), lambda bi, h, i: (bi, i, h)),
        out_shape=jax.ShapeDtypeStruct((b, s, MLA_HEADS * MLA_V), BF16),
        scratch_shapes=[pltpu.VMEM((2, MLA_HEAD_GROUP, tq, tq), F32)],
        compiler_params=_params(("parallel", "parallel", "arbitrary")),
        name="mla_attn",
    )(mq, mk, v_t)


def _mem_kv_kernel(mem_ref, g_ref, wkv_ref, kg_ref, k_o, v_o):
    m = mem_ref[0]
    hm = _rms(m, g_ref[...], m.shape[-1]).astype(BF16)
    kv = jnp.dot(hm, wkv_ref[...], preferred_element_type=F32)
    for hd in range(MEM_HEADS):
        sl = slice(hd * LANES, (hd + 1) * LANES)
        k_o[0, :, sl] = _rms(kv[:, sl], kg_ref[...], MEM_HEAD_DIM).astype(BF16)
    v_o[0] = kv[:, MEM_HEADS * LANES:].astype(BF16)


def _mem_kv(mem, mem_g, w_kv, k_g):
    b, m, d = mem.shape
    hw = MEM_HEADS * MEM_HEAD_DIM
    wkv = jnp.concatenate([_pad_heads(w_kv[:, :hw], MEM_HEADS, MEM_HEAD_DIM),
                           _pad_heads(w_kv[:, hw:], MEM_HEADS, MEM_HEAD_DIM)], axis=1).astype(BF16)
    consts = [mem_g.reshape(1, d), wkv, _pad_cols(k_g.reshape(1, -1))]
    wide = MEM_HEADS * LANES
    spec = pl.BlockSpec((1, m, wide), lambda bi: (bi, 0, 0))
    return pl.pallas_call(
        _mem_kv_kernel,
        grid=(b,),
        in_specs=[pl.BlockSpec((1, m, d), lambda bi: (bi, 0, 0))] + [_const_spec(c.shape) for c in consts],
        out_specs=[spec, spec],
        out_shape=[jax.ShapeDtypeStruct((b, m, wide), BF16)] * 2,
        compiler_params=_params(("parallel",)),
        name="mem_kv",
    )(mem, *consts)


def _split_dot(a, w):
    a_hi = a.astype(BF16)
    a_lo = (a - a_hi.astype(F32)).astype(BF16)
    w_hi = w.astype(BF16)
    w_lo = (w - w_hi.astype(F32)).astype(BF16)
    w_both = jnp.concatenate([w_hi, w_lo], axis=1)
    n = w.shape[1]
    hi = jnp.dot(a_hi, w_both, preferred_element_type=F32)
    lo = jnp.dot(a_lo, w_both, preferred_element_type=F32)
    return (hi[:, :n] + hi[:, n:]) + (lo[:, :n] + lo[:, n:])


def _merge_kernel(oa_ref, ob_ref, sgd_ref, sgm_ref, x_ref, wa_ref, wb_ref, wo_ref, mxg_ref, wq_ref, qg_ref,
                  km_ref, vm_ref, wmo_ref, moeg_ref, wr_ref, bias_ref, x2_o, h3_o, comb_o):
    d = functools.partial(jnp.dot, preferred_element_type=F32)
    oa = jnp.concatenate([oa_ref[0, hp] for hp in range(DSA_HEADS // 2)], axis=-1)
    merged = (sgd_ref[0].astype(F32) * d(oa, wa_ref[...]) + sgm_ref[0].astype(F32) * d(ob_ref[0], wb_ref[...]))
    x1 = x_ref[0] + d(merged.astype(BF16), wo_ref[...])

    d_model = x1.shape[-1]
    h2 = _rms(x1, mxg_ref[...], d_model).astype(BF16)
    qm = d(h2, wq_ref[...])
    outs = []
    for hd in range(MEM_HEADS):
        sl = slice(hd * LANES, (hd + 1) * LANES)
        q = (_rms(qm[:, sl], qg_ref[...], MEM_HEAD_DIM) * MEM_HEAD_DIM ** -0.5).astype(BF16)
        logit = lax.dot_general(q, km_ref[0, :, sl], (((1,), (1,)), ((), ())), preferred_element_type=F32)
        logit = logit - jnp.max(logit, axis=-1, keepdims=True)
        p = jnp.exp(logit)
        pv = d(p.astype(BF16), vm_ref[0, :, sl])
        outs.append((pv / jnp.sum(p, axis=-1, keepdims=True)).astype(BF16))
    x2 = x1 + d(jnp.concatenate(outs, axis=-1), wmo_ref[...])
    x2_o[0] = x2

    h3 = _rms(x2, moeg_ref[...], d_model)
    h3_o[0] = h3.astype(BF16)
    logits = _split_dot(h3, wr_ref[...])
    tm = logits.shape[0]
    lane = lax.broadcasted_iota(jnp.int32, (tm, LANES), 1)
    big = jnp.int32(LANES)
    is_grp = jnp.logical_and(lane >= N_EXPERTS, lane < N_EXPERTS + N_GROUPS)
    glog = jnp.where(is_grp, logits, -jnp.inf)
    gmax = jnp.max(glog, axis=-1, keepdims=True)
    g_sel = jnp.min(jnp.where(glog == gmax, lane, big), axis=-1, keepdims=True) - N_EXPERTS
    p_g = 1.0 / jnp.sum(jnp.where(is_grp, jnp.exp(logits - gmax), 0.0), axis=-1, keepdims=True)
    aff = jax.nn.sigmoid(logits)
    in_grp = jnp.logical_and(lane >= g_sel * EXPERTS_PER_GROUP, lane < (g_sel + 1) * EXPERTS_PER_GROUP)
    val = jnp.where(in_grp, aff + bias_ref[...], -jnp.inf)
    m1 = jnp.max(val, axis=-1, keepdims=True)
    i1 = jnp.min(jnp.where(val == m1, lane, big), axis=-1, keepdims=True)
    val2 = jnp.where(lane == i1, -jnp.inf, val)
    m2 = jnp.max(val2, axis=-1, keepdims=True)
    i2 = jnp.min(jnp.where(val2 == m2, lane, big), axis=-1, keepdims=True)
    chosen = jnp.logical_or(lane == i1, lane == i2)
    a_sel = jnp.where(chosen, aff, 0.0)
    comb_o[0] = p_g * a_sel / jnp.sum(a_sel, axis=-1, keepdims=True)


def _merge_mem_router(oa, ob, sgd, sgm, x, w_br_dsa, w_br_mla, w_out, mem_x_g, mem_w_q, mem_q_g, km, vm,
                      mem_w_o, moe_g, w_group, w_expert, expert_bias, tm):
    b, s, d = x.shape
    wa = w_br_dsa.astype(BF16)
    wb = w_br_mla.astype(BF16)
    wq = _pad_heads(mem_w_q, MEM_HEADS, MEM_HEAD_DIM).astype(BF16)
    wmo = jnp.pad(mem_w_o.reshape(MEM_HEADS, MEM_HEAD_DIM, d), ((0, 0), (0, LANES - MEM_HEAD_DIM), (0, 0)))
    wmo = wmo.reshape(MEM_HEADS * LANES, d).astype(BF16)
    wr = _pad_cols(jnp.concatenate([w_expert, w_group], axis=1).astype(F32))
    bias = _pad_cols(expert_bias.reshape(1, -1).astype(F32))
    consts_a = [wa, wb, w_out.astype(BF16), mem_x_g.reshape(1, d), wq, _pad_cols(mem_q_g.reshape(1, -1))]
    consts_b = [wmo, moe_g.reshape(1, d), wr, bias]
    m = km.shape[1]
    mw = MEM_HEADS * LANES
    tok = lambda w: pl.BlockSpec((1, tm, w), lambda bi, i: (bi, i, 0))
    memspec = pl.BlockSpec((1, m, mw), lambda bi, i: (bi, 0, 0))
    return pl.pallas_call(
        _merge_kernel,
        grid=(b, s // tm),
        in_specs=[pl.BlockSpec((1, DSA_HEADS // 2, tm, LANES), lambda bi, i: (bi, 0, i, 0)),
                  tok(MLA_HEADS * MLA_V), tok(d), tok(d), tok(d)]
                 + [_const_spec(c.shape) for c in consts_a] + [memspec, memspec]
                 + [_const_spec(c.shape) for c in consts_b],
        out_specs=[tok(d), tok(d), tok(LANES)],
        out_shape=[jax.ShapeDtypeStruct((b, s, d), F32), jax.ShapeDtypeStruct((b, s, d), BF16),
                   jax.ShapeDtypeStruct((b, s, LANES), F32)],
        compiler_params=_params(("parallel", "parallel")),
        name="merge_mem",
    )(oa, ob, sgd, sgm, x, *consts_a, km, vm, *consts_b)


def _moe_kernel(h_ref, comb_ref, x_ref, wg_ref, wu_ref, wd_ref, o_ref):
    g = pl.program_id(1)
    h = h_ref[...]
    comb = comb_ref[...]
    lane = lax.broadcasted_iota(jnp.int32, comb.shape, 1)
    parts = []
    for e in range(EXPERTS_PER_GROUP):
        gate = jnp.dot(h, wg_ref[e], preferred_element_type=F32)
        up = jnp.dot(h, wu_ref[e], preferred_element_type=F32)
        c_e = jnp.sum(jnp.where(lane == g * EXPERTS_PER_GROUP + e, comb, 0.0), axis=-1, keepdims=True)
        parts.append(((jax.nn.silu(gate) * up) * c_e).astype(BF16))
    w_down = wd_ref[...].reshape(EXPERTS_PER_GROUP * D_EXPERT, wd_ref.shape[-1])
    contrib = jnp.dot(jnp.concatenate(parts, axis=1), w_down, preferred_element_type=F32)

    @pl.when(g == 0)
    def _():
        o_ref[...] = x_ref[...] + contrib

    @pl.when(g > 0)
    def _():
        o_ref[...] += contrib


def _moe(h3, comb, x2, w_gate, w_up, w_down, tm):
    n, d = x2.shape
    per = EXPERTS_PER_GROUP
    return pl.pallas_call(
        _moe_kernel,
        grid=(n // tm, N_GROUPS),
        in_specs=[pl.BlockSpec((tm, d), lambda i, g: (i, 0)),
                  pl.BlockSpec((tm, LANES), lambda i, g: (i, 0)),
                  pl.BlockSpec((tm, d), lambda i, g: (i, 0)),
                  pl.BlockSpec((per, d, D_EXPERT), lambda i, g: (g, 0, 0)),
                  pl.BlockSpec((per, d, D_EXPERT), lambda i, g: (g, 0, 0)),
                  pl.BlockSpec((per, D_EXPERT, d), lambda i, g: (g, 0, 0))],
        out_specs=pl.BlockSpec((tm, d), lambda i, g: (i, 0)),
        out_shape=jax.ShapeDtypeStruct((n, d), F32),
        compiler_params=_params(("parallel", "arbitrary")),
        name="moe",
    )(h3, comb, x2, w_gate.astype(BF16), w_up.astype(BF16), w_down.astype(BF16))


def _layer(x, mem, positions, p):
    b, s, d = x.shape
    n = b * s
    assert s % (2 * KEY_CHUNK) == 0 and s % Q_TILE_MLA == 0
    topk = min(TOPK_MAX, s // 4)
    tm = min(TOKENS_PER_STEP_PROJ, s)
    qq, ka, va, ki, iw, mq, mk, mv, sgd, sgm = _in_proj(
        x.reshape(n, d), positions.reshape(n, 1), p["attn_norm_g"], p["w_in"], p["dsa_q_norm_g"],
        p["dsa_k_norm_g"], p["mla_cq_norm_g"], p["mla_ckv_norm_g"], p["mla_w_uq"], p["mla_w_ukv"],
        p["mla_q_norm_g"], p["mla_k_norm_g"], tm)
    r3 = lambda a: a.reshape(b, s, a.shape[-1])
    oa = _dsa_attention(r3(qq), iw, r3(ka), va, r3(ki), topk)
    ob = _mla_attention(r3(mq), r3(mk), mv)
    km, vm = _mem_kv(mem, p["mem_norm_g"], p["mem_w_kv"], p["mem_k_norm_g"])
    x2, h3, comb = _merge_mem_router(
        oa, ob, r3(sgd), r3(sgm), x, p["w_branch_dsa"], p["w_branch_mla"], p["w_out"], p["mem_x_norm_g"],
        p["mem_w_q"], p["mem_q_norm_g"], km, vm, p["mem_w_o"], p["moe_norm_g"], p["moe_w_group"],
        p["moe_w_expert"], p["moe_expert_bias"], min(TOKENS_PER_STEP_MERGE, s))
    out = _moe(h3.reshape(n, d), comb.reshape(n, LANES), x2.reshape(n, d), p["moe_w_gate"], p["moe_w_up"],
               p["moe_w_down"], min(TOKENS_PER_STEP_MOE, n))
    return out.reshape(b, s, d)


_PARAM_NAMES = ("attn_norm_g", "w_in", "dsa_q_norm_g", "dsa_k_norm_g", "mla_cq_norm_g", "mla_ckv_norm_g",
                "mla_w_uq", "mla_w_ukv", "mla_q_norm_g", "mla_k_norm_g", "w_branch_dsa", "w_branch_mla", "w_out",
                "mem_x_norm_g", "mem_norm_g", "mem_w_q", "mem_w_kv", "mem_q_norm_g", "mem_k_norm_g", "mem_w_o",
                "moe_norm_g", "moe_w_group", "moe_w_expert", "moe_expert_bias", "moe_w_gate", "moe_w_up",
                "moe_w_down")


def kernel(x, mem, positions, attn_norm_g, w_in, dsa_q_norm_g, dsa_k_norm_g, mla_cq_norm_g, mla_ckv_norm_g, mla_w_uq, mla_w_ukv, mla_q_norm_g, mla_k_norm_g, w_branch_dsa, w_branch_mla, w_out, mem_x_norm_g, mem_norm_g, mem_w_q, mem_w_kv, mem_q_norm_g, mem_k_norm_g, mem_w_o, moe_norm_g, moe_w_group, moe_w_expert, moe_expert_bias, moe_w_gate, moe_w_up, moe_w_down):
    stacked = (attn_norm_g, w_in, dsa_q_norm_g, dsa_k_norm_g, mla_cq_norm_g, mla_ckv_norm_g, mla_w_uq, mla_w_ukv,
               mla_q_norm_g, mla_k_norm_g, w_branch_dsa, w_branch_mla, w_out, mem_x_norm_g, mem_norm_g, mem_w_q,
               mem_w_kv, mem_q_norm_g, mem_k_norm_g, mem_w_o, moe_norm_g, moe_w_group, moe_w_expert,
               moe_expert_bias, moe_w_gate, moe_w_up, moe_w_down)
    for layer in range(attn_norm_g.shape[0]):
        p = {name: arr[layer] for name, arr in zip(_PARAM_NAMES, stacked)}
        x = _layer(x, mem, positions, p)
    return x
```

```python
import functools

import jax
import jax.numpy as jnp
from jax import lax
from jax.experimental import pallas as pl
from jax.experimental.pallas import tpu as pltpu

F32 = jnp.float32
BF16 = jnp.bfloat16

LANES = 128
SUBLANES = 8
VMEM_LIMIT = 56 * 1024 * 1024

ROPE_THETA = 10000.0
EPS = 1e-6
DSA_HEADS = 8
DSA_HEAD_DIM = 64
IDX_HEADS = 8
IDX_DIM = 64
TOPK_MAX = 256
MLA_HEADS = 8
MLA_NOPE = 64
MLA_ROPE = 32
MLA_QK = MLA_NOPE + MLA_ROPE
MLA_V = 64
MLA_Q_RANK = 256
MLA_KV_RANK = 128
MEM_HEADS = 4
MEM_HEAD_DIM = 64
N_GROUPS = 4
EXPERTS_PER_GROUP = 4
N_EXPERTS = N_GROUPS * EXPERTS_PER_GROUP
D_EXPERT = 256

NEG_BIG = -1e30
KEY_CHUNK = 256
Q_TILE_DSA = 256
DSA_V_ROWS = 80
MLA_V_ROWS = 80
Q_TILE_MLA = 512
MLA_HEAD_GROUP = 4
LOG2_E = 1.4426950408889634
BISECT_ITERS = 15
TOKENS_PER_STEP_PROJ = 256
TOKENS_PER_STEP_MERGE = 512
TOKENS_PER_STEP_MOE = 1024
MOE_EXPERTS_PER_STEP = 8


def _const_spec(shape):
    nd = len(shape)
    return pl.BlockSpec(shape, lambda *_: (0,) * nd)


def _params(sem):
    return pltpu.CompilerParams(dimension_semantics=sem, vmem_limit_bytes=VMEM_LIMIT)


def _rms(x, g, n):
    ms = jnp.sum(x * x, axis=-1, keepdims=True) * (1.0 / n)
    return (x * lax.rsqrt(ms + EPS)) * g


def _rope(x, cos, sin_signed, lo_mask, half):
    fwd = pltpu.roll(x, LANES - half, 1)
    bwd = pltpu.roll(x, half, 1)
    return x * cos + jnp.where(lo_mask, fwd, bwd) * sin_signed


def _in_proj_kernel(x_ref, pos_ref, g_ref, wqq_ref, wsm_ref, wcq_ref, wg_ref, wuq_ref, wuk_ref,
                    wuv_ref, qag_ref, kag_ref, cqg_ref, ckvg_ref, mqg_ref, mkg_ref, freq_ref,
                    qq_o, ka_o, va_o, ki_o, iw_o, mq_o, mk_o, mv_o, sgd_o, sgm_o):
    x = x_ref[...]
    d_model = x.shape[-1]
    h = _rms(x, g_ref[...], d_model).astype(BF16)

    pos = pos_ref[...].astype(F32)
    lane = lax.broadcasted_iota(jnp.int32, (x.shape[0], LANES), 1)
    half_d, half_m = DSA_HEAD_DIM // 2, MLA_ROPE // 2
    ang = pos * freq_ref[...]
    lo_d = (lane & (DSA_HEAD_DIM - 1)) < half_d
    lo_m = lane < MLA_NOPE + half_m
    in_m_lo = jnp.logical_and(lane >= MLA_NOPE, lo_m)
    in_m_hi = jnp.logical_and(lane >= MLA_NOPE + half_m, lane < MLA_QK)

    def tables(t, fill):
        by_half = pltpu.roll(t, half_d, 1)
        head = jnp.where(lane < half_d, t, by_half)
        dsa = jnp.where(lane < DSA_HEAD_DIM, head, pltpu.roll(head, DSA_HEAD_DIM, 1))
        mla = jnp.where(in_m_lo, by_half, jnp.where(in_m_hi, pltpu.roll(t, half_d + half_m, 1), fill))
        return dsa, mla

    cos_d, cos_m = tables(jnp.cos(ang), 1.0)
    sin_d, sin_m = tables(jnp.sin(ang), 0.0)
    sin_d = jnp.where(lo_d, -sin_d, sin_d)
    sin_m = jnp.where(lo_m, -sin_m, sin_m)
    rope_d = functools.partial(_rope, cos=cos_d, sin_signed=sin_d, lo_mask=lo_d, half=DSA_HEAD_DIM // 2)
    rope_m = functools.partial(_rope, cos=cos_m, sin_signed=sin_m, lo_mask=lo_m, half=MLA_ROPE // 2)

    def dot(a, w_ref):
        return jnp.dot(a, w_ref[...], preferred_element_type=F32)

    att_scale = DSA_HEAD_DIM ** -0.5 * LOG2_E
    idx_scale = IDX_DIM ** -0.5 * IDX_HEADS ** -0.5
    mla_scale = MLA_QK ** -0.5 * LOG2_E

    qq = dot(h, wqq_ref)
    is_qa = lane < DSA_HEAD_DIM
    for hd in range(DSA_HEADS):
        sl = slice(hd * LANES, (hd + 1) * LANES)
        xh = qq[:, sl]
        ms = jnp.sum(jnp.where(is_qa, xh * xh, 0.0), axis=-1, keepdims=True) * (1.0 / DSA_HEAD_DIM)
        mult = jnp.where(is_qa, (lax.rsqrt(ms + EPS) * att_scale) * qag_ref[...], 1.0)
        qq_o[:, sl] = rope_d(xh * mult).astype(BF16)

    sm = dot(h, wsm_ref)
    ka_o[...] = rope_d(_rms(sm[:, 0:LANES], kag_ref[...], DSA_HEAD_DIM)).astype(BF16)
    va = sm[:, LANES:2 * LANES] + jnp.where(lane[0:1] == DSA_HEAD_DIM, 1.0, 0.0)
    va_o[0] = va.T[:DSA_V_ROWS].astype(BF16)
    ki_o[...] = rope_d(sm[:, 2 * LANES:3 * LANES]).astype(BF16)
    iw_o[0] = (sm[:, 3 * LANES:4 * LANES] * idx_scale).T[:IDX_HEADS]
    ckv = _rms(sm[:, 4 * LANES:5 * LANES], ckvg_ref[...], MLA_KV_RANK).astype(BF16)
    kpe = sm[:, 5 * LANES:6 * LANES]

    kn = dot(ckv, wuk_ref)
    mkg = mkg_ref[...]
    pe_rot = rope_m(kpe * mkg)
    ss_pe = jnp.sum(kpe * kpe, axis=-1, keepdims=True)
    for hd in range(MLA_HEADS):
        sl = slice(hd * LANES, (hd + 1) * LANES)
        xh = kn[:, sl]
        ms = (jnp.sum(xh * xh, axis=-1, keepdims=True) + ss_pe) * (1.0 / MLA_QK)
        mk_o[:, sl] = ((xh * mkg + pe_rot) * lax.rsqrt(ms + EPS)).astype(BF16)
    wide_lane = lax.broadcasted_iota(jnp.int32, (1, MLA_HEADS * LANES), 1)
    ones_col = jnp.where(wide_lane % LANES == MLA_V, 1.0, 0.0)
    mv = dot(ckv, wuv_ref) + ones_col
    for hd in range(MLA_HEADS):
        mv_o[0, hd] = mv[:, hd * LANES:(hd + 1) * LANES].T[:MLA_V_ROWS].astype(BF16)

    cq = _rms(dot(h, wcq_ref), cqg_ref[...], MLA_Q_RANK).astype(BF16)
    qb = dot(cq, wuq_ref)
    for hd in range(MLA_HEADS):
        sl = slice(hd * LANES, (hd + 1) * LANES)
        y = rope_m(_rms(qb[:, sl], mqg_ref[...], MLA_QK))
        mq_o[:, sl] = (y * mla_scale).astype(BF16)

    gates = jax.nn.sigmoid(dot(h, wg_ref))
    sgd_o[...] = gates[:, :d_model].astype(BF16)
    sgm_o[...] = gates[:, d_model:].astype(BF16)


def _pad_heads(w, heads, dim):
    k = w.shape[0]
    w = w.reshape(k, heads, dim)
    w = jnp.pad(w, ((0, 0), (0, 0), (0, LANES - dim)))
    return w.reshape(k, heads * LANES)


def _pad_cols(w, width=LANES, offset=0):
    return jnp.pad(w, ((0, 0), (offset, width - offset - w.shape[1])))


def _in_proj(x2, pos2, attn_g, w_in, dsa_q_g, dsa_k_g, cq_g, ckv_g, w_uq, w_ukv, mq_g, mk_g, tm):
    n, d = x2.shape
    sizes = (DSA_HEADS * DSA_HEAD_DIM, DSA_HEAD_DIM, DSA_HEAD_DIM, IDX_HEADS * IDX_DIM, IDX_DIM, IDX_HEADS,
             MLA_Q_RANK, MLA_KV_RANK, MLA_ROPE, d, d)
    offs = [0]
    for s in sizes:
        offs.append(offs[-1] + s)
    seg = [w_in[:, offs[i]:offs[i + 1]] for i in range(len(sizes))]
    w_dq, w_dk, w_dv, w_iq, w_ik, w_iw, w_cq, w_ckv, w_kpe, w_gd, w_gm = seg

    assert DSA_HEADS == IDX_HEADS and DSA_HEAD_DIM + IDX_DIM == LANES
    wqq = jnp.concatenate([w_dq.reshape(d, DSA_HEADS, DSA_HEAD_DIM), w_iq.reshape(d, IDX_HEADS, IDX_DIM)], axis=-1)
    wqq = wqq.reshape(d, DSA_HEADS * LANES).astype(BF16)
    wsm = jnp.concatenate([_pad_cols(w_dk), _pad_cols(w_dv), _pad_cols(w_ik, offset=DSA_HEAD_DIM), _pad_cols(w_iw), w_ckv,
                           _pad_cols(w_kpe, offset=MLA_NOPE)], axis=1).astype(BF16)
    wcq = w_cq.astype(BF16)
    wg = jnp.concatenate([w_gd, w_gm], axis=1).astype(BF16)
    wuq = _pad_heads(w_uq, MLA_HEADS, MLA_QK).astype(BF16)
    ukv = w_ukv.reshape(MLA_KV_RANK, MLA_HEADS, MLA_NOPE + MLA_V)
    wuk = _pad_heads(ukv[:, :, :MLA_NOPE].reshape(MLA_KV_RANK, -1), MLA_HEADS, MLA_NOPE).astype(BF16)
    wuv = _pad_heads(ukv[:, :, MLA_NOPE:].reshape(MLA_KV_RANK, -1), MLA_HEADS, MLA_V).astype(BF16)

    row = lambda v: _pad_cols(v.reshape(1, -1).astype(F32), width=max(LANES, v.size))
    half_d = DSA_HEAD_DIM // 2
    inv_d = ROPE_THETA ** (-jnp.arange(half_d, dtype=F32) / half_d)
    half_m = MLA_ROPE // 2
    inv_m = ROPE_THETA ** (-jnp.arange(half_m, dtype=F32) / half_m)
    freq = _pad_cols(jnp.concatenate([inv_d, inv_m]).reshape(1, -1))

    consts = [attn_g.reshape(1, d), wqq, wsm, wcq, wg, wuq, wuk, wuv, row(dsa_q_g), row(dsa_k_g),
              row(cq_g), row(ckv_g), row(mq_g), row(mk_g), freq]
    wide = DSA_HEADS * LANES
    out_shapes = [
        jax.ShapeDtypeStruct((n, wide), BF16),
        jax.ShapeDtypeStruct((n, LANES), BF16),
        jax.ShapeDtypeStruct((n // tm, DSA_V_ROWS, tm), BF16),
        jax.ShapeDtypeStruct((n, LANES), BF16),
        jax.ShapeDtypeStruct((n // tm, IDX_HEADS, tm), F32),
        jax.ShapeDtypeStruct((n, wide), BF16),
        jax.ShapeDtypeStruct((n, wide), BF16),
        jax.ShapeDtypeStruct((n // tm, MLA_HEADS, MLA_V_ROWS, tm), BF16),
        jax.ShapeDtypeStruct((n, d), BF16),
        jax.ShapeDtypeStruct((n, d), BF16),
    ]
    tile = lambda w: pl.BlockSpec((tm, w), lambda i: (i, 0))
    return pl.pallas_call(
        _in_proj_kernel,
        grid=(n // tm,),
        in_specs=[tile(d), tile(1)] + [_const_spec(c.shape) for c in consts],
        out_specs=[tile(s.shape[1]) if len(s.shape) == 2 else
                   pl.BlockSpec((1,) + s.shape[1:], lambda i, nd=len(s.shape): (i,) + (0,) * (nd - 1))
                   for s in out_shapes],
        out_shape=out_shapes,
        compiler_params=_params(("parallel",)),
        name="in_proj",
    )(x2, pos2, *consts)


def _dsa_kernel(qq_ref, iw_ref, ka_ref, vt_ref, ki_ref, o_ref, score_ref, logit_ref, *, topk):
    i = pl.program_id(1)
    tq = qq_ref.shape[1]
    heads = DSA_HEADS
    nch = (i * tq + tq + KEY_CHUNK - 1) // KEY_CHUNK
    npair = (nch + 1) // 2
    nfull = nch // 2
    odd = nch % 2 == 1
    kf = float(topk)
    shape = (KEY_CHUNK, tq)
    folded = (SUBLANES, tq)

    def fold(a, op):
        return op(a.reshape(KEY_CHUNK // SUBLANES, SUBLANES, tq), axis=0)

    krow = lax.broadcasted_iota(jnp.int32, shape, 0)
    qcol = i * tq + lax.broadcasted_iota(jnp.int32, shape, 1)
    contract_last = (((1,), (1,)), ((), ()))

    def pair_loop(body, init):
        return lax.fori_loop(0, npair, lambda j, carry: body(2 * j, 2 * j + 1, carry), init)

    qq = jnp.concatenate([qq_ref[0, :, hd * LANES:(hd + 1) * LANES] for hd in range(heads)], axis=0)
    iw = iw_ref[0]

    def score_chunk(c, mx, mn):
        kc = ki_ref[0, pl.ds(pl.multiple_of(c * KEY_CHUNK, KEY_CHUNK), KEY_CHUNK), :]
        sc = None
        for hp in range(IDX_HEADS // 2):
            rel = lax.dot_general(kc, qq[2 * hp * tq:(2 * hp + 2) * tq], contract_last,
                                  preferred_element_type=F32)
            rel = jnp.maximum(rel, 0.0)
            part = rel[:, :tq] * iw[2 * hp:2 * hp + 1, :] + rel[:, tq:] * iw[2 * hp + 1:2 * hp + 2, :]
            sc = part if sc is None else sc + part
        causal = (krow + c * KEY_CHUNK) <= qcol
        masked = jnp.where(causal, sc, -jnp.inf)
        score_ref[c] = masked
        return (jnp.maximum(mx, fold(masked, jnp.max)),
                jnp.minimum(mn, fold(jnp.where(causal, sc, jnp.inf), jnp.min)))

    def score_body(j, carry):
        return score_chunk(2 * j + 1, *score_chunk(2 * j, *carry))

    def score_tail(carry):
        score_ref[nch] = jnp.full(shape, -jnp.inf, F32)
        return score_chunk(nch - 1, *carry)

    carry = lax.fori_loop(0, nfull, score_body, (jnp.full(folded, -jnp.inf, F32), jnp.full(folded, jnp.inf, F32)))
    mx, mn = lax.cond(odd, score_tail, lambda cr: cr, carry)
    hi = jnp.max(mx, axis=0, keepdims=True)
    lo = jnp.min(mn, axis=0, keepdims=True)

    def count_gt(t):
        def body(c0, c1, acc):
            for c in (c0, c1):
                acc = acc + fold(jnp.where(score_ref[c] > t, 1.0, 0.0), jnp.sum)
            return acc
        return jnp.sum(pair_loop(body, jnp.zeros(folded, F32)), axis=0, keepdims=True)

    def bisect(_, carry):
        lo, hi = carry
        mid = 0.5 * (lo + hi)
        below = count_gt(mid) < kf
        return jnp.where(below, lo, mid), jnp.where(below, mid, hi)

    lo, hi = lax.fori_loop(0, BISECT_ITERS, bisect, (lo, hi))

    def max_le(t):
        def body(c0, c1, acc):
            for c in (c0, c1):
                s = score_ref[c]
                acc = jnp.maximum(acc, fold(jnp.where(s <= t, s, -jnp.inf), jnp.max))
            return acc
        return jnp.max(pair_loop(body, jnp.full(folded, -jnp.inf, F32)), axis=0, keepdims=True)

    n_causal = (i * tq + lax.broadcasted_iota(jnp.int32, (1, tq), 1) + 1).astype(F32)
    small = n_causal <= kf

    def refine_cond(carry):
        return jnp.min(carry[2]) < 0.5

    def refine_body(carry):
        m, thr, done, thr_cnt = carry

        def body(c0, c1, acc):
            cnt, nxt = acc
            for c in (c0, c1):
                s = score_ref[c]
                cnt = cnt + fold(jnp.where(s >= m, 1.0, 0.0), jnp.sum)
                nxt = jnp.maximum(nxt, fold(jnp.where(s < m, s, -jnp.inf), jnp.max))
            return cnt, nxt

        cnt, nxt = pair_loop(body, (jnp.zeros(folded, F32), jnp.full(folded, -jnp.inf, F32)))
        cnt = jnp.sum(cnt, axis=0, keepdims=True)
        nxt = jnp.max(nxt, axis=0, keepdims=True)
        reached = cnt >= kf
        hit = jnp.logical_and(reached, done < 0.5)
        return nxt, jnp.where(hit, m, thr), jnp.where(reached, 1.0, done), jnp.where(hit, cnt, thr_cnt)

    init = (max_le(hi), jnp.full((1, tq), -jnp.inf, F32), jnp.where(small, 1.0, 0.0), jnp.zeros((1, tq), F32))
    _, thr, _, thr_cnt = lax.while_loop(refine_cond, refine_body, init)

    def mask_with_ties():
        quota = jnp.where(small, 0.0, kf - count_gt(thr))
        earlier = (lax.broadcasted_iota(jnp.int32, (KEY_CHUNK, KEY_CHUNK), 1)
                   < lax.broadcasted_iota(jnp.int32, (KEY_CHUNK, KEY_CHUNK), 0)).astype(BF16)

        def body(c0, c1, seen):
            for c in (c0, c1):
                s = score_ref[c]
                eq = s == thr
                eqf = jnp.where(eq, 1.0, 0.0)
                before = jnp.dot(earlier, eqf.astype(BF16), preferred_element_type=F32) + seen
                sel = jnp.logical_or(s > thr, jnp.logical_and(eq, before < quota))
                seen = seen + jnp.sum(eqf, axis=0, keepdims=True)
                score_ref[c] = jnp.where(sel, jnp.inf, -jnp.inf)
            return seen
        pair_loop(body, jnp.zeros((1, tq), F32))

    surplus = jnp.max(jnp.where(small, 0.0, thr_cnt - kf)) > 0.5
    lax.cond(surplus, mask_with_ties, lambda: None)

    def qk_into(slot, c):
        kc = ka_ref[0, pl.ds(pl.multiple_of(c * KEY_CHUNK, KEY_CHUNK), KEY_CHUNK), :]
        logit_ref[slot] = lax.dot_general(kc, qq, contract_last, preferred_element_type=F32)

    def softmax_pv(slot, c, m_run, acc):
        s = score_ref[c]
        bias = jnp.where(jnp.logical_and(s >= thr, s > -jnp.inf), 0.0, NEG_BIG)
        vt = vt_ref[0, c]
        new_m, new_acc = [], []
        for hp in range(heads // 2):
            probs, alphas = [], []
            for hd in (2 * hp, 2 * hp + 1):
                sl = slice(hd * tq, (hd + 1) * tq)
                lh = logit_ref[slot, :, sl] + bias
                m_old = m_run[:, sl]
                m_new = jnp.maximum(m_old, jnp.max(lh, axis=0, keepdims=True))
                probs.append(jnp.exp2(lh - m_new).astype(BF16))
                alphas.append(jnp.exp2(m_old - m_new))
                new_m.append(m_new)
            pv = jnp.dot(vt, jnp.concatenate(probs, axis=1), preferred_element_type=F32)
            new_acc.append(jnp.concatenate(alphas, axis=1) * acc[:, 2 * hp * tq:(2 * hp + 2) * tq] + pv)
        return jnp.concatenate(new_m, axis=1), jnp.concatenate(new_acc, axis=1)

    def attn_body(j, carry):
        m_run, acc = carry
        qk_into(1, 2 * j + 1)
        m_run, acc = softmax_pv(0, 2 * j, m_run, acc)
        qk_into(0, jnp.minimum(2 * j + 2, nch - 1))
        return softmax_pv(1, 2 * j + 1, m_run, acc)

    qk_into(0, 0)
    d_rows = vt_ref.shape[2]
    init = (jnp.full((1, heads * tq), NEG_BIG, F32), jnp.zeros((d_rows, heads * tq), F32))
    carry = lax.fori_loop(0, nfull, attn_body, init)
    _, acc = lax.cond(odd, lambda cr: softmax_pv(0, nch - 1, *cr), lambda cr: cr, carry)
    for hp in range(heads // 2):
        halves = []
        for hd in (2 * hp, 2 * hp + 1):
            blk = acc[:, hd * tq:(hd + 1) * tq]
            halves.append(blk[:DSA_HEAD_DIM] / blk[DSA_HEAD_DIM:DSA_HEAD_DIM + 1, :])
        o_ref[0, hp] = jnp.concatenate(halves, axis=0).T.astype(o_ref.dtype)


def _dsa_attention(qq, iw_t, ka, va_t, ki, topk):
    b, s, wide = qq.shape
    tq = Q_TILE_DSA
    nkc = s // KEY_CHUNK
    assert iw_t.shape[-1] == tq and va_t.shape[-1] == KEY_CHUNK
    v_t = va_t.reshape(b, nkc, DSA_V_ROWS, KEY_CHUNK)
    qspec = pl.BlockSpec((1, tq, wide), lambda bi, i: (bi, i, 0))
    kspec = pl.BlockSpec((1, s, LANES), lambda bi, i: (bi, 0, 0))
    return pl.pallas_call(
        functools.partial(_dsa_kernel, topk=topk),
        grid=(b, s // tq),
        in_specs=[qspec, pl.BlockSpec((1, IDX_HEADS, tq), lambda bi, i: (bi * (s // tq) + i, 0, 0)), kspec,
                  pl.BlockSpec((1, nkc, DSA_V_ROWS, KEY_CHUNK), lambda bi, i: (bi, 0, 0, 0)), kspec],
        out_specs=pl.BlockSpec((1, DSA_HEADS // 2, tq, LANES), lambda bi, i: (bi, 0, i, 0)),
        out_shape=jax.ShapeDtypeStruct((b, DSA_HEADS // 2, s, LANES), BF16),
        scratch_shapes=[pltpu.VMEM((nkc, KEY_CHUNK, tq), F32),
                        pltpu.VMEM((2, KEY_CHUNK, DSA_HEADS * tq), F32)],
        compiler_params=_params(("parallel", "arbitrary")),
        name="dsa_attn",
    )(qq, iw_t, ka, v_t, ki)


def _mla_kernel(q_ref, k_ref, vt_ref, o_ref, logit_ref):
    i = pl.program_id(2)
    tq = q_ref.shape[1]
    group = q_ref.shape[2] // LANES
    d_rows = vt_ref.shape[3]
    per_block = tq // vt_ref.shape[4]
    qs = [q_ref[0, :, g * LANES:(g + 1) * LANES] for g in range(group)]
    contract_last = (((1,), (1,)), ((), ()))

    def qk_into(slot, c):
        k0 = pl.multiple_of(c * tq, tq)
        for g in range(group):
            logit_ref[slot, g] = lax.dot_general(k_ref[0, pl.ds(k0, tq), g * LANES:(g + 1) * LANES], qs[g],
                                                 contract_last, preferred_element_type=F32)

    def softmax_pv(slot, c, carry, masked):
        if masked:
            keep = (lax.broadcasted_iota(jnp.int32, (tq, tq), 0) <= lax.broadcasted_iota(jnp.int32, (tq, tq), 1))
            bias = jnp.where(keep, 0.0, NEG_BIG)
        probs, alphas, maxes = [], [], []
        for g in range(group):
            m_run = carry[2 * g]
            logit = logit_ref[slot, g]
            if masked:
                logit = logit + bias
            m_new = jnp.maximum(m_run, jnp.max(logit, axis=0, keepdims=True))
            probs.append(jnp.exp2(logit - m_new).astype(BF16))
            alphas.append(jnp.exp2(m_run - m_new))
            maxes.append(m_new)
        res = []
        for g in range(group):
            vt = jnp.concatenate([vt_ref[0, c * per_block + t, g] for t in range(per_block)], axis=1)
            pv = jnp.dot(vt, probs[g], preferred_element_type=F32)
            res += [maxes[g], alphas[g] * carry[2 * g + 1] + pv]
        return tuple(res)

    def pair_body(j, carry):
        qk_into(1, 2 * j + 1)
        carry = softmax_pv(0, 2 * j, carry, False)
        qk_into(0, 2 * j + 2)
        return softmax_pv(1, 2 * j + 1, carry, False)

    def tail_even(carry):
        return softmax_pv(0, i, carry, True)

    def tail_odd(carry):
        qk_into(1, i)
        carry = softmax_pv(0, i - 1, carry, False)
        return softmax_pv(1, i, carry, True)

    qk_into(0, 0)
    init = (jnp.full((1, tq), NEG_BIG, F32), jnp.zeros((d_rows, tq), F32)) * group
    carry = lax.fori_loop(0, i // 2, pair_body, init)
    carry = lax.cond(i % 2 == 0, tail_even, tail_odd, carry)
    for gp in range(group // 2):
        halves = [carry[2 * g + 1][:MLA_V] / carry[2 * g + 1][MLA_V:MLA_V + 1, :] for g in (2 * gp, 2 * gp + 1)]
        o_ref[0, :, gp * LANES:(gp + 1) * LANES] = jnp.concatenate(halves, axis=0).T.astype(o_ref.dtype)


def _mla_attention(mq, mk, mv_t):
    b, s, wide = mq.shape
    tq = Q_TILE_MLA
    nkc = s // tq
    gw = MLA_HEAD_GROUP * LANES
    nt, _, _, tm = mv_t.shape
    nt //= b
    assert tq % tm == 0
    v_t = mv_t.reshape(b, nt, MLA_HEADS, MLA_V_ROWS, tm)
    return pl.pallas_call(
        _mla_kernel,
        grid=(b, MLA_HEADS // MLA_HEAD_GROUP, s // tq),
        in_specs=[pl.BlockSpec((1, tq, gw), lambda bi, h, i: (bi, i, h)),
                  pl.BlockSpec((1, s, gw), lambda bi, h, i: (bi, 0, h)),
                  pl.BlockSpec((1, nt, MLA_HEAD_GROUP, MLA_V_ROWS, tm), lambda bi, h, i: (bi, 0, h, 0, 0))],
        out_specs=pl.BlockSpec((1, tq, gw // 2), lambda bi, h, i: (bi, i, h)),
        out_shape=jax.ShapeDtypeStruct((b, s, MLA_HEADS * MLA_V), BF16),
        scratch_shapes=[pltpu.VMEM((2, MLA_HEAD_GROUP, tq, tq), F32)],
        compiler_params=_params(("parallel", "parallel", "arbitrary")),
        name="mla_attn",
    )(mq, mk, v_t)


def _mem_kv_kernel(mem_ref, g_ref, wkv_ref, kg_ref, k_o, v_o):
    m = mem_ref[0]
    hm = _rms(m, g_ref[...], m.shape[-1]).astype(BF16)
    kv = jnp.dot(hm, wkv_ref[...], preferred_element_type=F32)
    for hd in range(MEM_HEADS):
        sl = slice(hd * LANES, (hd + 1) * LANES)
        k_o[0, :, sl] = _rms(kv[:, sl], kg_ref[...], MEM_HEAD_DIM).astype(BF16)
    v_o[0] = kv[:, MEM_HEADS * LANES:].astype(BF16)


def _mem_kv(mem, mem_g, w_kv, k_g):
    b, m, d = mem.shape
    hw = MEM_HEADS * MEM_HEAD_DIM
    wkv = jnp.concatenate([_pad_heads(w_kv[:, :hw], MEM_HEADS, MEM_HEAD_DIM),
                           _pad_heads(w_kv[:, hw:], MEM_HEADS, MEM_HEAD_DIM)], axis=1).astype(BF16)
    consts = [mem_g.reshape(1, d), wkv, _pad_cols(k_g.reshape(1, -1))]
    wide = MEM_HEADS * LANES
    spec = pl.BlockSpec((1, m, wide), lambda bi: (bi, 0, 0))
    return pl.pallas_call(
        _mem_kv_kernel,
        grid=(b,),
        in_specs=[pl.BlockSpec((1, m, d), lambda bi: (bi, 0, 0))] + [_const_spec(c.shape) for c in consts],
        out_specs=[spec, spec],
        out_shape=[jax.ShapeDtypeStruct((b, m, wide), BF16)] * 2,
        compiler_params=_params(("parallel",)),
        name="mem_kv",
    )(mem, *consts)


def _split_dot(a, w):
    a_hi = a.astype(BF16)
    a_lo = (a - a_hi.astype(F32)).astype(BF16)
    w_hi = w.astype(BF16)
    w_lo = (w - w_hi.astype(F32)).astype(BF16)
    w_both = jnp.concatenate([w_hi, w_lo], axis=1)
    n = w.shape[1]
    hi = jnp.dot(a_hi, w_both, preferred_element_type=F32)
    lo = jnp.dot(a_lo, w_both, preferred_element_type=F32)
    return (hi[:, :n] + hi[:, n:]) + (lo[:, :n] + lo[:, n:])


def _merge_kernel(oa_ref, ob_ref, sgd_ref, sgm_ref, x_ref, wa_ref, wb_ref, wo_ref, mxg_ref, wq_ref, qg_ref,
                  km_ref, vm_ref, wmo_ref, moeg_ref, wr_ref, bias_ref, x2_o, h3_o, comb_o):
    d = functools.partial(jnp.dot, preferred_element_type=F32)
    oa = jnp.concatenate([oa_ref[0, hp] for hp in range(DSA_HEADS // 2)], axis=-1)
    merged = (sgd_ref[0].astype(F32) * d(oa, wa_ref[...]) + sgm_ref[0].astype(F32) * d(ob_ref[0], wb_ref[...]))
    x1 = x_ref[0] + d(merged.astype(BF16), wo_ref[...])

    d_model = x1.shape[-1]
    h2 = _rms(x1, mxg_ref[...], d_model).astype(BF16)
    qm = d(h2, wq_ref[...])
    outs = []
    for hd in range(MEM_HEADS):
        sl = slice(hd * LANES, (hd + 1) * LANES)
        q = (_rms(qm[:, sl], qg_ref[...], MEM_HEAD_DIM) * MEM_HEAD_DIM ** -0.5).astype(BF16)
        logit = lax.dot_general(q, km_ref[0, :, sl], (((1,), (1,)), ((), ())), preferred_element_type=F32)
        logit = logit - jnp.max(logit, axis=-1, keepdims=True)
        p = jnp.exp(logit)
        pv = d(p.astype(BF16), vm_ref[0, :, sl])
        outs.append((pv / jnp.sum(p, axis=-1, keepdims=True)).astype(BF16))
    x2 = x1 + d(jnp.concatenate(outs, axis=-1), wmo_ref[...])
    x2_o[0] = x2

    h3 = _rms(x2, moeg_ref[...], d_model)
    h3_o[0] = h3.astype(BF16)
    logits = _split_dot(h3, wr_ref[...])
    tm = logits.shape[0]
    lane = lax.broadcasted_iota(jnp.int32, (tm, LANES), 1)
    big = jnp.int32(LANES)
    is_grp = jnp.logical_and(lane >= N_EXPERTS, lane < N_EXPERTS + N_GROUPS)
    glog = jnp.where(is_grp, logits, -jnp.inf)
    gmax = jnp.max(glog, axis=-1, keepdims=True)
    g_sel = jnp.min(jnp.where(glog == gmax, lane, big), axis=-1, keepdims=True) - N_EXPERTS
    p_g = 1.0 / jnp.sum(jnp.where(is_grp, jnp.exp(logits - gmax), 0.0), axis=-1, keepdims=True)
    aff = jax.nn.sigmoid(logits)
    in_grp = jnp.logical_and(lane >= g_sel * EXPERTS_PER_GROUP, lane < (g_sel + 1) * EXPERTS_PER_GROUP)
    val = jnp.where(in_grp, aff + bias_ref[...], -jnp.inf)
    m1 = jnp.max(val, axis=-1, keepdims=True)
    i1 = jnp.min(jnp.where(val == m1, lane, big), axis=-1, keepdims=True)
    val2 = jnp.where(lane == i1, -jnp.inf, val)
    m2 = jnp.max(val2, axis=-1, keepdims=True)
    i2 = jnp.min(jnp.where(val2 == m2, lane, big), axis=-1, keepdims=True)
    chosen = jnp.logical_or(lane == i1, lane == i2)
    a_sel = jnp.where(chosen, aff, 0.0)
    comb_o[0] = p_g * a_sel / jnp.sum(a_sel, axis=-1, keepdims=True)


def _merge_mem_router(oa, ob, sgd, sgm, x, w_br_dsa, w_br_mla, w_out, mem_x_g, mem_w_q, mem_q_g, km, vm,
                      mem_w_o, moe_g, w_group, w_expert, expert_bias, tm):
    b, s, d = x.shape
    wa = w_br_dsa.astype(BF16)
    wb = w_br_mla.astype(BF16)
    wq = _pad_heads(mem_w_q, MEM_HEADS, MEM_HEAD_DIM).astype(BF16)
    wmo = jnp.pad(mem_w_o.reshape(MEM_HEADS, MEM_HEAD_DIM, d), ((0, 0), (0, LANES - MEM_HEAD_DIM), (0, 0)))
    wmo = wmo.reshape(MEM_HEADS * LANES, d).astype(BF16)
    wr = _pad_cols(jnp.concatenate([w_expert, w_group], axis=1).astype(F32))
    bias = _pad_cols(expert_bias.reshape(1, -1).astype(F32))
    consts_a = [wa, wb, w_out.astype(BF16), mem_x_g.reshape(1, d), wq, _pad_cols(mem_q_g.reshape(1, -1))]
    consts_b = [wmo, moe_g.reshape(1, d), wr, bias]
    m = km.shape[1]
    mw = MEM_HEADS * LANES
    tok = lambda w: pl.BlockSpec((1, tm, w), lambda bi, i: (bi, i, 0))
    memspec = pl.BlockSpec((1, m, mw), lambda bi, i: (bi, 0, 0))
    return pl.pallas_call(
        _merge_kernel,
        grid=(b, s // tm),
        in_specs=[pl.BlockSpec((1, DSA_HEADS // 2, tm, LANES), lambda bi, i: (bi, 0, i, 0)),
                  tok(MLA_HEADS * MLA_V), tok(d), tok(d), tok(d)]
                 + [_const_spec(c.shape) for c in consts_a] + [memspec, memspec]
                 + [_const_spec(c.shape) for c in consts_b],
        out_specs=[tok(d), tok(d), tok(LANES)],
        out_shape=[jax.ShapeDtypeStruct((b, s, d), F32), jax.ShapeDtypeStruct((b, s, d), BF16),
                   jax.ShapeDtypeStruct((b, s, LANES), F32)],
        compiler_params=_params(("parallel", "parallel")),
        name="merge_mem",
    )(oa, ob, sgd, sgm, x, *consts_a, km, vm, *consts_b)


def _moe_kernel(h_ref, comb_ref, x_ref, wg_ref, wu_ref, wd_ref, o_ref):
    step = pl.program_id(1)
    per = wg_ref.shape[0]
    h = h_ref[...]
    comb = comb_ref[...]
    lane = lax.broadcasted_iota(jnp.int32, comb.shape, 1)
    parts = []
    for e in range(per):
        gate = jnp.dot(h, wg_ref[e], preferred_element_type=F32)
        up = jnp.dot(h, wu_ref[e], preferred_element_type=F32)
        c_e = jnp.sum(jnp.where(lane == step * per + e, comb, 0.0), axis=-1, keepdims=True)
        parts.append(((jax.nn.silu(gate) * up) * c_e).astype(BF16))
    w_down = wd_ref[...].reshape(per * D_EXPERT, wd_ref.shape[-1])
    contrib = jnp.dot(jnp.concatenate(parts, axis=1), w_down, preferred_element_type=F32)

    @pl.when(step == 0)
    def _():
        o_ref[...] = x_ref[...] + contrib

    @pl.when(step > 0)
    def _():
        o_ref[...] += contrib


def _moe(h3, comb, x2, w_gate, w_up, w_down, tm):
    n, d = x2.shape
    per = MOE_EXPERTS_PER_STEP
    return pl.pallas_call(
        _moe_kernel,
        grid=(n // tm, N_EXPERTS // per),
        in_specs=[pl.BlockSpec((tm, d), lambda i, g: (i, 0)),
                  pl.BlockSpec((tm, LANES), lambda i, g: (i, 0)),
                  pl.BlockSpec((tm, d), lambda i, g: (i, 0)),
                  pl.BlockSpec((per, d, D_EXPERT), lambda i, g: (g, 0, 0)),
                  pl.BlockSpec((per, d, D_EXPERT), lambda i, g: (g, 0, 0)),
                  pl.BlockSpec((per, D_EXPERT, d), lambda i, g: (g, 0, 0))],
        out_specs=pl.BlockSpec((tm, d), lambda i, g: (i, 0)),
        out_shape=jax.ShapeDtypeStruct((n, d), F32),
        compiler_params=_params(("parallel", "arbitrary")),
        name="moe",
    )(h3, comb, x2, w_gate.astype(BF16), w_up.astype(BF16), w_down.astype(BF16))


def _layer(x, mem, positions, p):
    b, s, d = x.shape
    n = b * s
    assert s % (2 * KEY_CHUNK) == 0 and s % Q_TILE_MLA == 0
    topk = min(TOPK_MAX, s // 4)
    tm = min(TOKENS_PER_STEP_PROJ, s)
    qq, ka, va, ki, iw, mq, mk, mv, sgd, sgm = _in_proj(
        x.reshape(n, d), positions.reshape(n, 1), p["attn_norm_g"], p["w_in"], p["dsa_q_norm_g"],
        p["dsa_k_norm_g"], p["mla_cq_norm_g"], p["mla_ckv_norm_g"], p["mla_w_uq"], p["mla_w_ukv"],
        p["mla_q_norm_g"], p["mla_k_norm_g"], tm)
    r3 = lambda a: a.reshape(b, s, a.shape[-1])
    oa = _dsa_attention(r3(qq), iw, r3(ka), va, r3(ki), topk)
    ob = _mla_attention(r3(mq), r3(mk), mv)
    km, vm = _mem_kv(mem, p["mem_norm_g"], p["mem_w_kv"], p["mem_k_norm_g"])
    x2, h3, comb = _merge_mem_router(
        oa, ob, r3(sgd), r3(sgm), x, p["w_branch_dsa"], p["w_branch_mla"], p["w_out"], p["mem_x_norm_g"],
        p["mem_w_q"], p["mem_q_norm_g"], km, vm, p["mem_w_o"], p["moe_norm_g"], p["moe_w_group"],
        p["moe_w_expert"], p["moe_expert_bias"], min(TOKENS_PER_STEP_MERGE, s))
    out = _moe(h3.reshape(n, d), comb.reshape(n, LANES), x2.reshape(n, d), p["moe_w_gate"], p["moe_w_up"],
               p["moe_w_down"], min(TOKENS_PER_STEP_MOE, n))
    return out.reshape(b, s, d)


_PARAM_NAMES = ("attn_norm_g", "w_in", "dsa_q_norm_g", "dsa_k_norm_g", "mla_cq_norm_g", "mla_ckv_norm_g",
                "mla_w_uq", "mla_w_ukv", "mla_q_norm_g", "mla_k_norm_g", "w_branch_dsa", "w_branch_mla", "w_out",
                "mem_x_norm_g", "mem_norm_g", "mem_w_q", "mem_w_kv", "mem_q_norm_g", "mem_k_norm_g", "mem_w_o",
                "moe_norm_g", "moe_w_group", "moe_w_expert", "moe_expert_bias", "moe_w_gate", "moe_w_up",
                "moe_w_down")


def kernel(x, mem, positions, attn_norm_g, w_in, dsa_q_norm_g, dsa_k_norm_g, mla_cq_norm_g, mla_ckv_norm_g, mla_w_uq, mla_w_ukv, mla_q_norm_g, mla_k_norm_g, w_branch_dsa, w_branch_mla, w_out, mem_x_norm_g, mem_norm_g, mem_w_q, mem_w_kv, mem_q_norm_g, mem_k_norm_g, mem_w_o, moe_norm_g, moe_w_group, moe_w_expert, moe_expert_bias, moe_w_gate, moe_w_up, moe_w_down):
    stacked = (attn_norm_g, w_in, dsa_q_norm_g, dsa_k_norm_g, mla_cq_norm_g, mla_ckv_norm_g, mla_w_uq, mla_w_ukv,
               mla_q_norm_g, mla_k_norm_g, w_branch_dsa, w_branch_mla, w_out, mem_x_norm_g, mem_norm_g, mem_w_q,
               mem_w_kv, mem_q_norm_g, mem_k_norm_g, mem_w_o, moe_norm_g, moe_w_group, moe_w_expert,
               moe_expert_bias, moe_w_gate, moe_w_up, moe_w_down)
    for layer in range(attn_norm_g.shape[0]):
        p = {name: arr[layer] for name, arr in zip(_PARAM_NAMES, stacked)}
        x = _layer(x, mem, positions, p)
    return x
```

```python
import functools

import jax
import jax.numpy as jnp
from jax import lax
from jax.experimental import pallas as pl
from jax.experimental.pallas import tpu as pltpu

F32 = jnp.float32
BF16 = jnp.bfloat16

LANES = 128
SUBLANES = 8
VMEM_LIMIT = 56 * 1024 * 1024

ROPE_THETA = 10000.0
EPS = 1e-6
DSA_HEADS = 8
DSA_HEAD_DIM = 64
IDX_HEADS = 8
IDX_DIM = 64
TOPK_MAX = 256
MLA_HEADS = 8
MLA_NOPE = 64
MLA_ROPE = 32
MLA_QK = MLA_NOPE + MLA_ROPE
MLA_V = 64
MLA_Q_RANK = 256
MLA_KV_RANK = 128
MEM_HEADS = 4
MEM_HEAD_DIM = 64
N_GROUPS = 4
EXPERTS_PER_GROUP = 4
N_EXPERTS = N_GROUPS * EXPERTS_PER_GROUP
D_EXPERT = 256

NEG_BIG = -1e30
KEY_CHUNK = 256
Q_TILE_DSA = 512
DSA_V_ROWS = 80
MLA_V_ROWS = 80
Q_TILE_MLA = 512
MLA_HEAD_GROUP = 4
LOG2_E = 1.4426950408889634
BISECT_ITERS = 15
TOKENS_PER_STEP_PROJ = 256
TOKENS_PER_STEP_MERGE = 512
TOKENS_PER_STEP_MOE = 1024
MOE_EXPERTS_PER_STEP = 8


def _const_spec(shape):
    nd = len(shape)
    return pl.BlockSpec(shape, lambda *_: (0,) * nd)


def _params(sem):
    return pltpu.CompilerParams(dimension_semantics=sem, vmem_limit_bytes=VMEM_LIMIT)


def _rms(x, g, n):
    ms = jnp.sum(x * x, axis=-1, keepdims=True) * (1.0 / n)
    return (x * lax.rsqrt(ms + EPS)) * g


def _rope(x, cos, sin_signed, lo_mask, half):
    fwd = pltpu.roll(x, LANES - half, 1)
    bwd = pltpu.roll(x, half, 1)
    return x * cos + jnp.where(lo_mask, fwd, bwd) * sin_signed


def _in_proj_kernel(x_ref, pos_ref, g_ref, wqq_ref, wsm_ref, wcq_ref, wg_ref, wuq_ref, wuk_ref,
                    wuv_ref, qag_ref, kag_ref, cqg_ref, ckvg_ref, mqg_ref, mkg_ref, freq_ref,
                    qq_o, ka_o, va_o, ki_o, iw_o, mq_o, mk_o, mv_o, sgd_o, sgm_o):
    x = x_ref[...]
    d_model = x.shape[-1]
    h = _rms(x, g_ref[...], d_model).astype(BF16)

    pos = pos_ref[...].astype(F32)
    lane = lax.broadcasted_iota(jnp.int32, (x.shape[0], LANES), 1)
    half_d, half_m = DSA_HEAD_DIM // 2, MLA_ROPE // 2
    ang = pos * freq_ref[...]
    lo_d = (lane & (DSA_HEAD_DIM - 1)) < half_d
    lo_m = lane < MLA_NOPE + half_m
    in_m_lo = jnp.logical_and(lane >= MLA_NOPE, lo_m)
    in_m_hi = jnp.logical_and(lane >= MLA_NOPE + half_m, lane < MLA_QK)

    def tables(t, fill):
        by_half = pltpu.roll(t, half_d, 1)
        head = jnp.where(lane < half_d, t, by_half)
        dsa = jnp.where(lane < DSA_HEAD_DIM, head, pltpu.roll(head, DSA_HEAD_DIM, 1))
        mla = jnp.where(in_m_lo, by_half, jnp.where(in_m_hi, pltpu.roll(t, half_d + half_m, 1), fill))
        return dsa, mla

    cos_d, cos_m = tables(jnp.cos(ang), 1.0)
    sin_d, sin_m = tables(jnp.sin(ang), 0.0)
    sin_d = jnp.where(lo_d, -sin_d, sin_d)
    sin_m = jnp.where(lo_m, -sin_m, sin_m)
    rope_d = functools.partial(_rope, cos=cos_d, sin_signed=sin_d, lo_mask=lo_d, half=DSA_HEAD_DIM // 2)
    rope_m = functools.partial(_rope, cos=cos_m, sin_signed=sin_m, lo_mask=lo_m, half=MLA_ROPE // 2)

    def dot(a, w_ref):
        return jnp.dot(a, w_ref[...], preferred_element_type=F32)

    att_scale = DSA_HEAD_DIM ** -0.5 * LOG2_E
    idx_scale = IDX_DIM ** -0.5 * IDX_HEADS ** -0.5
    mla_scale = MLA_QK ** -0.5 * LOG2_E

    qq = dot(h, wqq_ref)
    is_qa = lane < DSA_HEAD_DIM
    for hd in range(DSA_HEADS):
        sl = slice(hd * LANES, (hd + 1) * LANES)
        xh = qq[:, sl]
        ms = jnp.sum(jnp.where(is_qa, xh * xh, 0.0), axis=-1, keepdims=True) * (1.0 / DSA_HEAD_DIM)
        mult = jnp.where(is_qa, (lax.rsqrt(ms + EPS) * att_scale) * qag_ref[...], 1.0)
        qq_o[:, sl] = rope_d(xh * mult).astype(BF16)

    sm = dot(h, wsm_ref)
    ka_o[...] = rope_d(_rms(sm[:, 0:LANES], kag_ref[...], DSA_HEAD_DIM)).astype(BF16)
    va = sm[:, LANES:2 * LANES] + jnp.where(lane[0:1] == DSA_HEAD_DIM, 1.0, 0.0)
    va_o[0] = va.T[:DSA_V_ROWS].astype(BF16)
    ki_o[...] = rope_d(sm[:, 2 * LANES:3 * LANES]).astype(BF16)
    iw_o[0] = (sm[:, 3 * LANES:4 * LANES] * idx_scale).T[:IDX_HEADS]
    ckv = _rms(sm[:, 4 * LANES:5 * LANES], ckvg_ref[...], MLA_KV_RANK).astype(BF16)
    kpe = sm[:, 5 * LANES:6 * LANES]

    kn = dot(ckv, wuk_ref)
    mkg = mkg_ref[...]
    pe_rot = rope_m(kpe * mkg)
    ss_pe = jnp.sum(kpe * kpe, axis=-1, keepdims=True)
    for hd in range(MLA_HEADS):
        sl = slice(hd * LANES, (hd + 1) * LANES)
        xh = kn[:, sl]
        ms = (jnp.sum(xh * xh, axis=-1, keepdims=True) + ss_pe) * (1.0 / MLA_QK)
        mk_o[:, sl] = ((xh * mkg + pe_rot) * lax.rsqrt(ms + EPS)).astype(BF16)
    wide_lane = lax.broadcasted_iota(jnp.int32, (1, MLA_HEADS * LANES), 1)
    ones_col = jnp.where(wide_lane % LANES == MLA_V, 1.0, 0.0)
    mv = dot(ckv, wuv_ref) + ones_col
    for hd in range(MLA_HEADS):
        mv_o[0, hd] = mv[:, hd * LANES:(hd + 1) * LANES].T[:MLA_V_ROWS].astype(BF16)

    cq = _rms(dot(h, wcq_ref), cqg_ref[...], MLA_Q_RANK).astype(BF16)
    qb = dot(cq, wuq_ref)
    for hd in range(MLA_HEADS):
        sl = slice(hd * LANES, (hd + 1) * LANES)
        y = rope_m(_rms(qb[:, sl], mqg_ref[...], MLA_QK))
        mq_o[:, sl] = (y * mla_scale).astype(BF16)

    gates = jax.nn.sigmoid(dot(h, wg_ref))
    sgd_o[...] = gates[:, :d_model].astype(BF16)
    sgm_o[...] = gates[:, d_model:].astype(BF16)


def _pad_heads(w, heads, dim):
    k = w.shape[0]
    w = w.reshape(k, heads, dim)
    w = jnp.pad(w, ((0, 0), (0, 0), (0, LANES - dim)))
    return w.reshape(k, heads * LANES)


def _pad_cols(w, width=LANES, offset=0):
    return jnp.pad(w, ((0, 0), (offset, width - offset - w.shape[1])))


def _in_proj(x2, pos2, attn_g, w_in, dsa_q_g, dsa_k_g, cq_g, ckv_g, w_uq, w_ukv, mq_g, mk_g, tm):
    n, d = x2.shape
    sizes = (DSA_HEADS * DSA_HEAD_DIM, DSA_HEAD_DIM, DSA_HEAD_DIM, IDX_HEADS * IDX_DIM, IDX_DIM, IDX_HEADS,
             MLA_Q_RANK, MLA_KV_RANK, MLA_ROPE, d, d)
    offs = [0]
    for s in sizes:
        offs.append(offs[-1] + s)
    seg = [w_in[:, offs[i]:offs[i + 1]] for i in range(len(sizes))]
    w_dq, w_dk, w_dv, w_iq, w_ik, w_iw, w_cq, w_ckv, w_kpe, w_gd, w_gm = seg

    assert DSA_HEADS == IDX_HEADS and DSA_HEAD_DIM + IDX_DIM == LANES
    wqq = jnp.concatenate([w_dq.reshape(d, DSA_HEADS, DSA_HEAD_DIM), w_iq.reshape(d, IDX_HEADS, IDX_DIM)], axis=-1)
    wqq = wqq.reshape(d, DSA_HEADS * LANES).astype(BF16)
    wsm = jnp.concatenate([_pad_cols(w_dk), _pad_cols(w_dv), _pad_cols(w_ik, offset=DSA_HEAD_DIM), _pad_cols(w_iw), w_ckv,
                           _pad_cols(w_kpe, offset=MLA_NOPE)], axis=1).astype(BF16)
    wcq = w_cq.astype(BF16)
    wg = jnp.concatenate([w_gd, w_gm], axis=1).astype(BF16)
    wuq = _pad_heads(w_uq, MLA_HEADS, MLA_QK).astype(BF16)
    ukv = w_ukv.reshape(MLA_KV_RANK, MLA_HEADS, MLA_NOPE + MLA_V)
    wuk = _pad_heads(ukv[:, :, :MLA_NOPE].reshape(MLA_KV_RANK, -1), MLA_HEADS, MLA_NOPE).astype(BF16)
    wuv = _pad_heads(ukv[:, :, MLA_NOPE:].reshape(MLA_KV_RANK, -1), MLA_HEADS, MLA_V).astype(BF16)

    row = lambda v: _pad_cols(v.reshape(1, -1).astype(F32), width=max(LANES, v.size))
    half_d = DSA_HEAD_DIM // 2
    inv_d = ROPE_THETA ** (-jnp.arange(half_d, dtype=F32) / half_d)
    half_m = MLA_ROPE // 2
    inv_m = ROPE_THETA ** (-jnp.arange(half_m, dtype=F32) / half_m)
    freq = _pad_cols(jnp.concatenate([inv_d, inv_m]).reshape(1, -1))

    consts = [attn_g.reshape(1, d), wqq, wsm, wcq, wg, wuq, wuk, wuv, row(dsa_q_g), row(dsa_k_g),
              row(cq_g), row(ckv_g), row(mq_g), row(mk_g), freq]
    wide = DSA_HEADS * LANES
    out_shapes = [
        jax.ShapeDtypeStruct((n, wide), BF16),
        jax.ShapeDtypeStruct((n, LANES), BF16),
        jax.ShapeDtypeStruct((n // tm, DSA_V_ROWS, tm), BF16),
        jax.ShapeDtypeStruct((n, LANES), BF16),
        jax.ShapeDtypeStruct((n // tm, IDX_HEADS, tm), F32),
        jax.ShapeDtypeStruct((n, wide), BF16),
        jax.ShapeDtypeStruct((n, wide), BF16),
        jax.ShapeDtypeStruct((n // tm, MLA_HEADS, MLA_V_ROWS, tm), BF16),
        jax.ShapeDtypeStruct((n, d), BF16),
        jax.ShapeDtypeStruct((n, d), BF16),
    ]
    tile = lambda w: pl.BlockSpec((tm, w), lambda i: (i, 0))
    return pl.pallas_call(
        _in_proj_kernel,
        grid=(n // tm,),
        in_specs=[tile(d), tile(1)] + [_const_spec(c.shape) for c in consts],
        out_specs=[tile(s.shape[1]) if len(s.shape) == 2 else
                   pl.BlockSpec((1,) + s.shape[1:], lambda i, nd=len(s.shape): (i,) + (0,) * (nd - 1))
                   for s in out_shapes],
        out_shape=out_shapes,
        compiler_params=_params(("parallel",)),
        name="in_proj",
    )(x2, pos2, *consts)


def _dsa_kernel(qq_ref, iw_ref, ka_ref, vt_ref, ki_ref, o_ref, score_ref, logit_ref, *, topk):
    i = pl.program_id(1)
    tq = qq_ref.shape[1]
    heads = DSA_HEADS
    nch = (i * tq + tq + KEY_CHUNK - 1) // KEY_CHUNK
    npair = (nch + 1) // 2
    nfull = nch // 2
    odd = nch % 2 == 1
    kf = float(topk)
    shape = (KEY_CHUNK, tq)
    folded = (SUBLANES, tq)

    def fold(a, op):
        return op(a.reshape(KEY_CHUNK // SUBLANES, SUBLANES, tq), axis=0)

    krow = lax.broadcasted_iota(jnp.int32, shape, 0)
    qcol = i * tq + lax.broadcasted_iota(jnp.int32, shape, 1)
    contract_last = (((1,), (1,)), ((), ()))

    def pair_loop(body, init):
        return lax.fori_loop(0, npair, lambda j, carry: body(2 * j, 2 * j + 1, carry), init)

    qq = jnp.concatenate([qq_ref[0, :, hd * LANES:(hd + 1) * LANES] for hd in range(heads)], axis=0)
    iw = jnp.concatenate([iw_ref[t] for t in range(iw_ref.shape[0])], axis=1)

    def score_chunk(c, mx, mn):
        kc = ki_ref[0, pl.ds(pl.multiple_of(c * KEY_CHUNK, KEY_CHUNK), KEY_CHUNK), :]
        sc = None
        for hp in range(IDX_HEADS // 2):
            rel = lax.dot_general(kc, qq[2 * hp * tq:(2 * hp + 2) * tq], contract_last,
                                  preferred_element_type=F32)
            rel = jnp.maximum(rel, 0.0)
            part = rel[:, :tq] * iw[2 * hp:2 * hp + 1, :] + rel[:, tq:] * iw[2 * hp + 1:2 * hp + 2, :]
            sc = part if sc is None else sc + part
        causal = (krow + c * KEY_CHUNK) <= qcol
        masked = jnp.where(causal, sc, -jnp.inf)
        score_ref[c] = masked
        return (jnp.maximum(mx, fold(masked, jnp.max)),
                jnp.minimum(mn, fold(jnp.where(causal, sc, jnp.inf), jnp.min)))

    def score_body(j, carry):
        return score_chunk(2 * j + 1, *score_chunk(2 * j, *carry))

    def score_tail(carry):
        score_ref[nch] = jnp.full(shape, -jnp.inf, F32)
        return score_chunk(nch - 1, *carry)

    carry = lax.fori_loop(0, nfull, score_body, (jnp.full(folded, -jnp.inf, F32), jnp.full(folded, jnp.inf, F32)))
    mx, mn = lax.cond(odd, score_tail, lambda cr: cr, carry)
    hi = jnp.max(mx, axis=0, keepdims=True)
    lo = jnp.min(mn, axis=0, keepdims=True)

    def count_gt(t):
        def body(c0, c1, acc):
            for c in (c0, c1):
                acc = acc + fold(jnp.where(score_ref[c] > t, 1.0, 0.0), jnp.sum)
            return acc
        return jnp.sum(pair_loop(body, jnp.zeros(folded, F32)), axis=0, keepdims=True)

    def bisect(_, carry):
        lo, hi = carry
        mid = 0.5 * (lo + hi)
        below = count_gt(mid) < kf
        return jnp.where(below, lo, mid), jnp.where(below, mid, hi)

    lo, hi = lax.fori_loop(0, BISECT_ITERS, bisect, (lo, hi))

    def max_le(t):
        def body(c0, c1, acc):
            for c in (c0, c1):
                s = score_ref[c]
                acc = jnp.maximum(acc, fold(jnp.where(s <= t, s, -jnp.inf), jnp.max))
            return acc
        return jnp.max(pair_loop(body, jnp.full(folded, -jnp.inf, F32)), axis=0, keepdims=True)

    n_causal = (i * tq + lax.broadcasted_iota(jnp.int32, (1, tq), 1) + 1).astype(F32)
    small = n_causal <= kf

    def refine_cond(carry):
        return jnp.min(carry[2]) < 0.5

    def refine_body(carry):
        m, thr, done, thr_cnt = carry

        def body(c0, c1, acc):
            cnt, nxt = acc
            for c in (c0, c1):
                s = score_ref[c]
                cnt = cnt + fold(jnp.where(s >= m, 1.0, 0.0), jnp.sum)
                nxt = jnp.maximum(nxt, fold(jnp.where(s < m, s, -jnp.inf), jnp.max))
            return cnt, nxt

        cnt, nxt = pair_loop(body, (jnp.zeros(folded, F32), jnp.full(folded, -jnp.inf, F32)))
        cnt = jnp.sum(cnt, axis=0, keepdims=True)
        nxt = jnp.max(nxt, axis=0, keepdims=True)
        reached = cnt >= kf
        hit = jnp.logical_and(reached, done < 0.5)
        return nxt, jnp.where(hit, m, thr), jnp.where(reached, 1.0, done), jnp.where(hit, cnt, thr_cnt)

    init = (max_le(hi), jnp.full((1, tq), -jnp.inf, F32), jnp.where(small, 1.0, 0.0), jnp.zeros((1, tq), F32))
    _, thr, _, thr_cnt = lax.while_loop(refine_cond, refine_body, init)

    def mask_with_ties():
        quota = jnp.where(small, 0.0, kf - count_gt(thr))
        earlier = (lax.broadcasted_iota(jnp.int32, (KEY_CHUNK, KEY_CHUNK), 1)
                   < lax.broadcasted_iota(jnp.int32, (KEY_CHUNK, KEY_CHUNK), 0)).astype(BF16)

        def body(c0, c1, seen):
            for c in (c0, c1):
                s = score_ref[c]
                eq = s == thr
                eqf = jnp.where(eq, 1.0, 0.0)
                before = jnp.dot(earlier, eqf.astype(BF16), preferred_element_type=F32) + seen
                sel = jnp.logical_or(s > thr, jnp.logical_and(eq, before < quota))
                seen = seen + jnp.sum(eqf, axis=0, keepdims=True)
                score_ref[c] = jnp.where(sel, jnp.inf, -jnp.inf)
            return seen
        pair_loop(body, jnp.zeros((1, tq), F32))

    surplus = jnp.max(jnp.where(small, 0.0, thr_cnt - kf)) > 0.5
    lax.cond(surplus, mask_with_ties, lambda: None)

    def qk_into(slot, c):
        kc = ka_ref[0, pl.ds(pl.multiple_of(c * KEY_CHUNK, KEY_CHUNK), KEY_CHUNK), :]
        logit_ref[slot] = lax.dot_general(kc, qq, contract_last, preferred_element_type=F32)

    def softmax_pv(slot, c, m_run, acc):
        s = score_ref[c]
        bias = jnp.where(jnp.logical_and(s >= thr, s > -jnp.inf), 0.0, NEG_BIG)
        vt = vt_ref[0, c]
        new_m, new_acc = [], []
        for hp in range(heads // 2):
            probs, alphas = [], []
            for hd in (2 * hp, 2 * hp + 1):
                sl = slice(hd * tq, (hd + 1) * tq)
                lh = logit_ref[slot, :, sl] + bias
                m_old = m_run[:, sl]
                m_new = jnp.maximum(m_old, jnp.max(lh, axis=0, keepdims=True))
                probs.append(jnp.exp2(lh - m_new).astype(BF16))
                alphas.append(jnp.exp2(m_old - m_new))
                new_m.append(m_new)
            pv = jnp.dot(vt, jnp.concatenate(probs, axis=1), preferred_element_type=F32)
            new_acc.append(jnp.concatenate(alphas, axis=1) * acc[:, 2 * hp * tq:(2 * hp + 2) * tq] + pv)
        return jnp.concatenate(new_m, axis=1), jnp.concatenate(new_acc, axis=1)

    def attn_body(j, carry):
        m_run, acc = carry
        qk_into(1, 2 * j + 1)
        m_run, acc = softmax_pv(0, 2 * j, m_run, acc)
        qk_into(0, jnp.minimum(2 * j + 2, nch - 1))
        return softmax_pv(1, 2 * j + 1, m_run, acc)

    qk_into(0, 0)
    d_rows = vt_ref.shape[2]
    init = (jnp.full((1, heads * tq), NEG_BIG, F32), jnp.zeros((d_rows, heads * tq), F32))
    carry = lax.fori_loop(0, nfull, attn_body, init)
    _, acc = lax.cond(odd, lambda cr: softmax_pv(0, nch - 1, *cr), lambda cr: cr, carry)
    for hp in range(heads // 2):
        halves = []
        for hd in (2 * hp, 2 * hp + 1):
            blk = acc[:, hd * tq:(hd + 1) * tq]
            halves.append(blk[:DSA_HEAD_DIM] / blk[DSA_HEAD_DIM:DSA_HEAD_DIM + 1, :])
        o_ref[0, hp] = jnp.concatenate(halves, axis=0).T.astype(o_ref.dtype)


def _dsa_attention(qq, iw_t, ka, va_t, ki, topk):
    b, s, wide = qq.shape
    tq = Q_TILE_DSA
    nkc = s // KEY_CHUNK
    tiles = tq // iw_t.shape[-1]
    assert tiles * iw_t.shape[-1] == tq and va_t.shape[-1] == KEY_CHUNK
    v_t = va_t.reshape(b, nkc, DSA_V_ROWS, KEY_CHUNK)
    qspec = pl.BlockSpec((1, tq, wide), lambda bi, i: (bi, i, 0))
    kspec = pl.BlockSpec((1, s, LANES), lambda bi, i: (bi, 0, 0))
    return pl.pallas_call(
        functools.partial(_dsa_kernel, topk=topk),
        grid=(b, s // tq),
        in_specs=[qspec, pl.BlockSpec((tiles, IDX_HEADS, tq // tiles), lambda bi, i: (bi * (s // tq) + i, 0, 0)), kspec,
                  pl.BlockSpec((1, nkc, DSA_V_ROWS, KEY_CHUNK), lambda bi, i: (bi, 0, 0, 0)), kspec],
        out_specs=pl.BlockSpec((1, DSA_HEADS // 2, tq, LANES), lambda bi, i: (bi, 0, i, 0)),
        out_shape=jax.ShapeDtypeStruct((b, DSA_HEADS // 2, s, LANES), BF16),
        scratch_shapes=[pltpu.VMEM((nkc, KEY_CHUNK, tq), F32),
                        pltpu.VMEM((2, KEY_CHUNK, DSA_HEADS * tq), F32)],
        compiler_params=_params(("parallel", "arbitrary")),
        name="dsa_attn",
    )(qq, iw_t, ka, v_t, ki)


def _mla_kernel(q_ref, k_ref, vt_ref, o_ref, logit_ref):
    i = pl.program_id(2)
    tq = q_ref.shape[1]
    group = q_ref.shape[2] // LANES
    d_rows = vt_ref.shape[3]
    per_block = tq // vt_ref.shape[4]
    qs = [q_ref[0, :, g * LANES:(g + 1) * LANES] for g in range(group)]
    contract_last = (((1,), (1,)), ((), ()))

    def qk_into(slot, c):
        k0 = pl.multiple_of(c * tq, tq)
        for g in range(group):
            logit_ref[slot, g] = lax.dot_general(k_ref[0, pl.ds(k0, tq), g * LANES:(g + 1) * LANES], qs[g],
                                                 contract_last, preferred_element_type=F32)

    def softmax_pv(slot, c, carry, masked):
        if masked:
            keep = (lax.broadcasted_iota(jnp.int32, (tq, tq), 0) <= lax.broadcasted_iota(jnp.int32, (tq, tq), 1))
            bias = jnp.where(keep, 0.0, NEG_BIG)
        probs, alphas, maxes = [], [], []
        for g in range(group):
            m_run = carry[2 * g]
            logit = logit_ref[slot, g]
            if masked:
                logit = logit + bias
            m_new = jnp.maximum(m_run, jnp.max(logit, axis=0, keepdims=True))
            probs.append(jnp.exp2(logit - m_new).astype(BF16))
            alphas.append(jnp.exp2(m_run - m_new))
            maxes.append(m_new)
        res = []
        for g in range(group):
            vt = jnp.concatenate([vt_ref[0, c * per_block + t, g] for t in range(per_block)], axis=1)
            pv = jnp.dot(vt, probs[g], preferred_element_type=F32)
            res += [maxes[g], alphas[g] * carry[2 * g + 1] + pv]
        return tuple(res)

    def pair_body(j, carry):
        qk_into(1, 2 * j + 1)
        carry = softmax_pv(0, 2 * j, carry, False)
        qk_into(0, 2 * j + 2)
        return softmax_pv(1, 2 * j + 1, carry, False)

    def tail_even(carry):
        return softmax_pv(0, i, carry, True)

    def tail_odd(carry):
        qk_into(1, i)
        carry = softmax_pv(0, i - 1, carry, False)
        return softmax_pv(1, i, carry, True)

    qk_into(0, 0)
    init = (jnp.full((1, tq), NEG_BIG, F32), jnp.zeros((d_rows, tq), F32)) * group
    carry = lax.fori_loop(0, i // 2, pair_body, init)
    carry = lax.cond(i % 2 == 0, tail_even, tail_odd, carry)
    for gp in range(group // 2):
        halves = [carry[2 * g + 1][:MLA_V] / carry[2 * g + 1][MLA_V:MLA_V + 1, :] for g in (2 * gp, 2 * gp + 1)]
        o_ref[0, :, gp * LANES:(gp + 1) * LANES] = jnp.concatenate(halves, axis=0).T.astype(o_ref.dtype)


def _mla_attention(mq, mk, mv_t):
    b, s, wide = mq.shape
    tq = Q_TILE_MLA
    nkc = s // tq
    gw = MLA_HEAD_GROUP * LANES
    nt, _, _, tm = mv_t.shape
    nt //= b
    assert tq % tm == 0
    v_t = mv_t.reshape(b, nt, MLA_HEADS, MLA_V_ROWS, tm)
    return pl.pallas_call(
        _mla_kernel,
        grid=(b, MLA_HEADS // MLA_HEAD_GROUP, s // tq),
        in_specs=[pl.BlockSpec((1, tq, gw), lambda bi, h, i: (bi, i, h)),
                  pl.BlockSpec((1, s, gw), lambda bi, h, i: (bi, 0, h)),
                  pl.BlockSpec((1, nt, MLA_HEAD_GROUP, MLA_V_ROWS, tm), lambda bi, h, i: (bi, 0, h, 0, 0))],
        out_specs=pl.BlockSpec((1, tq, gw // 2), lambda bi, h, i: (bi, i, h)),
        out_shape=jax.ShapeDtypeStruct((b, s, MLA_HEADS * MLA_V), BF16),
        scratch_shapes=[pltpu.VMEM((2, MLA_HEAD_GROUP, tq, tq), F32)],
        compiler_params=_params(("parallel", "parallel", "arbitrary")),
        name="mla_attn",
    )(mq, mk, v_t)


def _mem_kv_kernel(mem_ref, g_ref, wkv_ref, kg_ref, k_o, v_o):
    m = mem_ref[0]
    hm = _rms(m, g_ref[...], m.shape[-1]).astype(BF16)
    kv = jnp.dot(hm, wkv_ref[...], preferred_element_type=F32)
    for hd in range(MEM_HEADS):
        sl = slice(hd * LANES, (hd + 1) * LANES)
        k_o[0, :, sl] = _rms(kv[:, sl], kg_ref[...], MEM_HEAD_DIM).astype(BF16)
    v_o[0] = kv[:, MEM_HEADS * LANES:].astype(BF16)


def _mem_kv(mem, mem_g, w_kv, k_g):
    b, m, d = mem.shape
    hw = MEM_HEADS * MEM_HEAD_DIM
    wkv = jnp.concatenate([_pad_heads(w_kv[:, :hw], MEM_HEADS, MEM_HEAD_DIM),
                           _pad_heads(w_kv[:, hw:], MEM_HEADS, MEM_HEAD_DIM)], axis=1).astype(BF16)
    consts = [mem_g.reshape(1, d), wkv, _pad_cols(k_g.reshape(1, -1))]
    wide = MEM_HEADS * LANES
    spec = pl.BlockSpec((1, m, wide), lambda bi: (bi, 0, 0))
    return pl.pallas_call(
        _mem_kv_kernel,
        grid=(b,),
        in_specs=[pl.BlockSpec((1, m, d), lambda bi: (bi, 0, 0))] + [_const_spec(c.shape) for c in consts],
        out_specs=[spec, spec],
        out_shape=[jax.ShapeDtypeStruct((b, m, wide), BF16)] * 2,
        compiler_params=_params(("parallel",)),
        name="mem_kv",
    )(mem, *consts)


def _split_dot(a, w):
    a_hi = a.astype(BF16)
    a_lo = (a - a_hi.astype(F32)).astype(BF16)
    w_hi = w.astype(BF16)
    w_lo = (w - w_hi.astype(F32)).astype(BF16)
    w_both = jnp.concatenate([w_hi, w_lo], axis=1)
    n = w.shape[1]
    hi = jnp.dot(a_hi, w_both, preferred_element_type=F32)
    lo = jnp.dot(a_lo, w_both, preferred_element_type=F32)
    return (hi[:, :n] + hi[:, n:]) + (lo[:, :n] + lo[:, n:])


def _merge_kernel(oa_ref, ob_ref, sgd_ref, sgm_ref, x_ref, wa_ref, wb_ref, wo_ref, mxg_ref, wq_ref, qg_ref,
                  km_ref, vm_ref, wmo_ref, moeg_ref, wr_ref, bias_ref, x2_o, h3_o, comb_o):
    d = functools.partial(jnp.dot, preferred_element_type=F32)
    oa = jnp.concatenate([oa_ref[0, hp] for hp in range(DSA_HEADS // 2)], axis=-1)
    merged = (sgd_ref[0].astype(F32) * d(oa, wa_ref[...]) + sgm_ref[0].astype(F32) * d(ob_ref[0], wb_ref[...]))
    x1 = x_ref[0] + d(merged.astype(BF16), wo_ref[...])

    d_model = x1.shape[-1]
    h2 = _rms(x1, mxg_ref[...], d_model).astype(BF16)
    qm = d(h2, wq_ref[...])
    outs = []
    for hd in range(MEM_HEADS):
        sl = slice(hd * LANES, (hd + 1) * LANES)
        q = (_rms(qm[:, sl], qg_ref[...], MEM_HEAD_DIM) * MEM_HEAD_DIM ** -0.5).astype(BF16)
        logit = lax.dot_general(q, km_ref[0, :, sl], (((1,), (1,)), ((), ())), preferred_element_type=F32)
        logit = logit - jnp.max(logit, axis=-1, keepdims=True)
        p = jnp.exp(logit)
        pv = d(p.astype(BF16), vm_ref[0, :, sl])
        outs.append((pv / jnp.sum(p, axis=-1, keepdims=True)).astype(BF16))
    x2 = x1 + d(jnp.concatenate(outs, axis=-1), wmo_ref[...])
    x2_o[0] = x2

    h3 = _rms(x2, moeg_ref[...], d_model)
    h3_o[0] = h3.astype(BF16)
    logits = _split_dot(h3, wr_ref[...])
    tm = logits.shape[0]
    lane = lax.broadcasted_iota(jnp.int32, (tm, LANES), 1)
    big = jnp.int32(LANES)
    is_grp = jnp.logical_and(lane >= N_EXPERTS, lane < N_EXPERTS + N_GROUPS)
    glog = jnp.where(is_grp, logits, -jnp.inf)
    gmax = jnp.max(glog, axis=-1, keepdims=True)
    g_sel = jnp.min(jnp.where(glog == gmax, lane, big), axis=-1, keepdims=True) - N_EXPERTS
    p_g = 1.0 / jnp.sum(jnp.where(is_grp, jnp.exp(logits - gmax), 0.0), axis=-1, keepdims=True)
    aff = jax.nn.sigmoid(logits)
    in_grp = jnp.logical_and(lane >= g_sel * EXPERTS_PER_GROUP, lane < (g_sel + 1) * EXPERTS_PER_GROUP)
    val = jnp.where(in_grp, aff + bias_ref[...], -jnp.inf)
    m1 = jnp.max(val, axis=-1, keepdims=True)
    i1 = jnp.min(jnp.where(val == m1, lane, big), axis=-1, keepdims=True)
    val2 = jnp.where(lane == i1, -jnp.inf, val)
    m2 = jnp.max(val2, axis=-1, keepdims=True)
    i2 = jnp.min(jnp.where(val2 == m2, lane, big), axis=-1, keepdims=True)
    chosen = jnp.logical_or(lane == i1, lane == i2)
    a_sel = jnp.where(chosen, aff, 0.0)
    comb_o[0] = p_g * a_sel / jnp.sum(a_sel, axis=-1, keepdims=True)


def _merge_mem_router(oa, ob, sgd, sgm, x, w_br_dsa, w_br_mla, w_out, mem_x_g, mem_w_q, mem_q_g, km, vm,
                      mem_w_o, moe_g, w_group, w_expert, expert_bias, tm):
    b, s, d = x.shape
    wa = w_br_dsa.astype(BF16)
    wb = w_br_mla.astype(BF16)
    wq = _pad_heads(mem_w_q, MEM_HEADS, MEM_HEAD_DIM).astype(BF16)
    wmo = jnp.pad(mem_w_o.reshape(MEM_HEADS, MEM_HEAD_DIM, d), ((0, 0), (0, LANES - MEM_HEAD_DIM), (0, 0)))
    wmo = wmo.reshape(MEM_HEADS * LANES, d).astype(BF16)
    wr = _pad_cols(jnp.concatenate([w_expert, w_group], axis=1).astype(F32))
    bias = _pad_cols(expert_bias.reshape(1, -1).astype(F32))
    consts_a = [wa, wb, w_out.astype(BF16), mem_x_g.reshape(1, d), wq, _pad_cols(mem_q_g.reshape(1, -1))]
    consts_b = [wmo, moe_g.reshape(1, d), wr, bias]
    m = km.shape[1]
    mw = MEM_HEADS * LANES
    tok = lambda w: pl.BlockSpec((1, tm, w), lambda bi, i: (bi, i, 0))
    memspec = pl.BlockSpec((1, m, mw), lambda bi, i: (bi, 0, 0))
    return pl.pallas_call(
        _merge_kernel,
        grid=(b, s // tm),
        in_specs=[pl.BlockSpec((1, DSA_HEADS // 2, tm, LANES), lambda bi, i: (bi, 0, i, 0)),
                  tok(MLA_HEADS * MLA_V), tok(d), tok(d), tok(d)]
                 + [_const_spec(c.shape) for c in consts_a] + [memspec, memspec]
                 + [_const_spec(c.shape) for c in consts_b],
        out_specs=[tok(d), tok(d), tok(LANES)],
        out_shape=[jax.ShapeDtypeStruct((b, s, d), F32), jax.ShapeDtypeStruct((b, s, d), BF16),
                   jax.ShapeDtypeStruct((b, s, LANES), F32)],
        compiler_params=_params(("parallel", "parallel")),
        name="merge_mem",
    )(oa, ob, sgd, sgm, x, *consts_a, km, vm, *consts_b)


def _moe_kernel(h_ref, comb_ref, x_ref, wg_ref, wu_ref, wd_ref, o_ref):
    step = pl.program_id(1)
    per = wg_ref.shape[0]
    h = h_ref[...]
    comb = comb_ref[...]
    lane = lax.broadcasted_iota(jnp.int32, comb.shape, 1)
    parts = []
    for e in range(per):
        gate = jnp.dot(h, wg_ref[e], preferred_element_type=F32)
        up = jnp.dot(h, wu_ref[e], preferred_element_type=F32)
        c_e = jnp.sum(jnp.where(lane == step * per + e, comb, 0.0), axis=-1, keepdims=True)
        parts.append(((jax.nn.silu(gate) * up) * c_e).astype(BF16))
    w_down = wd_ref[...].reshape(per * D_EXPERT, wd_ref.shape[-1])
    contrib = jnp.dot(jnp.concatenate(parts, axis=1), w_down, preferred_element_type=F32)

    @pl.when(step == 0)
    def _():
        o_ref[...] = x_ref[...] + contrib

    @pl.when(step > 0)
    def _():
        o_ref[...] += contrib


def _moe(h3, comb, x2, w_gate, w_up, w_down, tm):
    n, d = x2.shape
    per = MOE_EXPERTS_PER_STEP
    return pl.pallas_call(
        _moe_kernel,
        grid=(n // tm, N_EXPERTS // per),
        in_specs=[pl.BlockSpec((tm, d), lambda i, g: (i, 0)),
                  pl.BlockSpec((tm, LANES), lambda i, g: (i, 0)),
                  pl.BlockSpec((tm, d), lambda i, g: (i, 0)),
                  pl.BlockSpec((per, d, D_EXPERT), lambda i, g: (g, 0, 0)),
                  pl.BlockSpec((per, d, D_EXPERT), lambda i, g: (g, 0, 0)),
                  pl.BlockSpec((per, D_EXPERT, d), lambda i, g: (g, 0, 0))],
        out_specs=pl.BlockSpec((tm, d), lambda i, g: (i, 0)),
        out_shape=jax.ShapeDtypeStruct((n, d), F32),
        compiler_params=_params(("parallel", "arbitrary")),
        name="moe",
    )(h3, comb, x2, w_gate.astype(BF16), w_up.astype(BF16), w_down.astype(BF16))


def _layer(x, mem, positions, p):
    b, s, d = x.shape
    n = b * s
    assert s % (2 * KEY_CHUNK) == 0 and s % Q_TILE_MLA == 0
    topk = min(TOPK_MAX, s // 4)
    tm = min(TOKENS_PER_STEP_PROJ, s)
    qq, ka, va, ki, iw, mq, mk, mv, sgd, sgm = _in_proj(
        x.reshape(n, d), positions.reshape(n, 1), p["attn_norm_g"], p["w_in"], p["dsa_q_norm_g"],
        p["dsa_k_norm_g"], p["mla_cq_norm_g"], p["mla_ckv_norm_g"], p["mla_w_uq"], p["mla_w_ukv"],
        p["mla_q_norm_g"], p["mla_k_norm_g"], tm)
    r3 = lambda a: a.reshape(b, s, a.shape[-1])
    oa = _dsa_attention(r3(qq), iw, r3(ka), va, r3(ki), topk)
    ob = _mla_attention(r3(mq), r3(mk), mv)
    km, vm = _mem_kv(mem, p["mem_norm_g"], p["mem_w_kv"], p["mem_k_norm_g"])
    x2, h3, comb = _merge_mem_router(
        oa, ob, r3(sgd), r3(sgm), x, p["w_branch_dsa"], p["w_branch_mla"], p["w_out"], p["mem_x_norm_g"],
        p["mem_w_q"], p["mem_q_norm_g"], km, vm, p["mem_w_o"], p["moe_norm_g"], p["moe_w_group"],
        p["moe_w_expert"], p["moe_expert_bias"], min(TOKENS_PER_STEP_MERGE, s))
    out = _moe(h3.reshape(n, d), comb.reshape(n, LANES), x2.reshape(n, d), p["moe_w_gate"], p["moe_w_up"],
               p["moe_w_down"], min(TOKENS_PER_STEP_MOE, n))
    return out.reshape(b, s, d)


_PARAM_NAMES = ("attn_norm_g", "w_in", "dsa_q_norm_g", "dsa_k_norm_g", "mla_cq_norm_g", "mla_ckv_norm_g",
                "mla_w_uq", "mla_w_ukv", "mla_q_norm_g", "mla_k_norm_g", "w_branch_dsa", "w_branch_mla", "w_out",
                "mem_x_norm_g", "mem_norm_g", "mem_w_q", "mem_w_kv", "mem_q_norm_g", "mem_k_norm_g", "mem_w_o",
                "moe_norm_g", "moe_w_group", "moe_w_expert", "moe_expert_bias", "moe_w_gate", "moe_w_up",
                "moe_w_down")


def kernel(x, mem, positions, attn_norm_g, w_in, dsa_q_norm_g, dsa_k_norm_g, mla_cq_norm_g, mla_ckv_norm_g, mla_w_uq, mla_w_ukv, mla_q_norm_g, mla_k_norm_g, w_branch_dsa, w_branch_mla, w_out, mem_x_norm_g, mem_norm_g, mem_w_q, mem_w_kv, mem_q_norm_g, mem_k_norm_g, mem_w_o, moe_norm_g, moe_w_group, moe_w_expert, moe_expert_bias, moe_w_gate, moe_w_up, moe_w_down):
    stacked = (attn_norm_g, w_in, dsa_q_norm_g, dsa_k_norm_g, mla_cq_norm_g, mla_ckv_norm_g, mla_w_uq, mla_w_ukv,
               mla_q_norm_g, mla_k_norm_g, w_branch_dsa, w_branch_mla, w_out, mem_x_norm_g, mem_norm_g, mem_w_q,
               mem_w_kv, mem_q_norm_g, mem_k_norm_g, mem_w_o, moe_norm_g, moe_w_group, moe_w_expert,
               moe_expert_bias, moe_w_gate, moe_w_up, moe_w_down)
    for layer in range(attn_norm_g.shape[0]):
        p = {name: arr[layer] for name, arr in zip(_PARAM_NAMES, stacked)}
        x = _layer(x, mem, positions, p)
    return x
```

```python
import functools

import jax
import jax.numpy as jnp
from jax import lax
from jax.experimental import pallas as pl
from jax.experimental.pallas import tpu as pltpu

F32 = jnp.float32
BF16 = jnp.bfloat16

LANES = 128
SUBLANES = 8
VMEM_LIMIT = 56 * 1024 * 1024

ROPE_THETA = 10000.0
EPS = 1e-6
DSA_HEADS = 8
DSA_HEAD_DIM = 64
IDX_HEADS = 8
IDX_DIM = 64
TOPK_MAX = 256
MLA_HEADS = 8
MLA_NOPE = 64
MLA_ROPE = 32
MLA_QK = MLA_NOPE + MLA_ROPE
MLA_V = 64
MLA_Q_RANK = 256
MLA_KV_RANK = 128
MEM_HEADS = 4
MEM_HEAD_DIM = 64
N_GROUPS = 4
EXPERTS_PER_GROUP = 4
N_EXPERTS = N_GROUPS * EXPERTS_PER_GROUP
D_EXPERT = 256

NEG_BIG = -1e30
KEY_CHUNK = 256
Q_TILE_DSA = 256
DSA_V_ROWS = 80
MLA_V_ROWS = 80
Q_TILE_MLA = 512
MLA_HEAD_GROUP = 4
LOG2_E = 1.4426950408889634
BISECT_ITERS = 15
TOKENS_PER_STEP_PROJ = 256
TOKENS_PER_STEP_MERGE = 512
TOKENS_PER_STEP_MOE = 1024
MOE_EXPERTS_PER_STEP = 16


def _const_spec(shape):
    nd = len(shape)
    return pl.BlockSpec(shape, lambda *_: (0,) * nd)


def _params(sem):
    return pltpu.CompilerParams(dimension_semantics=sem, vmem_limit_bytes=VMEM_LIMIT)


def _rms(x, g, n):
    ms = jnp.sum(x * x, axis=-1, keepdims=True) * (1.0 / n)
    return (x * lax.rsqrt(ms + EPS)) * g


def _rope(x, cos, sin_signed, lo_mask, half):
    fwd = pltpu.roll(x, LANES - half, 1)
    bwd = pltpu.roll(x, half, 1)
    return x * cos + jnp.where(lo_mask, fwd, bwd) * sin_signed


def _in_proj_kernel(x_ref, pos_ref, g_ref, wqq_ref, wsm_ref, wcq_ref, wg_ref, wuq_ref, wuk_ref,
                    wuv_ref, qag_ref, kag_ref, cqg_ref, ckvg_ref, mqg_ref, mkg_ref, freq_ref,
                    qq_o, ka_o, va_o, ki_o, iw_o, mq_o, mk_o, mv_o, sgd_o, sgm_o):
    x = x_ref[...]
    d_model = x.shape[-1]
    h = _rms(x, g_ref[...], d_model).astype(BF16)

    pos = pos_ref[...].astype(F32)
    lane = lax.broadcasted_iota(jnp.int32, (x.shape[0], LANES), 1)
    half_d, half_m = DSA_HEAD_DIM // 2, MLA_ROPE // 2
    ang = pos * freq_ref[...]
    lo_d = (lane & (DSA_HEAD_DIM - 1)) < half_d
    lo_m = lane < MLA_NOPE + half_m
    in_m_lo = jnp.logical_and(lane >= MLA_NOPE, lo_m)
    in_m_hi = jnp.logical_and(lane >= MLA_NOPE + half_m, lane < MLA_QK)

    def tables(t, fill):
        by_half = pltpu.roll(t, half_d, 1)
        head = jnp.where(lane < half_d, t, by_half)
        dsa = jnp.where(lane < DSA_HEAD_DIM, head, pltpu.roll(head, DSA_HEAD_DIM, 1))
        mla = jnp.where(in_m_lo, by_half, jnp.where(in_m_hi, pltpu.roll(t, half_d + half_m, 1), fill))
        return dsa, mla

    cos_d, cos_m = tables(jnp.cos(ang), 1.0)
    sin_d, sin_m = tables(jnp.sin(ang), 0.0)
    sin_d = jnp.where(lo_d, -sin_d, sin_d)
    sin_m = jnp.where(lo_m, -sin_m, sin_m)
    rope_d = functools.partial(_rope, cos=cos_d, sin_signed=sin_d, lo_mask=lo_d, half=DSA_HEAD_DIM // 2)
    rope_m = functools.partial(_rope, cos=cos_m, sin_signed=sin_m, lo_mask=lo_m, half=MLA_ROPE // 2)

    def dot(a, w_ref):
        return jnp.dot(a, w_ref[...], preferred_element_type=F32)

    att_scale = DSA_HEAD_DIM ** -0.5 * LOG2_E
    idx_scale = IDX_DIM ** -0.5 * IDX_HEADS ** -0.5
    mla_scale = MLA_QK ** -0.5 * LOG2_E

    qq = dot(h, wqq_ref)
    is_qa = lane < DSA_HEAD_DIM
    for hd in range(DSA_HEADS):
        sl = slice(hd * LANES, (hd + 1) * LANES)
        xh = qq[:, sl]
        ms = jnp.sum(jnp.where(is_qa, xh * xh, 0.0), axis=-1, keepdims=True) * (1.0 / DSA_HEAD_DIM)
        mult = jnp.where(is_qa, (lax.rsqrt(ms + EPS) * att_scale) * qag_ref[...], 1.0)
        qq_o[:, sl] = rope_d(xh * mult).astype(BF16)

    sm = dot(h, wsm_ref)
    ka_o[...] = rope_d(_rms(sm[:, 0:LANES], kag_ref[...], DSA_HEAD_DIM)).astype(BF16)
    va = sm[:, LANES:2 * LANES] + jnp.where(lane[0:1] == DSA_HEAD_DIM, 1.0, 0.0)
    va_o[0] = va.T[:DSA_V_ROWS].astype(BF16)
    ki_o[...] = rope_d(sm[:, 2 * LANES:3 * LANES]).astype(BF16)
    iw_o[0] = (sm[:, 3 * LANES:4 * LANES] * idx_scale).T[:IDX_HEADS]
    ckv = _rms(sm[:, 4 * LANES:5 * LANES], ckvg_ref[...], MLA_KV_RANK).astype(BF16)
    kpe = sm[:, 5 * LANES:6 * LANES]

    kn = dot(ckv, wuk_ref)
    mkg = mkg_ref[...]
    pe_rot = rope_m(kpe * mkg)
    ss_pe = jnp.sum(kpe * kpe, axis=-1, keepdims=True)
    for hd in range(MLA_HEADS):
        sl = slice(hd * LANES, (hd + 1) * LANES)
        xh = kn[:, sl]
        ms = (jnp.sum(xh * xh, axis=-1, keepdims=True) + ss_pe) * (1.0 / MLA_QK)
        mk_o[:, sl] = ((xh * mkg + pe_rot) * lax.rsqrt(ms + EPS)).astype(BF16)
    wide_lane = lax.broadcasted_iota(jnp.int32, (1, MLA_HEADS * LANES), 1)
    ones_col = jnp.where(wide_lane % LANES == MLA_V, 1.0, 0.0)
    mv = dot(ckv, wuv_ref) + ones_col
    for hd in range(MLA_HEADS):
        mv_o[0, hd] = mv[:, hd * LANES:(hd + 1) * LANES].T[:MLA_V_ROWS].astype(BF16)

    cq = _rms(dot(h, wcq_ref), cqg_ref[...], MLA_Q_RANK).astype(BF16)
    qb = dot(cq, wuq_ref)
    for hd in range(MLA_HEADS):
        sl = slice(hd * LANES, (hd + 1) * LANES)
        y = rope_m(_rms(qb[:, sl], mqg_ref[...], MLA_QK))
        mq_o[:, sl] = (y * mla_scale).astype(BF16)

    gates = jax.nn.sigmoid(dot(h, wg_ref))
    sgd_o[...] = gates[:, :d_model].astype(BF16)
    sgm_o[...] = gates[:, d_model:].astype(BF16)


def _pad_heads(w, heads, dim):
    k = w.shape[0]
    w = w.reshape(k, heads, dim)
    w = jnp.pad(w, ((0, 0), (0, 0), (0, LANES - dim)))
    return w.reshape(k, heads * LANES)


def _pad_cols(w, width=LANES, offset=0):
    return jnp.pad(w, ((0, 0), (offset, width - offset - w.shape[1])))


def _in_proj(x2, pos2, attn_g, w_in, dsa_q_g, dsa_k_g, cq_g, ckv_g, w_uq, w_ukv, mq_g, mk_g, tm):
    n, d = x2.shape
    sizes = (DSA_HEADS * DSA_HEAD_DIM, DSA_HEAD_DIM, DSA_HEAD_DIM, IDX_HEADS * IDX_DIM, IDX_DIM, IDX_HEADS,
             MLA_Q_RANK, MLA_KV_RANK, MLA_ROPE, d, d)
    offs = [0]
    for s in sizes:
        offs.append(offs[-1] + s)
    seg = [w_in[:, offs[i]:offs[i + 1]] for i in range(len(sizes))]
    w_dq, w_dk, w_dv, w_iq, w_ik, w_iw, w_cq, w_ckv, w_kpe, w_gd, w_gm = seg

    assert DSA_HEADS == IDX_HEADS and DSA_HEAD_DIM + IDX_DIM == LANES
    wqq = jnp.concatenate([w_dq.reshape(d, DSA_HEADS, DSA_HEAD_DIM), w_iq.reshape(d, IDX_HEADS, IDX_DIM)], axis=-1)
    wqq = wqq.reshape(d, DSA_HEADS * LANES).astype(BF16)
    wsm = jnp.concatenate([_pad_cols(w_dk), _pad_cols(w_dv), _pad_cols(w_ik, offset=DSA_HEAD_DIM), _pad_cols(w_iw), w_ckv,
                           _pad_cols(w_kpe, offset=MLA_NOPE)], axis=1).astype(BF16)
    wcq = w_cq.astype(BF16)
    wg = jnp.concatenate([w_gd, w_gm], axis=1).astype(BF16)
    wuq = _pad_heads(w_uq, MLA_HEADS, MLA_QK).astype(BF16)
    ukv = w_ukv.reshape(MLA_KV_RANK, MLA_HEADS, MLA_NOPE + MLA_V)
    wuk = _pad_heads(ukv[:, :, :MLA_NOPE].reshape(MLA_KV_RANK, -1), MLA_HEADS, MLA_NOPE).astype(BF16)
    wuv = _pad_heads(ukv[:, :, MLA_NOPE:].reshape(MLA_KV_RANK, -1), MLA_HEADS, MLA_V).astype(BF16)

    row = lambda v: _pad_cols(v.reshape(1, -1).astype(F32), width=max(LANES, v.size))
    half_d = DSA_HEAD_DIM // 2
    inv_d = ROPE_THETA ** (-jnp.arange(half_d, dtype=F32) / half_d)
    half_m = MLA_ROPE // 2
    inv_m = ROPE_THETA ** (-jnp.arange(half_m, dtype=F32) / half_m)
    freq = _pad_cols(jnp.concatenate([inv_d, inv_m]).reshape(1, -1))

    consts = [attn_g.reshape(1, d), wqq, wsm, wcq, wg, wuq, wuk, wuv, row(dsa_q_g), row(dsa_k_g),
              row(cq_g), row(ckv_g), row(mq_g), row(mk_g), freq]
    wide = DSA_HEADS * LANES
    out_shapes = [
        jax.ShapeDtypeStruct((n, wide), BF16),
        jax.ShapeDtypeStruct((n, LANES), BF16),
        jax.ShapeDtypeStruct((n // tm, DSA_V_ROWS, tm), BF16),
        jax.ShapeDtypeStruct((n, LANES), BF16),
        jax.ShapeDtypeStruct((n // tm, IDX_HEADS, tm), F32),
        jax.ShapeDtypeStruct((n, wide), BF16),
        jax.ShapeDtypeStruct((n, wide), BF16),
        jax.ShapeDtypeStruct((n // tm, MLA_HEADS, MLA_V_ROWS, tm), BF16),
        jax.ShapeDtypeStruct((n, d), BF16),
        jax.ShapeDtypeStruct((n, d), BF16),
    ]
    tile = lambda w: pl.BlockSpec((tm, w), lambda i: (i, 0))
    return pl.pallas_call(
        _in_proj_kernel,
        grid=(n // tm,),
        in_specs=[tile(d), tile(1)] + [_const_spec(c.shape) for c in consts],
        out_specs=[tile(s.shape[1]) if len(s.shape) == 2 else
                   pl.BlockSpec((1,) + s.shape[1:], lambda i, nd=len(s.shape): (i,) + (0,) * (nd - 1))
                   for s in out_shapes],
        out_shape=out_shapes,
        compiler_params=_params(("parallel",)),
        name="in_proj",
    )(x2, pos2, *consts)


def _dsa_kernel(qq_ref, iw_ref, ka_ref, vt_ref, ki_ref, o_ref, score_ref, logit_ref, *, topk):
    i = pl.program_id(1)
    tq = qq_ref.shape[1]
    heads = DSA_HEADS
    nch = (i * tq + tq + KEY_CHUNK - 1) // KEY_CHUNK
    npair = (nch + 1) // 2
    nfull = nch // 2
    odd = nch % 2 == 1
    kf = float(topk)
    shape = (KEY_CHUNK, tq)
    folded = (SUBLANES, tq)

    def fold(a, op):
        return op(a.reshape(KEY_CHUNK // SUBLANES, SUBLANES, tq), axis=0)

    krow = lax.broadcasted_iota(jnp.int32, shape, 0)
    qcol = i * tq + lax.broadcasted_iota(jnp.int32, shape, 1)
    contract_last = (((1,), (1,)), ((), ()))

    def pair_loop(body, init):
        return lax.fori_loop(0, npair, lambda j, carry: body(2 * j, 2 * j + 1, carry), init)

    qq = jnp.concatenate([qq_ref[0, :, hd * LANES:(hd + 1) * LANES] for hd in range(heads)], axis=0)
    iw = iw_ref[0]

    def score_chunk(c, mx, mn):
        kc = ki_ref[0, pl.ds(pl.multiple_of(c * KEY_CHUNK, KEY_CHUNK), KEY_CHUNK), :]
        sc = None
        for hp in range(IDX_HEADS // 2):
            rel = lax.dot_general(kc, qq[2 * hp * tq:(2 * hp + 2) * tq], contract_last,
                                  preferred_element_type=F32)
            rel = jnp.maximum(rel, 0.0)
            part = rel[:, :tq] * iw[2 * hp:2 * hp + 1, :] + rel[:, tq:] * iw[2 * hp + 1:2 * hp + 2, :]
            sc = part if sc is None else sc + part
        causal = (krow + c * KEY_CHUNK) <= qcol
        masked = jnp.where(causal, sc, -jnp.inf)
        score_ref[c] = masked
        return (jnp.maximum(mx, fold(masked, jnp.max)),
                jnp.minimum(mn, fold(jnp.where(causal, sc, jnp.inf), jnp.min)))

    def score_body(j, carry):
        return score_chunk(2 * j + 1, *score_chunk(2 * j, *carry))

    def score_tail(carry):
        score_ref[nch] = jnp.full(shape, -jnp.inf, F32)
        return score_chunk(nch - 1, *carry)

    carry = lax.fori_loop(0, nfull, score_body, (jnp.full(folded, -jnp.inf, F32), jnp.full(folded, jnp.inf, F32)))
    mx, mn = lax.cond(odd, score_tail, lambda cr: cr, carry)
    hi = jnp.max(mx, axis=0, keepdims=True)
    lo = jnp.min(mn, axis=0, keepdims=True)

    def count_gt(t):
        def body(c0, c1, acc):
            for c in (c0, c1):
                acc = acc + fold(jnp.where(score_ref[c] > t, 1.0, 0.0), jnp.sum)
            return acc
        return jnp.sum(pair_loop(body, jnp.zeros(folded, F32)), axis=0, keepdims=True)

    def bisect(_, carry):
        lo, hi = carry
        mid = 0.5 * (lo + hi)
        below = count_gt(mid) < kf
        return jnp.where(below, lo, mid), jnp.where(below, mid, hi)

    lo, hi = lax.fori_loop(0, BISECT_ITERS, bisect, (lo, hi))

    def max_le(t):
        def body(c0, c1, acc):
            for c in (c0, c1):
                s = score_ref[c]
                acc = jnp.maximum(acc, fold(jnp.where(s <= t, s, -jnp.inf), jnp.max))
            return acc
        return jnp.max(pair_loop(body, jnp.full(folded, -jnp.inf, F32)), axis=0, keepdims=True)

    n_causal = (i * tq + lax.broadcasted_iota(jnp.int32, (1, tq), 1) + 1).astype(F32)
    small = n_causal <= kf

    def refine_cond(carry):
        return jnp.min(carry[2]) < 0.5

    def refine_body(carry):
        m, thr, done, thr_cnt = carry

        def body(c0, c1, acc):
            cnt, nxt = acc
            for c in (c0, c1):
                s = score_ref[c]
                cnt = cnt + fold(jnp.where(s >= m, 1.0, 0.0), jnp.sum)
                nxt = jnp.maximum(nxt, fold(jnp.where(s < m, s, -jnp.inf), jnp.max))
            return cnt, nxt

        cnt, nxt = pair_loop(body, (jnp.zeros(folded, F32), jnp.full(folded, -jnp.inf, F32)))
        cnt = jnp.sum(cnt, axis=0, keepdims=True)
        nxt = jnp.max(nxt, axis=0, keepdims=True)
        reached = cnt >= kf
        hit = jnp.logical_and(reached, done < 0.5)
        return nxt, jnp.where(hit, m, thr), jnp.where(reached, 1.0, done), jnp.where(hit, cnt, thr_cnt)

    init = (max_le(hi), jnp.full((1, tq), -jnp.inf, F32), jnp.where(small, 1.0, 0.0), jnp.zeros((1, tq), F32))
    _, thr, _, thr_cnt = lax.while_loop(refine_cond, refine_body, init)

    def mask_with_ties():
        quota = jnp.where(small, 0.0, kf - count_gt(thr))
        earlier = (lax.broadcasted_iota(jnp.int32, (KEY_CHUNK, KEY_CHUNK), 1)
                   < lax.broadcasted_iota(jnp.int32, (KEY_CHUNK, KEY_CHUNK), 0)).astype(BF16)

        def body(c0, c1, seen):
            for c in (c0, c1):
                s = score_ref[c]
                eq = s == thr
                eqf = jnp.where(eq, 1.0, 0.0)
                before = jnp.dot(earlier, eqf.astype(BF16), preferred_element_type=F32) + seen
                sel = jnp.logical_or(s > thr, jnp.logical_and(eq, before < quota))
                seen = seen + jnp.sum(eqf, axis=0, keepdims=True)
                score_ref[c] = jnp.where(sel, jnp.inf, -jnp.inf)
            return seen
        pair_loop(body, jnp.zeros((1, tq), F32))

    surplus = jnp.max(jnp.where(small, 0.0, thr_cnt - kf)) > 0.5
    lax.cond(surplus, mask_with_ties, lambda: None)

    def qk_into(slot, c):
        kc = ka_ref[0, pl.ds(pl.multiple_of(c * KEY_CHUNK, KEY_CHUNK), KEY_CHUNK), :]
        logit_ref[slot] = lax.dot_general(kc, qq, contract_last, preferred_element_type=F32)

    def softmax_pv(slot, c, m_run, acc):
        s = score_ref[c]
        bias = jnp.where(jnp.logical_and(s >= thr, s > -jnp.inf), 0.0, NEG_BIG)
        vt = vt_ref[0, c]
        new_m, new_acc = [], []
        for hp in range(heads // 2):
            probs, alphas = [], []
            for hd in (2 * hp, 2 * hp + 1):
                sl = slice(hd * tq, (hd + 1) * tq)
                lh = logit_ref[slot, :, sl] + bias
                m_old = m_run[:, sl]
                m_new = jnp.maximum(m_old, jnp.max(lh, axis=0, keepdims=True))
                probs.append(jnp.exp2(lh - m_new).astype(BF16))
                alphas.append(jnp.exp2(m_old - m_new))
                new_m.append(m_new)
            pv = jnp.dot(vt, jnp.concatenate(probs, axis=1), preferred_element_type=F32)
            new_acc.append(jnp.concatenate(alphas, axis=1) * acc[:, 2 * hp * tq:(2 * hp + 2) * tq] + pv)
        return jnp.concatenate(new_m, axis=1), jnp.concatenate(new_acc, axis=1)

    def attn_body(j, carry):
        m_run, acc = carry
        qk_into(1, 2 * j + 1)
        m_run, acc = softmax_pv(0, 2 * j, m_run, acc)
        qk_into(0, jnp.minimum(2 * j + 2, nch - 1))
        return softmax_pv(1, 2 * j + 1, m_run, acc)

    qk_into(0, 0)
    d_rows = vt_ref.shape[2]
    init = (jnp.full((1, heads * tq), NEG_BIG, F32), jnp.zeros((d_rows, heads * tq), F32))
    carry = lax.fori_loop(0, nfull, attn_body, init)
    _, acc = lax.cond(odd, lambda cr: softmax_pv(0, nch - 1, *cr), lambda cr: cr, carry)
    for hp in range(heads // 2):
        halves = []
        for hd in (2 * hp, 2 * hp + 1):
            blk = acc[:, hd * tq:(hd + 1) * tq]
            halves.append(blk[:DSA_HEAD_DIM] / blk[DSA_HEAD_DIM:DSA_HEAD_DIM + 1, :])
        o_ref[0, hp] = jnp.concatenate(halves, axis=0).T.astype(o_ref.dtype)


def _dsa_attention(qq, iw_t, ka, va_t, ki, topk):
    b, s, wide = qq.shape
    tq = Q_TILE_DSA
    nkc = s // KEY_CHUNK
    assert iw_t.shape[-1] == tq and va_t.shape[-1] == KEY_CHUNK
    v_t = va_t.reshape(b, nkc, DSA_V_ROWS, KEY_CHUNK)
    qspec = pl.BlockSpec((1, tq, wide), lambda bi, i: (bi, i, 0))
    kspec = pl.BlockSpec((1, s, LANES), lambda bi, i: (bi, 0, 0))
    return pl.pallas_call(
        functools.partial(_dsa_kernel, topk=topk),
        grid=(b, s // tq),
        in_specs=[qspec, pl.BlockSpec((1, IDX_HEADS, tq), lambda bi, i: (bi * (s // tq) + i, 0, 0)), kspec,
                  pl.BlockSpec((1, nkc, DSA_V_ROWS, KEY_CHUNK), lambda bi, i: (bi, 0, 0, 0)), kspec],
        out_specs=pl.BlockSpec((1, DSA_HEADS // 2, tq, LANES), lambda bi, i: (bi, 0, i, 0)),
        out_shape=jax.ShapeDtypeStruct((b, DSA_HEADS // 2, s, LANES), BF16),
        scratch_shapes=[pltpu.VMEM((nkc, KEY_CHUNK, tq), F32),
                        pltpu.VMEM((2, KEY_CHUNK, DSA_HEADS * tq), F32)],
        compiler_params=_params(("parallel", "arbitrary")),
        name="dsa_attn",
    )(qq, iw_t, ka, v_t, ki)


def _mla_kernel(q_ref, k_ref, vt_ref, o_ref, logit_ref):
    i = pl.program_id(2)
    tq = q_ref.shape[1]
    group = q_ref.shape[2] // LANES
    d_rows = vt_ref.shape[3]
    per_block = tq // vt_ref.shape[4]
    qs = [q_ref[0, :, g * LANES:(g + 1) * LANES] for g in range(group)]
    contract_last = (((1,), (1,)), ((), ()))

    def qk_into(slot, c):
        k0 = pl.multiple_of(c * tq, tq)
        for g in range(group):
            logit_ref[slot, g] = lax.dot_general(k_ref[0, pl.ds(k0, tq), g * LANES:(g + 1) * LANES], qs[g],
                                                 contract_last, preferred_element_type=F32)

    def softmax_pv(slot, c, carry, masked):
        if masked:
            keep = (lax.broadcasted_iota(jnp.int32, (tq, tq), 0) <= lax.broadcasted_iota(jnp.int32, (tq, tq), 1))
            bias = jnp.where(keep, 0.0, NEG_BIG)
        probs, alphas, maxes = [], [], []
        for g in range(group):
            m_run = carry[2 * g]
            logit = logit_ref[slot, g]
            if masked:
                logit = logit + bias
            m_new = jnp.maximum(m_run, jnp.max(logit, axis=0, keepdims=True))
            probs.append(jnp.exp2(logit - m_new).astype(BF16))
            alphas.append(jnp.exp2(m_run - m_new))
            maxes.append(m_new)
        res = []
        for g in range(group):
            vt = jnp.concatenate([vt_ref[0, c * per_block + t, g] for t in range(per_block)], axis=1)
            pv = jnp.dot(vt, probs[g], preferred_element_type=F32)
            res += [maxes[g], alphas[g] * carry[2 * g + 1] + pv]
        return tuple(res)

    def pair_body(j, carry):
        qk_into(1, 2 * j + 1)
        carry = softmax_pv(0, 2 * j, carry, False)
        qk_into(0, 2 * j + 2)
        return softmax_pv(1, 2 * j + 1, carry, False)

    def tail_even(carry):
        return softmax_pv(0, i, carry, True)

    def tail_odd(carry):
        qk_into(1, i)
        carry = softmax_pv(0, i - 1, carry, False)
        return softmax_pv(1, i, carry, True)

    qk_into(0, 0)
    init = (jnp.full((1, tq), NEG_BIG, F32), jnp.zeros((d_rows, tq), F32)) * group
    carry = lax.fori_loop(0, i // 2, pair_body, init)
    carry = lax.cond(i % 2 == 0, tail_even, tail_odd, carry)
    for gp in range(group // 2):
        halves = [carry[2 * g + 1][:MLA_V] / carry[2 * g + 1][MLA_V:MLA_V + 1, :] for g in (2 * gp, 2 * gp + 1)]
        o_ref[0, :, gp * LANES:(gp + 1) * LANES] = jnp.concatenate(halves, axis=0).T.astype(o_ref.dtype)


def _mla_attention(mq, mk, mv_t):
    b, s, wide = mq.shape
    tq = Q_TILE_MLA
    nkc = s // tq
    gw = MLA_HEAD_GROUP * LANES
    nt, _, _, tm = mv_t.shape
    nt //= b
    assert tq % tm == 0
    v_t = mv_t.reshape(b, nt, MLA_HEADS, MLA_V_ROWS, tm)
    return pl.pallas_call(
        _mla_kernel,
        grid=(b, MLA_HEADS // MLA_HEAD_GROUP, s // tq),
        in_specs=[pl.BlockSpec((1, tq, gw), lambda bi, h, i: (bi, i, h)),
                  pl.BlockSpec((1, s, gw), lambda bi, h, i: (bi, 0, h)),
                  pl.BlockSpec((1, nt, MLA_HEAD_GROUP, MLA_V_ROWS, tm), lambda bi, h, i: (bi, 0, h, 0, 0))],
        out_specs=pl.BlockSpec((1, tq, gw // 2), lambda bi, h, i: (bi, i, h)),
        out_shape=jax.ShapeDtypeStruct((b, s, MLA_HEADS * MLA_V), BF16),
        scratch_shapes=[pltpu.VMEM((2, MLA_HEAD_GROUP, tq, tq), F32)],
        compiler_params=_params(("parallel", "parallel", "arbitrary")),
        name="mla_attn",
    )(mq, mk, v_t)


def _mem_kv_kernel(mem_ref, g_ref, wkv_ref, kg_ref, k_o, v_o):
    m = mem_ref[0]
    hm = _rms(m, g_ref[...], m.shape[-1]).astype(BF16)
    kv = jnp.dot(hm, wkv_ref[...], preferred_element_type=F32)
    for hd in range(MEM_HEADS):
        sl = slice(hd * LANES, (hd + 1) * LANES)
        k_o[0, :, sl] = _rms(kv[:, sl], kg_ref[...], MEM_HEAD_DIM).astype(BF16)
    v_o[0] = kv[:, MEM_HEADS * LANES:].astype(BF16)


def _mem_kv(mem, mem_g, w_kv, k_g):
    b, m, d = mem.shape
    hw = MEM_HEADS * MEM_HEAD_DIM
    wkv = jnp.concatenate([_pad_heads(w_kv[:, :hw], MEM_HEADS, MEM_HEAD_DIM),
                           _pad_heads(w_kv[:, hw:], MEM_HEADS, MEM_HEAD_DIM)], axis=1).astype(BF16)
    consts = [mem_g.reshape(1, d), wkv, _pad_cols(k_g.reshape(1, -1))]
    wide = MEM_HEADS * LANES
    spec = pl.BlockSpec((1, m, wide), lambda bi: (bi, 0, 0))
    return pl.pallas_call(
        _mem_kv_kernel,
        grid=(b,),
        in_specs=[pl.BlockSpec((1, m, d), lambda bi: (bi, 0, 0))] + [_const_spec(c.shape) for c in consts],
        out_specs=[spec, spec],
        out_shape=[jax.ShapeDtypeStruct((b, m, wide), BF16)] * 2,
        compiler_params=_params(("parallel",)),
        name="mem_kv",
    )(mem, *consts)


def _split_dot(a, w):
    a_hi = a.astype(BF16)
    a_lo = (a - a_hi.astype(F32)).astype(BF16)
    w_hi = w.astype(BF16)
    w_lo = (w - w_hi.astype(F32)).astype(BF16)
    w_both = jnp.concatenate([w_hi, w_lo], axis=1)
    n = w.shape[1]
    hi = jnp.dot(a_hi, w_both, preferred_element_type=F32)
    lo = jnp.dot(a_lo, w_both, preferred_element_type=F32)
    return (hi[:, :n] + hi[:, n:]) + (lo[:, :n] + lo[:, n:])


def _merge_kernel(oa_ref, ob_ref, sgd_ref, sgm_ref, x_ref, wa_ref, wb_ref, wo_ref, mxg_ref, wq_ref, qg_ref,
                  km_ref, vm_ref, wmo_ref, moeg_ref, wr_ref, bias_ref, x2_o, h3_o, comb_o):
    d = functools.partial(jnp.dot, preferred_element_type=F32)
    oa = jnp.concatenate([oa_ref[0, hp] for hp in range(DSA_HEADS // 2)], axis=-1)
    merged = (sgd_ref[0].astype(F32) * d(oa, wa_ref[...]) + sgm_ref[0].astype(F32) * d(ob_ref[0], wb_ref[...]))
    x1 = x_ref[0] + d(merged.astype(BF16), wo_ref[...])

    d_model = x1.shape[-1]
    h2 = _rms(x1, mxg_ref[...], d_model).astype(BF16)
    qm = d(h2, wq_ref[...])
    outs = []
    for hd in range(MEM_HEADS):
        sl = slice(hd * LANES, (hd + 1) * LANES)
        q = (_rms(qm[:, sl], qg_ref[...], MEM_HEAD_DIM) * MEM_HEAD_DIM ** -0.5).astype(BF16)
        logit = lax.dot_general(q, km_ref[0, :, sl], (((1,), (1,)), ((), ())), preferred_element_type=F32)
        logit = logit - jnp.max(logit, axis=-1, keepdims=True)
        p = jnp.exp(logit)
        pv = d(p.astype(BF16), vm_ref[0, :, sl])
        outs.append((pv / jnp.sum(p, axis=-1, keepdims=True)).astype(BF16))
    x2 = x1 + d(jnp.concatenate(outs, axis=-1), wmo_ref[...])
    x2_o[0] = x2

    h3 = _rms(x2, moeg_ref[...], d_model)
    h3_o[0] = h3.astype(BF16)
    logits = _split_dot(h3, wr_ref[...])
    tm = logits.shape[0]
    lane = lax.broadcasted_iota(jnp.int32, (tm, LANES), 1)
    big = jnp.int32(LANES)
    is_grp = jnp.logical_and(lane >= N_EXPERTS, lane < N_EXPERTS + N_GROUPS)
    glog = jnp.where(is_grp, logits, -jnp.inf)
    gmax = jnp.max(glog, axis=-1, keepdims=True)
    g_sel = jnp.min(jnp.where(glog == gmax, lane, big), axis=-1, keepdims=True) - N_EXPERTS
    p_g = 1.0 / jnp.sum(jnp.where(is_grp, jnp.exp(logits - gmax), 0.0), axis=-1, keepdims=True)
    aff = jax.nn.sigmoid(logits)
    in_grp = jnp.logical_and(lane >= g_sel * EXPERTS_PER_GROUP, lane < (g_sel + 1) * EXPERTS_PER_GROUP)
    val = jnp.where(in_grp, aff + bias_ref[...], -jnp.inf)
    m1 = jnp.max(val, axis=-1, keepdims=True)
    i1 = jnp.min(jnp.where(val == m1, lane, big), axis=-1, keepdims=True)
    val2 = jnp.where(lane == i1, -jnp.inf, val)
    m2 = jnp.max(val2, axis=-1, keepdims=True)
    i2 = jnp.min(jnp.where(val2 == m2, lane, big), axis=-1, keepdims=True)
    chosen = jnp.logical_or(lane == i1, lane == i2)
    a_sel = jnp.where(chosen, aff, 0.0)
    comb_o[0] = p_g * a_sel / jnp.sum(a_sel, axis=-1, keepdims=True)


def _merge_mem_router(oa, ob, sgd, sgm, x, w_br_dsa, w_br_mla, w_out, mem_x_g, mem_w_q, mem_q_g, km, vm,
                      mem_w_o, moe_g, w_group, w_expert, expert_bias, tm):
    b, s, d = x.shape
    wa = w_br_dsa.astype(BF16)
    wb = w_br_mla.astype(BF16)
    wq = _pad_heads(mem_w_q, MEM_HEADS, MEM_HEAD_DIM).astype(BF16)
    wmo = jnp.pad(mem_w_o.reshape(MEM_HEADS, MEM_HEAD_DIM, d), ((0, 0), (0, LANES - MEM_HEAD_DIM), (0, 0)))
    wmo = wmo.reshape(MEM_HEADS * LANES, d).astype(BF16)
    wr = _pad_cols(jnp.concatenate([w_expert, w_group], axis=1).astype(F32))
    bias = _pad_cols(expert_bias.reshape(1, -1).astype(F32))
    consts_a = [wa, wb, w_out.astype(BF16), mem_x_g.reshape(1, d), wq, _pad_cols(mem_q_g.reshape(1, -1))]
    consts_b = [wmo, moe_g.reshape(1, d), wr, bias]
    m = km.shape[1]
    mw = MEM_HEADS * LANES
    tok = lambda w: pl.BlockSpec((1, tm, w), lambda bi, i: (bi, i, 0))
    memspec = pl.BlockSpec((1, m, mw), lambda bi, i: (bi, 0, 0))
    return pl.pallas_call(
        _merge_kernel,
        grid=(b, s // tm),
        in_specs=[pl.BlockSpec((1, DSA_HEADS // 2, tm, LANES), lambda bi, i: (bi, 0, i, 0)),
                  tok(MLA_HEADS * MLA_V), tok(d), tok(d), tok(d)]
                 + [_const_spec(c.shape) for c in consts_a] + [memspec, memspec]
                 + [_const_spec(c.shape) for c in consts_b],
        out_specs=[tok(d), tok(d), tok(LANES)],
        out_shape=[jax.ShapeDtypeStruct((b, s, d), F32), jax.ShapeDtypeStruct((b, s, d), BF16),
                   jax.ShapeDtypeStruct((b, s, LANES), F32)],
        compiler_params=_params(("parallel", "parallel")),
        name="merge_mem",
    )(oa, ob, sgd, sgm, x, *consts_a, km, vm, *consts_b)


def _moe_kernel(h_ref, comb_ref, x_ref, wg_ref, wu_ref, wd_ref, o_ref):
    step = pl.program_id(1)
    per = wg_ref.shape[0]
    h = h_ref[...]
    comb = comb_ref[...]
    lane = lax.broadcasted_iota(jnp.int32, comb.shape, 1)
    parts = []
    for e in range(per):
        gate = jnp.dot(h, wg_ref[e], preferred_element_type=F32)
        up = jnp.dot(h, wu_ref[e], preferred_element_type=F32)
        c_e = jnp.sum(jnp.where(lane == step * per + e, comb, 0.0), axis=-1, keepdims=True)
        parts.append(((jax.nn.silu(gate) * up) * c_e).astype(BF16))
    w_down = wd_ref[...].reshape(per * D_EXPERT, wd_ref.shape[-1])
    contrib = jnp.dot(jnp.concatenate(parts, axis=1), w_down, preferred_element_type=F32)

    @pl.when(step == 0)
    def _():
        o_ref[...] = x_ref[...] + contrib

    @pl.when(step > 0)
    def _():
        o_ref[...] += contrib


def _moe(h3, comb, x2, w_gate, w_up, w_down, tm):
    n, d = x2.shape
    per = MOE_EXPERTS_PER_STEP
    return pl.pallas_call(
        _moe_kernel,
        grid=(n // tm, N_EXPERTS // per),
        in_specs=[pl.BlockSpec((tm, d), lambda i, g: (i, 0)),
                  pl.BlockSpec((tm, LANES), lambda i, g: (i, 0)),
                  pl.BlockSpec((tm, d), lambda i, g: (i, 0)),
                  pl.BlockSpec((per, d, D_EXPERT), lambda i, g: (g, 0, 0), pipeline_mode=pl.Buffered(1)),
                  pl.BlockSpec((per, d, D_EXPERT), lambda i, g: (g, 0, 0), pipeline_mode=pl.Buffered(1)),
                  pl.BlockSpec((per, D_EXPERT, d), lambda i, g: (g, 0, 0), pipeline_mode=pl.Buffered(1))],
        out_specs=pl.BlockSpec((tm, d), lambda i, g: (i, 0)),
        out_shape=jax.ShapeDtypeStruct((n, d), F32),
        compiler_params=_params(("parallel", "arbitrary")),
        name="moe",
    )(h3, comb, x2, w_gate.astype(BF16), w_up.astype(BF16), w_down.astype(BF16))


def _layer(x, mem, positions, p):
    b, s, d = x.shape
    n = b * s
    assert s % (2 * KEY_CHUNK) == 0 and s % Q_TILE_MLA == 0
    topk = min(TOPK_MAX, s // 4)
    tm = min(TOKENS_PER_STEP_PROJ, s)
    qq, ka, va, ki, iw, mq, mk, mv, sgd, sgm = _in_proj(
        x.reshape(n, d), positions.reshape(n, 1), p["attn_norm_g"], p["w_in"], p["dsa_q_norm_g"],
        p["dsa_k_norm_g"], p["mla_cq_norm_g"], p["mla_ckv_norm_g"], p["mla_w_uq"], p["mla_w_ukv"],
        p["mla_q_norm_g"], p["mla_k_norm_g"], tm)
    r3 = lambda a: a.reshape(b, s, a.shape[-1])
    oa = _dsa_attention(r3(qq), iw, r3(ka), va, r3(ki), topk)
    ob = _mla_attention(r3(mq), r3(mk), mv)
    km, vm = _mem_kv(mem, p["mem_norm_g"], p["mem_w_kv"], p["mem_k_norm_g"])
    x2, h3, comb = _merge_mem_router(
        oa, ob, r3(sgd), r3(sgm), x, p["w_branch_dsa"], p["w_branch_mla"], p["w_out"], p["mem_x_norm_g"],
        p["mem_w_q"], p["mem_q_norm_g"], km, vm, p["mem_w_o"], p["moe_norm_g"], p["moe_w_group"],
        p["moe_w_expert"], p["moe_expert_bias"], min(TOKENS_PER_STEP_MERGE, s))
    out = _moe(h3.reshape(n, d), comb.reshape(n, LANES), x2.reshape(n, d), p["moe_w_gate"], p["moe_w_up"],
               p["moe_w_down"], min(TOKENS_PER_STEP_MOE, n))
    return out.reshape(b, s, d)


_PARAM_NAMES = ("attn_norm_g", "w_in", "dsa_q_norm_g", "dsa_k_norm_g", "mla_cq_norm_g", "mla_ckv_norm_g",
                "mla_w_uq", "mla_w_ukv", "mla_q_norm_g", "mla_k_norm_g", "w_branch_dsa", "w_branch_mla", "w_out",
                "mem_x_norm_g", "mem_norm_g", "mem_w_q", "mem_w_kv", "mem_q_norm_g", "mem_k_norm_g", "mem_w_o",
                "moe_norm_g", "moe_w_group", "moe_w_expert", "moe_expert_bias", "moe_w_gate", "moe_w_up",
                "moe_w_down")


def kernel(x, mem, positions, attn_norm_g, w_in, dsa_q_norm_g, dsa_k_norm_g, mla_cq_norm_g, mla_ckv_norm_g, mla_w_uq, mla_w_ukv, mla_q_norm_g, mla_k_norm_g, w_branch_dsa, w_branch_mla, w_out, mem_x_norm_g, mem_norm_g, mem_w_q, mem_w_kv, mem_q_norm_g, mem_k_norm_g, mem_w_o, moe_norm_g, moe_w_group, moe_w_expert, moe_expert_bias, moe_w_gate, moe_w_up, moe_w_down):
    stacked = (attn_norm_g, w_in, dsa_q_norm_g, dsa_k_norm_g, mla_cq_norm_g, mla_ckv_norm_g, mla_w_uq, mla_w_ukv,
               mla_q_norm_g, mla_k_norm_g, w_branch_dsa, w_branch_mla, w_out, mem_x_norm_g, mem_norm_g, mem_w_q,
               mem_w_kv, mem_q_norm_g, mem_k_norm_g, mem_w_o, moe_norm_g, moe_w_group, moe_w_expert,
               moe_expert_bias, moe_w_gate, moe_w_up, moe_w_down)
    for layer in range(attn_norm_g.shape[0]):
        p = {name: arr[layer] for name, arr in zip(_PARAM_NAMES, stacked)}
        x = _layer(x, mem, positions, p)
    return x
```

```python
import functools

import jax
import jax.numpy as jnp
from jax import lax
from jax.experimental import pallas as pl
from jax.experimental.pallas import tpu as pltpu

F32 = jnp.float32
BF16 = jnp.bfloat16

LANES = 128
SUBLANES = 8
VMEM_LIMIT = 56 * 1024 * 1024

ROPE_THETA = 10000.0
EPS = 1e-6
DSA_HEADS = 8
DSA_HEAD_DIM = 64
IDX_HEADS = 8
IDX_DIM = 64
TOPK_MAX = 256
MLA_HEADS = 8
MLA_NOPE = 64
MLA_ROPE = 32
MLA_QK = MLA_NOPE + MLA_ROPE
MLA_V = 64
MLA_Q_RANK = 256
MLA_KV_RANK = 128
MEM_HEADS = 4
MEM_HEAD_DIM = 64
N_GROUPS = 4
EXPERTS_PER_GROUP = 4
N_EXPERTS = N_GROUPS * EXPERTS_PER_GROUP
D_EXPERT = 256

NEG_BIG = -1e30
KEY_CHUNK = 256
Q_TILE_DSA = 256
DSA_V_ROWS = 80
MLA_V_ROWS = 80
Q_TILE_MLA = 512
MLA_HEAD_GROUP = 4
LOG2_E = 1.4426950408889634
BISECT_ITERS = 15
TOKENS_PER_STEP_PROJ = 256
TOKENS_PER_STEP_MERGE = 1024
TOKENS_PER_STEP_MOE = 1024
MOE_EXPERTS_PER_STEP = 16


def _const_spec(shape, resident=False):
    nd = len(shape)
    return pl.BlockSpec(shape, lambda *_: (0,) * nd, pipeline_mode=pl.Buffered(1) if resident else None)


def _params(sem):
    return pltpu.CompilerParams(dimension_semantics=sem, vmem_limit_bytes=VMEM_LIMIT)


def _rms(x, g, n):
    ms = jnp.sum(x * x, axis=-1, keepdims=True) * (1.0 / n)
    return (x * lax.rsqrt(ms + EPS)) * g


def _rope(x, cos, sin_signed, lo_mask, half):
    fwd = pltpu.roll(x, LANES - half, 1)
    bwd = pltpu.roll(x, half, 1)
    return x * cos + jnp.where(lo_mask, fwd, bwd) * sin_signed


def _in_proj_kernel(x_ref, pos_ref, g_ref, wqq_ref, wsm_ref, wcq_ref, wg_ref, wuq_ref, wuk_ref,
                    wuv_ref, qag_ref, kag_ref, cqg_ref, ckvg_ref, mqg_ref, mkg_ref, freq_ref,
                    qq_o, ka_o, va_o, ki_o, iw_o, mq_o, mk_o, mv_o, sgd_o, sgm_o):
    x = x_ref[...]
    d_model = x.shape[-1]
    h = _rms(x, g_ref[...], d_model).astype(BF16)

    pos = pos_ref[...].astype(F32)
    lane = lax.broadcasted_iota(jnp.int32, (x.shape[0], LANES), 1)
    half_d, half_m = DSA_HEAD_DIM // 2, MLA_ROPE // 2
    ang = pos * freq_ref[...]
    lo_d = (lane & (DSA_HEAD_DIM - 1)) < half_d
    lo_m = lane < MLA_NOPE + half_m
    in_m_lo = jnp.logical_and(lane >= MLA_NOPE, lo_m)
    in_m_hi = jnp.logical_and(lane >= MLA_NOPE + half_m, lane < MLA_QK)

    def tables(t, fill):
        by_half = pltpu.roll(t, half_d, 1)
        head = jnp.where(lane < half_d, t, by_half)
        dsa = jnp.where(lane < DSA_HEAD_DIM, head, pltpu.roll(head, DSA_HEAD_DIM, 1))
        mla = jnp.where(in_m_lo, by_half, jnp.where(in_m_hi, pltpu.roll(t, half_d + half_m, 1), fill))
        return dsa, mla

    cos_d, cos_m = tables(jnp.cos(ang), 1.0)
    sin_d, sin_m = tables(jnp.sin(ang), 0.0)
    sin_d = jnp.where(lo_d, -sin_d, sin_d)
    sin_m = jnp.where(lo_m, -sin_m, sin_m)
    rope_d = functools.partial(_rope, cos=cos_d, sin_signed=sin_d, lo_mask=lo_d, half=DSA_HEAD_DIM // 2)
    rope_m = functools.partial(_rope, cos=cos_m, sin_signed=sin_m, lo_mask=lo_m, half=MLA_ROPE // 2)

    def dot(a, w_ref):
        return jnp.dot(a, w_ref[...], preferred_element_type=F32)

    att_scale = DSA_HEAD_DIM ** -0.5 * LOG2_E
    idx_scale = IDX_DIM ** -0.5 * IDX_HEADS ** -0.5
    mla_scale = MLA_QK ** -0.5 * LOG2_E

    qq = dot(h, wqq_ref)
    is_qa = lane < DSA_HEAD_DIM
    for hd in range(DSA_HEADS):
        sl = slice(hd * LANES, (hd + 1) * LANES)
        xh = qq[:, sl]
        ms = jnp.sum(jnp.where(is_qa, xh * xh, 0.0), axis=-1, keepdims=True) * (1.0 / DSA_HEAD_DIM)
        mult = jnp.where(is_qa, (lax.rsqrt(ms + EPS) * att_scale) * qag_ref[...], 1.0)
        qq_o[:, sl] = rope_d(xh * mult).astype(BF16)

    sm = dot(h, wsm_ref)
    ka_o[...] = rope_d(_rms(sm[:, 0:LANES], kag_ref[...], DSA_HEAD_DIM)).astype(BF16)
    va = sm[:, LANES:2 * LANES] + jnp.where(lane[0:1] == DSA_HEAD_DIM, 1.0, 0.0)
    va_o[0] = va.T[:DSA_V_ROWS].astype(BF16)
    ki_o[...] = rope_d(sm[:, 2 * LANES:3 * LANES]).astype(BF16)
    iw_o[0] = (sm[:, 3 * LANES:4 * LANES] * idx_scale).T[:IDX_HEADS]
    ckv = _rms(sm[:, 4 * LANES:5 * LANES], ckvg_ref[...], MLA_KV_RANK).astype(BF16)
    kpe = sm[:, 5 * LANES:6 * LANES]

    kn = dot(ckv, wuk_ref)
    mkg = mkg_ref[...]
    pe_rot = rope_m(kpe * mkg)
    ss_pe = jnp.sum(kpe * kpe, axis=-1, keepdims=True)
    for hd in range(MLA_HEADS):
        sl = slice(hd * LANES, (hd + 1) * LANES)
        xh = kn[:, sl]
        ms = (jnp.sum(xh * xh, axis=-1, keepdims=True) + ss_pe) * (1.0 / MLA_QK)
        mk_o[:, sl] = ((xh * mkg + pe_rot) * lax.rsqrt(ms + EPS)).astype(BF16)
    wide_lane = lax.broadcasted_iota(jnp.int32, (1, MLA_HEADS * LANES), 1)
    ones_col = jnp.where(wide_lane % LANES == MLA_V, 1.0, 0.0)
    mv = dot(ckv, wuv_ref) + ones_col
    for hd in range(MLA_HEADS):
        mv_o[0, hd] = mv[:, hd * LANES:(hd + 1) * LANES].T[:MLA_V_ROWS].astype(BF16)

    cq = _rms(dot(h, wcq_ref), cqg_ref[...], MLA_Q_RANK).astype(BF16)
    qb = dot(cq, wuq_ref)
    for hd in range(MLA_HEADS):
        sl = slice(hd * LANES, (hd + 1) * LANES)
        y = rope_m(_rms(qb[:, sl], mqg_ref[...], MLA_QK))
        mq_o[:, sl] = (y * mla_scale).astype(BF16)

    gates = jax.nn.sigmoid(dot(h, wg_ref))
    sgd_o[...] = gates[:, :d_model].astype(BF16)
    sgm_o[...] = gates[:, d_model:].astype(BF16)


def _pad_heads(w, heads, dim):
    k = w.shape[0]
    w = w.reshape(k, heads, dim)
    w = jnp.pad(w, ((0, 0), (0, 0), (0, LANES - dim)))
    return w.reshape(k, heads * LANES)


def _pad_cols(w, width=LANES, offset=0):
    return jnp.pad(w, ((0, 0), (offset, width - offset - w.shape[1])))


def _in_proj(x2, pos2, attn_g, w_in, dsa_q_g, dsa_k_g, cq_g, ckv_g, w_uq, w_ukv, mq_g, mk_g, tm):
    n, d = x2.shape
    sizes = (DSA_HEADS * DSA_HEAD_DIM, DSA_HEAD_DIM, DSA_HEAD_DIM, IDX_HEADS * IDX_DIM, IDX_DIM, IDX_HEADS,
             MLA_Q_RANK, MLA_KV_RANK, MLA_ROPE, d, d)
    offs = [0]
    for s in sizes:
        offs.append(offs[-1] + s)
    seg = [w_in[:, offs[i]:offs[i + 1]] for i in range(len(sizes))]
    w_dq, w_dk, w_dv, w_iq, w_ik, w_iw, w_cq, w_ckv, w_kpe, w_gd, w_gm = seg

    assert DSA_HEADS == IDX_HEADS and DSA_HEAD_DIM + IDX_DIM == LANES
    wqq = jnp.concatenate([w_dq.reshape(d, DSA_HEADS, DSA_HEAD_DIM), w_iq.reshape(d, IDX_HEADS, IDX_DIM)], axis=-1)
    wqq = wqq.reshape(d, DSA_HEADS * LANES).astype(BF16)
    wsm = jnp.concatenate([_pad_cols(w_dk), _pad_cols(w_dv), _pad_cols(w_ik, offset=DSA_HEAD_DIM), _pad_cols(w_iw), w_ckv,
                           _pad_cols(w_kpe, offset=MLA_NOPE)], axis=1).astype(BF16)
    wcq = w_cq.astype(BF16)
    wg = jnp.concatenate([w_gd, w_gm], axis=1).astype(BF16)
    wuq = _pad_heads(w_uq, MLA_HEADS, MLA_QK).astype(BF16)
    ukv = w_ukv.reshape(MLA_KV_RANK, MLA_HEADS, MLA_NOPE + MLA_V)
    wuk = _pad_heads(ukv[:, :, :MLA_NOPE].reshape(MLA_KV_RANK, -1), MLA_HEADS, MLA_NOPE).astype(BF16)
    wuv = _pad_heads(ukv[:, :, MLA_NOPE:].reshape(MLA_KV_RANK, -1), MLA_HEADS, MLA_V).astype(BF16)

    row = lambda v: _pad_cols(v.reshape(1, -1).astype(F32), width=max(LANES, v.size))
    half_d = DSA_HEAD_DIM // 2
    inv_d = ROPE_THETA ** (-jnp.arange(half_d, dtype=F32) / half_d)
    half_m = MLA_ROPE // 2
    inv_m = ROPE_THETA ** (-jnp.arange(half_m, dtype=F32) / half_m)
    freq = _pad_cols(jnp.concatenate([inv_d, inv_m]).reshape(1, -1))

    consts = [attn_g.reshape(1, d), wqq, wsm, wcq, wg, wuq, wuk, wuv, row(dsa_q_g), row(dsa_k_g),
              row(cq_g), row(ckv_g), row(mq_g), row(mk_g), freq]
    wide = DSA_HEADS * LANES
    out_shapes = [
        jax.ShapeDtypeStruct((n, wide), BF16),
        jax.ShapeDtypeStruct((n, LANES), BF16),
        jax.ShapeDtypeStruct((n // tm, DSA_V_ROWS, tm), BF16),
        jax.ShapeDtypeStruct((n, LANES), BF16),
        jax.ShapeDtypeStruct((n // tm, IDX_HEADS, tm), F32),
        jax.ShapeDtypeStruct((n, wide), BF16),
        jax.ShapeDtypeStruct((n, wide), BF16),
        jax.ShapeDtypeStruct((n // tm, MLA_HEADS, MLA_V_ROWS, tm), BF16),
        jax.ShapeDtypeStruct((n, d), BF16),
        jax.ShapeDtypeStruct((n, d), BF16),
    ]
    tile = lambda w: pl.BlockSpec((tm, w), lambda i: (i, 0))
    return pl.pallas_call(
        _in_proj_kernel,
        grid=(n // tm,),
        in_specs=[tile(d), tile(1)] + [_const_spec(c.shape) for c in consts],
        out_specs=[tile(s.shape[1]) if len(s.shape) == 2 else
                   pl.BlockSpec((1,) + s.shape[1:], lambda i, nd=len(s.shape): (i,) + (0,) * (nd - 1))
                   for s in out_shapes],
        out_shape=out_shapes,
        compiler_params=_params(("parallel",)),
        name="in_proj",
    )(x2, pos2, *consts)


def _dsa_kernel(qq_ref, iw_ref, ka_ref, vt_ref, ki_ref, o_ref, score_ref, logit_ref, *, topk):
    i = pl.program_id(1)
    tq = qq_ref.shape[1]
    heads = DSA_HEADS
    nch = (i * tq + tq + KEY_CHUNK - 1) // KEY_CHUNK
    npair = (nch + 1) // 2
    nfull = nch // 2
    odd = nch % 2 == 1
    kf = float(topk)
    shape = (KEY_CHUNK, tq)
    folded = (SUBLANES, tq)

    def fold(a, op):
        return op(a.reshape(KEY_CHUNK // SUBLANES, SUBLANES, tq), axis=0)

    krow = lax.broadcasted_iota(jnp.int32, shape, 0)
    qcol = i * tq + lax.broadcasted_iota(jnp.int32, shape, 1)
    contract_last = (((1,), (1,)), ((), ()))

    def pair_loop(body, init):
        return lax.fori_loop(0, npair, lambda j, carry: body(2 * j, 2 * j + 1, carry), init)

    qq = jnp.concatenate([qq_ref[0, :, hd * LANES:(hd + 1) * LANES] for hd in range(heads)], axis=0)
    iw = iw_ref[0]

    def score_chunk(c, mx, mn):
        kc = ki_ref[0, pl.ds(pl.multiple_of(c * KEY_CHUNK, KEY_CHUNK), KEY_CHUNK), :]
        sc = None
        for hp in range(IDX_HEADS // 2):
            rel = lax.dot_general(kc, qq[2 * hp * tq:(2 * hp + 2) * tq], contract_last,
                                  preferred_element_type=F32)
            rel = jnp.maximum(rel, 0.0)
            part = rel[:, :tq] * iw[2 * hp:2 * hp + 1, :] + rel[:, tq:] * iw[2 * hp + 1:2 * hp + 2, :]
            sc = part if sc is None else sc + part
        causal = (krow + c * KEY_CHUNK) <= qcol
        masked = jnp.where(causal, sc, -jnp.inf)
        score_ref[c] = masked
        return (jnp.maximum(mx, fold(masked, jnp.max)),
                jnp.minimum(mn, fold(jnp.where(causal, sc, jnp.inf), jnp.min)))

    def score_body(j, carry):
        return score_chunk(2 * j + 1, *score_chunk(2 * j, *carry))

    def score_tail(carry):
        score_ref[nch] = jnp.full(shape, -jnp.inf, F32)
        return score_chunk(nch - 1, *carry)

    carry = lax.fori_loop(0, nfull, score_body, (jnp.full(folded, -jnp.inf, F32), jnp.full(folded, jnp.inf, F32)))
    mx, mn = lax.cond(odd, score_tail, lambda cr: cr, carry)
    hi = jnp.max(mx, axis=0, keepdims=True)
    lo = jnp.min(mn, axis=0, keepdims=True)

    def count_gt(t):
        def body(c0, c1, acc):
            for c in (c0, c1):
                acc = acc + fold(jnp.where(score_ref[c] > t, 1.0, 0.0), jnp.sum)
            return acc
        return jnp.sum(pair_loop(body, jnp.zeros(folded, F32)), axis=0, keepdims=True)

    def bisect(_, carry):
        lo, hi = carry
        mid = 0.5 * (lo + hi)
        below = count_gt(mid) < kf
        return jnp.where(below, lo, mid), jnp.where(below, mid, hi)

    lo, hi = lax.fori_loop(0, BISECT_ITERS, bisect, (lo, hi))

    def max_le(t):
        def body(c0, c1, acc):
            for c in (c0, c1):
                s = score_ref[c]
                acc = jnp.maximum(acc, fold(jnp.where(s <= t, s, -jnp.inf), jnp.max))
            return acc
        return jnp.max(pair_loop(body, jnp.full(folded, -jnp.inf, F32)), axis=0, keepdims=True)

    n_causal = (i * tq + lax.broadcasted_iota(jnp.int32, (1, tq), 1) + 1).astype(F32)
    small = n_causal <= kf

    def refine_cond(carry):
        return jnp.min(carry[2]) < 0.5

    def refine_body(carry):
        m, thr, done, thr_cnt = carry

        def body(c0, c1, acc):
            cnt, nxt = acc
            for c in (c0, c1):
                s = score_ref[c]
                cnt = cnt + fold(jnp.where(s >= m, 1.0, 0.0), jnp.sum)
                nxt = jnp.maximum(nxt, fold(jnp.where(s < m, s, -jnp.inf), jnp.max))
            return cnt, nxt

        cnt, nxt = pair_loop(body, (jnp.zeros(folded, F32), jnp.full(folded, -jnp.inf, F32)))
        cnt = jnp.sum(cnt, axis=0, keepdims=True)
        nxt = jnp.max(nxt, axis=0, keepdims=True)
        reached = cnt >= kf
        hit = jnp.logical_and(reached, done < 0.5)
        return nxt, jnp.where(hit, m, thr), jnp.where(reached, 1.0, done), jnp.where(hit, cnt, thr_cnt)

    init = (max_le(hi), jnp.full((1, tq), -jnp.inf, F32), jnp.where(small, 1.0, 0.0), jnp.zeros((1, tq), F32))
    _, thr, _, thr_cnt = lax.while_loop(refine_cond, refine_body, init)

    def mask_with_ties():
        quota = jnp.where(small, 0.0, kf - count_gt(thr))
        earlier = (lax.broadcasted_iota(jnp.int32, (KEY_CHUNK, KEY_CHUNK), 1)
                   < lax.broadcasted_iota(jnp.int32, (KEY_CHUNK, KEY_CHUNK), 0)).astype(BF16)

        def body(c0, c1, seen):
            for c in (c0, c1):
                s = score_ref[c]
                eq = s == thr
                eqf = jnp.where(eq, 1.0, 0.0)
                before = jnp.dot(earlier, eqf.astype(BF16), preferred_element_type=F32) + seen
                sel = jnp.logical_or(s > thr, jnp.logical_and(eq, before < quota))
                seen = seen + jnp.sum(eqf, axis=0, keepdims=True)
                score_ref[c] = jnp.where(sel, jnp.inf, -jnp.inf)
            return seen
        pair_loop(body, jnp.zeros((1, tq), F32))

    surplus = jnp.max(jnp.where(small, 0.0, thr_cnt - kf)) > 0.5
    lax.cond(surplus, mask_with_ties, lambda: None)

    def qk_into(slot, c):
        kc = ka_ref[0, pl.ds(pl.multiple_of(c * KEY_CHUNK, KEY_CHUNK), KEY_CHUNK), :]
        logit_ref[slot] = lax.dot_general(kc, qq, contract_last, preferred_element_type=F32)

    def softmax_pv(slot, c, m_run, acc):
        s = score_ref[c]
        bias = jnp.where(jnp.logical_and(s >= thr, s > -jnp.inf), 0.0, NEG_BIG)
        vt = vt_ref[0, c]
        new_m, new_acc = [], []
        for hp in range(heads // 2):
            probs, alphas = [], []
            for hd in (2 * hp, 2 * hp + 1):
                sl = slice(hd * tq, (hd + 1) * tq)
                lh = logit_ref[slot, :, sl] + bias
                m_old = m_run[:, sl]
                m_new = jnp.maximum(m_old, jnp.max(lh, axis=0, keepdims=True))
                probs.append(jnp.exp2(lh - m_new).astype(BF16))
                alphas.append(jnp.exp2(m_old - m_new))
                new_m.append(m_new)
            pv = jnp.dot(vt, jnp.concatenate(probs, axis=1), preferred_element_type=F32)
            new_acc.append(jnp.concatenate(alphas, axis=1) * acc[:, 2 * hp * tq:(2 * hp + 2) * tq] + pv)
        return jnp.concatenate(new_m, axis=1), jnp.concatenate(new_acc, axis=1)

    def attn_body(j, carry):
        m_run, acc = carry
        qk_into(1, 2 * j + 1)
        m_run, acc = softmax_pv(0, 2 * j, m_run, acc)
        qk_into(0, jnp.minimum(2 * j + 2, nch - 1))
        return softmax_pv(1, 2 * j + 1, m_run, acc)

    qk_into(0, 0)
    d_rows = vt_ref.shape[2]
    init = (jnp.full((1, heads * tq), NEG_BIG, F32), jnp.zeros((d_rows, heads * tq), F32))
    carry = lax.fori_loop(0, nfull, attn_body, init)
    _, acc = lax.cond(odd, lambda cr: softmax_pv(0, nch - 1, *cr), lambda cr: cr, carry)
    for hp in range(heads // 2):
        halves = []
        for hd in (2 * hp, 2 * hp + 1):
            blk = acc[:, hd * tq:(hd + 1) * tq]
            halves.append(blk[:DSA_HEAD_DIM] / blk[DSA_HEAD_DIM:DSA_HEAD_DIM + 1, :])
        o_ref[0, hp] = jnp.concatenate(halves, axis=0).T.astype(o_ref.dtype)


def _dsa_attention(qq, iw_t, ka, va_t, ki, topk):
    b, s, wide = qq.shape
    tq = Q_TILE_DSA
    nkc = s // KEY_CHUNK
    assert iw_t.shape[-1] == tq and va_t.shape[-1] == KEY_CHUNK
    v_t = va_t.reshape(b, nkc, DSA_V_ROWS, KEY_CHUNK)
    qspec = pl.BlockSpec((1, tq, wide), lambda bi, i: (bi, i, 0))
    kspec = pl.BlockSpec((1, s, LANES), lambda bi, i: (bi, 0, 0))
    return pl.pallas_call(
        functools.partial(_dsa_kernel, topk=topk),
        grid=(b, s // tq),
        in_specs=[qspec, pl.BlockSpec((1, IDX_HEADS, tq), lambda bi, i: (bi * (s // tq) + i, 0, 0)), kspec,
                  pl.BlockSpec((1, nkc, DSA_V_ROWS, KEY_CHUNK), lambda bi, i: (bi, 0, 0, 0)), kspec],
        out_specs=pl.BlockSpec((1, DSA_HEADS // 2, tq, LANES), lambda bi, i: (bi, 0, i, 0)),
        out_shape=jax.ShapeDtypeStruct((b, DSA_HEADS // 2, s, LANES), BF16),
        scratch_shapes=[pltpu.VMEM((nkc, KEY_CHUNK, tq), F32),
                        pltpu.VMEM((2, KEY_CHUNK, DSA_HEADS * tq), F32)],
        compiler_params=_params(("parallel", "arbitrary")),
        name="dsa_attn",
    )(qq, iw_t, ka, v_t, ki)


def _mla_kernel(q_ref, k_ref, vt_ref, o_ref, logit_ref):
    i = pl.program_id(2)
    tq = q_ref.shape[1]
    group = q_ref.shape[2] // LANES
    d_rows = vt_ref.shape[3]
    per_block = tq // vt_ref.shape[4]
    qs = [q_ref[0, :, g * LANES:(g + 1) * LANES] for g in range(group)]
    contract_last = (((1,), (1,)), ((), ()))

    def qk_into(slot, c):
        k0 = pl.multiple_of(c * tq, tq)
        for g in range(group):
            logit_ref[slot, g] = lax.dot_general(k_ref[0, pl.ds(k0, tq), g * LANES:(g + 1) * LANES], qs[g],
                                                 contract_last, preferred_element_type=F32)

    def softmax_pv(slot, c, carry, masked):
        if masked:
            keep = (lax.broadcasted_iota(jnp.int32, (tq, tq), 0) <= lax.broadcasted_iota(jnp.int32, (tq, tq), 1))
            bias = jnp.where(keep, 0.0, NEG_BIG)
        probs, alphas, maxes = [], [], []
        for g in range(group):
            m_run = carry[2 * g]
            logit = logit_ref[slot, g]
            if masked:
                logit = logit + bias
            m_new = jnp.maximum(m_run, jnp.max(logit, axis=0, keepdims=True))
            probs.append(jnp.exp2(logit - m_new).astype(BF16))
            alphas.append(jnp.exp2(m_run - m_new))
            maxes.append(m_new)
        res = []
        for g in range(group):
            vt = jnp.concatenate([vt_ref[0, c * per_block + t, g] for t in range(per_block)], axis=1)
            pv = jnp.dot(vt, probs[g], preferred_element_type=F32)
            res += [maxes[g], alphas[g] * carry[2 * g + 1] + pv]
        return tuple(res)

    def pair_body(j, carry):
        qk_into(1, 2 * j + 1)
        carry = softmax_pv(0, 2 * j, carry, False)
        qk_into(0, 2 * j + 2)
        return softmax_pv(1, 2 * j + 1, carry, False)

    def tail_even(carry):
        return softmax_pv(0, i, carry, True)

    def tail_odd(carry):
        qk_into(1, i)
        carry = softmax_pv(0, i - 1, carry, False)
        return softmax_pv(1, i, carry, True)

    qk_into(0, 0)
    init = (jnp.full((1, tq), NEG_BIG, F32), jnp.zeros((d_rows, tq), F32)) * group
    carry = lax.fori_loop(0, i // 2, pair_body, init)
    carry = lax.cond(i % 2 == 0, tail_even, tail_odd, carry)
    for gp in range(group // 2):
        halves = [carry[2 * g + 1][:MLA_V] / carry[2 * g + 1][MLA_V:MLA_V + 1, :] for g in (2 * gp, 2 * gp + 1)]
        o_ref[0, :, gp * LANES:(gp + 1) * LANES] = jnp.concatenate(halves, axis=0).T.astype(o_ref.dtype)


def _mla_attention(mq, mk, mv_t):
    b, s, wide = mq.shape
    tq = Q_TILE_MLA
    nkc = s // tq
    gw = MLA_HEAD_GROUP * LANES
    nt, _, _, tm = mv_t.shape
    nt //= b
    assert tq % tm == 0
    v_t = mv_t.reshape(b, nt, MLA_HEADS, MLA_V_ROWS, tm)
    return pl.pallas_call(
        _mla_kernel,
        grid=(b, MLA_HEADS // MLA_HEAD_GROUP, s // tq),
        in_specs=[pl.BlockSpec((1, tq, gw), lambda bi, h, i: (bi, i, h)),
                  pl.BlockSpec((1, s, gw), lambda bi, h, i: (bi, 0, h)),
                  pl.BlockSpec((1, nt, MLA_HEAD_GROUP, MLA_V_ROWS, tm), lambda bi, h, i: (bi, 0, h, 0, 0))],
        out_specs=pl.BlockSpec((1, tq, gw // 2), lambda bi, h, i: (bi, i, h)),
        out_shape=jax.ShapeDtypeStruct((b, s, MLA_HEADS * MLA_V), BF16),
        scratch_shapes=[pltpu.VMEM((2, MLA_HEAD_GROUP, tq, tq), F32)],
        compiler_params=_params(("parallel", "parallel", "arbitrary")),
        name="mla_attn",
    )(mq, mk, v_t)


def _mem_kv_kernel(mem_ref, g_ref, wkv_ref, kg_ref, k_o, v_o):
    m = mem_ref[0]
    hm = _rms(m, g_ref[...], m.shape[-1]).astype(BF16)
    kv = jnp.dot(hm, wkv_ref[...], preferred_element_type=F32)
    for hd in range(MEM_HEADS):
        sl = slice(hd * LANES, (hd + 1) * LANES)
        k_o[0, :, sl] = _rms(kv[:, sl], kg_ref[...], MEM_HEAD_DIM).astype(BF16)
    v_o[0] = kv[:, MEM_HEADS * LANES:].astype(BF16)


def _mem_kv(mem, mem_g, w_kv, k_g):
    b, m, d = mem.shape
    hw = MEM_HEADS * MEM_HEAD_DIM
    wkv = jnp.concatenate([_pad_heads(w_kv[:, :hw], MEM_HEADS, MEM_HEAD_DIM),
                           _pad_heads(w_kv[:, hw:], MEM_HEADS, MEM_HEAD_DIM)], axis=1).astype(BF16)
    consts = [mem_g.reshape(1, d), wkv, _pad_cols(k_g.reshape(1, -1))]
    wide = MEM_HEADS * LANES
    spec = pl.BlockSpec((1, m, wide), lambda bi: (bi, 0, 0))
    return pl.pallas_call(
        _mem_kv_kernel,
        grid=(b,),
        in_specs=[pl.BlockSpec((1, m, d), lambda bi: (bi, 0, 0))] + [_const_spec(c.shape) for c in consts],
        out_specs=[spec, spec],
        out_shape=[jax.ShapeDtypeStruct((b, m, wide), BF16)] * 2,
        compiler_params=_params(("parallel",)),
        name="mem_kv",
    )(mem, *consts)


def _split_dot(a, w):
    a_hi = a.astype(BF16)
    a_lo = (a - a_hi.astype(F32)).astype(BF16)
    w_hi = w.astype(BF16)
    w_lo = (w - w_hi.astype(F32)).astype(BF16)
    w_both = jnp.concatenate([w_hi, w_lo], axis=1)
    n = w.shape[1]
    hi = jnp.dot(a_hi, w_both, preferred_element_type=F32)
    lo = jnp.dot(a_lo, w_both, preferred_element_type=F32)
    return (hi[:, :n] + hi[:, n:]) + (lo[:, :n] + lo[:, n:])


def _merge_kernel(oa_ref, ob_ref, sgd_ref, sgm_ref, x_ref, wa_ref, wb_ref, wo_ref, mxg_ref, wq_ref, qg_ref,
                  km_ref, vm_ref, wmo_ref, moeg_ref, wr_ref, bias_ref, x2_o, h3_o, comb_o):
    d = functools.partial(jnp.dot, preferred_element_type=F32)
    oa = jnp.concatenate([oa_ref[0, hp] for hp in range(DSA_HEADS // 2)], axis=-1)
    merged = (sgd_ref[0].astype(F32) * d(oa, wa_ref[...]) + sgm_ref[0].astype(F32) * d(ob_ref[0], wb_ref[...]))
    x1 = x_ref[0] + d(merged.astype(BF16), wo_ref[...])

    d_model = x1.shape[-1]
    h2 = _rms(x1, mxg_ref[...], d_model).astype(BF16)
    qm = d(h2, wq_ref[...])
    outs = []
    for hd in range(MEM_HEADS):
        sl = slice(hd * LANES, (hd + 1) * LANES)
        q = (_rms(qm[:, sl], qg_ref[...], MEM_HEAD_DIM) * MEM_HEAD_DIM ** -0.5).astype(BF16)
        logit = lax.dot_general(q, km_ref[0, :, sl], (((1,), (1,)), ((), ())), preferred_element_type=F32)
        logit = logit - jnp.max(logit, axis=-1, keepdims=True)
        p = jnp.exp(logit)
        pv = d(p.astype(BF16), vm_ref[0, :, sl])
        outs.append((pv / jnp.sum(p, axis=-1, keepdims=True)).astype(BF16))
    x2 = x1 + d(jnp.concatenate(outs, axis=-1), wmo_ref[...])
    x2_o[0] = x2

    h3 = _rms(x2, moeg_ref[...], d_model)
    h3_o[0] = h3.astype(BF16)
    logits = _split_dot(h3, wr_ref[...])
    tm = logits.shape[0]
    lane = lax.broadcasted_iota(jnp.int32, (tm, LANES), 1)
    big = jnp.int32(LANES)
    is_grp = jnp.logical_and(lane >= N_EXPERTS, lane < N_EXPERTS + N_GROUPS)
    glog = jnp.where(is_grp, logits, -jnp.inf)
    gmax = jnp.max(glog, axis=-1, keepdims=True)
    g_sel = jnp.min(jnp.where(glog == gmax, lane, big), axis=-1, keepdims=True) - N_EXPERTS
    p_g = 1.0 / jnp.sum(jnp.where(is_grp, jnp.exp(logits - gmax), 0.0), axis=-1, keepdims=True)
    aff = jax.nn.sigmoid(logits)
    in_grp = jnp.logical_and(lane >= g_sel * EXPERTS_PER_GROUP, lane < (g_sel + 1) * EXPERTS_PER_GROUP)
    val = jnp.where(in_grp, aff + bias_ref[...], -jnp.inf)
    m1 = jnp.max(val, axis=-1, keepdims=True)
    i1 = jnp.min(jnp.where(val == m1, lane, big), axis=-1, keepdims=True)
    val2 = jnp.where(lane == i1, -jnp.inf, val)
    m2 = jnp.max(val2, axis=-1, keepdims=True)
    i2 = jnp.min(jnp.where(val2 == m2, lane, big), axis=-1, keepdims=True)
    chosen = jnp.logical_or(lane == i1, lane == i2)
    a_sel = jnp.where(chosen, aff, 0.0)
    comb_o[0] = p_g * a_sel / jnp.sum(a_sel, axis=-1, keepdims=True)


def _merge_mem_router(oa, ob, sgd, sgm, x, w_br_dsa, w_br_mla, w_out, mem_x_g, mem_w_q, mem_q_g, km, vm,
                      mem_w_o, moe_g, w_group, w_expert, expert_bias, tm):
    b, s, d = x.shape
    wa = w_br_dsa.astype(BF16)
    wb = w_br_mla.astype(BF16)
    wq = _pad_heads(mem_w_q, MEM_HEADS, MEM_HEAD_DIM).astype(BF16)
    wmo = jnp.pad(mem_w_o.reshape(MEM_HEADS, MEM_HEAD_DIM, d), ((0, 0), (0, LANES - MEM_HEAD_DIM), (0, 0)))
    wmo = wmo.reshape(MEM_HEADS * LANES, d).astype(BF16)
    wr = _pad_cols(jnp.concatenate([w_expert, w_group], axis=1).astype(F32))
    bias = _pad_cols(expert_bias.reshape(1, -1).astype(F32))
    consts_a = [wa, wb, w_out.astype(BF16), mem_x_g.reshape(1, d), wq, _pad_cols(mem_q_g.reshape(1, -1))]
    consts_b = [wmo, moe_g.reshape(1, d), wr, bias]
    m = km.shape[1]
    mw = MEM_HEADS * LANES
    tok = lambda w: pl.BlockSpec((1, tm, w), lambda bi, i: (bi, i, 0))
    memspec = pl.BlockSpec((1, m, mw), lambda bi, i: (bi, 0, 0))
    return pl.pallas_call(
        _merge_kernel,
        grid=(b, s // tm),
        in_specs=[pl.BlockSpec((1, DSA_HEADS // 2, tm, LANES), lambda bi, i: (bi, 0, i, 0)),
                  tok(MLA_HEADS * MLA_V), tok(d), tok(d), tok(d)]
                 + [_const_spec(c.shape, True) for c in consts_a] + [memspec, memspec]
                 + [_const_spec(c.shape, True) for c in consts_b],
        out_specs=[tok(d), tok(d), tok(LANES)],
        out_shape=[jax.ShapeDtypeStruct((b, s, d), F32), jax.ShapeDtypeStruct((b, s, d), BF16),
                   jax.ShapeDtypeStruct((b, s, LANES), F32)],
        compiler_params=_params(("parallel", "parallel")),
        name="merge_mem",
    )(oa, ob, sgd, sgm, x, *consts_a, km, vm, *consts_b)


def _moe_kernel(h_ref, comb_ref, x_ref, wg_ref, wu_ref, wd_ref, o_ref):
    step = pl.program_id(1)
    per = wg_ref.shape[0]
    h = h_ref[...]
    comb = comb_ref[...]
    lane = lax.broadcasted_iota(jnp.int32, comb.shape, 1)
    parts = []
    for e in range(per):
        gate = jnp.dot(h, wg_ref[e], preferred_element_type=F32)
        up = jnp.dot(h, wu_ref[e], preferred_element_type=F32)
        c_e = jnp.sum(jnp.where(lane == step * per + e, comb, 0.0), axis=-1, keepdims=True)
        parts.append(((jax.nn.silu(gate) * up) * c_e).astype(BF16))
    w_down = wd_ref[...].reshape(per * D_EXPERT, wd_ref.shape[-1])
    contrib = jnp.dot(jnp.concatenate(parts, axis=1), w_down, preferred_element_type=F32)

    @pl.when(step == 0)
    def _():
        o_ref[...] = x_ref[...] + contrib

    @pl.when(step > 0)
    def _():
        o_ref[...] += contrib


def _moe(h3, comb, x2, w_gate, w_up, w_down, tm):
    n, d = x2.shape
    per = MOE_EXPERTS_PER_STEP
    return pl.pallas_call(
        _moe_kernel,
        grid=(n // tm, N_EXPERTS // per),
        in_specs=[pl.BlockSpec((tm, d), lambda i, g: (i, 0)),
                  pl.BlockSpec((tm, LANES), lambda i, g: (i, 0)),
                  pl.BlockSpec((tm, d), lambda i, g: (i, 0)),
                  pl.BlockSpec((per, d, D_EXPERT), lambda i, g: (g, 0, 0), pipeline_mode=pl.Buffered(1)),
                  pl.BlockSpec((per, d, D_EXPERT), lambda i, g: (g, 0, 0), pipeline_mode=pl.Buffered(1)),
                  pl.BlockSpec((per, D_EXPERT, d), lambda i, g: (g, 0, 0), pipeline_mode=pl.Buffered(1))],
        out_specs=pl.BlockSpec((tm, d), lambda i, g: (i, 0)),
        out_shape=jax.ShapeDtypeStruct((n, d), F32),
        compiler_params=_params(("parallel", "arbitrary")),
        name="moe",
    )(h3, comb, x2, w_gate.astype(BF16), w_up.astype(BF16), w_down.astype(BF16))


def _layer(x, mem, positions, p):
    b, s, d = x.shape
    n = b * s
    assert s % (2 * KEY_CHUNK) == 0 and s % Q_TILE_MLA == 0
    topk = min(TOPK_MAX, s // 4)
    tm = min(TOKENS_PER_STEP_PROJ, s)
    qq, ka, va, ki, iw, mq, mk, mv, sgd, sgm = _in_proj(
        x.reshape(n, d), positions.reshape(n, 1), p["attn_norm_g"], p["w_in"], p["dsa_q_norm_g"],
        p["dsa_k_norm_g"], p["mla_cq_norm_g"], p["mla_ckv_norm_g"], p["mla_w_uq"], p["mla_w_ukv"],
        p["mla_q_norm_g"], p["mla_k_norm_g"], tm)
    r3 = lambda a: a.reshape(b, s, a.shape[-1])
    oa = _dsa_attention(r3(qq), iw, r3(ka), va, r3(ki), topk)
    ob = _mla_attention(r3(mq), r3(mk), mv)
    km, vm = _mem_kv(mem, p["mem_norm_g"], p["mem_w_kv"], p["mem_k_norm_g"])
    x2, h3, comb = _merge_mem_router(
        oa, ob, r3(sgd), r3(sgm), x, p["w_branch_dsa"], p["w_branch_mla"], p["w_out"], p["mem_x_norm_g"],
        p["mem_w_q"], p["mem_q_norm_g"], km, vm, p["mem_w_o"], p["moe_norm_g"], p["moe_w_group"],
        p["moe_w_expert"], p["moe_expert_bias"], min(TOKENS_PER_STEP_MERGE, s))
    out = _moe(h3.reshape(n, d), comb.reshape(n, LANES), x2.reshape(n, d), p["moe_w_gate"], p["moe_w_up"],
               p["moe_w_down"], min(TOKENS_PER_STEP_MOE, n))
    return out.reshape(b, s, d)


_PARAM_NAMES = ("attn_norm_g", "w_in", "dsa_q_norm_g", "dsa_k_norm_g", "mla_cq_norm_g", "mla_ckv_norm_g",
                "mla_w_uq", "mla_w_ukv", "mla_q_norm_g", "mla_k_norm_g", "w_branch_dsa", "w_branch_mla", "w_out",
                "mem_x_norm_g", "mem_norm_g", "mem_w_q", "mem_w_kv", "mem_q_norm_g", "mem_k_norm_g", "mem_w_o",
                "moe_norm_g", "moe_w_group", "moe_w_expert", "moe_expert_bias", "moe_w_gate", "moe_w_up",
                "moe_w_down")


def kernel(x, mem, positions, attn_norm_g, w_in, dsa_q_norm_g, dsa_k_norm_g, mla_cq_norm_g, mla_ckv_norm_g, mla_w_uq, mla_w_ukv, mla_q_norm_g, mla_k_norm_g, w_branch_dsa, w_branch_mla, w_out, mem_x_norm_g, mem_norm_g, mem_w_q, mem_w_kv, mem_q_norm_g, mem_k_norm_g, mem_w_o, moe_norm_g, moe_w_group, moe_w_expert, moe_expert_bias, moe_w_gate, moe_w_up, moe_w_down):
    stacked = (attn_norm_g, w_in, dsa_q_norm_g, dsa_k_norm_g, mla_cq_norm_g, mla_ckv_norm_g, mla_w_uq, mla_w_ukv,
               mla_q_norm_g, mla_k_norm_g, w_branch_dsa, w_branch_mla, w_out, mem_x_norm_g, mem_norm_g, mem_w_q,
               mem_w_kv, mem_q_norm_g, mem_k_norm_g, mem_w_o, moe_norm_g, moe_w_group, moe_w_expert,
               moe_expert_bias, moe_w_gate, moe_w_up, moe_w_down)
    for layer in range(attn_norm_g.shape[0]):
        p = {name: arr[layer] for name, arr in zip(_PARAM_NAMES, stacked)}
        x = _layer(x, mem, positions, p)
    return x
```
